```python
import jax, jax.numpy as jnp
from jax import lax
import numpy as np

D_MODEL = 1024
BATCH = 8
SEQ = 2048
DEPTH = 1
DEC_BATCH = 32
DEC_SEQ = 4
PAST_LEN = 8192
PAGE_SIZE = 128

N_MEM = 256
MEM_HEADS = 4
MEM_HD = 128
RET_HEADS = 4
RET_DK = 64
RET_DV = 128
RET_CHUNK = 128
SWA_HEADS = 8
SWA_HD = 64
SWA_PATTERNS = ((128, 1), (512, 4), (2048, 16))
SWA_SPAN = 2048
SWA_BLOCK = 128
ROPE_THETA = 10000.0
LN_EPS = 1e-5
GN_EPS = 1e-5
RET_W = RET_HEADS * RET_DV
SWA_W = SWA_HEADS * SWA_HD
MEM_W = MEM_HEADS * MEM_HD
D_MIX = RET_W + SWA_W + MEM_W
IN_SPLITS = (RET_HEADS * RET_DK, RET_HEADS * RET_DK, RET_W, RET_W, SWA_W, SWA_W, SWA_W, SWA_W, MEM_W, MEM_W)
D_IN = sum(IN_SPLITS)
DEEPNORM_ALPHA = (2.0 * DEPTH) ** 0.25
DEEPNORM_BETA = (8.0 * DEPTH) ** -0.25

kernel_name = "hymba_retention_dilated_swa_memory_deepnorm_step"

F32 = jnp.float32


def _rope(x, pos):
    d = x.shape[-1]
    half = d // 2
    inv = ROPE_THETA ** (-jnp.arange(half, dtype=F32) * 2.0 / d)
    ang = pos.astype(F32)[:, None] * inv[None, :]
    cos = jnp.cos(ang)[:, None, :]
    sin = jnp.sin(ang)[:, None, :]
    x32 = x.astype(F32)
    x1, x2 = x32[..., :half], x32[..., half:]
    return jnp.concatenate([x1 * cos - x2 * sin, x2 * cos + x1 * sin], axis=-1).astype(x.dtype)


def _layernorm(z, g, b):
    z32 = z.astype(F32)
    mu = z32.mean(-1, keepdims=True)
    var = jnp.square(z32 - mu).mean(-1, keepdims=True)
    return (z32 - mu) * lax.rsqrt(var + LN_EPS) * g.astype(F32) + b.astype(F32)


def _head_norm(o):
    mu = o.mean(-1, keepdims=True)
    var = jnp.square(o - mu).mean(-1, keepdims=True)
    return (o - mu) * lax.rsqrt(var + GN_EPS)


def _project(x, w_in, pos):
    B, T, _ = x.shape
    h = jnp.einsum('btd,de->bte', x, w_in)
    offsets = np.cumsum(IN_SPLITS)[:-1].tolist()
    rq, rk, rv, rg, sq, sk, sv, sg, mq, mg = jnp.split(h, offsets, axis=-1)
    rq = _rope(rq.reshape(B, T, RET_HEADS, RET_DK), pos)
    rk = _rope(rk.reshape(B, T, RET_HEADS, RET_DK), pos) * (RET_DK ** -0.5)
    rv = rv.reshape(B, T, RET_HEADS, RET_DV)
    sq = _rope(sq.reshape(B, T, SWA_HEADS, SWA_HD), pos)
    sk = _rope(sk.reshape(B, T, SWA_HEADS, SWA_HD), pos)
    sv = sv.reshape(B, T, SWA_HEADS, SWA_HD)
    mq = mq.reshape(B, T, MEM_HEADS, MEM_HD)
    return (rq, rk, rv, sq, sk, sv, mq), (rg, sg, mg)


def _retention(q, k, v, state0, chunk):
    B, T, H, dk = q.shape
    dv = v.shape[-1]
    n = T // chunk
    lg = jnp.log1p(-jnp.exp2(-5.0 - jnp.arange(H, dtype=F32)))
    idx = jnp.arange(chunk, dtype=F32)
    rel = idx[:, None] - idx[None, :]
    dmat = jnp.where(rel >= 0, jnp.exp(jnp.maximum(rel, 0.0)[None] * lg[:, None, None]), 0.0)
    qc = q.astype(F32).reshape(B, n, chunk, H, dk)
    kc = k.astype(F32).reshape(B, n, chunk, H, dk)
    vc = v.astype(F32).reshape(B, n, chunk, H, dv)
    s = jnp.einsum('bnihk,bnjhk->bnhij', qc, kc) * dmat
    intra = jnp.einsum('bnhij,bnjhv->bnihv', s, vc)
    k_dec = kc * jnp.exp((chunk - 1.0 - idx)[:, None] * lg[None, :])[..., None]
    kv = jnp.einsum('bnjhk,bnjhv->nbhkv', k_dec, vc)
    g_chunk = jnp.exp(chunk * lg)[:, None, None]

    def step(r, kv_c):
        return g_chunk * r + kv_c, r

    r_fin, r_prev = lax.scan(step, state0, kv)
    q_dec = qc * jnp.exp((idx + 1.0)[:, None] * lg[None, :])[..., None]
    cross = jnp.einsum('bnihk,nbhkv->bnihv', q_dec, r_prev)
    return (intra + cross).reshape(B, T, H, dv), r_fin


def _dilated_prompt(q, k, v, dil, steps):
    B, S, H, D = q.shape
    L = S // dil
    nblk = -(-L // SWA_BLOCK)
    Lp = nblk * SWA_BLOCK

    def to_res(a):
        a = a.reshape(B, L, dil, H, D).transpose(0, 2, 1, 3, 4).reshape(B * dil, L, H, D)
        return jnp.pad(a, ((0, 0), (0, Lp - L), (0, 0), (0, 0)))

    bd = B * dil
    qb = to_res(q).reshape(bd, nblk, SWA_BLOCK, H, D)
    kb = to_res(k).reshape(bd, nblk, SWA_BLOCK, H, D)
    vb = to_res(v).reshape(bd, nblk, SWA_BLOCK, H, D)

    def with_prev(a):
        prev = jnp.concatenate([jnp.zeros_like(a[:, :1]), a[:, :-1]], axis=1)
        return jnp.concatenate([prev, a], axis=2)

    kk, vv = with_prev(kb), with_prev(vb)
    s = jnp.einsum('bnqhd,bnkhd->bnhqk', qb, kk).astype(F32) * (D ** -0.5)
    qi = jnp.arange(SWA_BLOCK)[:, None] + SWA_BLOCK
    kj = jnp.arange(2 * SWA_BLOCK)[None, :]
    rel = qi - kj
    band = (rel >= 0) & (rel <= steps)
    first = (jnp.arange(nblk)[:, None, None] > 0) | (kj >= SWA_BLOCK)[None]
    mask = band[None] & first
    s = jnp.where(mask[None, :, None], s, -jnp.inf)
    m = s.max(-1)
    p = jnp.exp(s - m[..., None])
    l = p.sum(-1)
    m_t = jnp.swapaxes(m, -1, -2)
    l_t = jnp.swapaxes(l, -1, -2)
    o = jnp.einsum('bnhqk,bnkhd->bnqhd', p, vv.astype(F32)) / l_t[..., None]

    def back(a):
        a = a.reshape(B, dil, Lp, *a.shape[3:])[:, :, :L]
        a = jnp.moveaxis(a, 1, 2)
        return a.reshape(B, S, *a.shape[3:])

    return back(o), back(m_t), back(l_t)


def _dilated_sample(q, keys, vals, wb, dil, steps):
    D = q.shape[-1]
    T = q.shape[1]
    idx = wb + jnp.arange(T)[:, None] - dil * jnp.arange(steps + 1)[None, :]
    valid = idx >= 0
    idx_c = jnp.maximum(idx, 0)
    kg = keys[:, idx_c]
    vg = vals[:, idx_c]
    s = jnp.einsum('bthd,btnhd->bthn', q, kg).astype(F32) * (D ** -0.5)
    s = jnp.where(valid[None, :, None, :], s, -jnp.inf)
    m = s.max(-1)
    p = jnp.exp(s - m[..., None])
    l = p.sum(-1)
    o = jnp.einsum('bthn,btnhd->bthd', p, vg.astype(F32)) / l[..., None]
    return o, m, l


def _combine(parts):
    ms = jnp.stack([p[1] for p in parts])
    ls = jnp.stack([p[2] for p in parts])
    os_ = jnp.stack([p[0] for p in parts])
    w = ls * jnp.exp(ms - ms.max(0))
    return (w[..., None] * os_).sum(0) / w.sum(0)[..., None]


def _mem_attn(q, k, v):
    s = jnp.einsum('bthd,bmhd->bhtm', q, k).astype(F32) * (MEM_HD ** -0.5)
    p = jax.nn.softmax(s, axis=-1)
    return jnp.einsum('bhtm,bmhd->bthd', p, v.astype(F32))


def _finish(x, ret_o, swa_o, mem_o, gates, w_out, g, b):
    B, T, _ = x.shape
    rg, sg, mg = gates
    silu = lambda a: jax.nn.silu(a.astype(F32))
    mix = jnp.concatenate([
        silu(rg) * _head_norm(ret_o).reshape(B, T, RET_W),
        silu(sg) * swa_o.reshape(B, T, SWA_W),
        silu(mg) * mem_o.reshape(B, T, MEM_W)], axis=-1).astype(x.dtype)
    h = jnp.einsum('bte,ed->btd', mix, w_out)
    return _layernorm(DEEPNORM_ALPHA * x + h, g, b).astype(x.dtype)


def _prompt_layer(x, mem, w_in, w_mem_kv, w_out, g, b):
    B, S, _ = x.shape
    pos = jnp.arange(S)
    (rq, rk, rv, sq, sk, sv, mq), gates = _project(x, w_in, pos)
    state0 = jnp.zeros((B, RET_HEADS, RET_DK, RET_DV), F32)
    ret_o, ret_state = _retention(rq, rk, rv, state0, min(RET_CHUNK, S))
    swa_o = _combine([_dilated_prompt(sq, sk, sv, d, w // d) for (w, d) in SWA_PATTERNS])
    mkv = jnp.einsum('bmd,de->bme', mem, w_mem_kv)
    mk, mv = jnp.split(mkv, 2, axis=-1)
    mk = mk.reshape(B, mem.shape[1], MEM_HEADS, MEM_HD)
    mv = mv.reshape(B, mem.shape[1], MEM_HEADS, MEM_HD)
    mem_o = _mem_attn(mq, mk, mv)
    y = _finish(x, ret_o, swa_o, mem_o, gates, w_out, g, b)
    wb = min(SWA_SPAN, S)
    return y, ret_state.astype(x.dtype), sk[:, S - wb:], sv[:, S - wb:], mk, mv


def _sample_layer(x, state_ret, ck, cv, mk, mv, w_in, w_out, g, b):
    B, T, _ = x.shape
    wb = ck.shape[1]
    pos = PAST_LEN + jnp.arange(T)
    (rq, rk, rv, sq, sk, sv, mq), gates = _project(x, w_in, pos)
    ret_o, ret_state = _retention(rq, rk, rv, state_ret.astype(F32), T)
    keys = jnp.concatenate([ck, sk.astype(ck.dtype)], axis=1)
    vals = jnp.concatenate([cv, sv.astype(cv.dtype)], axis=1)
    swa_o = _combine([_dilated_sample(sq, keys, vals, wb, d, w // d) for (w, d) in SWA_PATTERNS])
    mem_o = _mem_attn(mq, mk, mv)
    y = _finish(x, ret_o, swa_o, mem_o, gates, w_out, g, b)
    return y, ret_state.astype(state_ret.dtype), sk, sv


def setup_inputs(seed: int = 0) -> dict:
    key = jax.random.key(seed)
    ks = jax.random.split(key, 13)
    wb = min(SWA_SPAN, PAST_LEN)
    nrm = jax.random.normal
    in_gains = (1.0, 1.0, DEEPNORM_BETA, 1.0, 1.0, 1.0, DEEPNORM_BETA, 1.0, 1.0, 1.0)
    col_scale = jnp.concatenate([jnp.full((w,), s, F32) for w, s in zip(IN_SPLITS, in_gains)])
    mem_scale = jnp.concatenate([jnp.ones((MEM_W,), F32), jnp.full((MEM_W,), DEEPNORM_BETA, F32)])
    return {
        'x_prompt': nrm(ks[0], (BATCH, SEQ, D_MODEL), F32),
        'x_sample': nrm(ks[1], (DEC_BATCH, DEC_SEQ, D_MODEL), F32),
        'state_ret': nrm(ks[2], (DEPTH, DEC_BATCH, RET_HEADS, RET_DK, RET_DV), F32),
        'cache_swa_k': nrm(ks[3], (DEPTH, DEC_BATCH, wb, SWA_HEADS, SWA_HD), F32),
        'cache_swa_v': nrm(ks[4], (DEPTH, DEC_BATCH, wb, SWA_HEADS, SWA_HD), F32),
        'cache_mem_k': nrm(ks[5], (DEPTH, DEC_BATCH, N_MEM, MEM_HEADS, MEM_HD), F32),
        'cache_mem_v': nrm(ks[6], (DEPTH, DEC_BATCH, N_MEM, MEM_HEADS, MEM_HD), F32),
        'mem_prompt': nrm(ks[7], (BATCH, N_MEM, D_MODEL), F32),
        'w_in': nrm(ks[8], (DEPTH, D_MODEL, D_IN), F32) * (D_MODEL ** -0.5) * col_scale,
        'w_mem_kv': nrm(ks[9], (DEPTH, D_MODEL, 2 * MEM_W), F32) * (D_MODEL ** -0.5) * mem_scale,
        'w_out': nrm(ks[10], (DEPTH, D_MIX, D_MODEL), F32) * (D_MIX ** -0.5) * DEEPNORM_BETA,
        'ln_gain': 1.0 + 0.02 * nrm(ks[11], (DEPTH, D_MODEL), F32),
        'ln_bias': 0.02 * nrm(ks[12], (DEPTH, D_MODEL), F32),
    }


def reference(x_prompt, x_sample, state_ret, cache_swa_k, cache_swa_v, cache_mem_k, cache_mem_v,
              mem_prompt, w_in, w_mem_kv, w_out, ln_gain, ln_bias):
    yp, ys = x_prompt, x_sample
    rp_l, rs_l, kp_l, vp_l, ks_l, vs_l, mkp_l, mvp_l = [], [], [], [], [], [], [], []
    for layer in range(DEPTH):
        yp, rp, kp, vp, mkp, mvp = _prompt_layer(yp, mem_prompt, w_in[layer], w_mem_kv[layer],
                                                 w_out[layer], ln_gain[layer], ln_bias[layer])
        ys, rs, ksn, vsn = _sample_layer(ys, state_ret[layer], cache_swa_k[layer], cache_swa_v[layer],
                                         cache_mem_k[layer], cache_mem_v[layer], w_in[layer],
                                         w_out[layer], ln_gain[layer], ln_bias[layer])
        rp_l.append(rp); rs_l.append(rs); kp_l.append(kp); vp_l.append(vp)
        ks_l.append(ksn); vs_l.append(vsn); mkp_l.append(mkp); mvp_l.append(mvp)
    return (yp, ys, jnp.stack(rp_l), jnp.stack(rs_l), jnp.stack(kp_l), jnp.stack(vp_l),
            jnp.stack(ks_l), jnp.stack(vs_l), jnp.stack(mkp_l), jnp.stack(mvp_l))
```

```python
import functools

import jax
import jax.numpy as jnp
from jax import lax
from jax.experimental import pallas as pl
from jax.experimental.pallas import tpu as pltpu

F32 = jnp.float32
BF16 = jnp.bfloat16

D_MODEL = 1024
BATCH = 8
SEQ = 2048
DEC_BATCH = 32
DEC_SEQ = 4
PAST_LEN = 8192
N_MEM = 256
MEM_HEADS = 4
MEM_HD = 128
RET_HEADS = 4
RET_DK = 64
RET_DV = 128
RET_CHUNK = 128
SWA_HEADS = 8
SWA_HD = 64
SWA_DILATIONS = (1, 4, 16)
SWA_STEPS = 128
SWA_BLOCK = 128
ROPE_THETA = 10000.0
LN_EPS = 1e-5
GN_EPS = 1e-5
RET_W = RET_HEADS * RET_DV
SWA_W = SWA_HEADS * SWA_HD
MEM_W = MEM_HEADS * MEM_HD
D_MIX = RET_W + SWA_W + MEM_W
DEEPNORM_ALPHA = 2.0 ** 0.25
MEM_SCALE = MEM_HD ** -0.5
QK_SCALE = 0.125

LANES = 128
DEC_PAD = 16
VMEM_LIMIT = 56 * 1024 * 1024
NEG = -1e30

_PROJ_COLS = (
    ("rq", RET_HEADS * RET_DK, "rope", 1.0),
    ("rk", RET_HEADS * RET_DK, "rope", QK_SCALE),
    ("rv", RET_W, "id", 1.0),
    ("rg", RET_W, "silu", 1.0),
    ("sq", SWA_W, "rope", QK_SCALE),
    ("sk", SWA_W, "rope", 1.0),
    ("sv", SWA_W, "id", 1.0),
    ("sg", SWA_W, "silu", 1.0),
    ("mq", MEM_W, "id", 1.0),
    ("mg", MEM_W, "silu", 1.0),
)


def _dot(a, b):
    return jnp.dot(a, b, preferred_element_type=F32)


def _dot_nt(a, b):
    return lax.dot_general(a, b, (((1,), (1,)), ((), ())), preferred_element_type=F32)


def _dot_tn(a, b):
    return lax.dot_general(a, b, (((0,), (0,)), ((), ())), preferred_element_type=F32)


def _params(n_axes):
    return pltpu.CompilerParams(dimension_semantics=("arbitrary",) * n_axes,
                                vmem_limit_bytes=VMEM_LIMIT)


def _proj_kernel(x_ref, w_ref, cos_ref, sin_ref, *out_refs, dests):
    xb = x_ref[...].astype(BF16)
    cos = cos_ref[...]
    sin = sin_ref[...]
    lane = lax.broadcasted_iota(jnp.int32, cos.shape, 1)
    first_half = (lane % 64) < 32
    col = 0
    for name, width, kind, scale in _PROJ_COLS:
        targets = [(o_ref, layout) for o_ref, (dname, layout) in zip(out_refs, dests) if dname == name]
        for c in range(0, width, 2 * LANES):
            h2 = _dot(xb, w_ref[:, col + c:col + c + 2 * LANES])
            for half in range(2):
                h = h2[:, half * LANES:(half + 1) * LANES]
                if kind == "rope":
                    swapped = jnp.where(first_half, pltpu.roll(h, 96, 1), pltpu.roll(h, 32, 1))
                    h = h * cos + swapped * sin
                    if scale != 1.0:
                        h = h * scale
                elif kind == "silu":
                    h = h * (1.0 / (1.0 + jnp.exp(-h)))
                lo = c + half * LANES
                for o_ref, layout in targets:
                    if layout == "rows":
                        o_ref[:, lo:lo + LANES] = h.astype(o_ref.dtype)
                    else:
                        o_ref[lo // LANES] = h.astype(o_ref.dtype)
        col += width


def _project(x2d, w_bf, cos_t, sin_t, tile, outs):
    n = x2d.shape[0]
    n_tab = cos_t.shape[0] // tile
    d_in = w_bf.shape[1]
    widths = {name: w for name, w, _, _ in _PROJ_COLS}
    out_shape, out_specs = [], []
    for name, layout, dt in outs:
        w = widths[name]
        if layout == "rows":
            out_shape.append(jax.ShapeDtypeStruct((n, w), dt))
            out_specs.append(pl.BlockSpec((tile, w), lambda i: (i, 0)))
        else:
            out_shape.append(jax.ShapeDtypeStruct((w // LANES, n, LANES), dt))
            out_specs.append(pl.BlockSpec((w // LANES, tile, LANES), lambda i: (0, i, 0)))
    return pl.pallas_call(
        functools.partial(_proj_kernel, dests=tuple((name, layout) for name, layout, _ in outs)),
        grid=(n // tile,),
        in_specs=[
            pl.BlockSpec((tile, D_MODEL), lambda i: (i, 0)),
            pl.BlockSpec((D_MODEL, d_in), lambda i: (0, 0)),
            pl.BlockSpec((tile, LANES), lambda i: (i % n_tab, 0)),
            pl.BlockSpec((tile, LANES), lambda i: (i % n_tab, 0)),
        ],
        out_specs=out_specs,
        out_shape=out_shape,
        compiler_params=_params(1),
        name="proj",
    )(x2d, w_bf, cos_t, sin_t)


def _memkv_kernel(m_ref, w_ref, mk_ref, mv_ref):
    mb = m_ref[...].astype(BF16)
    mk_ref[...] = _dot(mb, w_ref[:, :MEM_W])
    mv_ref[...] = _dot(mb, w_ref[:, MEM_W:])


def _memkv(mem2d, w_bf):
    n = mem2d.shape[0]
    tile = 512
    return pl.pallas_call(
        _memkv_kernel,
        grid=(n // tile,),
        in_specs=[pl.BlockSpec((tile, D_MODEL), lambda i: (i, 0)),
                  pl.BlockSpec((D_MODEL, 2 * MEM_W), lambda i: (0, 0))],
        out_specs=[pl.BlockSpec((tile, MEM_W), lambda i: (i, 0))] * 2,
        out_shape=[jax.ShapeDtypeStruct((n, MEM_W), F32)] * 2,
        compiler_params=_params(1),
        name="memkv",
    )(mem2d, w_bf)


def _swa_prompt_kernel(q_ref, k_ref, v_ref, o_ref, m_ref, l_ref):
    blk = SWA_BLOCK
    row2 = lax.broadcasted_iota(jnp.int32, (blk, 2 * blk), 0)
    col2 = lax.broadcasted_iota(jnp.int32, (blk, 2 * blk), 1)
    rel2 = row2 + blk - col2
    bias_prev = jnp.where((rel2 >= 0) & (rel2 <= SWA_STEPS), 0.0, NEG).astype(F32)
    row1 = lax.broadcasted_iota(jnp.int32, (blk, blk), 0)
    col1 = lax.broadcasted_iota(jnp.int32, (blk, blk), 1)
    bias_own = jnp.where(row1 >= col1, 0.0, NEG).astype(F32)
    lo = col1 < SWA_HD

    def block(start, d, has_prev, first_pattern, last_pattern):
        def rows(s, n):
            return pl.ds(s, n) if d == 1 else pl.ds(s, n, stride=d)

        rq = rows(start, blk)
        rk = rows(start - blk * d, 2 * blk) if has_prev else rq
        bias = bias_prev if has_prev else bias_own
        for pair in range(SWA_HEADS // 2):
            q = q_ref[pair, rq, :]
            kb = k_ref[pair, rk, :].astype(BF16)
            vb = v_ref[pair, rk, :].astype(BF16)
            stats = []
            for hh in range(2):
                qm = jnp.where(lo if hh == 0 else ~lo, q, 0.0).astype(BF16)
                s = _dot_nt(qm, kb) + bias
                m = jnp.max(s, axis=-1, keepdims=True)
                p = jnp.exp(s - m)
                l = jnp.sum(p, axis=-1, keepdims=True)
                stats.append((m, l, _dot(p.astype(BF16), vb)))
            m_g = jnp.where(lo, stats[0][0], stats[1][0])
            l_g = jnp.where(lo, stats[0][1], stats[1][1])
            a_g = jnp.where(lo, stats[0][2], stats[1][2])
            if first_pattern:
                m_new, l_new, a_new = m_g, l_g, a_g
            else:
                m_old = m_ref[pair, rq, :]
                m_new = jnp.maximum(m_old, m_g)
                w_old = jnp.exp(m_old - m_new)
                w_g = jnp.exp(m_g - m_new)
                l_new = w_old * l_ref[pair, rq, :] + w_g * l_g
                a_new = w_old * o_ref[pair, rq, :] + w_g * a_g
            if last_pattern:
                o_ref[pair, rq, :] = a_new / l_new
            else:
                m_ref[pair, rq, :] = m_new
                l_ref[pair, rq, :] = l_new
                o_ref[pair, rq, :] = a_new

    n_pat = len(SWA_DILATIONS)
    for pi, d in enumerate(SWA_DILATIONS):
        nblk = SEQ // d // blk
        first_p, last_p = pi == 0, pi == n_pat - 1

        def per_class(r, carry, d=d, nblk=nblk, first_p=first_p, last_p=last_p):
            block(r, d, False, first_p, last_p)
            if nblk > 1:
                def per_block(j, c2):
                    start = r + j * (blk * d)
                    if d == 1:
                        start = pl.multiple_of(start, blk)
                    block(start, d, True, first_p, last_p)
                    return c2
                lax.fori_loop(1, nblk, per_block, 0)
            return carry

        if d == 1:
            per_class(0, 0)
        else:
            lax.fori_loop(0, d, per_class, 0)


def _swa_prompt(sq, sk, sv):
    slabs = SWA_W // LANES
    spec = pl.BlockSpec((slabs, None, SEQ, LANES), lambda b: (0, b, 0, 0))
    return pl.pallas_call(
        _swa_prompt_kernel,
        grid=(BATCH,),
        in_specs=[spec, spec, spec],
        out_specs=spec,
        out_shape=jax.ShapeDtypeStruct((slabs, BATCH, SEQ, LANES), F32),
        scratch_shapes=[pltpu.VMEM((slabs, SEQ, LANES), F32), pltpu.VMEM((slabs, SEQ, LANES), F32)],
        compiler_params=_params(1),
        name="swa_prompt",
    )(sq, sk, sv)


def _head_norm(o):
    mu = jnp.mean(o, axis=-1, keepdims=True)
    d = o - mu
    var = jnp.mean(d * d, axis=-1, keepdims=True)
    return d * lax.rsqrt(var + GN_EPS)


def _deepnorm_ln(x, h, gain, bias):
    z = DEEPNORM_ALPHA * x + h
    mu = jnp.mean(z, axis=-1, keepdims=True)
    d = z - mu
    var = jnp.mean(d * d, axis=-1, keepdims=True)
    return d * lax.rsqrt(var + LN_EPS) * gain + bias


def _softmax_rows(s):
    m = jnp.max(s, axis=-1, keepdims=True)
    p = jnp.exp(s - m)
    return p * (1.0 / jnp.sum(p, axis=-1, keepdims=True))


def _mix_prompt_kernel(x_ref, rq_ref, rk_ref, rv_ref, rg_ref, so_ref, sg_ref, mq_ref, mg_ref,
                       mk_ref, mv_ref, wout_ref, dmat_ref, kdec_ref, qdec_ref, gdec_ref,
                       gain_ref, bias_ref, y_ref, state_out_ref,
                       state_ref, mix_ref, mkb_ref, mvb_ref, *, tile):
    t = pl.program_id(1)

    @pl.when(t == 0)
    def _():
        state_ref[...] = jnp.zeros_like(state_ref)
        mkb_ref[...] = mk_ref[...].astype(BF16)
        mvb_ref[...] = mv_ref[...].astype(BF16)

    ck = RET_CHUNK
    lane = lax.broadcasted_iota(jnp.int32, (ck, LANES), 1)
    lo = lane < RET_DK
    top = lax.broadcasted_iota(jnp.int32, (LANES, LANES), 0) < RET_DK
    gain = gain_ref[...]
    bias = bias_ref[...]

    def chunk(c, carry):
        rows = pl.ds(pl.multiple_of(c * ck, ck), ck)
        for pair in range(RET_HEADS // 2):
            cs = pl.ds(pair * LANES, LANES)
            q = rq_ref[rows, cs]
            k = rk_ref[rows, cs]
            kb = k.astype(BF16)
            kd = (k * kdec_ref[:, cs]).astype(BF16)
            qd = q * qdec_ref[:, cs]
            st = state_ref[pair]
            stb = st.astype(BF16)
            kv = []
            for hh in range(2):
                h = 2 * pair + hh
                hs = pl.ds(h * RET_DV, RET_DV)
                msk = lo if hh == 0 else ~lo
                v = rv_ref[rows, hs]
                s = _dot_nt(jnp.where(msk, q, 0.0).astype(BF16), kb) * dmat_ref[h]
                o = _dot(s.astype(BF16), v) + _dot(jnp.where(msk, qd, 0.0).astype(BF16), stb)
                kv.append(_dot_tn(kd, v))
                mix_ref[:, hs] = (rg_ref[rows, hs].astype(F32) * _head_norm(o)).astype(BF16)
            state_ref[pair] = gdec_ref[pl.ds(pair * LANES, LANES), :] * st + jnp.where(top, kv[0], kv[1])
        for pair in range(SWA_W // LANES):
            cs = pl.ds(pair * LANES, LANES)
            mix_ref[:, pl.ds(RET_W + pair * LANES, LANES)] = (
                sg_ref[rows, cs].astype(F32) * so_ref[pair, rows, :]).astype(BF16)
        for h in range(MEM_HEADS):
            hs = pl.ds(h * MEM_HD, MEM_HD)
            p = _softmax_rows(_dot_nt(mq_ref[rows, hs], mkb_ref[:, hs]) * MEM_SCALE)
            o = _dot(p.astype(BF16), mvb_ref[:, hs])
            mix_ref[:, pl.ds(RET_W + SWA_W + h * MEM_HD, MEM_HD)] = (
                mg_ref[rows, hs].astype(F32) * o).astype(BF16)
        hout = _dot(mix_ref[...], wout_ref[...])
        y_ref[rows, :] = _deepnorm_ln(x_ref[rows, :], hout, gain, bias)
        return carry

    lax.fori_loop(0, tile // ck, chunk, 0)

    @pl.when(t == pl.num_programs(1) - 1)
    def _():
        state_out_ref[pl.ds(0, LANES), :] = state_ref[0]
        state_out_ref[pl.ds(LANES, LANES), :] = state_ref[1]


def _mix_prompt(x2d, pr, swa_o, mk, mv, wout_bf, tabs, gain, bias, tile):
    nt = SEQ // tile
    row = lambda w: pl.BlockSpec((tile, w), lambda b, t: (b * nt + t, 0))
    const2 = lambda a: pl.BlockSpec(a.shape, lambda b, t: (0, 0))
    dmat, kdec, qdec, gdec = tabs
    return pl.pallas_call(
        functools.partial(_mix_prompt_kernel, tile=tile),
        grid=(BATCH, nt),
        in_specs=[
            row(D_MODEL), row(RET_HEADS * RET_DK), row(RET_HEADS * RET_DK), row(RET_W), row(RET_W),
            pl.BlockSpec((SWA_W // LANES, tile, LANES), lambda b, t: (0, b * nt + t, 0)),
            row(SWA_W), row(MEM_W), row(MEM_W),
            pl.BlockSpec((N_MEM, MEM_W), lambda b, t: (b, 0)),
            pl.BlockSpec((N_MEM, MEM_W), lambda b, t: (b, 0)),
            const2(wout_bf),
            pl.BlockSpec(dmat.shape, lambda b, t: (0, 0, 0)),
            const2(kdec), const2(qdec), const2(gdec), const2(gain), const2(bias),
        ],
        out_specs=[row(D_MODEL),
                   pl.BlockSpec((None, RET_HEADS * RET_DK, RET_DV), lambda b, t: (b, 0, 0))],
        out_shape=[jax.ShapeDtypeStruct((BATCH * SEQ, D_MODEL), F32),
                   jax.ShapeDtypeStruct((BATCH, RET_HEADS * RET_DK, RET_DV), F32)],
        scratch_shapes=[pltpu.VMEM((RET_HEADS // 2, LANES, RET_DV), F32),
                        pltpu.VMEM((RET_CHUNK, D_MIX), BF16),
                        pltpu.VMEM((N_MEM, MEM_W), BF16),
                        pltpu.VMEM((N_MEM, MEM_W), BF16)],
        compiler_params=_params(2),
        name="mix_prompt",
    )(x2d, pr["rq"], pr["rk"], pr["rv"], pr["rg"], swa_o, pr["sg"], pr["mq"], pr["mg"],
      mk, mv, wout_bf, dmat, kdec, qdec, gdec, gain, bias)


def _mix_sample_kernel(rq_ref, rk_ref, rv_ref, rg_ref, sq_ref, sk_ref, sv_ref, sg_ref, mq_ref, mg_ref,
                       state_ref, ckt_ref, cks_ref, cvt_ref, cvs_ref, cmk_ref, cmv_ref,
                       dmat_ref, kdec_ref, qdec_ref, gdec_ref,
                       mix_ref, state_out_ref, swa_ref):
    n = DEC_PAD
    q = rq_ref[...]
    k = rk_ref[...]
    kb = k.astype(BF16)
    kd = (k * kdec_ref[...]).astype(BF16)
    qd = q * qdec_ref[...]
    st = state_ref[...]
    stb = st.astype(BF16)
    vb = rv_ref[...].astype(BF16)
    kv = _dot_tn(kd, vb)
    lane_qk = lax.broadcasted_iota(jnp.int32, q.shape, 1) // RET_DK
    for h in range(RET_HEADS):
        hs = pl.ds(h * RET_DV, RET_DV)
        ks = pl.ds(h * RET_DK, RET_DK)
        msk = lane_qk == h
        s = _dot_nt(jnp.where(msk, q, 0.0).astype(BF16), kb) * dmat_ref[h]
        o = _dot(s.astype(BF16), vb[:, h * RET_DV:(h + 1) * RET_DV])
        o = o + _dot(jnp.where(msk, qd, 0.0).astype(BF16), stb)
        mix_ref[:, hs] = rg_ref[:, hs] * _head_norm(o)
        state_out_ref[ks, :] = (gdec_ref[ks, :] * st[h * RET_DK:(h + 1) * RET_DK, :]
                                + kv[h * RET_DK:(h + 1) * RET_DK, h * RET_DV:(h + 1) * RET_DV])

    blk = SWA_BLOCK
    nq = DEC_SEQ * n
    sub = lax.broadcasted_iota(jnp.int32, (n, SWA_W), 0)
    diag = sub == lax.broadcasted_iota(jnp.int32, (n, SWA_W), 1) // SWA_HD
    sq = sq_ref[...]
    qe = jnp.concatenate(
        [jnp.where(diag, jnp.broadcast_to(sq[t:t + 1, :], (n, SWA_W)), 0.0) for t in range(DEC_SEQ)],
        axis=0).astype(BF16)
    pad = jnp.zeros((blk - n, SWA_W), BF16)
    k_new = jnp.concatenate([sk_ref[...].astype(BF16), pad], axis=0)
    v_new = jnp.concatenate([sv_ref[...].astype(BF16), pad], axis=0)
    s_new = _dot_nt(qe, k_new)
    tok_n = lax.broadcasted_iota(jnp.int32, (nq, blk), 0) // n
    col_n = lax.broadcasted_iota(jnp.int32, (nq, blk), 1)
    k_tail = ckt_ref[...].astype(BF16)
    v_tail = cvt_ref[...].astype(BF16)
    n_grp, grp = cks_ref.shape[0], cks_ref.shape[1]
    k_str = cks_ref[...].reshape(n_grp * grp, SWA_W).astype(BF16)
    v_str = cvs_ref[...].reshape(n_grp * grp, SWA_W).astype(BF16)
    tail = k_tail.shape[0]
    parts = []
    for d in SWA_DILATIONS:
        if d == 1:
            kw, vw = k_tail[tail - blk:, :], v_tail[tail - blk:, :]
            ok_new = col_n <= tok_n
        elif d == 4:
            kw, vw = k_tail, v_tail
            ok_new = col_n == tok_n
        else:
            kw, vw = k_str, v_str
            ok_new = col_n == tok_n
        nk = kw.shape[0]
        tok = lax.broadcasted_iota(jnp.int32, (nq, nk), 0) // n
        colk = lax.broadcasted_iota(jnp.int32, (nq, nk), 1)
        if d == 1:
            ok = colk >= tok
        elif d == 4:
            ok = colk % d == tok
        else:
            ok = colk % grp == tok
        s_c = _dot_nt(qe, kw) + jnp.where(ok, 0.0, NEG)
        s_n = s_new + jnp.where(ok_new, 0.0, NEG)
        m = jnp.maximum(jnp.max(s_c, axis=-1, keepdims=True), jnp.max(s_n, axis=-1, keepdims=True))
        p_c = jnp.exp(s_c - m)
        p_n = jnp.exp(s_n - m)
        l = jnp.sum(p_c, axis=-1, keepdims=True) + jnp.sum(p_n, axis=-1, keepdims=True)
        o = (_dot(p_c.astype(BF16), vw) + _dot(p_n.astype(BF16), v_new)) / l
        parts.append((o, m, l))
    m_all = jnp.maximum(jnp.maximum(parts[0][1], parts[1][1]), parts[2][1])
    ws = [l * jnp.exp(m - m_all) for (_, m, l) in parts]
    o = (ws[0] * parts[0][0] + ws[1] * parts[1][0] + ws[2] * parts[2][0]) / (ws[0] + ws[1] + ws[2])
    swa_ref[...] = jnp.zeros_like(swa_ref)
    for t in range(DEC_SEQ):
        swa_ref[pl.ds(t, 1), :] = jnp.sum(jnp.where(diag, o[t * n:(t + 1) * n, :], 0.0),
                                           axis=0, keepdims=True)
    mix_ref[:, pl.ds(RET_W, SWA_W)] = sg_ref[...] * swa_ref[...]

    mq = mq_ref[...]
    lane_m = lax.broadcasted_iota(jnp.int32, mq.shape, 1) // MEM_HD
    qe = jnp.concatenate([jnp.where(lane_m == h, mq, 0.0) for h in range(MEM_HEADS)], axis=0).astype(BF16)
    p = _softmax_rows(_dot_nt(qe, cmk_ref[...].astype(BF16)) * MEM_SCALE)
    o = _dot(p.astype(BF16), cmv_ref[...].astype(BF16))
    for h in range(MEM_HEADS):
        hs = pl.ds(h * MEM_HD, MEM_HD)
        mix_ref[:, pl.ds(RET_W + SWA_W + h * MEM_HD, MEM_HD)] = (
            mg_ref[:, hs] * o[h * n:(h + 1) * n, h * MEM_HD:(h + 1) * MEM_HD])


def _mix_sample(ps, state, ck, cv, cmk, cmv, tabs):
    n = DEC_PAD
    wb = ck.shape[1]
    tail = 4 * SWA_BLOCK
    row = lambda w: pl.BlockSpec((n, w), lambda b: (b, 0))
    st_spec = pl.BlockSpec((None, RET_HEADS * RET_DK, RET_DV), lambda b: (b, 0, 0))
    tail_spec = pl.BlockSpec((None, tail, SWA_W), lambda b: (b, wb // tail - 1, 0))
    str_spec = pl.BlockSpec((None, wb // 16, 8, SWA_W), lambda b: (b, 0, 0, 0))
    mem_spec = pl.BlockSpec((None, N_MEM, MEM_W), lambda b: (b, 0, 0))
    dmat, kdec, qdec, gdec = tabs
    ck4 = ck.reshape(DEC_BATCH, wb // 16, 16, SWA_W)
    cv4 = cv.reshape(DEC_BATCH, wb // 16, 16, SWA_W)
    return pl.pallas_call(
        _mix_sample_kernel,
        grid=(DEC_BATCH,),
        in_specs=[
            row(RET_HEADS * RET_DK), row(RET_HEADS * RET_DK), row(RET_W), row(RET_W),
            row(SWA_W), row(SWA_W), row(SWA_W), row(SWA_W), row(MEM_W), row(MEM_W),
            st_spec, tail_spec, str_spec, tail_spec, str_spec, mem_spec, mem_spec,
            pl.BlockSpec(dmat.shape, lambda b: (0, 0, 0)),
            pl.BlockSpec(kdec.shape, lambda b: (0, 0)),
            pl.BlockSpec(qdec.shape, lambda b: (0, 0)),
            pl.BlockSpec(gdec.shape, lambda b: (0, 0)),
        ],
        out_specs=[row(D_MIX), st_spec],
        out_shape=[jax.ShapeDtypeStruct((DEC_BATCH * n, D_MIX), F32),
                   jax.ShapeDtypeStruct((DEC_BATCH, RET_HEADS * RET_DK, RET_DV), F32)],
        scratch_shapes=[pltpu.VMEM((n, SWA_W), F32)],
        compiler_params=_params(1),
        name="mix_sample",
    )(ps["rq"], ps["rk"], ps["rv"], ps["rg"], ps["sq"], ps["sk"], ps["sv"], ps["sg"], ps["mq"], ps["mg"],
      state, ck, ck4, cv, cv4, cmk, cmv, dmat, kdec, qdec, gdec)


def _finish_kernel(x_ref, mix_ref, wout_ref, gain_ref, bias_ref, y_ref):
    hout = _dot(mix_ref[...].astype(BF16), wout_ref[...])
    y_ref[...] = _deepnorm_ln(x_ref[...], hout, gain_ref[...], bias_ref[...])


def _finish(x2d, mix, wout_bf, gain, bias):
    n = x2d.shape[0]
    tile = 256
    const2 = lambda a: pl.BlockSpec(a.shape, lambda i: (0, 0))
    return pl.pallas_call(
        _finish_kernel,
        grid=(n // tile,),
        in_specs=[pl.BlockSpec((tile, D_MODEL), lambda i: (i, 0)),
                  pl.BlockSpec((tile, D_MIX), lambda i: (i, 0)),
                  const2(wout_bf), const2(gain), const2(bias)],
        out_specs=pl.BlockSpec((tile, D_MODEL), lambda i: (i, 0)),
        out_shape=jax.ShapeDtypeStruct((n, D_MODEL), F32),
        compiler_params=_params(1),
        name="finish",
    )(x2d, mix, wout_bf, gain, bias)


def _rope_tables(pos):
    half = SWA_HD // 2
    inv = ROPE_THETA ** (-jnp.arange(half, dtype=F32) * 2.0 / SWA_HD)
    ang = pos.astype(F32)[:, None] * inv[None, :]
    cos = jnp.cos(ang)
    sin = jnp.sin(ang)
    reps = LANES // SWA_HD
    return (jnp.tile(jnp.concatenate([cos, cos], axis=1), (1, reps)),
            jnp.tile(jnp.concatenate([-sin, sin], axis=1), (1, reps)))


def _retention_tables(chunk, rows):
    lg = jnp.log1p(-jnp.exp2(-5.0 - jnp.arange(RET_HEADS, dtype=F32)))
    idx = jnp.arange(rows, dtype=F32)
    live = idx < chunk
    rel = idx[:, None] - idx[None, :]
    ok = (rel >= 0) & live[:, None] & live[None, :]
    dmat = jnp.where(ok[None], jnp.exp(jnp.maximum(rel, 0.0)[None] * lg[:, None, None]), 0.0)
    kdec = jnp.where(live[:, None], jnp.exp((chunk - 1.0 - idx)[:, None] * lg[None, :]), 0.0)
    qdec = jnp.where(live[:, None], jnp.exp((idx + 1.0)[:, None] * lg[None, :]), 0.0)
    g = jnp.exp(chunk * lg)
    kdec = jnp.repeat(kdec, RET_DK, axis=1)
    qdec = jnp.repeat(qdec, RET_DK, axis=1)
    gdec = jnp.broadcast_to(jnp.repeat(g, RET_DK)[:, None], (RET_HEADS * RET_DK, RET_DV))
    return dmat, kdec, qdec, gdec


def kernel(x_prompt, x_sample, state_ret, cache_swa_k, cache_swa_v, cache_mem_k, cache_mem_v,
           mem_prompt, w_in, w_mem_kv, w_out, ln_gain, ln_bias):
    depth = w_in.shape[0]
    assert depth == 1
    win_bf = w_in[0].astype(BF16)
    wmem_bf = w_mem_kv[0].astype(BF16)
    wout_bf = w_out[0].astype(BF16)
    gain = ln_gain[0].reshape(1, D_MODEL)
    bias = ln_bias[0].reshape(1, D_MODEL)

    xp = x_prompt.reshape(BATCH * SEQ, D_MODEL)
    cos_p, sin_p = _rope_tables(jnp.arange(SEQ))
    p_outs = (("rq", "rows", F32), ("rk", "rows", F32), ("rv", "rows", BF16), ("rg", "rows", BF16),
              ("sq", "lanes", F32), ("sk", "rows", F32), ("sk", "lanes", F32), ("sv", "rows", F32),
              ("sv", "lanes", F32), ("sg", "rows", BF16), ("mq", "rows", BF16), ("mg", "rows", BF16))
    p_keys = ("rq", "rk", "rv", "rg", "sq4", "sk", "sk4", "sv", "sv4", "sg", "mq", "mg")
    pr = dict(zip(p_keys, _project(xp, win_bf, cos_p, sin_p, 512, p_outs)))
    mk, mv = _memkv(mem_prompt.reshape(BATCH * N_MEM, D_MODEL), wmem_bf)
    slab = lambda a: a.reshape(SWA_W // LANES, BATCH, SEQ, LANES)
    swa_o = _swa_prompt(slab(pr["sq4"]), slab(pr["sk4"]), slab(pr["sv4"]))
    swa_o = swa_o.reshape(SWA_W // LANES, BATCH * SEQ, LANES)
    tabs_p = _retention_tables(RET_CHUNK, RET_CHUNK)
    yp, ret_p = _mix_prompt(xp, pr, swa_o, mk, mv, wout_bf, tabs_p, gain, bias, 512)

    xs = jnp.pad(x_sample, ((0, 0), (0, DEC_PAD - DEC_SEQ), (0, 0))).reshape(DEC_BATCH * DEC_PAD, D_MODEL)
    pos_s = PAST_LEN + jnp.arange(DEC_BATCH * DEC_PAD) % DEC_PAD
    cos_s, sin_s = _rope_tables(pos_s)
    names = [c[0] for c in _PROJ_COLS]
    ps = dict(zip(names, _project(xs, win_bf, cos_s, sin_s, DEC_BATCH * DEC_PAD,
                                  [(k, "rows", F32) for k in names])))
    wb = cache_swa_k.shape[2]
    tabs_s = _retention_tables(DEC_SEQ, DEC_PAD)
    mix_s, ret_s = _mix_sample(
        ps, state_ret[0].reshape(DEC_BATCH, RET_HEADS * RET_DK, RET_DV),
        cache_swa_k[0].reshape(DEC_BATCH, wb, SWA_W), cache_swa_v[0].reshape(DEC_BATCH, wb, SWA_W),
        cache_mem_k[0].reshape(DEC_BATCH, N_MEM, MEM_W), cache_mem_v[0].reshape(DEC_BATCH, N_MEM, MEM_W),
        tabs_s)
    ys = _finish(xs, mix_s, wout_bf, gain, bias)

    take = lambda a, w: a.reshape(DEC_BATCH, DEC_PAD, w)[:, :DEC_SEQ]
    return (
        yp.reshape(BATCH, SEQ, D_MODEL),
        take(ys, D_MODEL),
        ret_p.reshape(1, BATCH, RET_HEADS, RET_DK, RET_DV),
        ret_s.reshape(1, DEC_BATCH, RET_HEADS, RET_DK, RET_DV),
        pr["sk"].reshape(1, BATCH, SEQ, SWA_HEADS, SWA_HD),
        pr["sv"].reshape(1, BATCH, SEQ, SWA_HEADS, SWA_HD),
        take(ps["sk"], SWA_W).reshape(1, DEC_BATCH, DEC_SEQ, SWA_HEADS, SWA_HD),
        take(ps["sv"], SWA_W).reshape(1, DEC_BATCH, DEC_SEQ, SWA_HEADS, SWA_HD),
        mk.reshape(1, BATCH, N_MEM, MEM_HEADS, MEM_HD),
        mv.reshape(1, BATCH, N_MEM, MEM_HEADS, MEM_HD),
    )
```

```python
import functools

import jax
import jax.numpy as jnp
from jax import lax
from jax.experimental import pallas as pl
from jax.experimental.pallas import tpu as pltpu

F32 = jnp.float32
BF16 = jnp.bfloat16

D_MODEL = 1024
BATCH = 8
SEQ = 2048
DEC_BATCH = 32
DEC_SEQ = 4
PAST_LEN = 8192
N_MEM = 256
MEM_HEADS = 4
MEM_HD = 128
RET_HEADS = 4
RET_DK = 64
RET_DV = 128
RET_CHUNK = 128
SWA_HEADS = 8
SWA_HD = 64
SWA_DILATIONS = (1, 4, 16)
SWA_STEPS = 128
SWA_BLOCK = 128
ROPE_THETA = 10000.0
LN_EPS = 1e-5
GN_EPS = 1e-5
RET_W = RET_HEADS * RET_DV
SWA_W = SWA_HEADS * SWA_HD
MEM_W = MEM_HEADS * MEM_HD
D_MIX = RET_W + SWA_W + MEM_W
DEEPNORM_ALPHA = 2.0 ** 0.25
MEM_SCALE = MEM_HD ** -0.5
QK_SCALE = 0.125

LANES = 128
DEC_PAD = 16
VMEM_LIMIT = 56 * 1024 * 1024
NEG = -1e30

_PROJ_COLS = (
    ("rq", RET_HEADS * RET_DK, "rope", 1.0),
    ("rk", RET_HEADS * RET_DK, "rope", QK_SCALE),
    ("rv", RET_W, "id", 1.0),
    ("rg", RET_W, "silu", 1.0),
    ("sq", SWA_W, "rope", QK_SCALE),
    ("sk", SWA_W, "rope", 1.0),
    ("sv", SWA_W, "id", 1.0),
    ("sg", SWA_W, "silu", 1.0),
    ("mq", MEM_W, "id", 1.0),
    ("mg", MEM_W, "silu", 1.0),
)


def _dot(a, b):
    return jnp.dot(a, b, preferred_element_type=F32)


def _dot_nt(a, b):
    return lax.dot_general(a, b, (((1,), (1,)), ((), ())), preferred_element_type=F32)


def _dot_tn(a, b):
    return lax.dot_general(a, b, (((0,), (0,)), ((), ())), preferred_element_type=F32)


def _params(n_axes):
    return pltpu.CompilerParams(dimension_semantics=("arbitrary",) * n_axes,
                                vmem_limit_bytes=VMEM_LIMIT)


def _proj_kernel(x_ref, w_ref, cos_ref, sin_ref, *out_refs, dests):
    xb = x_ref[...].astype(BF16)
    cos = cos_ref[...]
    sin = sin_ref[...]
    lane = lax.broadcasted_iota(jnp.int32, cos.shape, 1)
    first_half = (lane % 64) < 32
    col = 0
    for name, width, kind, scale in _PROJ_COLS:
        targets = [(o_ref, layout) for o_ref, (dname, layout) in zip(out_refs, dests) if dname == name]
        for c in range(0, width, 2 * LANES):
            h2 = _dot(xb, w_ref[:, col + c:col + c + 2 * LANES])
            for half in range(2):
                h = h2[:, half * LANES:(half + 1) * LANES]
                if kind == "rope":
                    swapped = jnp.where(first_half, pltpu.roll(h, 96, 1), pltpu.roll(h, 32, 1))
                    h = h * cos + swapped * sin
                    if scale != 1.0:
                        h = h * scale
                elif kind == "silu":
                    h = h * (1.0 / (1.0 + jnp.exp(-h)))
                lo = c + half * LANES
                for o_ref, layout in targets:
                    if layout == "rows":
                        o_ref[:, lo:lo + LANES] = h.astype(o_ref.dtype)
                    elif layout == "lanes":
                        o_ref[lo // LANES] = h.astype(o_ref.dtype)
                    else:
                        o_ref[lo:lo + LANES, :] = h.T.astype(o_ref.dtype)
        col += width


def _project(x2d, w_bf, cos_t, sin_t, tile, outs):
    n = x2d.shape[0]
    seq = cos_t.shape[0]
    n_tab = seq // tile
    d_in = w_bf.shape[1]
    widths = {name: w for name, w, _, _ in _PROJ_COLS}
    out_shape, out_specs = [], []
    for name, layout, dt in outs:
        w = widths[name]
        if layout == "rows":
            out_shape.append(jax.ShapeDtypeStruct((n, w), dt))
            out_specs.append(pl.BlockSpec((tile, w), lambda i: (i, 0)))
        elif layout == "lanes":
            out_shape.append(jax.ShapeDtypeStruct((w // LANES, n, LANES), dt))
            out_specs.append(pl.BlockSpec((w // LANES, tile, LANES), lambda i: (0, i, 0)))
        else:
            out_shape.append(jax.ShapeDtypeStruct((n // seq, w, seq), dt))
            out_specs.append(pl.BlockSpec((None, w, tile), lambda i: (i // n_tab, 0, i % n_tab)))
    return pl.pallas_call(
        functools.partial(_proj_kernel, dests=tuple((name, layout) for name, layout, _ in outs)),
        grid=(n // tile,),
        in_specs=[
            pl.BlockSpec((tile, D_MODEL), lambda i: (i, 0)),
            pl.BlockSpec((D_MODEL, d_in), lambda i: (0, 0)),
            pl.BlockSpec((tile, LANES), lambda i: (i % n_tab, 0)),
            pl.BlockSpec((tile, LANES), lambda i: (i % n_tab, 0)),
        ],
        out_specs=out_specs,
        out_shape=out_shape,
        compiler_params=_params(1),
        name="proj",
    )(x2d, w_bf, cos_t, sin_t)


def _memkv_kernel(m_ref, w_ref, mk_ref, mv_ref):
    mb = m_ref[...].astype(BF16)
    mk_ref[...] = _dot(mb, w_ref[:, :MEM_W])
    mv_ref[...] = _dot(mb, w_ref[:, MEM_W:])


def _memkv(mem2d, w_bf):
    n = mem2d.shape[0]
    tile = 512
    return pl.pallas_call(
        _memkv_kernel,
        grid=(n // tile,),
        in_specs=[pl.BlockSpec((tile, D_MODEL), lambda i: (i, 0)),
                  pl.BlockSpec((D_MODEL, 2 * MEM_W), lambda i: (0, 0))],
        out_specs=[pl.BlockSpec((tile, MEM_W), lambda i: (i, 0))] * 2,
        out_shape=[jax.ShapeDtypeStruct((n, MEM_W), F32)] * 2,
        compiler_params=_params(1),
        name="memkv",
    )(mem2d, w_bf)


def _swa_prompt_kernel(q_ref, k_ref, v_ref, o_ref, m_ref, l_ref):
    blk = SWA_BLOCK
    row2 = lax.broadcasted_iota(jnp.int32, (blk, 2 * blk), 0)
    col2 = lax.broadcasted_iota(jnp.int32, (blk, 2 * blk), 1)
    rel2 = row2 + blk - col2
    bias_prev = jnp.where((rel2 >= 0) & (rel2 <= SWA_STEPS), 0.0, NEG).astype(F32)
    row1 = lax.broadcasted_iota(jnp.int32, (blk, blk), 0)
    col1 = lax.broadcasted_iota(jnp.int32, (blk, blk), 1)
    bias_own = jnp.where(row1 >= col1, 0.0, NEG).astype(F32)
    lo = col1 < SWA_HD

    def block(start, d, has_prev, first_pattern, last_pattern):
        def rows(s, n):
            return pl.ds(s, n) if d == 1 else pl.ds(s, n, stride=d)

        rq = rows(start, blk)
        rk = rows(start - blk * d, 2 * blk) if has_prev else rq
        bias = bias_prev if has_prev else bias_own
        for pair in range(SWA_HEADS // 2):
            q = q_ref[pair, rq, :]
            kb = k_ref[pair, rk, :].astype(BF16)
            vb = v_ref[pair, rk, :].astype(BF16)
            stats = []
            for hh in range(2):
                qm = jnp.where(lo if hh == 0 else ~lo, q, 0.0).astype(BF16)
                s = _dot_nt(qm, kb) + bias
                m = jnp.max(s, axis=-1, keepdims=True)
                p = jnp.exp(s - m)
                l = jnp.sum(p, axis=-1, keepdims=True)
                stats.append((m, l, _dot(p.astype(BF16), vb)))
            m_g = jnp.where(lo, stats[0][0], stats[1][0])
            l_g = jnp.where(lo, stats[0][1], stats[1][1])
            a_g = jnp.where(lo, stats[0][2], stats[1][2])
            if first_pattern:
                m_new, l_new, a_new = m_g, l_g, a_g
            else:
                m_old = m_ref[pair, rq, :]
                m_new = jnp.maximum(m_old, m_g)
                w_old = jnp.exp(m_old - m_new)
                w_g = jnp.exp(m_g - m_new)
                l_new = w_old * l_ref[pair, rq, :] + w_g * l_g
                a_new = w_old * o_ref[pair, rq, :] + w_g * a_g
            if last_pattern:
                o_ref[pair, rq, :] = a_new / l_new
            else:
                m_ref[pair, rq, :] = m_new
                l_ref[pair, rq, :] = l_new
                o_ref[pair, rq, :] = a_new

    n_pat = len(SWA_DILATIONS)
    for pi, d in enumerate(SWA_DILATIONS):
        nblk = SEQ // d // blk
        first_p, last_p = pi == 0, pi == n_pat - 1

        def per_class(r, carry, d=d, nblk=nblk, first_p=first_p, last_p=last_p):
            block(r, d, False, first_p, last_p)
            if nblk > 1:
                def per_block(j, c2):
                    start = r + j * (blk * d)
                    if d == 1:
                        start = pl.multiple_of(start, blk)
                    block(start, d, True, first_p, last_p)
                    return c2
                lax.fori_loop(1, nblk, per_block, 0)
            return carry

        if d == 1:
            per_class(0, 0)
        else:
            lax.fori_loop(0, d, per_class, 0)


def _swa_prompt(sq, sk, sv):
    slabs = SWA_W // LANES
    spec = pl.BlockSpec((slabs, None, SEQ, LANES), lambda b: (0, b, 0, 0))
    return pl.pallas_call(
        _swa_prompt_kernel,
        grid=(BATCH,),
        in_specs=[spec, spec, spec],
        out_specs=spec,
        out_shape=jax.ShapeDtypeStruct((slabs, BATCH, SEQ, LANES), F32),
        scratch_shapes=[pltpu.VMEM((slabs, SEQ, LANES), F32), pltpu.VMEM((slabs, SEQ, LANES), F32)],
        compiler_params=_params(1),
        name="swa_prompt",
    )(sq, sk, sv)


def _head_norm(o):
    mu = jnp.mean(o, axis=-1, keepdims=True)
    d = o - mu
    var = jnp.mean(d * d, axis=-1, keepdims=True)
    return d * lax.rsqrt(var + GN_EPS)


def _deepnorm_ln(x, h, gain, bias):
    z = DEEPNORM_ALPHA * x + h
    mu = jnp.mean(z, axis=-1, keepdims=True)
    d = z - mu
    var = jnp.mean(d * d, axis=-1, keepdims=True)
    return d * lax.rsqrt(var + LN_EPS) * gain + bias


def _softmax_rows(s):
    m = jnp.max(s, axis=-1, keepdims=True)
    p = jnp.exp(s - m)
    return p * (1.0 / jnp.sum(p, axis=-1, keepdims=True))


def _mix_prompt_kernel(x_ref, rq_ref, rk_ref, rv_ref, rg_ref, so_ref, sg_ref, mq_ref, mg_ref,
                       mk_ref, mv_ref, wout_ref, dmat_ref, kdec_ref, qdec_ref, gdec_ref,
                       gain_ref, bias_ref, y_ref, state_out_ref,
                       state_ref, mix_ref, mkb_ref, mvb_ref, *, tile):
    t = pl.program_id(1)

    @pl.when(t == 0)
    def _():
        state_ref[...] = jnp.zeros_like(state_ref)
        mkb_ref[...] = mk_ref[...].astype(BF16)
        mvb_ref[...] = mv_ref[...].astype(BF16)

    ck = RET_CHUNK
    lane = lax.broadcasted_iota(jnp.int32, (ck, LANES), 1)
    lo = lane < RET_DK
    top = lax.broadcasted_iota(jnp.int32, (LANES, LANES), 0) < RET_DK
    gain = gain_ref[...]
    bias = bias_ref[...]

    def chunk(c, carry):
        rows = pl.ds(pl.multiple_of(c * ck, ck), ck)
        for pair in range(RET_HEADS // 2):
            cs = pl.ds(pair * LANES, LANES)
            q = rq_ref[rows, cs]
            k = rk_ref[rows, cs]
            kb = k.astype(BF16)
            kd = (k * kdec_ref[:, cs]).astype(BF16)
            qd = q * qdec_ref[:, cs]
            st = state_ref[pair]
            stb = st.astype(BF16)
            kv = []
            for hh in range(2):
                h = 2 * pair + hh
                hs = pl.ds(h * RET_DV, RET_DV)
                msk = lo if hh == 0 else ~lo
                v = rv_ref[rows, hs]
                s = _dot_nt(jnp.where(msk, q, 0.0).astype(BF16), kb) * dmat_ref[h]
                o = _dot(s.astype(BF16), v) + _dot(jnp.where(msk, qd, 0.0).astype(BF16), stb)
                kv.append(_dot_tn(kd, v))
                mix_ref[:, hs] = (rg_ref[rows, hs].astype(F32) * _head_norm(o)).astype(BF16)
            state_ref[pair] = gdec_ref[pl.ds(pair * LANES, LANES), :] * st + jnp.where(top, kv[0], kv[1])
        for pair in range(SWA_W // LANES):
            cs = pl.ds(pair * LANES, LANES)
            mix_ref[:, pl.ds(RET_W + pair * LANES, LANES)] = (
                sg_ref[rows, cs].astype(F32) * so_ref[pair, rows, :]).astype(BF16)
        for h in range(MEM_HEADS):
            hs = pl.ds(h * MEM_HD, MEM_HD)
            p = _softmax_rows(_dot_nt(mq_ref[rows, hs], mkb_ref[:, hs]) * MEM_SCALE)
            o = _dot(p.astype(BF16), mvb_ref[:, hs])
            mix_ref[:, pl.ds(RET_W + SWA_W + h * MEM_HD, MEM_HD)] = (
                mg_ref[rows, hs].astype(F32) * o).astype(BF16)
        hout = _dot(mix_ref[...], wout_ref[...])
        y_ref[rows, :] = _deepnorm_ln(x_ref[rows, :], hout, gain, bias)
        return carry

    lax.fori_loop(0, tile // ck, chunk, 0)

    @pl.when(t == pl.num_programs(1) - 1)
    def _():
        state_out_ref[pl.ds(0, LANES), :] = state_ref[0]
        state_out_ref[pl.ds(LANES, LANES), :] = state_ref[1]


def _mix_prompt(x2d, pr, swa_o, mk, mv, wout_bf, tabs, gain, bias, tile):
    nt = SEQ // tile
    row = lambda w: pl.BlockSpec((tile, w), lambda b, t: (b * nt + t, 0))
    const2 = lambda a: pl.BlockSpec(a.shape, lambda b, t: (0, 0))
    dmat, kdec, qdec, gdec = tabs
    return pl.pallas_call(
        functools.partial(_mix_prompt_kernel, tile=tile),
        grid=(BATCH, nt),
        in_specs=[
            row(D_MODEL), row(RET_HEADS * RET_DK), row(RET_HEADS * RET_DK), row(RET_W), row(RET_W),
            pl.BlockSpec((SWA_W // LANES, tile, LANES), lambda b, t: (0, b * nt + t, 0)),
            row(SWA_W), row(MEM_W), row(MEM_W),
            pl.BlockSpec((N_MEM, MEM_W), lambda b, t: (b, 0)),
            pl.BlockSpec((N_MEM, MEM_W), lambda b, t: (b, 0)),
            const2(wout_bf),
            pl.BlockSpec(dmat.shape, lambda b, t: (0, 0, 0)),
            const2(kdec), const2(qdec), const2(gdec), const2(gain), const2(bias),
        ],
        out_specs=[row(D_MODEL),
                   pl.BlockSpec((None, RET_HEADS * RET_DK, RET_DV), lambda b, t: (b, 0, 0))],
        out_shape=[jax.ShapeDtypeStruct((BATCH * SEQ, D_MODEL), F32),
                   jax.ShapeDtypeStruct((BATCH, RET_HEADS * RET_DK, RET_DV), F32)],
        scratch_shapes=[pltpu.VMEM((RET_HEADS // 2, LANES, RET_DV), F32),
                        pltpu.VMEM((RET_CHUNK, D_MIX), BF16),
                        pltpu.VMEM((N_MEM, MEM_W), BF16),
                        pltpu.VMEM((N_MEM, MEM_W), BF16)],
        compiler_params=_params(2),
        name="mix_prompt",
    )(x2d, pr["rq"], pr["rk"], pr["rv"], pr["rg"], swa_o, pr["sg"], pr["mq"], pr["mg"],
      mk, mv, wout_bf, dmat, kdec, qdec, gdec, gain, bias)


def _mix_sample_kernel(rq_ref, rk_ref, rv_ref, rg_ref, sq_ref, sk_ref, sv_ref, sg_ref, mq_ref, mg_ref,
                       state_ref, ckt_ref, cvt_ref, cmk_ref, cmv_ref,
                       dmat_ref, kdec_ref, qdec_ref, gdec_ref,
                       mix_ref, state_out_ref):
    n = DEC_PAD
    q = rq_ref[...]
    k = rk_ref[...]
    kb = k.astype(BF16)
    kd = (k * kdec_ref[...]).astype(BF16)
    qd = q * qdec_ref[...]
    st = state_ref[...]
    stb = st.astype(BF16)
    vb = rv_ref[...].astype(BF16)
    kv = _dot_tn(kd, vb)
    lane_qk = lax.broadcasted_iota(jnp.int32, q.shape, 1) // RET_DK
    for h in range(RET_HEADS):
        hs = pl.ds(h * RET_DV, RET_DV)
        ks = pl.ds(h * RET_DK, RET_DK)
        msk = lane_qk == h
        s = _dot_nt(jnp.where(msk, q, 0.0).astype(BF16), kb) * dmat_ref[h]
        o = _dot(s.astype(BF16), vb[:, h * RET_DV:(h + 1) * RET_DV])
        o = o + _dot(jnp.where(msk, qd, 0.0).astype(BF16), stb)
        mix_ref[:, hs] = rg_ref[:, hs] * _head_norm(o)
        state_out_ref[ks, :] = (gdec_ref[ks, :] * st[h * RET_DK:(h + 1) * RET_DK, :]
                                + kv[h * RET_DK:(h + 1) * RET_DK, h * RET_DV:(h + 1) * RET_DV])

    blk = SWA_BLOCK
    wb = ckt_ref.shape[2]
    rt = 8
    lo = lax.broadcasted_iota(jnp.int32, (n, LANES), 1) < SWA_HD
    lo_t = lax.broadcasted_iota(jnp.int32, (rt, LANES), 1) < SWA_HD
    tok = lambda w: lax.broadcasted_iota(jnp.int32, (rt, w), 0)
    col = lambda w: lax.broadcasted_iota(jnp.int32, (rt, w), 1)
    windows = (
        (wb - blk, jnp.where(col(blk) >= tok(blk), 0.0, NEG), jnp.where(col(blk) <= tok(blk), 0.0, NEG)),
        (wb - 4 * blk, jnp.where(col(4 * blk) % 4 == tok(4 * blk), 0.0, NEG),
         jnp.where(col(blk) == tok(blk), 0.0, NEG)),
        (0, jnp.where(col(wb) % 16 == tok(wb), 0.0, NEG), jnp.where(col(blk) == tok(blk), 0.0, NEG)),
    )
    pad = jnp.zeros((blk - n, LANES), BF16)
    zero_c = jnp.zeros((rt, wb), F32)
    zero_n = jnp.zeros((rt, blk), F32)
    pieces = []
    for pair in range(SWA_HEADS // 2):
        cs = pl.ds(pair * LANES, LANES)
        kp = ckt_ref[2 * pair:2 * pair + 2].reshape(2 * SWA_HD, wb).astype(BF16)
        vp = cvt_ref[2 * pair:2 * pair + 2].reshape(2 * SWA_HD, wb).astype(BF16)
        k_new = jnp.concatenate([sk_ref[:, cs].astype(BF16), pad], axis=0)
        v_new = jnp.concatenate([sv_ref[:, cs].astype(BF16), pad], axis=0)
        q = sq_ref[:, cs]
        qs = jnp.concatenate([jnp.where(lo, q, 0.0), jnp.where(lo, 0.0, q)], axis=0).astype(BF16)
        s_all = _dot(qs, kp)
        sn_all = _dot_nt(qs, k_new)
        p_rows, pn_rows, stats = [], [], []
        for hh in range(2):
            s = s_all[hh * n:hh * n + rt]
            sn = sn_all[hh * n:hh * n + rt]
            for w0, bias_c, bias_n in windows:
                sc = s[:, w0:] + bias_c
                snb = sn + bias_n
                m = jnp.maximum(jnp.max(sc, axis=-1, keepdims=True), jnp.max(snb, axis=-1, keepdims=True))
                pc = jnp.exp(sc - m)
                pn = jnp.exp(snb - m)
                stats.append((m, jnp.sum(pc, axis=-1, keepdims=True) + jnp.sum(pn, axis=-1, keepdims=True)))
                if w0:
                    pc = jnp.concatenate([jnp.zeros((rt, w0), F32), pc], axis=1)
                p_rows.append(pc)
                pn_rows.append(pn)
            p_rows.append(zero_c)
            pn_rows.append(zero_n)
        pv = (_dot_nt(jnp.concatenate(p_rows, axis=0).astype(BF16), vp)
              + _dot(jnp.concatenate(pn_rows, axis=0).astype(BF16), v_new))
        heads = []
        for hh in range(2):
            parts = []
            for i in range(len(windows)):
                m, l = stats[hh * len(windows) + i]
                r0 = (hh * (len(windows) + 1) + i) * rt
                parts.append((pv[r0:r0 + rt] / l, m, l))
            m_all = jnp.maximum(jnp.maximum(parts[0][1], parts[1][1]), parts[2][1])
            ws = [l * jnp.exp(m - m_all) for (_, m, l) in parts]
            heads.append((ws[0] * parts[0][0] + ws[1] * parts[1][0] + ws[2] * parts[2][0])
                         / (ws[0] + ws[1] + ws[2]))
        pieces.append(jnp.where(lo_t, heads[0], heads[1]))
    swa = jnp.concatenate(pieces, axis=1)
    swa = jnp.concatenate([swa, jnp.zeros((n - rt, SWA_W), F32)], axis=0)
    mix_ref[:, pl.ds(RET_W, SWA_W)] = sg_ref[...] * swa

    mq = mq_ref[...]
    lane_m = lax.broadcasted_iota(jnp.int32, mq.shape, 1) // MEM_HD
    qe = jnp.concatenate([jnp.where(lane_m == h, mq, 0.0) for h in range(MEM_HEADS)], axis=0).astype(BF16)
    p = _softmax_rows(_dot_nt(qe, cmk_ref[...].astype(BF16)) * MEM_SCALE)
    o = _dot(p.astype(BF16), cmv_ref[...].astype(BF16))
    for h in range(MEM_HEADS):
        hs = pl.ds(h * MEM_HD, MEM_HD)
        mix_ref[:, pl.ds(RET_W + SWA_W + h * MEM_HD, MEM_HD)] = (
            mg_ref[:, hs] * o[h * n:(h + 1) * n, h * MEM_HD:(h + 1) * MEM_HD])


def _mix_sample(ps, state, ckt, cvt, cmk, cmv, tabs):
    n = DEC_PAD
    wb = ckt.shape[3]
    row = lambda w: pl.BlockSpec((n, w), lambda b: (b, 0))
    st_spec = pl.BlockSpec((None, RET_HEADS * RET_DK, RET_DV), lambda b: (b, 0, 0))
    cache_spec = pl.BlockSpec((None, SWA_HEADS, SWA_HD, wb), lambda b: (b, 0, 0, 0))
    mem_spec = pl.BlockSpec((None, N_MEM, MEM_W), lambda b: (b, 0, 0))
    dmat, kdec, qdec, gdec = tabs
    return pl.pallas_call(
        _mix_sample_kernel,
        grid=(DEC_BATCH,),
        in_specs=[
            row(RET_HEADS * RET_DK), row(RET_HEADS * RET_DK), row(RET_W), row(RET_W),
            row(SWA_W), row(SWA_W), row(SWA_W), row(SWA_W), row(MEM_W), row(MEM_W),
            st_spec, cache_spec, cache_spec, mem_spec, mem_spec,
            pl.BlockSpec(dmat.shape, lambda b: (0, 0, 0)),
            pl.BlockSpec(kdec.shape, lambda b: (0, 0)),
            pl.BlockSpec(qdec.shape, lambda b: (0, 0)),
            pl.BlockSpec(gdec.shape, lambda b: (0, 0)),
        ],
        out_specs=[row(D_MIX), st_spec],
        out_shape=[jax.ShapeDtypeStruct((DEC_BATCH * n, D_MIX), F32),
                   jax.ShapeDtypeStruct((DEC_BATCH, RET_HEADS * RET_DK, RET_DV), F32)],
        compiler_params=_params(1),
        name="mix_sample",
    )(ps["rq"], ps["rk"], ps["rv"], ps["rg"], ps["sq"], ps["sk"], ps["sv"], ps["sg"], ps["mq"], ps["mg"],
      state, ckt, cvt, cmk, cmv, dmat, kdec, qdec, gdec)


def _finish_kernel(x_ref, mix_ref, wout_ref, gain_ref, bias_ref, y_ref):
    hout = _dot(mix_ref[...].astype(BF16), wout_ref[...])
    y_ref[...] = _deepnorm_ln(x_ref[...], hout, gain_ref[...], bias_ref[...])


def _finish(x2d, mix, wout_bf, gain, bias):
    n = x2d.shape[0]
    tile = 256
    const2 = lambda a: pl.BlockSpec(a.shape, lambda i: (0, 0))
    return pl.pallas_call(
        _finish_kernel,
        grid=(n // tile,),
        in_specs=[pl.BlockSpec((tile, D_MODEL), lambda i: (i, 0)),
                  pl.BlockSpec((tile, D_MIX), lambda i: (i, 0)),
                  const2(wout_bf), const2(gain), const2(bias)],
        out_specs=pl.BlockSpec((tile, D_MODEL), lambda i: (i, 0)),
        out_shape=jax.ShapeDtypeStruct((n, D_MODEL), F32),
        compiler_params=_params(1),
        name="finish",
    )(x2d, mix, wout_bf, gain, bias)


def _rope_tables(pos):
    half = SWA_HD // 2
    inv = ROPE_THETA ** (-jnp.arange(half, dtype=F32) * 2.0 / SWA_HD)
    ang = pos.astype(F32)[:, None] * inv[None, :]
    cos = jnp.cos(ang)
    sin = jnp.sin(ang)
    reps = LANES // SWA_HD
    return (jnp.tile(jnp.concatenate([cos, cos], axis=1), (1, reps)),
            jnp.tile(jnp.concatenate([-sin, sin], axis=1), (1, reps)))


def _retention_tables(chunk, rows):
    lg = jnp.log1p(-jnp.exp2(-5.0 - jnp.arange(RET_HEADS, dtype=F32)))
    idx = jnp.arange(rows, dtype=F32)
    live = idx < chunk
    rel = idx[:, None] - idx[None, :]
    ok = (rel >= 0) & live[:, None] & live[None, :]
    dmat = jnp.where(ok[None], jnp.exp(jnp.maximum(rel, 0.0)[None] * lg[:, None, None]), 0.0)
    kdec = jnp.where(live[:, None], jnp.exp((chunk - 1.0 - idx)[:, None] * lg[None, :]), 0.0)
    qdec = jnp.where(live[:, None], jnp.exp((idx + 1.0)[:, None] * lg[None, :]), 0.0)
    g = jnp.exp(chunk * lg)
    kdec = jnp.repeat(kdec, RET_DK, axis=1)
    qdec = jnp.repeat(qdec, RET_DK, axis=1)
    gdec = jnp.broadcast_to(jnp.repeat(g, RET_DK)[:, None], (RET_HEADS * RET_DK, RET_DV))
    return dmat, kdec, qdec, gdec


def kernel(x_prompt, x_sample, state_ret, cache_swa_k, cache_swa_v, cache_mem_k, cache_mem_v,
           mem_prompt, w_in, w_mem_kv, w_out, ln_gain, ln_bias):
    depth = w_in.shape[0]
    assert depth == 1
    win_bf = w_in[0].astype(BF16)
    wmem_bf = w_mem_kv[0].astype(BF16)
    wout_bf = w_out[0].astype(BF16)
    gain = ln_gain[0].reshape(1, D_MODEL)
    bias = ln_bias[0].reshape(1, D_MODEL)

    xp = x_prompt.reshape(BATCH * SEQ, D_MODEL)
    cos_p, sin_p = _rope_tables(jnp.arange(SEQ))
    p_outs = (("rq", "rows", F32), ("rk", "rows", F32), ("rv", "rows", BF16), ("rg", "rows", BF16),
              ("sq", "lanes", F32), ("sk", "cols", F32), ("sk", "lanes", F32), ("sv", "cols", F32),
              ("sv", "lanes", F32), ("sg", "rows", BF16), ("mq", "rows", BF16), ("mg", "rows", BF16))
    p_keys = ("rq", "rk", "rv", "rg", "sq4", "sk", "sk4", "sv", "sv4", "sg", "mq", "mg")
    pr = dict(zip(p_keys, _project(xp, win_bf, cos_p, sin_p, 512, p_outs)))
    mk, mv = _memkv(mem_prompt.reshape(BATCH * N_MEM, D_MODEL), wmem_bf)
    slab = lambda a: a.reshape(SWA_W // LANES, BATCH, SEQ, LANES)
    swa_o = _swa_prompt(slab(pr["sq4"]), slab(pr["sk4"]), slab(pr["sv4"]))
    swa_o = swa_o.reshape(SWA_W // LANES, BATCH * SEQ, LANES)
    tabs_p = _retention_tables(RET_CHUNK, RET_CHUNK)
    yp, ret_p = _mix_prompt(xp, pr, swa_o, mk, mv, wout_bf, tabs_p, gain, bias, 512)

    xs = jnp.pad(x_sample, ((0, 0), (0, DEC_PAD - DEC_SEQ), (0, 0))).reshape(DEC_BATCH * DEC_PAD, D_MODEL)
    pos_s = PAST_LEN + jnp.arange(DEC_BATCH * DEC_PAD) % DEC_PAD
    cos_s, sin_s = _rope_tables(pos_s)
    names = [c[0] for c in _PROJ_COLS]
    ps = dict(zip(names, _project(xs, win_bf, cos_s, sin_s, DEC_BATCH * DEC_PAD,
                                  [(k, "rows", F32) for k in names])))
    tabs_s = _retention_tables(DEC_SEQ, DEC_PAD)
    mix_s, ret_s = _mix_sample(
        ps, state_ret[0].reshape(DEC_BATCH, RET_HEADS * RET_DK, RET_DV),
        cache_swa_k[0].transpose(0, 2, 3, 1), cache_swa_v[0].transpose(0, 2, 3, 1),
        cache_mem_k[0].reshape(DEC_BATCH, N_MEM, MEM_W), cache_mem_v[0].reshape(DEC_BATCH, N_MEM, MEM_W),
        tabs_s)
    ys = _finish(xs, mix_s, wout_bf, gain, bias)

    take = lambda a, w: a.reshape(DEC_BATCH, DEC_PAD, w)[:, :DEC_SEQ]
    swa_rows = lambda a: a.reshape(BATCH, SWA_HEADS, SWA_HD, SEQ).transpose(0, 3, 1, 2)[None]
    return (
        yp.reshape(BATCH, SEQ, D_MODEL),
        take(ys, D_MODEL),
        ret_p.reshape(1, BATCH, RET_HEADS, RET_DK, RET_DV),
        ret_s.reshape(1, DEC_BATCH, RET_HEADS, RET_DK, RET_DV),
        swa_rows(pr["sk"]),
        swa_rows(pr["sv"]),
        take(ps["sk"], SWA_W).reshape(1, DEC_BATCH, DEC_SEQ, SWA_HEADS, SWA_HD),
        take(ps["sv"], SWA_W).reshape(1, DEC_BATCH, DEC_SEQ, SWA_HEADS, SWA_HD),
        mk.reshape(1, BATCH, N_MEM, MEM_HEADS, MEM_HD),
        mv.reshape(1, BATCH, N_MEM, MEM_HEADS, MEM_HD),
    )
```

```python
import functools

import jax
import jax.numpy as jnp
from jax import lax
from jax.experimental import pallas as pl
from jax.experimental.pallas import tpu as pltpu

F32 = jnp.float32
BF16 = jnp.bfloat16

D_MODEL = 1024
BATCH = 8
SEQ = 2048
DEC_BATCH = 32
DEC_SEQ = 4
PAST_LEN = 8192
N_MEM = 256
MEM_HEADS = 4
MEM_HD = 128
RET_HEADS = 4
RET_DK = 64
RET_DV = 128
RET_CHUNK = 128
SWA_HEADS = 8
SWA_HD = 64
SWA_DILATIONS = (1, 4, 16)
SWA_STEPS = 128
SWA_BLOCK = 128
ROPE_THETA = 10000.0
LN_EPS = 1e-5
GN_EPS = 1e-5
RET_W = RET_HEADS * RET_DV
SWA_W = SWA_HEADS * SWA_HD
MEM_W = MEM_HEADS * MEM_HD
D_MIX = RET_W + SWA_W + MEM_W
DEEPNORM_ALPHA = 2.0 ** 0.25
MEM_SCALE = MEM_HD ** -0.5
QK_SCALE = 0.125

LANES = 128
DEC_PAD = 16
PERM_TILE = 512
VMEM_LIMIT = 56 * 1024 * 1024
NEG = -1e30

_PROJ_COLS = (
    ("rq", RET_HEADS * RET_DK, "rope", 1.0),
    ("rk", RET_HEADS * RET_DK, "rope", QK_SCALE),
    ("rv", RET_W, "id", 1.0),
    ("rg", RET_W, "silu", 1.0),
    ("sq", SWA_W, "rope", QK_SCALE),
    ("sk", SWA_W, "rope", 1.0),
    ("sv", SWA_W, "id", 1.0),
    ("sg", SWA_W, "silu", 1.0),
    ("mq", MEM_W, "id", 1.0),
    ("mg", MEM_W, "silu", 1.0),
)


def _dot(a, b):
    return jnp.dot(a, b, preferred_element_type=F32)


def _dot_nt(a, b):
    return lax.dot_general(a, b, (((1,), (1,)), ((), ())), preferred_element_type=F32)


def _dot_tn(a, b):
    return lax.dot_general(a, b, (((0,), (0,)), ((), ())), preferred_element_type=F32)


def _params(n_axes):
    return pltpu.CompilerParams(dimension_semantics=("arbitrary",) * n_axes,
                                vmem_limit_bytes=VMEM_LIMIT)


def _proj_kernel(x_ref, w_ref, cos_ref, sin_ref, *out_refs, dests):
    xb = x_ref[...].astype(BF16)
    cos = cos_ref[...]
    sin = sin_ref[...]
    lane = lax.broadcasted_iota(jnp.int32, cos.shape, 1)
    first_half = (lane % 64) < 32
    col = 0
    for name, width, kind, scale in _PROJ_COLS:
        targets = [(o_ref, layout) for o_ref, (dname, layout) in zip(out_refs, dests) if dname == name]
        for c in range(0, width, 2 * LANES):
            h2 = _dot(xb, w_ref[:, col + c:col + c + 2 * LANES])
            for half in range(2):
                h = h2[:, half * LANES:(half + 1) * LANES]
                if kind == "rope":
                    swapped = jnp.where(first_half, pltpu.roll(h, 96, 1), pltpu.roll(h, 32, 1))
                    h = h * cos + swapped * sin
                    if scale != 1.0:
                        h = h * scale
                elif kind == "silu":
                    h = h * (1.0 / (1.0 + jnp.exp(-h)))
                lo = c + half * LANES
                for o_ref, layout in targets:
                    if layout == "rows":
                        o_ref[:, lo:lo + LANES] = h.astype(o_ref.dtype)
                    elif layout == "perm":
                        tile = h.shape[0]
                        for g in range(tile // 8):
                            base = (g % 2) * (tile // 2) + g // 2
                            o_ref[lo // LANES, pl.ds(base, 8, stride=tile // 16), :] = (
                                h[8 * g:8 * g + 8].astype(o_ref.dtype))
                    else:
                        o_ref[lo:lo + LANES, :] = h.T.astype(o_ref.dtype)
        col += width


def _project(x2d, w_bf, cos_t, sin_t, tile, outs):
    n = x2d.shape[0]
    seq = cos_t.shape[0]
    n_tab = seq // tile
    d_in = w_bf.shape[1]
    widths = {name: w for name, w, _, _ in _PROJ_COLS}
    out_shape, out_specs = [], []
    for name, layout, dt in outs:
        w = widths[name]
        if layout == "rows":
            out_shape.append(jax.ShapeDtypeStruct((n, w), dt))
            out_specs.append(pl.BlockSpec((tile, w), lambda i: (i, 0)))
        elif layout == "perm":
            assert tile == PERM_TILE
            out_shape.append(jax.ShapeDtypeStruct((w // LANES, n, LANES), dt))
            out_specs.append(pl.BlockSpec((w // LANES, tile, LANES), lambda i: (0, i, 0)))
        else:
            out_shape.append(jax.ShapeDtypeStruct((n // seq, w, seq), dt))
            out_specs.append(pl.BlockSpec((None, w, tile), lambda i: (i // n_tab, 0, i % n_tab)))
    return pl.pallas_call(
        functools.partial(_proj_kernel, dests=tuple((name, layout) for name, layout, _ in outs)),
        grid=(n // tile,),
        in_specs=[
            pl.BlockSpec((tile, D_MODEL), lambda i: (i, 0)),
            pl.BlockSpec((D_MODEL, d_in), lambda i: (0, 0)),
            pl.BlockSpec((tile, LANES), lambda i: (i % n_tab, 0)),
            pl.BlockSpec((tile, LANES), lambda i: (i % n_tab, 0)),
        ],
        out_specs=out_specs,
        out_shape=out_shape,
        compiler_params=_params(1),
        name="proj",
    )(x2d, w_bf, cos_t, sin_t)


def _memkv_kernel(m_ref, w_ref, mk_ref, mv_ref):
    mb = m_ref[...].astype(BF16)
    mk_ref[...] = _dot(mb, w_ref[:, :MEM_W])
    mv_ref[...] = _dot(mb, w_ref[:, MEM_W:])


def _memkv(mem2d, w_bf):
    n = mem2d.shape[0]
    tile = 512
    return pl.pallas_call(
        _memkv_kernel,
        grid=(n // tile,),
        in_specs=[pl.BlockSpec((tile, D_MODEL), lambda i: (i, 0)),
                  pl.BlockSpec((D_MODEL, 2 * MEM_W), lambda i: (0, 0))],
        out_specs=[pl.BlockSpec((tile, MEM_W), lambda i: (i, 0))] * 2,
        out_shape=[jax.ShapeDtypeStruct((n, MEM_W), F32)] * 2,
        compiler_params=_params(1),
        name="memkv",
    )(mem2d, w_bf)


def _swa_prompt_kernel(q_ref, k_ref, v_ref, o_ref, a_ref, m_ref, l_ref):
    blk = SWA_BLOCK
    pt = PERM_TILE
    grp = pt // 16
    lo = lax.broadcasted_iota(jnp.int32, (blk, blk), 1) < SWA_HD

    def biases(seq_of):
        rq = seq_of(lax.broadcasted_iota(jnp.int32, (blk, 2 * blk), 0))
        c2 = lax.broadcasted_iota(jnp.int32, (blk, 2 * blk), 1)
        rel = blk + rq - (seq_of(c2 % blk) + blk * (c2 // blk))
        prev = jnp.where((rel >= 0) & (rel <= SWA_STEPS), 0.0, NEG).astype(F32)
        own = jnp.where((rel >= 0) & (rel <= SWA_STEPS) & (c2 >= blk), 0.0, NEG).astype(F32)
        return prev, own

    def gather(ref, pair, starts, n):
        return jnp.concatenate([ref[pair, pl.ds(s, n), :] for s in starts], axis=0)

    def block(starts, prev_starts, n, bias, first_pattern, last_pattern, out_start=None):
        n_pairs = SWA_HEADS // 2
        k_starts = (starts if prev_starts is None else prev_starts) + starts
        olds = None
        if not first_pattern:
            olds = [(gather(m_ref, pair, starts, n), gather(l_ref, pair, starts, n),
                     gather(a_ref, pair, starts, n)) for pair in range(n_pairs)]
        news = []
        for pair in range(n_pairs):
            q = gather(q_ref, pair, starts, n)
            kb = gather(k_ref, pair, k_starts, n).astype(BF16)
            vb = gather(v_ref, pair, k_starts, n).astype(BF16)
            stats = []
            for hh in range(2):
                qm = jnp.where(lo if hh == 0 else ~lo, q, 0.0).astype(BF16)
                s = _dot_nt(qm, kb) + bias
                m = jnp.max(s, axis=-1, keepdims=True)
                p = jnp.exp(s - m)
                l = jnp.sum(p, axis=-1, keepdims=True)
                stats.append((m, l, _dot(p.astype(BF16), vb)))
            m_g = jnp.where(lo, stats[0][0], stats[1][0])
            l_g = jnp.where(lo, stats[0][1], stats[1][1])
            a_g = jnp.where(lo, stats[0][2], stats[1][2])
            if first_pattern:
                m_new, l_new, a_new = m_g, l_g, a_g
            else:
                m_old, l_old, a_old = olds[pair]
                m_new = jnp.maximum(m_old, m_g)
                w_old = jnp.exp(m_old - m_new)
                w_g = jnp.exp(m_g - m_new)
                l_new = w_old * l_old + w_g * l_g
                a_new = w_old * a_old + w_g * a_g
            news.append((m_new, l_new, a_new))
        for pair, (m_new, l_new, a_new) in enumerate(news):
            if last_pattern:
                o = a_new / l_new
                for i in range(len(starts)):
                    o_ref[pair, pl.ds(out_start + i, n, stride=len(starts)), :] = o[i * n:(i + 1) * n]
            else:
                for i, s in enumerate(starts):
                    m_ref[pair, pl.ds(s, n), :] = m_new[i * n:(i + 1) * n]
                    l_ref[pair, pl.ds(s, n), :] = l_new[i * n:(i + 1) * n]
                    a_ref[pair, pl.ds(s, n), :] = a_new[i * n:(i + 1) * n]

    _, bias16 = biases(lambda i: i)

    def class16(r, carry):
        base = pl.multiple_of(r * grp, grp)
        block([base + t * pt for t in range(SEQ // pt)], None, grp, bias16, True, False)
        return carry

    lax.fori_loop(0, 16, class16, 0)

    bias4_prev, bias4_own = biases(lambda i: 4 * (i % grp) + i // grp)

    def class4(r, carry):
        def starts_of(tile_base):
            return [tile_base + pl.multiple_of((4 * a + r) * grp, grp) for a in range(4)]

        block(starts_of(0), None, grp, bias4_own, False, False)

        def per_block(b, c2):
            cur = pl.multiple_of(b * pt, pt)
            block(starts_of(cur), starts_of(cur - pt), grp, bias4_prev, False, False)
            return c2

        lax.fori_loop(1, SEQ // pt, per_block, 0)
        return carry

    lax.fori_loop(0, 4, class4, 0)

    bias1_prev, bias1_own = biases(lambda i: 16 * (i % 8) + i // 8)
    per_tile = pt // blk

    def starts1(c):
        base = (c // per_tile) * pt + (c % per_tile) * 8
        return [pl.multiple_of(base + i * grp, 8) for i in range(16)]

    block(starts1(0), None, 8, bias1_own, False, True, out_start=0)

    def block1(c, carry):
        block(starts1(c), starts1(c - 1), 8, bias1_prev, False, True, out_start=pl.multiple_of(c * blk, blk))
        return carry

    lax.fori_loop(1, SEQ // blk, block1, 0)


def _swa_prompt(sq, sk, sv):
    slabs = SWA_W // LANES
    spec = pl.BlockSpec((slabs, None, SEQ, LANES), lambda b: (0, b, 0, 0))
    return pl.pallas_call(
        _swa_prompt_kernel,
        grid=(BATCH,),
        in_specs=[spec, spec, spec],
        out_specs=spec,
        out_shape=jax.ShapeDtypeStruct((slabs, BATCH, SEQ, LANES), F32),
        scratch_shapes=[pltpu.VMEM((slabs, SEQ, LANES), F32)] * 3,
        compiler_params=_params(1),
        name="swa_prompt",
    )(sq, sk, sv)


def _head_norm(o):
    mu = jnp.mean(o, axis=-1, keepdims=True)
    d = o - mu
    var = jnp.mean(d * d, axis=-1, keepdims=True)
    return d * lax.rsqrt(var + GN_EPS)


def _deepnorm_ln(x, h, gain, bias):
    z = DEEPNORM_ALPHA * x + h
    mu = jnp.mean(z, axis=-1, keepdims=True)
    d = z - mu
    var = jnp.mean(d * d, axis=-1, keepdims=True)
    return d * lax.rsqrt(var + LN_EPS) * gain + bias


def _softmax_rows(s):
    m = jnp.max(s, axis=-1, keepdims=True)
    p = jnp.exp(s - m)
    return p * (1.0 / jnp.sum(p, axis=-1, keepdims=True))


def _mix_prompt_kernel(x_ref, rq_ref, rk_ref, rv_ref, rg_ref, so_ref, sg_ref, mq_ref, mg_ref,
                       mk_ref, mv_ref, wout_ref, dmat_ref, kdec_ref, qdec_ref, gdec_ref,
                       gain_ref, bias_ref, y_ref, state_out_ref,
                       state_ref, mix_ref, mkb_ref, mvb_ref, *, tile):
    t = pl.program_id(1)

    @pl.when(t == 0)
    def _():
        state_ref[...] = jnp.zeros_like(state_ref)
        mkb_ref[...] = mk_ref[...].astype(BF16)
        mvb_ref[...] = mv_ref[...].astype(BF16)

    ck = RET_CHUNK
    lane = lax.broadcasted_iota(jnp.int32, (ck, LANES), 1)
    lo = lane < RET_DK
    top = lax.broadcasted_iota(jnp.int32, (LANES, LANES), 0) < RET_DK
    gain = gain_ref[...]
    bias = bias_ref[...]

    def chunk(c, carry):
        rows = pl.ds(pl.multiple_of(c * ck, ck), ck)
        for pair in range(RET_HEADS // 2):
            cs = pl.ds(pair * LANES, LANES)
            q = rq_ref[rows, cs]
            k = rk_ref[rows, cs]
            kb = k.astype(BF16)
            kd = (k * kdec_ref[:, cs]).astype(BF16)
            qd = q * qdec_ref[:, cs]
            st = state_ref[pair]
            stb = st.astype(BF16)
            kv = []
            for hh in range(2):
                h = 2 * pair + hh
                hs = pl.ds(h * RET_DV, RET_DV)
                msk = lo if hh == 0 else ~lo
                v = rv_ref[rows, hs]
                s = _dot_nt(jnp.where(msk, q, 0.0).astype(BF16), kb) * dmat_ref[h]
                o = _dot(s.astype(BF16), v) + _dot(jnp.where(msk, qd, 0.0).astype(BF16), stb)
                kv.append(_dot_tn(kd, v))
                mix_ref[:, hs] = (rg_ref[rows, hs].astype(F32) * _head_norm(o)).astype(BF16)
            state_ref[pair] = gdec_ref[pl.ds(pair * LANES, LANES), :] * st + jnp.where(top, kv[0], kv[1])
        for pair in range(SWA_W // LANES):
            cs = pl.ds(pair * LANES, LANES)
            mix_ref[:, pl.ds(RET_W + pair * LANES, LANES)] = (
                sg_ref[rows, cs].astype(F32) * so_ref[pair, rows, :]).astype(BF16)
        for h in range(MEM_HEADS):
            hs = pl.ds(h * MEM_HD, MEM_HD)
            p = _softmax_rows(_dot_nt(mq_ref[rows, hs], mkb_ref[:, hs]) * MEM_SCALE)
            o = _dot(p.astype(BF16), mvb_ref[:, hs])
            mix_ref[:, pl.ds(RET_W + SWA_W + h * MEM_HD, MEM_HD)] = (
                mg_ref[rows, hs].astype(F32) * o).astype(BF16)
        hout = _dot(mix_ref[...], wout_ref[...])
        y_ref[rows, :] = _deepnorm_ln(x_ref[rows, :], hout, gain, bias)
        return carry

    lax.fori_loop(0, tile // ck, chunk, 0)

    @pl.when(t == pl.num_programs(1) - 1)
    def _():
        state_out_ref[pl.ds(0, LANES), :] = state_ref[0]
        state_out_ref[pl.ds(LANES, LANES), :] = state_ref[1]


def _mix_prompt(x2d, pr, swa_o, mk, mv, wout_bf, tabs, gain, bias, tile):
    nt = SEQ // tile
    row = lambda w: pl.BlockSpec((tile, w), lambda b, t: (b * nt + t, 0))
    const2 = lambda a: pl.BlockSpec(a.shape, lambda b, t: (0, 0))
    dmat, kdec, qdec, gdec = tabs
    return pl.pallas_call(
        functools.partial(_mix_prompt_kernel, tile=tile),
        grid=(BATCH, nt),
        in_specs=[
            row(D_MODEL), row(RET_HEADS * RET_DK), row(RET_HEADS * RET_DK), row(RET_W), row(RET_W),
            pl.BlockSpec((SWA_W // LANES, tile, LANES), lambda b, t: (0, b * nt + t, 0)),
            row(SWA_W), row(MEM_W), row(MEM_W),
            pl.BlockSpec((N_MEM, MEM_W), lambda b, t: (b, 0)),
            pl.BlockSpec((N_MEM, MEM_W), lambda b, t: (b, 0)),
            const2(wout_bf),
            pl.BlockSpec(dmat.shape, lambda b, t: (0, 0, 0)),
            const2(kdec), const2(qdec), const2(gdec), const2(gain), const2(bias),
        ],
        out_specs=[row(D_MODEL),
                   pl.BlockSpec((None, RET_HEADS * RET_DK, RET_DV), lambda b, t: (b, 0, 0))],
        out_shape=[jax.ShapeDtypeStruct((BATCH * SEQ, D_MODEL), F32),
                   jax.ShapeDtypeStruct((BATCH, RET_HEADS * RET_DK, RET_DV), F32)],
        scratch_shapes=[pltpu.VMEM((RET_HEADS // 2, LANES, RET_DV), F32),
                        pltpu.VMEM((RET_CHUNK, D_MIX), BF16),
                        pltpu.VMEM((N_MEM, MEM_W), BF16),
                        pltpu.VMEM((N_MEM, MEM_W), BF16)],
        compiler_params=_params(2),
        name="mix_prompt",
    )(x2d, pr["rq"], pr["rk"], pr["rv"], pr["rg"], swa_o, pr["sg"], pr["mq"], pr["mg"],
      mk, mv, wout_bf, dmat, kdec, qdec, gdec, gain, bias)


def _mix_sample_kernel(rq_ref, rk_ref, rv_ref, rg_ref, sq_ref, sk_ref, sv_ref, sg_ref, mq_ref, mg_ref,
                       state_ref, ckt_ref, cvt_ref, cmk_ref, cmv_ref,
                       dmat_ref, kdec_ref, qdec_ref, gdec_ref,
                       mix_ref, state_out_ref):
    n = DEC_PAD
    q = rq_ref[...]
    k = rk_ref[...]
    kb = k.astype(BF16)
    kd = (k * kdec_ref[...]).astype(BF16)
    qd = q * qdec_ref[...]
    st = state_ref[...]
    stb = st.astype(BF16)
    vb = rv_ref[...].astype(BF16)
    kv = _dot_tn(kd, vb)
    lane_qk = lax.broadcasted_iota(jnp.int32, q.shape, 1) // RET_DK
    for h in range(RET_HEADS):
        hs = pl.ds(h * RET_DV, RET_DV)
        ks = pl.ds(h * RET_DK, RET_DK)
        msk = lane_qk == h
        s = _dot_nt(jnp.where(msk, q, 0.0).astype(BF16), kb) * dmat_ref[h]
        o = _dot(s.astype(BF16), vb[:, h * RET_DV:(h + 1) * RET_DV])
        o = o + _dot(jnp.where(msk, qd, 0.0).astype(BF16), stb)
        mix_ref[:, hs] = rg_ref[:, hs] * _head_norm(o)
        state_out_ref[ks, :] = (gdec_ref[ks, :] * st[h * RET_DK:(h + 1) * RET_DK, :]
                                + kv[h * RET_DK:(h + 1) * RET_DK, h * RET_DV:(h + 1) * RET_DV])

    blk = SWA_BLOCK
    wb = ckt_ref.shape[2]
    rt = 8
    lo = lax.broadcasted_iota(jnp.int32, (n, LANES), 1) < SWA_HD
    lo_t = lax.broadcasted_iota(jnp.int32, (rt, LANES), 1) < SWA_HD
    tok = lambda w: lax.broadcasted_iota(jnp.int32, (rt, w), 0)
    col = lambda w: lax.broadcasted_iota(jnp.int32, (rt, w), 1)
    windows = (
        (wb - blk, jnp.where(col(blk) >= tok(blk), 0.0, NEG), jnp.where(col(blk) <= tok(blk), 0.0, NEG)),
        (wb - 4 * blk, jnp.where(col(4 * blk) % 4 == tok(4 * blk), 0.0, NEG),
         jnp.where(col(blk) == tok(blk), 0.0, NEG)),
        (0, jnp.where(col(wb) % 16 == tok(wb), 0.0, NEG), jnp.where(col(blk) == tok(blk), 0.0, NEG)),
    )
    pad = jnp.zeros((blk - n, LANES), BF16)
    zero_c = jnp.zeros((rt, wb), F32)
    zero_n = jnp.zeros((rt, blk), F32)
    pieces = []
    for pair in range(SWA_HEADS // 2):
        cs = pl.ds(pair * LANES, LANES)
        kp = ckt_ref[2 * pair:2 * pair + 2].reshape(2 * SWA_HD, wb).astype(BF16)
        vp = cvt_ref[2 * pair:2 * pair + 2].reshape(2 * SWA_HD, wb).astype(BF16)
        k_new = jnp.concatenate([sk_ref[:, cs].astype(BF16), pad], axis=0)
        v_new = jnp.concatenate([sv_ref[:, cs].astype(BF16), pad], axis=0)
        q = sq_ref[:, cs]
        qs = jnp.concatenate([jnp.where(lo, q, 0.0), jnp.where(lo, 0.0, q)], axis=0).astype(BF16)
        s_all = _dot(qs, kp)
        sn_all = _dot_nt(qs, k_new)
        p_rows, pn_rows, stats = [], [], []
        for hh in range(2):
            s = s_all[hh * n:hh * n + rt]
            sn = sn_all[hh * n:hh * n + rt]
            for w0, bias_c, bias_n in windows:
                sc = s[:, w0:] + bias_c
                snb = sn + bias_n
                m = jnp.maximum(jnp.max(sc, axis=-1, keepdims=True), jnp.max(snb, axis=-1, keepdims=True))
                pc = jnp.exp(sc - m)
                pn = jnp.exp(snb - m)
                stats.append((m, jnp.sum(pc, axis=-1, keepdims=True) + jnp.sum(pn, axis=-1, keepdims=True)))
                if w0:
                    pc = jnp.concatenate([jnp.zeros((rt, w0), F32), pc], axis=1)
                p_rows.append(pc)
                pn_rows.append(pn)
            p_rows.append(zero_c)
            pn_rows.append(zero_n)
        pv = (_dot_nt(jnp.concatenate(p_rows, axis=0).astype(BF16), vp)
              + _dot(jnp.concatenate(pn_rows, axis=0).astype(BF16), v_new))
        heads = []
        for hh in range(2):
            parts = []
            for i in range(len(windows)):
                m, l = stats[hh * len(windows) + i]
                r0 = (hh * (len(windows) + 1) + i) * rt
                parts.append((pv[r0:r0 + rt] / l, m, l))
            m_all = jnp.maximum(jnp.maximum(parts[0][1], parts[1][1]), parts[2][1])
            ws = [l * jnp.exp(m - m_all) for (_, m, l) in parts]
            heads.append((ws[0] * parts[0][0] + ws[1] * parts[1][0] + ws[2] * parts[2][0])
                         / (ws[0] + ws[1] + ws[2]))
        pieces.append(jnp.where(lo_t, heads[0], heads[1]))
    swa = jnp.concatenate(pieces, axis=1)
    swa = jnp.concatenate([swa, jnp.zeros((n - rt, SWA_W), F32)], axis=0)
    mix_ref[:, pl.ds(RET_W, SWA_W)] = sg_ref[...] * swa

    mq = mq_ref[...]
    lane_m = lax.broadcasted_iota(jnp.int32, mq.shape, 1) // MEM_HD
    qe = jnp.concatenate([jnp.where(lane_m == h, mq, 0.0) for h in range(MEM_HEADS)], axis=0).astype(BF16)
    p = _softmax_rows(_dot_nt(qe, cmk_ref[...].astype(BF16)) * MEM_SCALE)
    o = _dot(p.astype(BF16), cmv_ref[...].astype(BF16))
    for h in range(MEM_HEADS):
        hs = pl.ds(h * MEM_HD, MEM_HD)
        mix_ref[:, pl.ds(RET_W + SWA_W + h * MEM_HD, MEM_HD)] = (
            mg_ref[:, hs] * o[h * n:(h + 1) * n, h * MEM_HD:(h + 1) * MEM_HD])


def _mix_sample(ps, state, ckt, cvt, cmk, cmv, tabs):
    n = DEC_PAD
    wb = ckt.shape[3]
    row = lambda w: pl.BlockSpec((n, w), lambda b: (b, 0))
    st_spec = pl.BlockSpec((None, RET_HEADS * RET_DK, RET_DV), lambda b: (b, 0, 0))
    cache_spec = pl.BlockSpec((None, SWA_HEADS, SWA_HD, wb), lambda b: (b, 0, 0, 0))
    mem_spec = pl.BlockSpec((None, N_MEM, MEM_W), lambda b: (b, 0, 0))
    dmat, kdec, qdec, gdec = tabs
    return pl.pallas_call(
        _mix_sample_kernel,
        grid=(DEC_BATCH,),
        in_specs=[
            row(RET_HEADS * RET_DK), row(RET_HEADS * RET_DK), row(RET_W), row(RET_W),
            row(SWA_W), row(SWA_W), row(SWA_W), row(SWA_W), row(MEM_W), row(MEM_W),
            st_spec, cache_spec, cache_spec, mem_spec, mem_spec,
            pl.BlockSpec(dmat.shape, lambda b: (0, 0, 0)),
            pl.BlockSpec(kdec.shape, lambda b: (0, 0)),
            pl.BlockSpec(qdec.shape, lambda b: (0, 0)),
            pl.BlockSpec(gdec.shape, lambda b: (0, 0)),
        ],
        out_specs=[row(D_MIX), st_spec],
        out_shape=[jax.ShapeDtypeStruct((DEC_BATCH * n, D_MIX), F32),
                   jax.ShapeDtypeStruct((DEC_BATCH, RET_HEADS * RET_DK, RET_DV), F32)],
        compiler_params=_params(1),
        name="mix_sample",
    )(ps["rq"], ps["rk"], ps["rv"], ps["rg"], ps["sq"], ps["sk"], ps["sv"], ps["sg"], ps["mq"], ps["mg"],
      state, ckt, cvt, cmk, cmv, dmat, kdec, qdec, gdec)


def _finish_kernel(x_ref, mix_ref, wout_ref, gain_ref, bias_ref, y_ref):
    hout = _dot(mix_ref[...].astype(BF16), wout_ref[...])
    y_ref[...] = _deepnorm_ln(x_ref[...], hout, gain_ref[...], bias_ref[...])


def _finish(x2d, mix, wout_bf, gain, bias):
    n = x2d.shape[0]
    tile = 256
    const2 = lambda a: pl.BlockSpec(a.shape, lambda i: (0, 0))
    return pl.pallas_call(
        _finish_kernel,
        grid=(n // tile,),
        in_specs=[pl.BlockSpec((tile, D_MODEL), lambda i: (i, 0)),
                  pl.BlockSpec((tile, D_MIX), lambda i: (i, 0)),
                  const2(wout_bf), const2(gain), const2(bias)],
        out_specs=pl.BlockSpec((tile, D_MODEL), lambda i: (i, 0)),
        out_shape=jax.ShapeDtypeStruct((n, D_MODEL), F32),
        compiler_params=_params(1),
        name="finish",
    )(x2d, mix, wout_bf, gain, bias)


def _rope_tables(pos):
    half = SWA_HD // 2
    inv = ROPE_THETA ** (-jnp.arange(half, dtype=F32) * 2.0 / SWA_HD)
    ang = pos.astype(F32)[:, None] * inv[None, :]
    cos = jnp.cos(ang)
    sin = jnp.sin(ang)
    reps = LANES // SWA_HD
    return (jnp.tile(jnp.concatenate([cos, cos], axis=1), (1, reps)),
            jnp.tile(jnp.concatenate([-sin, sin], axis=1), (1, reps)))


def _retention_tables(chunk, rows):
    lg = jnp.log1p(-jnp.exp2(-5.0 - jnp.arange(RET_HEADS, dtype=F32)))
    idx = jnp.arange(rows, dtype=F32)
    live = idx < chunk
    rel = idx[:, None] - idx[None, :]
    ok = (rel >= 0) & live[:, None] & live[None, :]
    dmat = jnp.where(ok[None], jnp.exp(jnp.maximum(rel, 0.0)[None] * lg[:, None, None]), 0.0)
    kdec = jnp.where(live[:, None], jnp.exp((chunk - 1.0 - idx)[:, None] * lg[None, :]), 0.0)
    qdec = jnp.where(live[:, None], jnp.exp((idx + 1.0)[:, None] * lg[None, :]), 0.0)
    g = jnp.exp(chunk * lg)
    kdec = jnp.repeat(kdec, RET_DK, axis=1)
    qdec = jnp.repeat(qdec, RET_DK, axis=1)
    gdec = jnp.broadcast_to(jnp.repeat(g, RET_DK)[:, None], (RET_HEADS * RET_DK, RET_DV))
    return dmat, kdec, qdec, gdec


def kernel(x_prompt, x_sample, state_ret, cache_swa_k, cache_swa_v, cache_mem_k, cache_mem_v,
           mem_prompt, w_in, w_mem_kv, w_out, ln_gain, ln_bias):
    depth = w_in.shape[0]
    assert depth == 1
    win_bf = w_in[0].astype(BF16)
    wmem_bf = w_mem_kv[0].astype(BF16)
    wout_bf = w_out[0].astype(BF16)
    gain = ln_gain[0].reshape(1, D_MODEL)
    bias = ln_bias[0].reshape(1, D_MODEL)

    xp = x_prompt.reshape(BATCH * SEQ, D_MODEL)
    cos_p, sin_p = _rope_tables(jnp.arange(SEQ))
    p_outs = (("rq", "rows", F32), ("rk", "rows", F32), ("rv", "rows", BF16), ("rg", "rows", BF16),
              ("sq", "perm", F32), ("sk", "cols", F32), ("sk", "perm", F32), ("sv", "cols", F32),
              ("sv", "perm", F32), ("sg", "rows", BF16), ("mq", "rows", BF16), ("mg", "rows", BF16))
    p_keys = ("rq", "rk", "rv", "rg", "sq4", "sk", "sk4", "sv", "sv4", "sg", "mq", "mg")
    pr = dict(zip(p_keys, _project(xp, win_bf, cos_p, sin_p, PERM_TILE, p_outs)))
    mk, mv = _memkv(mem_prompt.reshape(BATCH * N_MEM, D_MODEL), wmem_bf)
    slab = lambda a: a.reshape(SWA_W // LANES, BATCH, SEQ, LANES)
    swa_o = _swa_prompt(slab(pr["sq4"]), slab(pr["sk4"]), slab(pr["sv4"]))
    swa_o = swa_o.reshape(SWA_W // LANES, BATCH * SEQ, LANES)
    tabs_p = _retention_tables(RET_CHUNK, RET_CHUNK)
    yp, ret_p = _mix_prompt(xp, pr, swa_o, mk, mv, wout_bf, tabs_p, gain, bias, 512)

    xs = jnp.pad(x_sample, ((0, 0), (0, DEC_PAD - DEC_SEQ), (0, 0))).reshape(DEC_BATCH * DEC_PAD, D_MODEL)
    pos_s = PAST_LEN + jnp.arange(DEC_BATCH * DEC_PAD) % DEC_PAD
    cos_s, sin_s = _rope_tables(pos_s)
    names = [c[0] for c in _PROJ_COLS]
    ps = dict(zip(names, _project(xs, win_bf, cos_s, sin_s, DEC_BATCH * DEC_PAD,
                                  [(k, "rows", F32) for k in names])))
    tabs_s = _retention_tables(DEC_SEQ, DEC_PAD)
    mix_s, ret_s = _mix_sample(
        ps, state_ret[0].reshape(DEC_BATCH, RET_HEADS * RET_DK, RET_DV),
        cache_swa_k[0].transpose(0, 2, 3, 1), cache_swa_v[0].transpose(0, 2, 3, 1),
        cache_mem_k[0].reshape(DEC_BATCH, N_MEM, MEM_W), cache_mem_v[0].reshape(DEC_BATCH, N_MEM, MEM_W),
        tabs_s)
    ys = _finish(xs, mix_s, wout_bf, gain, bias)

    take = lambda a, w: a.reshape(DEC_BATCH, DEC_PAD, w)[:, :DEC_SEQ]
    swa_rows = lambda a: a.reshape(BATCH, SWA_HEADS, SWA_HD, SEQ).transpose(0, 3, 1, 2)[None]
    return (
        yp.reshape(BATCH, SEQ, D_MODEL),
        take(ys, D_MODEL),
        ret_p.reshape(1, BATCH, RET_HEADS, RET_DK, RET_DV),
        ret_s.reshape(1, DEC_BATCH, RET_HEADS, RET_DK, RET_DV),
        swa_rows(pr["sk"]),
        swa_rows(pr["sv"]),
        take(ps["sk"], SWA_W).reshape(1, DEC_BATCH, DEC_SEQ, SWA_HEADS, SWA_HD),
        take(ps["sv"], SWA_W).reshape(1, DEC_BATCH, DEC_SEQ, SWA_HEADS, SWA_HD),
        mk.reshape(1, BATCH, N_MEM, MEM_HEADS, MEM_HD),
        mv.reshape(1, BATCH, N_MEM, MEM_HEADS, MEM_HD),
    )
```

```python
import functools

import jax
import jax.numpy as jnp
from jax import lax
from jax.experimental import pallas as pl
from jax.experimental.pallas import tpu as pltpu

F32 = jnp.float32
BF16 = jnp.bfloat16

D_MODEL = 1024
BATCH = 8
SEQ = 2048
DEC_BATCH = 32
DEC_SEQ = 4
PAST_LEN = 8192
N_MEM = 256
MEM_HEADS = 4
MEM_HD = 128
RET_HEADS = 4
RET_DK = 64
RET_DV = 128
RET_CHUNK = 128
SWA_HEADS = 8
SWA_HD = 64
SWA_DILATIONS = (1, 4, 16)
SWA_STEPS = 128
SWA_BLOCK = 128
ROPE_THETA = 10000.0
LN_EPS = 1e-5
GN_EPS = 1e-5
RET_W = RET_HEADS * RET_DV
SWA_W = SWA_HEADS * SWA_HD
MEM_W = MEM_HEADS * MEM_HD
D_MIX = RET_W + SWA_W + MEM_W
DEEPNORM_ALPHA = 2.0 ** 0.25
MEM_SCALE = MEM_HD ** -0.5
QK_SCALE = 0.125

LANES = 128
DEC_PAD = 16
PERM_TILE = 512
OUT_ROWS = 256
VMEM_LIMIT = 56 * 1024 * 1024
NEG = -1e30

_PROJ_COLS = (
    ("rq", RET_HEADS * RET_DK, "rope", 1.0),
    ("rk", RET_HEADS * RET_DK, "rope", QK_SCALE),
    ("rv", RET_W, "id", 1.0),
    ("rg", RET_W, "silu", 1.0),
    ("sq", SWA_W, "rope", QK_SCALE),
    ("sk", SWA_W, "rope", 1.0),
    ("sv", SWA_W, "id", 1.0),
    ("sg", SWA_W, "silu", 1.0),
    ("mq", MEM_W, "id", 1.0),
    ("mg", MEM_W, "silu", 1.0),
)


def _dot(a, b):
    return jnp.dot(a, b, preferred_element_type=F32)


def _dot_nt(a, b):
    return lax.dot_general(a, b, (((1,), (1,)), ((), ())), preferred_element_type=F32)


def _dot_tn(a, b):
    return lax.dot_general(a, b, (((0,), (0,)), ((), ())), preferred_element_type=F32)


def _params(n_axes):
    return pltpu.CompilerParams(dimension_semantics=("arbitrary",) * n_axes,
                                vmem_limit_bytes=VMEM_LIMIT)


def _proj_kernel(x_ref, w_ref, cos_ref, sin_ref, *out_refs, dests):
    xb = x_ref[...].astype(BF16)
    cos = cos_ref[...]
    sin = sin_ref[...]
    lane = lax.broadcasted_iota(jnp.int32, cos.shape, 1)
    first_half = (lane % 64) < 32
    col = 0
    for name, width, kind, scale in _PROJ_COLS:
        targets = [(o_ref, layout) for o_ref, (dname, layout) in zip(out_refs, dests) if dname == name]
        for c in range(0, width, 2 * LANES):
            h2 = _dot(xb, w_ref[:, col + c:col + c + 2 * LANES])
            for half in range(2):
                h = h2[:, half * LANES:(half + 1) * LANES]
                if kind == "rope":
                    swapped = jnp.where(first_half, pltpu.roll(h, 96, 1), pltpu.roll(h, 32, 1))
                    h = h * cos + swapped * sin
                    if scale != 1.0:
                        h = h * scale
                elif kind == "silu":
                    h = h * (1.0 / (1.0 + jnp.exp(-h)))
                lo = c + half * LANES
                for o_ref, layout in targets:
                    if layout == "rows":
                        o_ref[:, lo:lo + LANES] = h.astype(o_ref.dtype)
                    elif layout == "perm":
                        tile = h.shape[0]
                        for g in range(tile // 8):
                            base = (g % 2) * (tile // 2) + g // 2
                            o_ref[lo // LANES, pl.ds(base, 8, stride=tile // 16), :] = (
                                h[8 * g:8 * g + 8].astype(o_ref.dtype))
                    else:
                        o_ref[lo:lo + LANES, :] = h.T.astype(o_ref.dtype)
        col += width


def _project(x2d, w_bf, cos_t, sin_t, tile, outs):
    n = x2d.shape[0]
    seq = cos_t.shape[0]
    n_tab = seq // tile
    d_in = w_bf.shape[1]
    widths = {name: w for name, w, _, _ in _PROJ_COLS}
    out_shape, out_specs = [], []
    for name, layout, dt in outs:
        w = widths[name]
        if layout == "rows":
            out_shape.append(jax.ShapeDtypeStruct((n, w), dt))
            out_specs.append(pl.BlockSpec((tile, w), lambda i: (i, 0)))
        elif layout == "perm":
            assert tile == PERM_TILE
            out_shape.append(jax.ShapeDtypeStruct((w // LANES, n, LANES), dt))
            out_specs.append(pl.BlockSpec((w // LANES, tile, LANES), lambda i: (0, i, 0)))
        else:
            out_shape.append(jax.ShapeDtypeStruct((n // seq, w, seq), dt))
            out_specs.append(pl.BlockSpec((None, w, tile), lambda i: (i // n_tab, 0, i % n_tab)))
    return pl.pallas_call(
        functools.partial(_proj_kernel, dests=tuple((name, layout) for name, layout, _ in outs)),
        grid=(n // tile,),
        in_specs=[
            pl.BlockSpec((tile, D_MODEL), lambda i: (i, 0)),
            pl.BlockSpec((D_MODEL, d_in), lambda i: (0, 0)),
            pl.BlockSpec((tile, LANES), lambda i: (i % n_tab, 0)),
            pl.BlockSpec((tile, LANES), lambda i: (i % n_tab, 0)),
        ],
        out_specs=out_specs,
        out_shape=out_shape,
        compiler_params=_params(1),
        name="proj",
    )(x2d, w_bf, cos_t, sin_t)


def _memkv_kernel(m_ref, w_ref, mk_ref, mv_ref):
    mb = m_ref[...].astype(BF16)
    mk_ref[...] = _dot(mb, w_ref[:, :MEM_W])
    mv_ref[...] = _dot(mb, w_ref[:, MEM_W:])


def _memkv(mem2d, w_bf):
    n = mem2d.shape[0]
    tile = 512
    return pl.pallas_call(
        _memkv_kernel,
        grid=(n // tile,),
        in_specs=[pl.BlockSpec((tile, D_MODEL), lambda i: (i, 0)),
                  pl.BlockSpec((D_MODEL, 2 * MEM_W), lambda i: (0, 0))],
        out_specs=[pl.BlockSpec((tile, MEM_W), lambda i: (i, 0))] * 2,
        out_shape=[jax.ShapeDtypeStruct((n, MEM_W), F32)] * 2,
        compiler_params=_params(1),
        name="memkv",
    )(mem2d, w_bf)


def _swa_prompt_kernel(q_ref, k_ref, v_ref, o_ref, a_ref, m_ref, l_ref):
    blk = SWA_BLOCK
    pt = PERM_TILE
    grp = pt // 16
    lo = lax.broadcasted_iota(jnp.int32, (blk, blk), 1) < SWA_HD

    def biases(seq_of):
        rq = seq_of(lax.broadcasted_iota(jnp.int32, (blk, 2 * blk), 0))
        c2 = lax.broadcasted_iota(jnp.int32, (blk, 2 * blk), 1)
        rel = blk + rq - (seq_of(c2 % blk) + blk * (c2 // blk))
        prev = jnp.where((rel >= 0) & (rel <= SWA_STEPS), 0.0, NEG).astype(F32)
        own = jnp.where((rel >= 0) & (rel <= SWA_STEPS) & (c2 >= blk), 0.0, NEG).astype(F32)
        return prev, own

    def gather(ref, pair, starts, n):
        return jnp.concatenate([ref[pair, pl.ds(s, n), :] for s in starts], axis=0)

    def block(starts, prev_starts, n, bias, first_pattern, last_pattern, out_start=None):
        n_pairs = SWA_HEADS // 2
        k_starts = (starts if prev_starts is None else prev_starts) + starts
        olds = None
        if not first_pattern:
            olds = [(gather(m_ref, pair, starts, n), gather(l_ref, pair, starts, n),
                     gather(a_ref, pair, starts, n)) for pair in range(n_pairs)]
        heads = [(pair, hh) for pair in range(n_pairs) for hh in range(2)]
        vbs = [gather(v_ref, pair, k_starts, n).astype(BF16) for pair in range(n_pairs)]
        scores = []
        for pair in range(n_pairs):
            q = gather(q_ref, pair, starts, n)
            kb = gather(k_ref, pair, k_starts, n).astype(BF16)
            for hh in range(2):
                qm = jnp.where(lo if hh == 0 else ~lo, q, 0.0).astype(BF16)
                scores.append(_dot_nt(qm, kb) + bias)
        ms = [jnp.max(s, axis=-1, keepdims=True) for s in scores]
        ps = [jnp.exp(s - m) for s, m in zip(scores, ms)]
        ls = [jnp.sum(p, axis=-1, keepdims=True) for p in ps]
        pvs = [_dot(p.astype(BF16), vbs[pair]) for p, (pair, _) in zip(ps, heads)]
        news = []
        for pair in range(n_pairs):
            i0, i1 = 2 * pair, 2 * pair + 1
            m_g = jnp.where(lo, ms[i0], ms[i1])
            l_g = jnp.where(lo, ls[i0], ls[i1])
            a_g = jnp.where(lo, pvs[i0], pvs[i1])
            if first_pattern:
                m_new, l_new, a_new = m_g, l_g, a_g
            else:
                m_old, l_old, a_old = olds[pair]
                m_new = jnp.maximum(m_old, m_g)
                w_old = jnp.exp(m_old - m_new)
                w_g = jnp.exp(m_g - m_new)
                l_new = w_old * l_old + w_g * l_g
                a_new = w_old * a_old + w_g * a_g
            news.append((m_new, l_new, a_new))
        for pair, (m_new, l_new, a_new) in enumerate(news):
            if last_pattern:
                o = a_new / l_new
                for i in range(len(starts)):
                    o_ref[pair, pl.ds(out_start + i, n, stride=len(starts)), :] = o[i * n:(i + 1) * n]
            else:
                for i, s in enumerate(starts):
                    m_ref[pair, pl.ds(s, n), :] = m_new[i * n:(i + 1) * n]
                    l_ref[pair, pl.ds(s, n), :] = l_new[i * n:(i + 1) * n]
                    a_ref[pair, pl.ds(s, n), :] = a_new[i * n:(i + 1) * n]

    _, bias16 = biases(lambda i: i)

    def class16(r, carry):
        base = pl.multiple_of(r * grp, grp)
        block([base + t * pt for t in range(SEQ // pt)], None, grp, bias16, True, False)
        return carry

    lax.fori_loop(0, 16, class16, 0)

    bias4_prev, bias4_own = biases(lambda i: 4 * (i % grp) + i // grp)

    def class4(r, carry):
        def starts_of(tile_base):
            return [tile_base + pl.multiple_of((4 * a + r) * grp, grp) for a in range(4)]

        block(starts_of(0), None, grp, bias4_own, False, False)

        def per_block(b, c2):
            cur = pl.multiple_of(b * pt, pt)
            block(starts_of(cur), starts_of(cur - pt), grp, bias4_prev, False, False)
            return c2

        lax.fori_loop(1, SEQ // pt, per_block, 0)
        return carry

    lax.fori_loop(0, 4, class4, 0)

    bias1_prev, bias1_own = biases(lambda i: 16 * (i % 8) + i // 8)
    per_tile = pt // blk

    def starts1(c):
        base = (c // per_tile) * pt + (c % per_tile) * 8
        return [pl.multiple_of(base + i * grp, 8) for i in range(16)]

    block(starts1(0), None, 8, bias1_own, False, True, out_start=0)

    def block1(c, carry):
        block(starts1(c), starts1(c - 1), 8, bias1_prev, False, True, out_start=pl.multiple_of(c * blk, blk))
        return carry

    lax.fori_loop(1, SEQ // blk, block1, 0)


def _swa_prompt(sq, sk, sv):
    slabs = SWA_W // LANES
    spec = pl.BlockSpec((slabs, None, SEQ, LANES), lambda b: (0, b, 0, 0))
    return pl.pallas_call(
        _swa_prompt_kernel,
        grid=(BATCH,),
        in_specs=[spec, spec, spec],
        out_specs=spec,
        out_shape=jax.ShapeDtypeStruct((slabs, BATCH, SEQ, LANES), F32),
        scratch_shapes=[pltpu.VMEM((slabs, SEQ, LANES), F32)] * 3,
        compiler_params=_params(1),
        name="swa_prompt",
    )(sq, sk, sv)


def _head_norm(o):
    mu = jnp.mean(o, axis=-1, keepdims=True)
    d = o - mu
    var = jnp.mean(d * d, axis=-1, keepdims=True)
    return d * lax.rsqrt(var + GN_EPS)


def _deepnorm_ln(x, h, gain, bias):
    z = DEEPNORM_ALPHA * x + h
    mu = jnp.mean(z, axis=-1, keepdims=True)
    d = z - mu
    var = jnp.mean(d * d, axis=-1, keepdims=True)
    return d * lax.rsqrt(var + LN_EPS) * gain + bias


def _softmax_rows(s):
    m = jnp.max(s, axis=-1, keepdims=True)
    p = jnp.exp(s - m)
    return p * (1.0 / jnp.sum(p, axis=-1, keepdims=True))


def _mix_prompt_kernel(x_ref, rq_ref, rk_ref, rv_ref, rg_ref, so_ref, sg_ref, mq_ref, mg_ref,
                       mk_ref, mv_ref, wout_ref, dmat_ref, kdec_ref, qdec_ref, gdec_ref,
                       gain_ref, bias_ref, y_ref, state_out_ref,
                       state_ref, mix_ref, mkb_ref, mvb_ref, *, tile):
    t = pl.program_id(1)

    @pl.when(t == 0)
    def _():
        state_ref[...] = jnp.zeros_like(state_ref)
        mkb_ref[...] = mk_ref[...].astype(BF16)
        mvb_ref[...] = mv_ref[...].astype(BF16)

    ck = RET_CHUNK
    lane = lax.broadcasted_iota(jnp.int32, (ck, LANES), 1)
    lo = lane < RET_DK
    top = lax.broadcasted_iota(jnp.int32, (LANES, LANES), 0) < RET_DK
    gain = gain_ref[...]
    bias = bias_ref[...]

    n_ck = tile // ck
    n_pairs = RET_HEADS // 2
    rows_of = [pl.ds(c * ck, ck) for c in range(n_ck)]
    items = [(c, pair, hh) for c in range(n_ck) for pair in range(n_pairs) for hh in range(2)]
    hs_of = lambda pair, hh: pl.ds((2 * pair + hh) * RET_DV, RET_DV)
    qk = {}
    for c in range(n_ck):
        for pair in range(n_pairs):
            cs = pl.ds(pair * LANES, LANES)
            q = rq_ref[rows_of[c], cs]
            k = rk_ref[rows_of[c], cs]
            qk[c, pair] = (q, k.astype(BF16), (k * kdec_ref[:, cs]).astype(BF16), q * qdec_ref[:, cs])
    sel = lambda x, hh: jnp.where(lo if hh == 0 else ~lo, x, 0.0).astype(BF16)
    s = {(c, pair, hh): _dot_nt(sel(qk[c, pair][0], hh), qk[c, pair][1]) * dmat_ref[2 * pair + hh]
         for c, pair, hh in items}
    kv = {(c, pair, hh): _dot_tn(qk[c, pair][2], rv_ref[rows_of[c], hs_of(pair, hh)]) for c, pair, hh in items}
    intra = {(c, pair, hh): _dot(s[c, pair, hh].astype(BF16), rv_ref[rows_of[c], hs_of(pair, hh)])
             for c, pair, hh in items}
    state = {(0, pair): state_ref[pair] for pair in range(n_pairs)}
    for c in range(n_ck):
        for pair in range(n_pairs):
            state[c + 1, pair] = (gdec_ref[pl.ds(pair * LANES, LANES), :] * state[c, pair]
                                  + jnp.where(top, kv[c, pair, 0], kv[c, pair, 1]))
    for pair in range(n_pairs):
        state_ref[pair] = state[n_ck, pair]
    cross = {(c, pair, hh): _dot(sel(qk[c, pair][3], hh), state[c, pair].astype(BF16)) for c, pair, hh in items}
    for c, pair, hh in items:
        hs = hs_of(pair, hh)
        o = intra[c, pair, hh] + cross[c, pair, hh]
        mix_ref[rows_of[c], hs] = (rg_ref[rows_of[c], hs].astype(F32) * _head_norm(o)).astype(BF16)
    for pair in range(SWA_W // LANES):
        cs = pl.ds(pair * LANES, LANES)
        mix_ref[:, pl.ds(RET_W + pair * LANES, LANES)] = (
            sg_ref[:, cs].astype(F32) * so_ref[pair]).astype(BF16)
    for c in range(n_ck):
        ps = [_softmax_rows(_dot_nt(mq_ref[rows_of[c], pl.ds(h * MEM_HD, MEM_HD)],
                                    mkb_ref[:, pl.ds(h * MEM_HD, MEM_HD)]) * MEM_SCALE)
              for h in range(MEM_HEADS)]
        for h in range(MEM_HEADS):
            hs = pl.ds(h * MEM_HD, MEM_HD)
            o = _dot(ps[h].astype(BF16), mvb_ref[:, hs])
            mix_ref[rows_of[c], pl.ds(RET_W + SWA_W + h * MEM_HD, MEM_HD)] = (
                mg_ref[rows_of[c], hs].astype(F32) * o).astype(BF16)
    ob = OUT_ROWS
    for r in range(tile // ob):
        rows = pl.ds(r * ob, ob)
        hout = _dot(mix_ref[rows, :], wout_ref[...])
        y_ref[rows, :] = _deepnorm_ln(x_ref[rows, :], hout, gain, bias)

    @pl.when(t == pl.num_programs(1) - 1)
    def _():
        state_out_ref[pl.ds(0, LANES), :] = state_ref[0]
        state_out_ref[pl.ds(LANES, LANES), :] = state_ref[1]


def _mix_prompt(x2d, pr, swa_o, mk, mv, wout_bf, tabs, gain, bias, tile):
    nt = SEQ // tile
    row = lambda w: pl.BlockSpec((tile, w), lambda b, t: (b * nt + t, 0))
    const2 = lambda a: pl.BlockSpec(a.shape, lambda b, t: (0, 0))
    dmat, kdec, qdec, gdec = tabs
    return pl.pallas_call(
        functools.partial(_mix_prompt_kernel, tile=tile),
        grid=(BATCH, nt),
        in_specs=[
            row(D_MODEL), row(RET_HEADS * RET_DK), row(RET_HEADS * RET_DK), row(RET_W), row(RET_W),
            pl.BlockSpec((SWA_W // LANES, tile, LANES), lambda b, t: (0, b * nt + t, 0)),
            row(SWA_W), row(MEM_W), row(MEM_W),
            pl.BlockSpec((N_MEM, MEM_W), lambda b, t: (b, 0)),
            pl.BlockSpec((N_MEM, MEM_W), lambda b, t: (b, 0)),
            const2(wout_bf),
            pl.BlockSpec(dmat.shape, lambda b, t: (0, 0, 0)),
            const2(kdec), const2(qdec), const2(gdec), const2(gain), const2(bias),
        ],
        out_specs=[row(D_MODEL),
                   pl.BlockSpec((None, RET_HEADS * RET_DK, RET_DV), lambda b, t: (b, 0, 0))],
        out_shape=[jax.ShapeDtypeStruct((BATCH * SEQ, D_MODEL), F32),
                   jax.ShapeDtypeStruct((BATCH, RET_HEADS * RET_DK, RET_DV), F32)],
        scratch_shapes=[pltpu.VMEM((RET_HEADS // 2, LANES, RET_DV), F32),
                        pltpu.VMEM((tile, D_MIX), BF16),
                        pltpu.VMEM((N_MEM, MEM_W), BF16),
                        pltpu.VMEM((N_MEM, MEM_W), BF16)],
        compiler_params=_params(2),
        name="mix_prompt",
    )(x2d, pr["rq"], pr["rk"], pr["rv"], pr["rg"], swa_o, pr["sg"], pr["mq"], pr["mg"],
      mk, mv, wout_bf, dmat, kdec, qdec, gdec, gain, bias)


def _mix_sample_kernel(rq_ref, rk_ref, rv_ref, rg_ref, sq_ref, sk_ref, sv_ref, sg_ref, mq_ref, mg_ref,
                       state_ref, ckt_ref, cvt_ref, cmk_ref, cmv_ref,
                       dmat_ref, kdec_ref, qdec_ref, gdec_ref,
                       mix_ref, state_out_ref):
    n = DEC_PAD
    q = rq_ref[...]
    k = rk_ref[...]
    kb = k.astype(BF16)
    kd = (k * kdec_ref[...]).astype(BF16)
    qd = q * qdec_ref[...]
    st = state_ref[...]
    stb = st.astype(BF16)
    vb = rv_ref[...].astype(BF16)
    kv = _dot_tn(kd, vb)
    lane_qk = lax.broadcasted_iota(jnp.int32, q.shape, 1) // RET_DK
    for h in range(RET_HEADS):
        hs = pl.ds(h * RET_DV, RET_DV)
        ks = pl.ds(h * RET_DK, RET_DK)
        msk = lane_qk == h
        s = _dot_nt(jnp.where(msk, q, 0.0).astype(BF16), kb) * dmat_ref[h]
        o = _dot(s.astype(BF16), vb[:, h * RET_DV:(h + 1) * RET_DV])
        o = o + _dot(jnp.where(msk, qd, 0.0).astype(BF16), stb)
        mix_ref[:, hs] = rg_ref[:, hs] * _head_norm(o)
        state_out_ref[ks, :] = (gdec_ref[ks, :] * st[h * RET_DK:(h + 1) * RET_DK, :]
                                + kv[h * RET_DK:(h + 1) * RET_DK, h * RET_DV:(h + 1) * RET_DV])

    blk = SWA_BLOCK
    wb = ckt_ref.shape[2]
    rt = 8
    lo = lax.broadcasted_iota(jnp.int32, (n, LANES), 1) < SWA_HD
    lo_t = lax.broadcasted_iota(jnp.int32, (rt, LANES), 1) < SWA_HD
    tok = lambda w: lax.broadcasted_iota(jnp.int32, (rt, w), 0)
    col = lambda w: lax.broadcasted_iota(jnp.int32, (rt, w), 1)
    windows = (
        (wb - blk, jnp.where(col(blk) >= tok(blk), 0.0, NEG), jnp.where(col(blk) <= tok(blk), 0.0, NEG)),
        (wb - 4 * blk, jnp.where(col(4 * blk) % 4 == tok(4 * blk), 0.0, NEG),
         jnp.where(col(blk) == tok(blk), 0.0, NEG)),
        (0, jnp.where(col(wb) % 16 == tok(wb), 0.0, NEG), jnp.where(col(blk) == tok(blk), 0.0, NEG)),
    )
    pad = jnp.zeros((blk - n, LANES), BF16)
    zero_c = jnp.zeros((rt, wb), F32)
    zero_n = jnp.zeros((rt, blk), F32)
    pieces = []
    for pair in range(SWA_HEADS // 2):
        cs = pl.ds(pair * LANES, LANES)
        kp = ckt_ref[2 * pair:2 * pair + 2].reshape(2 * SWA_HD, wb).astype(BF16)
        vp = cvt_ref[2 * pair:2 * pair + 2].reshape(2 * SWA_HD, wb).astype(BF16)
        k_new = jnp.concatenate([sk_ref[:, cs].astype(BF16), pad], axis=0)
        v_new = jnp.concatenate([sv_ref[:, cs].astype(BF16), pad], axis=0)
        q = sq_ref[:, cs]
        qs = jnp.concatenate([jnp.where(lo, q, 0.0), jnp.where(lo, 0.0, q)], axis=0).astype(BF16)
        s_all = _dot(qs, kp)
        sn_all = _dot_nt(qs, k_new)
        p_rows, pn_rows, stats = [], [], []
        for hh in range(2):
            s = s_all[hh * n:hh * n + rt]
            sn = sn_all[hh * n:hh * n + rt]
            for w0, bias_c, bias_n in windows:
                sc = s[:, w0:] + bias_c
                snb = sn + bias_n
                m = jnp.maximum(jnp.max(sc, axis=-1, keepdims=True), jnp.max(snb, axis=-1, keepdims=True))
                pc = jnp.exp(sc - m)
                pn = jnp.exp(snb - m)
                stats.append((m, jnp.sum(pc, axis=-1, keepdims=True) + jnp.sum(pn, axis=-1, keepdims=True)))
                if w0:
                    pc = jnp.concatenate([jnp.zeros((rt, w0), F32), pc], axis=1)
                p_rows.append(pc)
                pn_rows.append(pn)
            p_rows.append(zero_c)
            pn_rows.append(zero_n)
        pv = (_dot_nt(jnp.concatenate(p_rows, axis=0).astype(BF16), vp)
              + _dot(jnp.concatenate(pn_rows, axis=0).astype(BF16), v_new))
        heads = []
        for hh in range(2):
            parts = []
            for i in range(len(windows)):
                m, l = stats[hh * len(windows) + i]
                r0 = (hh * (len(windows) + 1) + i) * rt
                parts.append((pv[r0:r0 + rt] / l, m, l))
            m_all = jnp.maximum(jnp.maximum(parts[0][1], parts[1][1]), parts[2][1])
            ws = [l * jnp.exp(m - m_all) for (_, m, l) in parts]
            heads.append((ws[0] * parts[0][0] + ws[1] * parts[1][0] + ws[2] * parts[2][0])
                         / (ws[0] + ws[1] + ws[2]))
        pieces.append(jnp.where(lo_t, heads[0], heads[1]))
    swa = jnp.concatenate(pieces, axis=1)
    swa = jnp.concatenate([swa, jnp.zeros((n - rt, SWA_W), F32)], axis=0)
    mix_ref[:, pl.ds(RET_W, SWA_W)] = sg_ref[...] * swa

    mq = mq_ref[...]
    lane_m = lax.broadcasted_iota(jnp.int32, mq.shape, 1) // MEM_HD
    qe = jnp.concatenate([jnp.where(lane_m == h, mq, 0.0) for h in range(MEM_HEADS)], axis=0).astype(BF16)
    p = _softmax_rows(_dot_nt(qe, cmk_ref[...].astype(BF16)) * MEM_SCALE)
    o = _dot(p.astype(BF16), cmv_ref[...].astype(BF16))
    for h in range(MEM_HEADS):
        hs = pl.ds(h * MEM_HD, MEM_HD)
        mix_ref[:, pl.ds(RET_W + SWA_W + h * MEM_HD, MEM_HD)] = (
            mg_ref[:, hs] * o[h * n:(h + 1) * n, h * MEM_HD:(h + 1) * MEM_HD])


def _mix_sample(ps, state, ckt, cvt, cmk, cmv, tabs):
    n = DEC_PAD
    wb = ckt.shape[3]
    row = lambda w: pl.BlockSpec((n, w), lambda b: (b, 0))
    st_spec = pl.BlockSpec((None, RET_HEADS * RET_DK, RET_DV), lambda b: (b, 0, 0))
    cache_spec = pl.BlockSpec((None, SWA_HEADS, SWA_HD, wb), lambda b: (b, 0, 0, 0))
    mem_spec = pl.BlockSpec((None, N_MEM, MEM_W), lambda b: (b, 0, 0))
    dmat, kdec, qdec, gdec = tabs
    return pl.pallas_call(
        _mix_sample_kernel,
        grid=(DEC_BATCH,),
        in_specs=[
            row(RET_HEADS * RET_DK), row(RET_HEADS * RET_DK), row(RET_W), row(RET_W),
            row(SWA_W), row(SWA_W), row(SWA_W), row(SWA_W), row(MEM_W), row(MEM_W),
            st_spec, cache_spec, cache_spec, mem_spec, mem_spec,
            pl.BlockSpec(dmat.shape, lambda b: (0, 0, 0)),
            pl.BlockSpec(kdec.shape, lambda b: (0, 0)),
            pl.BlockSpec(qdec.shape, lambda b: (0, 0)),
            pl.BlockSpec(gdec.shape, lambda b: (0, 0)),
        ],
        out_specs=[row(D_MIX), st_spec],
        out_shape=[jax.ShapeDtypeStruct((DEC_BATCH * n, D_MIX), F32),
                   jax.ShapeDtypeStruct((DEC_BATCH, RET_HEADS * RET_DK, RET_DV), F32)],
        compiler_params=_params(1),
        name="mix_sample",
    )(ps["rq"], ps["rk"], ps["rv"], ps["rg"], ps["sq"], ps["sk"], ps["sv"], ps["sg"], ps["mq"], ps["mg"],
      state, ckt, cvt, cmk, cmv, dmat, kdec, qdec, gdec)


def _finish_kernel(x_ref, mix_ref, wout_ref, gain_ref, bias_ref, y_ref):
    hout = _dot(mix_ref[...].astype(BF16), wout_ref[...])
    y_ref[...] = _deepnorm_ln(x_ref[...], hout, gain_ref[...], bias_ref[...])


def _finish(x2d, mix, wout_bf, gain, bias):
    n = x2d.shape[0]
    tile = 256
    const2 = lambda a: pl.BlockSpec(a.shape, lambda i: (0, 0))
    return pl.pallas_call(
        _finish_kernel,
        grid=(n // tile,),
        in_specs=[pl.BlockSpec((tile, D_MODEL), lambda i: (i, 0)),
                  pl.BlockSpec((tile, D_MIX), lambda i: (i, 0)),
                  const2(wout_bf), const2(gain), const2(bias)],
        out_specs=pl.BlockSpec((tile, D_MODEL), lambda i: (i, 0)),
        out_shape=jax.ShapeDtypeStruct((n, D_MODEL), F32),
        compiler_params=_params(1),
        name="finish",
    )(x2d, mix, wout_bf, gain, bias)


def _rope_tables(pos):
    half = SWA_HD // 2
    inv = ROPE_THETA ** (-jnp.arange(half, dtype=F32) * 2.0 / SWA_HD)
    ang = pos.astype(F32)[:, None] * inv[None, :]
    cos = jnp.cos(ang)
    sin = jnp.sin(ang)
    reps = LANES // SWA_HD
    return (jnp.tile(jnp.concatenate([cos, cos], axis=1), (1, reps)),
            jnp.tile(jnp.concatenate([-sin, sin], axis=1), (1, reps)))


def _retention_tables(chunk, rows):
    lg = jnp.log1p(-jnp.exp2(-5.0 - jnp.arange(RET_HEADS, dtype=F32)))
    idx = jnp.arange(rows, dtype=F32)
    live = idx < chunk
    rel = idx[:, None] - idx[None, :]
    ok = (rel >= 0) & live[:, None] & live[None, :]
    dmat = jnp.where(ok[None], jnp.exp(jnp.maximum(rel, 0.0)[None] * lg[:, None, None]), 0.0)
    kdec = jnp.where(live[:, None], jnp.exp((chunk - 1.0 - idx)[:, None] * lg[None, :]), 0.0)
    qdec = jnp.where(live[:, None], jnp.exp((idx + 1.0)[:, None] * lg[None, :]), 0.0)
    g = jnp.exp(chunk * lg)
    kdec = jnp.repeat(kdec, RET_DK, axis=1)
    qdec = jnp.repeat(qdec, RET_DK, axis=1)
    gdec = jnp.broadcast_to(jnp.repeat(g, RET_DK)[:, None], (RET_HEADS * RET_DK, RET_DV))
    return dmat, kdec, qdec, gdec


def kernel(x_prompt, x_sample, state_ret, cache_swa_k, cache_swa_v, cache_mem_k, cache_mem_v,
           mem_prompt, w_in, w_mem_kv, w_out, ln_gain, ln_bias):
    depth = w_in.shape[0]
    assert depth == 1
    win_bf = w_in[0].astype(BF16)
    wmem_bf = w_mem_kv[0].astype(BF16)
    wout_bf = w_out[0].astype(BF16)
    gain = ln_gain[0].reshape(1, D_MODEL)
    bias = ln_bias[0].reshape(1, D_MODEL)

    xp = x_prompt.reshape(BATCH * SEQ, D_MODEL)
    cos_p, sin_p = _rope_tables(jnp.arange(SEQ))
    p_outs = (("rq", "rows", F32), ("rk", "rows", F32), ("rv", "rows", BF16), ("rg", "rows", BF16),
              ("sq", "perm", F32), ("sk", "cols", F32), ("sk", "perm", F32), ("sv", "cols", F32),
              ("sv", "perm", F32), ("sg", "rows", BF16), ("mq", "rows", BF16), ("mg", "rows", BF16))
    p_keys = ("rq", "rk", "rv", "rg", "sq4", "sk", "sk4", "sv", "sv4", "sg", "mq", "mg")
    pr = dict(zip(p_keys, _project(xp, win_bf, cos_p, sin_p, PERM_TILE, p_outs)))
    mk, mv = _memkv(mem_prompt.reshape(BATCH * N_MEM, D_MODEL), wmem_bf)
    slab = lambda a: a.reshape(SWA_W // LANES, BATCH, SEQ, LANES)
    swa_o = _swa_prompt(slab(pr["sq4"]), slab(pr["sk4"]), slab(pr["sv4"]))
    swa_o = swa_o.reshape(SWA_W // LANES, BATCH * SEQ, LANES)
    tabs_p = _retention_tables(RET_CHUNK, RET_CHUNK)
    yp, ret_p = _mix_prompt(xp, pr, swa_o, mk, mv, wout_bf, tabs_p, gain, bias, 512)

    xs = jnp.pad(x_sample, ((0, 0), (0, DEC_PAD - DEC_SEQ), (0, 0))).reshape(DEC_BATCH * DEC_PAD, D_MODEL)
    pos_s = PAST_LEN + jnp.arange(DEC_BATCH * DEC_PAD) % DEC_PAD
    cos_s, sin_s = _rope_tables(pos_s)
    names = [c[0] for c in _PROJ_COLS]
    ps = dict(zip(names, _project(xs, win_bf, cos_s, sin_s, DEC_BATCH * DEC_PAD,
                                  [(k, "rows", F32) for k in names])))
    tabs_s = _retention_tables(DEC_SEQ, DEC_PAD)
    mix_s, ret_s = _mix_sample(
        ps, state_ret[0].reshape(DEC_BATCH, RET_HEADS * RET_DK, RET_DV),
        cache_swa_k[0].transpose(0, 2, 3, 1), cache_swa_v[0].transpose(0, 2, 3, 1),
        cache_mem_k[0].reshape(DEC_BATCH, N_MEM, MEM_W), cache_mem_v[0].reshape(DEC_BATCH, N_MEM, MEM_W),
        tabs_s)
    ys = _finish(xs, mix_s, wout_bf, gain, bias)

    take = lambda a, w: a.reshape(DEC_BATCH, DEC_PAD, w)[:, :DEC_SEQ]
    swa_rows = lambda a: a.reshape(BATCH, SWA_HEADS, SWA_HD, SEQ).transpose(0, 3, 1, 2)[None]
    return (
        yp.reshape(BATCH, SEQ, D_MODEL),
        take(ys, D_MODEL),
        ret_p.reshape(1, BATCH, RET_HEADS, RET_DK, RET_DV),
        ret_s.reshape(1, DEC_BATCH, RET_HEADS, RET_DK, RET_DV),
        swa_rows(pr["sk"]),
        swa_rows(pr["sv"]),
        take(ps["sk"], SWA_W).reshape(1, DEC_BATCH, DEC_SEQ, SWA_HEADS, SWA_HD),
        take(ps["sv"], SWA_W).reshape(1, DEC_BATCH, DEC_SEQ, SWA_HEADS, SWA_HD),
        mk.reshape(1, BATCH, N_MEM, MEM_HEADS, MEM_HD),
        mv.reshape(1, BATCH, N_MEM, MEM_HEADS, MEM_HD),
    )
```

```python
import functools

import jax
import jax.numpy as jnp
from jax import lax
from jax.experimental import pallas as pl
from jax.experimental.pallas import tpu as pltpu

F32 = jnp.float32
BF16 = jnp.bfloat16

D_MODEL = 1024
BATCH = 8
SEQ = 2048
DEC_BATCH = 32
DEC_SEQ = 4
PAST_LEN = 8192
N_MEM = 256
MEM_HEADS = 4
MEM_HD = 128
RET_HEADS = 4
RET_DK = 64
RET_DV = 128
RET_CHUNK = 128
SWA_HEADS = 8
SWA_HD = 64
SWA_DILATIONS = (1, 4, 16)
SWA_STEPS = 128
SWA_BLOCK = 128
ROPE_THETA = 10000.0
LN_EPS = 1e-5
GN_EPS = 1e-5
RET_W = RET_HEADS * RET_DV
SWA_W = SWA_HEADS * SWA_HD
MEM_W = MEM_HEADS * MEM_HD
D_MIX = RET_W + SWA_W + MEM_W
DEEPNORM_ALPHA = 2.0 ** 0.25
MEM_SCALE = MEM_HD ** -0.5
QK_SCALE = 0.125

LANES = 128
DEC_PAD = 16
PERM_TILE = 512
PERM_PITCH = 40
PERM_ROWS = 16 * PERM_PITCH
OUT_ROWS = 256
VMEM_LIMIT = 56 * 1024 * 1024
NEG = -1e30

_PROJ_COLS = (
    ("rq", RET_HEADS * RET_DK, "rope", 1.0),
    ("rk", RET_HEADS * RET_DK, "rope", QK_SCALE),
    ("rv", RET_W, "id", 1.0),
    ("rg", RET_W, "silu", 1.0),
    ("sq", SWA_W, "rope", QK_SCALE),
    ("sk", SWA_W, "rope", 1.0),
    ("sv", SWA_W, "id", 1.0),
    ("sg", SWA_W, "silu", 1.0),
    ("mq", MEM_W, "id", 1.0),
    ("mg", MEM_W, "silu", 1.0),
)


def _dot(a, b):
    return jnp.dot(a, b, preferred_element_type=F32)


def _dot_nt(a, b):
    return lax.dot_general(a, b, (((1,), (1,)), ((), ())), preferred_element_type=F32)


def _dot_tn(a, b):
    return lax.dot_general(a, b, (((0,), (0,)), ((), ())), preferred_element_type=F32)


def _params(n_axes):
    return pltpu.CompilerParams(dimension_semantics=("arbitrary",) * n_axes,
                                vmem_limit_bytes=VMEM_LIMIT)


def _proj_kernel(x_ref, w_ref, cos_ref, sin_ref, *out_refs, dests):
    xb = x_ref[...].astype(BF16)
    cos = cos_ref[...]
    sin = sin_ref[...]
    lane = lax.broadcasted_iota(jnp.int32, cos.shape, 1)
    first_half = (lane % 64) < 32
    col = 0
    for name, width, kind, scale in _PROJ_COLS:
        targets = [(o_ref, layout) for o_ref, (dname, layout) in zip(out_refs, dests) if dname == name]
        for c in range(0, width, 2 * LANES):
            h2 = _dot(xb, w_ref[:, col + c:col + c + 2 * LANES])
            for half in range(2):
                h = h2[:, half * LANES:(half + 1) * LANES]
                if kind == "rope":
                    swapped = jnp.where(first_half, pltpu.roll(h, 96, 1), pltpu.roll(h, 32, 1))
                    h = h * cos + swapped * sin
                    if scale != 1.0:
                        h = h * scale
                elif kind == "silu":
                    h = h * (1.0 / (1.0 + jnp.exp(-h)))
                lo = c + half * LANES
                for o_ref, layout in targets:
                    if layout == "rows":
                        o_ref[:, lo:lo + LANES] = h.astype(o_ref.dtype)
                    elif layout == "perm":
                        slab = lo // LANES
                        for g in range(h.shape[0] // 8):
                            base = (g % 2) * 8 * PERM_PITCH + g // 2
                            o_ref[slab, pl.ds(base, 8, stride=PERM_PITCH), :] = (
                                h[8 * g:8 * g + 8].astype(o_ref.dtype))
                        n_live = h.shape[0] // 16
                        for cls in range(16):
                            o_ref[slab, pl.ds(cls * PERM_PITCH + n_live, PERM_PITCH - n_live), :] = (
                                jnp.zeros((PERM_PITCH - n_live, LANES), o_ref.dtype))
                    else:
                        o_ref[lo:lo + LANES, :] = h.T.astype(o_ref.dtype)
        col += width


def _project(x2d, w_bf, cos_t, sin_t, tile, outs):
    n = x2d.shape[0]
    seq = cos_t.shape[0]
    n_tab = seq // tile
    d_in = w_bf.shape[1]
    widths = {name: w for name, w, _, _ in _PROJ_COLS}
    out_shape, out_specs = [], []
    for name, layout, dt in outs:
        w = widths[name]
        if layout == "rows":
            out_shape.append(jax.ShapeDtypeStruct((n, w), dt))
            out_specs.append(pl.BlockSpec((tile, w), lambda i: (i, 0)))
        elif layout == "perm":
            assert tile == PERM_TILE
            out_shape.append(jax.ShapeDtypeStruct((w // LANES, n // tile * PERM_ROWS, LANES), dt))
            out_specs.append(pl.BlockSpec((w // LANES, PERM_ROWS, LANES), lambda i: (0, i, 0)))
        else:
            out_shape.append(jax.ShapeDtypeStruct((n // seq, w, seq), dt))
            out_specs.append(pl.BlockSpec((None, w, tile), lambda i: (i // n_tab, 0, i % n_tab)))
    return pl.pallas_call(
        functools.partial(_proj_kernel, dests=tuple((name, layout) for name, layout, _ in outs)),
        grid=(n // tile,),
        in_specs=[
            pl.BlockSpec((tile, D_MODEL), lambda i: (i, 0)),
            pl.BlockSpec((D_MODEL, d_in), lambda i: (0, 0)),
            pl.BlockSpec((tile, LANES), lambda i: (i % n_tab, 0)),
            pl.BlockSpec((tile, LANES), lambda i: (i % n_tab, 0)),
        ],
        out_specs=out_specs,
        out_shape=out_shape,
        compiler_params=_params(1),
        name="proj",
    )(x2d, w_bf, cos_t, sin_t)


def _memkv_kernel(m_ref, w_ref, mk_ref, mv_ref):
    mb = m_ref[...].astype(BF16)
    for h in range(MEM_HEADS):
        rows = pl.ds(h, N_MEM, stride=MEM_HEADS)
        mk_ref[rows, :] = _dot(mb, w_ref[:, h * MEM_HD:(h + 1) * MEM_HD])
        mv_ref[rows, :] = _dot(mb, w_ref[:, MEM_W + h * MEM_HD:MEM_W + (h + 1) * MEM_HD])


def _memkv(mem2d, w_bf):
    n = mem2d.shape[0] // N_MEM
    out = pl.BlockSpec((None, N_MEM * MEM_HEADS, MEM_HD), lambda i: (i, 0, 0))
    return pl.pallas_call(
        _memkv_kernel,
        grid=(n,),
        in_specs=[pl.BlockSpec((N_MEM, D_MODEL), lambda i: (i, 0)),
                  pl.BlockSpec((D_MODEL, 2 * MEM_W), lambda i: (0, 0))],
        out_specs=[out, out],
        out_shape=[jax.ShapeDtypeStruct((n, N_MEM * MEM_HEADS, MEM_HD), F32)] * 2,
        compiler_params=_params(1),
        name="memkv",
    )(mem2d, w_bf)


def _swa_prompt_kernel(q_ref, k_ref, v_ref, o_ref, a_ref, m_ref, l_ref):
    blk = SWA_BLOCK
    pt = PERM_TILE
    grp = pt // 16
    lo = lax.broadcasted_iota(jnp.int32, (blk, blk), 1) < SWA_HD

    def biases(seq_of):
        rq = seq_of(lax.broadcasted_iota(jnp.int32, (blk, 2 * blk), 0))
        c2 = lax.broadcasted_iota(jnp.int32, (blk, 2 * blk), 1)
        rel = blk + rq - (seq_of(c2 % blk) + blk * (c2 // blk))
        prev = jnp.where((rel >= 0) & (rel <= SWA_STEPS), 0.0, NEG).astype(F32)
        own = jnp.where((rel >= 0) & (rel <= SWA_STEPS) & (c2 >= blk), 0.0, NEG).astype(F32)
        return prev, own

    def gather(ref, pair, starts, n):
        return jnp.concatenate([ref[pair, pl.ds(s, n), :] for s in starts], axis=0)

    def aligned(x):
        return x if isinstance(x, int) else pl.multiple_of(x, 8)

    def block(chunks, prev_chunks, n, bias, first_pattern, last_pattern, out_start=None):
        n_pairs = SWA_HEADS // 2
        in_at = lambda c: aligned(c[0] * PERM_ROWS + c[1] * PERM_PITCH + c[2])
        starts = [aligned(c[0] * pt + c[1] * grp + c[2]) for c in chunks]
        q_starts = [in_at(c) for c in chunks]
        k_starts = [in_at(c) for c in (chunks if prev_chunks is None else prev_chunks)] + q_starts
        olds = None
        if not first_pattern:
            olds = [(gather(m_ref, pair, starts, n), gather(l_ref, pair, starts, n),
                     gather(a_ref, pair, starts, n)) for pair in range(n_pairs)]
        heads = [(pair, hh) for pair in range(n_pairs) for hh in range(2)]
        vbs = [gather(v_ref, pair, k_starts, n).astype(BF16) for pair in range(n_pairs)]
        scores = []
        for pair in range(n_pairs):
            q = gather(q_ref, pair, q_starts, n)
            kb = gather(k_ref, pair, k_starts, n).astype(BF16)
            for hh in range(2):
                qm = jnp.where(lo if hh == 0 else ~lo, q, 0.0).astype(BF16)
                scores.append(_dot_nt(qm, kb) + bias)
        ms = [jnp.max(s, axis=-1, keepdims=True) for s in scores]
        ps = [jnp.exp(s - m) for s, m in zip(scores, ms)]
        ls = [jnp.sum(p, axis=-1, keepdims=True) for p in ps]
        pvs = [_dot(p.astype(BF16), vbs[pair]) for p, (pair, _) in zip(ps, heads)]
        news = []
        for pair in range(n_pairs):
            i0, i1 = 2 * pair, 2 * pair + 1
            m_g = jnp.where(lo, ms[i0], ms[i1])
            l_g = jnp.where(lo, ls[i0], ls[i1])
            a_g = jnp.where(lo, pvs[i0], pvs[i1])
            if first_pattern:
                m_new, l_new, a_new = m_g, l_g, a_g
            else:
                m_old, l_old, a_old = olds[pair]
                m_new = jnp.maximum(m_old, m_g)
                w_old = jnp.exp(m_old - m_new)
                w_g = jnp.exp(m_g - m_new)
                l_new = w_old * l_old + w_g * l_g
                a_new = w_old * a_old + w_g * a_g
            news.append((m_new, l_new, a_new))
        for pair, (m_new, l_new, a_new) in enumerate(news):
            if last_pattern:
                o = a_new / l_new
                for i in range(len(starts)):
                    o_ref[pair, pl.ds(out_start + i, n, stride=len(starts)), :] = o[i * n:(i + 1) * n]
            else:
                for i, s in enumerate(starts):
                    m_ref[pair, pl.ds(s, n), :] = m_new[i * n:(i + 1) * n]
                    l_ref[pair, pl.ds(s, n), :] = l_new[i * n:(i + 1) * n]
                    a_ref[pair, pl.ds(s, n), :] = a_new[i * n:(i + 1) * n]

    _, bias16 = biases(lambda i: i)

    def class16(r, carry):
        block([(t, r, 0) for t in range(SEQ // pt)], None, grp, bias16, True, False)
        return carry

    lax.fori_loop(0, 16, class16, 0)

    bias4_prev, bias4_own = biases(lambda i: 4 * (i % grp) + i // grp)

    def class4(r, carry):
        chunks_of = lambda tile: [(tile, 4 * a + r, 0) for a in range(4)]
        block(chunks_of(0), None, grp, bias4_own, False, False)

        def per_block(b, c2):
            block(chunks_of(b), chunks_of(b - 1), grp, bias4_prev, False, False)
            return c2

        lax.fori_loop(1, SEQ // pt, per_block, 0)
        return carry

    lax.fori_loop(0, 4, class4, 0)

    bias1_prev, bias1_own = biases(lambda i: 16 * (i % 8) + i // 8)
    per_tile = pt // blk
    chunks1 = lambda c: [(c // per_tile, i, (c % per_tile) * 8) for i in range(16)]
    block(chunks1(0), None, 8, bias1_own, False, True, out_start=0)

    def block1(c, carry):
        block(chunks1(c), chunks1(c - 1), 8, bias1_prev, False, True, out_start=pl.multiple_of(c * blk, blk))
        return carry

    lax.fori_loop(1, SEQ // blk, block1, 0)


def _swa_prompt(sq, sk, sv):
    slabs = SWA_W // LANES
    spec = pl.BlockSpec((slabs, None, SEQ, LANES), lambda b: (0, b, 0, 0))
    in_spec = pl.BlockSpec((slabs, None, sq.shape[2], LANES), lambda b: (0, b, 0, 0))
    return pl.pallas_call(
        _swa_prompt_kernel,
        grid=(BATCH,),
        in_specs=[in_spec, in_spec, in_spec],
        out_specs=spec,
        out_shape=jax.ShapeDtypeStruct((slabs, BATCH, SEQ, LANES), F32),
        scratch_shapes=[pltpu.VMEM((slabs, SEQ, LANES), F32)] * 3,
        compiler_params=_params(1),
        name="swa_prompt",
    )(sq, sk, sv)


def _head_norm(o):
    mu = jnp.mean(o, axis=-1, keepdims=True)
    d = o - mu
    var = jnp.mean(d * d, axis=-1, keepdims=True)
    return d * lax.rsqrt(var + GN_EPS)


def _deepnorm_ln(x, h, gain, bias):
    z = DEEPNORM_ALPHA * x + h
    mu = jnp.mean(z, axis=-1, keepdims=True)
    d = z - mu
    var = jnp.mean(d * d, axis=-1, keepdims=True)
    return d * lax.rsqrt(var + LN_EPS) * gain + bias


def _softmax_rows(s):
    m = jnp.max(s, axis=-1, keepdims=True)
    p = jnp.exp(s - m)
    return p * (1.0 / jnp.sum(p, axis=-1, keepdims=True))


def _mix_prompt_kernel(x_ref, rq_ref, rk_ref, rv_ref, rg_ref, so_ref, sg_ref, mq_ref, mg_ref,
                       mk_ref, mv_ref, wout_ref, dmat_ref, kdec_ref, qdec_ref, gdec_ref,
                       gain_ref, bias_ref, y_ref, state_out_ref,
                       state_ref, mix_ref, mkb_ref, mvb_ref, *, tile):
    t = pl.program_id(1)

    @pl.when(t == 0)
    def _():
        state_ref[...] = jnp.zeros_like(state_ref)
        for h in range(MEM_HEADS):
            rows = pl.ds(h, N_MEM, stride=MEM_HEADS)
            mkb_ref[:, pl.ds(h * MEM_HD, MEM_HD)] = mk_ref[rows, :].astype(BF16)
            mvb_ref[:, pl.ds(h * MEM_HD, MEM_HD)] = mv_ref[rows, :].astype(BF16)

    ck = RET_CHUNK
    lane = lax.broadcasted_iota(jnp.int32, (ck, LANES), 1)
    lo = lane < RET_DK
    top = lax.broadcasted_iota(jnp.int32, (LANES, LANES), 0) < RET_DK
    gain = gain_ref[...]
    bias = bias_ref[...]

    n_ck = tile // ck
    n_pairs = RET_HEADS // 2
    rows_of = [pl.ds(c * ck, ck) for c in range(n_ck)]
    items = [(c, pair, hh) for c in range(n_ck) for pair in range(n_pairs) for hh in range(2)]
    hs_of = lambda pair, hh: pl.ds((2 * pair + hh) * RET_DV, RET_DV)
    qk = {}
    for c in range(n_ck):
        for pair in range(n_pairs):
            cs = pl.ds(pair * LANES, LANES)
            q = rq_ref[rows_of[c], cs]
            k = rk_ref[rows_of[c], cs]
            qk[c, pair] = (q, k.astype(BF16), (k * kdec_ref[:, cs]).astype(BF16), q * qdec_ref[:, cs])
    sel = lambda x, hh: jnp.where(lo if hh == 0 else ~lo, x, 0.0).astype(BF16)
    s = {(c, pair, hh): _dot_nt(sel(qk[c, pair][0], hh), qk[c, pair][1]) * dmat_ref[2 * pair + hh]
         for c, pair, hh in items}
    kv = {(c, pair, hh): _dot_tn(qk[c, pair][2], rv_ref[rows_of[c], hs_of(pair, hh)]) for c, pair, hh in items}
    intra = {(c, pair, hh): _dot(s[c, pair, hh].astype(BF16), rv_ref[rows_of[c], hs_of(pair, hh)])
             for c, pair, hh in items}
    state = {(0, pair): state_ref[pair] for pair in range(n_pairs)}
    for c in range(n_ck):
        for pair in range(n_pairs):
            state[c + 1, pair] = (gdec_ref[pl.ds(pair * LANES, LANES), :] * state[c, pair]
                                  + jnp.where(top, kv[c, pair, 0], kv[c, pair, 1]))
    for pair in range(n_pairs):
        state_ref[pair] = state[n_ck, pair]
    cross = {(c, pair, hh): _dot(sel(qk[c, pair][3], hh), state[c, pair].astype(BF16)) for c, pair, hh in items}
    for c, pair, hh in items:
        hs = hs_of(pair, hh)
        o = intra[c, pair, hh] + cross[c, pair, hh]
        mix_ref[rows_of[c], hs] = (rg_ref[rows_of[c], hs].astype(F32) * _head_norm(o)).astype(BF16)
    for pair in range(SWA_W // LANES):
        cs = pl.ds(pair * LANES, LANES)
        mix_ref[:, pl.ds(RET_W + pair * LANES, LANES)] = (
            sg_ref[:, cs].astype(F32) * so_ref[pair]).astype(BF16)
    for c in range(n_ck):
        ps = [_softmax_rows(_dot_nt(mq_ref[rows_of[c], pl.ds(h * MEM_HD, MEM_HD)],
                                    mkb_ref[:, pl.ds(h * MEM_HD, MEM_HD)]) * MEM_SCALE)
              for h in range(MEM_HEADS)]
        for h in range(MEM_HEADS):
            hs = pl.ds(h * MEM_HD, MEM_HD)
            o = _dot(ps[h].astype(BF16), mvb_ref[:, hs])
            mix_ref[rows_of[c], pl.ds(RET_W + SWA_W + h * MEM_HD, MEM_HD)] = (
                mg_ref[rows_of[c], hs].astype(F32) * o).astype(BF16)
    ob = OUT_ROWS
    for r in range(tile // ob):
        rows = pl.ds(r * ob, ob)
        hout = _dot(mix_ref[rows, :], wout_ref[...])
        y_ref[rows, :] = _deepnorm_ln(x_ref[rows, :], hout, gain, bias)

    @pl.when(t == pl.num_programs(1) - 1)
    def _():
        state_out_ref[pl.ds(0, LANES), :] = state_ref[0]
        state_out_ref[pl.ds(LANES, LANES), :] = state_ref[1]


def _mix_prompt(x2d, pr, swa_o, mk, mv, wout_bf, tabs, gain, bias, tile):
    nt = SEQ // tile
    row = lambda w: pl.BlockSpec((tile, w), lambda b, t: (b * nt + t, 0))
    const2 = lambda a: pl.BlockSpec(a.shape, lambda b, t: (0, 0))
    dmat, kdec, qdec, gdec = tabs
    return pl.pallas_call(
        functools.partial(_mix_prompt_kernel, tile=tile),
        grid=(BATCH, nt),
        in_specs=[
            row(D_MODEL), row(RET_HEADS * RET_DK), row(RET_HEADS * RET_DK), row(RET_W), row(RET_W),
            pl.BlockSpec((SWA_W // LANES, tile, LANES), lambda b, t: (0, b * nt + t, 0)),
            row(SWA_W), row(MEM_W), row(MEM_W),
            pl.BlockSpec((None, N_MEM * MEM_HEADS, MEM_HD), lambda b, t: (b, 0, 0)),
            pl.BlockSpec((None, N_MEM * MEM_HEADS, MEM_HD), lambda b, t: (b, 0, 0)),
            const2(wout_bf),
            pl.BlockSpec(dmat.shape, lambda b, t: (0, 0, 0)),
            const2(kdec), const2(qdec), const2(gdec), const2(gain), const2(bias),
        ],
        out_specs=[row(D_MODEL),
                   pl.BlockSpec((None, RET_HEADS * RET_DK, RET_DV), lambda b, t: (b, 0, 0))],
        out_shape=[jax.ShapeDtypeStruct((BATCH * SEQ, D_MODEL), F32),
                   jax.ShapeDtypeStruct((BATCH, RET_HEADS * RET_DK, RET_DV), F32)],
        scratch_shapes=[pltpu.VMEM((RET_HEADS // 2, LANES, RET_DV), F32),
                        pltpu.VMEM((tile, D_MIX), BF16),
                        pltpu.VMEM((N_MEM, MEM_W), BF16),
                        pltpu.VMEM((N_MEM, MEM_W), BF16)],
        compiler_params=_params(2),
        name="mix_prompt",
    )(x2d, pr["rq"], pr["rk"], pr["rv"], pr["rg"], swa_o, pr["sg"], pr["mq"], pr["mg"],
      mk, mv, wout_bf, dmat, kdec, qdec, gdec, gain, bias)


def _mix_sample_kernel(rq_ref, rk_ref, rv_ref, rg_ref, sq_ref, sk_ref, sv_ref, sg_ref, mq_ref, mg_ref,
                       state_ref, ckt_ref, cvt_ref, cmk_ref, cmv_ref,
                       dmat_ref, kdec_ref, qdec_ref, gdec_ref,
                       mix_ref, state_out_ref):
    n = DEC_PAD
    q = rq_ref[...]
    k = rk_ref[...]
    kb = k.astype(BF16)
    kd = (k * kdec_ref[...]).astype(BF16)
    qd = q * qdec_ref[...]
    st = state_ref[...]
    stb = st.astype(BF16)
    vb = rv_ref[...].astype(BF16)
    kv = _dot_tn(kd, vb)
    lane_qk = lax.broadcasted_iota(jnp.int32, q.shape, 1) // RET_DK
    for h in range(RET_HEADS):
        hs = pl.ds(h * RET_DV, RET_DV)
        ks = pl.ds(h * RET_DK, RET_DK)
        msk = lane_qk == h
        s = _dot_nt(jnp.where(msk, q, 0.0).astype(BF16), kb) * dmat_ref[h]
        o = _dot(s.astype(BF16), vb[:, h * RET_DV:(h + 1) * RET_DV])
        o = o + _dot(jnp.where(msk, qd, 0.0).astype(BF16), stb)
        mix_ref[:, hs] = rg_ref[:, hs] * _head_norm(o)
        state_out_ref[ks, :] = (gdec_ref[ks, :] * st[h * RET_DK:(h + 1) * RET_DK, :]
                                + kv[h * RET_DK:(h + 1) * RET_DK, h * RET_DV:(h + 1) * RET_DV])

    blk = SWA_BLOCK
    wb = ckt_ref.shape[2]
    rt = 8
    lo = lax.broadcasted_iota(jnp.int32, (n, LANES), 1) < SWA_HD
    lo_t = lax.broadcasted_iota(jnp.int32, (rt, LANES), 1) < SWA_HD
    tok = lambda w: lax.broadcasted_iota(jnp.int32, (rt, w), 0)
    col = lambda w: lax.broadcasted_iota(jnp.int32, (rt, w), 1)
    windows = (
        (wb - blk, jnp.where(col(blk) >= tok(blk), 0.0, NEG), jnp.where(col(blk) <= tok(blk), 0.0, NEG)),
        (wb - 4 * blk, jnp.where(col(4 * blk) % 4 == tok(4 * blk), 0.0, NEG),
         jnp.where(col(blk) == tok(blk), 0.0, NEG)),
        (0, jnp.where(col(wb) % 16 == tok(wb), 0.0, NEG), jnp.where(col(blk) == tok(blk), 0.0, NEG)),
    )
    pad = jnp.zeros((blk - n, LANES), BF16)
    zero_c = jnp.zeros((rt, wb), F32)
    zero_n = jnp.zeros((rt, blk), F32)
    pieces = []
    for pair in range(SWA_HEADS // 2):
        cs = pl.ds(pair * LANES, LANES)
        kp = ckt_ref[2 * pair:2 * pair + 2].reshape(2 * SWA_HD, wb).astype(BF16)
        vp = cvt_ref[2 * pair:2 * pair + 2].reshape(2 * SWA_HD, wb).astype(BF16)
        k_new = jnp.concatenate([sk_ref[:, cs].astype(BF16), pad], axis=0)
        v_new = jnp.concatenate([sv_ref[:, cs].astype(BF16), pad], axis=0)
        q = sq_ref[:, cs]
        qs = jnp.concatenate([jnp.where(lo, q, 0.0), jnp.where(lo, 0.0, q)], axis=0).astype(BF16)
        s_all = _dot(qs, kp)
        sn_all = _dot_nt(qs, k_new)
        p_rows, pn_rows, stats = [], [], []
        for hh in range(2):
            s = s_all[hh * n:hh * n + rt]
            sn = sn_all[hh * n:hh * n + rt]
            for w0, bias_c, bias_n in windows:
                sc = s[:, w0:] + bias_c
                snb = sn + bias_n
                m = jnp.maximum(jnp.max(sc, axis=-1, keepdims=True), jnp.max(snb, axis=-1, keepdims=True))
                pc = jnp.exp(sc - m)
                pn = jnp.exp(snb - m)
                stats.append((m, jnp.sum(pc, axis=-1, keepdims=True) + jnp.sum(pn, axis=-1, keepdims=True)))
                if w0:
                    pc = jnp.concatenate([jnp.zeros((rt, w0), F32), pc], axis=1)
                p_rows.append(pc)
                pn_rows.append(pn)
            p_rows.append(zero_c)
            pn_rows.append(zero_n)
        pv = (_dot_nt(jnp.concatenate(p_rows, axis=0).astype(BF16), vp)
              + _dot(jnp.concatenate(pn_rows, axis=0).astype(BF16), v_new))
        heads = []
        for hh in range(2):
            parts = []
            for i in range(len(windows)):
                m, l = stats[hh * len(windows) + i]
                r0 = (hh * (len(windows) + 1) + i) * rt
                parts.append((pv[r0:r0 + rt] / l, m, l))
            m_all = jnp.maximum(jnp.maximum(parts[0][1], parts[1][1]), parts[2][1])
            ws = [l * jnp.exp(m - m_all) for (_, m, l) in parts]
            heads.append((ws[0] * parts[0][0] + ws[1] * parts[1][0] + ws[2] * parts[2][0])
                         / (ws[0] + ws[1] + ws[2]))
        pieces.append(jnp.where(lo_t, heads[0], heads[1]))
    swa = jnp.concatenate(pieces, axis=1)
    swa = jnp.concatenate([swa, jnp.zeros((n - rt, SWA_W), F32)], axis=0)
    mix_ref[:, pl.ds(RET_W, SWA_W)] = sg_ref[...] * swa

    for h in range(MEM_HEADS):
        hs = pl.ds(h * MEM_HD, MEM_HD)
        rows = pl.ds(h, N_MEM, stride=MEM_HEADS)
        p = _softmax_rows(_dot_nt(mq_ref[:, hs].astype(BF16), cmk_ref[rows, :].astype(BF16)) * MEM_SCALE)
        o = _dot(p.astype(BF16), cmv_ref[rows, :].astype(BF16))
        mix_ref[:, pl.ds(RET_W + SWA_W + h * MEM_HD, MEM_HD)] = mg_ref[:, hs] * o


def _mix_sample(ps, state, ckt, cvt, cmk, cmv, tabs):
    n = DEC_PAD
    wb = ckt.shape[3]
    row = lambda w: pl.BlockSpec((n, w), lambda b: (b, 0))
    st_spec = pl.BlockSpec((None, RET_HEADS * RET_DK, RET_DV), lambda b: (b, 0, 0))
    cache_spec = pl.BlockSpec((None, SWA_HEADS, SWA_HD, wb), lambda b: (b, 0, 0, 0))
    mem_spec = pl.BlockSpec((None, N_MEM * MEM_HEADS, MEM_HD), lambda b: (b, 0, 0))
    dmat, kdec, qdec, gdec = tabs
    return pl.pallas_call(
        _mix_sample_kernel,
        grid=(DEC_BATCH,),
        in_specs=[
            row(RET_HEADS * RET_DK), row(RET_HEADS * RET_DK), row(RET_W), row(RET_W),
            row(SWA_W), row(SWA_W), row(SWA_W), row(SWA_W), row(MEM_W), row(MEM_W),
            st_spec, cache_spec, cache_spec, mem_spec, mem_spec,
            pl.BlockSpec(dmat.shape, lambda b: (0, 0, 0)),
            pl.BlockSpec(kdec.shape, lambda b: (0, 0)),
            pl.BlockSpec(qdec.shape, lambda b: (0, 0)),
            pl.BlockSpec(gdec.shape, lambda b: (0, 0)),
        ],
        out_specs=[row(D_MIX), st_spec],
        out_shape=[jax.ShapeDtypeStruct((DEC_BATCH * n, D_MIX), F32),
                   jax.ShapeDtypeStruct((DEC_BATCH, RET_HEADS * RET_DK, RET_DV), F32)],
        compiler_params=_params(1),
        name="mix_sample",
    )(ps["rq"], ps["rk"], ps["rv"], ps["rg"], ps["sq"], ps["sk"], ps["sv"], ps["sg"], ps["mq"], ps["mg"],
      state, ckt, cvt, cmk, cmv, dmat, kdec, qdec, gdec)


def _finish_kernel(x_ref, mix_ref, wout_ref, gain_ref, bias_ref, y_ref):
    hout = _dot(mix_ref[...].astype(BF16), wout_ref[...])
    y_ref[...] = _deepnorm_ln(x_ref[...], hout, gain_ref[...], bias_ref[...])


def _finish(x2d, mix, wout_bf, gain, bias):
    n = x2d.shape[0]
    tile = 256
    const2 = lambda a: pl.BlockSpec(a.shape, lambda i: (0, 0))
    return pl.pallas_call(
        _finish_kernel,
        grid=(n // tile,),
        in_specs=[pl.BlockSpec((tile, D_MODEL), lambda i: (i, 0)),
                  pl.BlockSpec((tile, D_MIX), lambda i: (i, 0)),
                  const2(wout_bf), const2(gain), const2(bias)],
        out_specs=pl.BlockSpec((tile, D_MODEL), lambda i: (i, 0)),
        out_shape=jax.ShapeDtypeStruct((n, D_MODEL), F32),
        compiler_params=_params(1),
        name="finish",
    )(x2d, mix, wout_bf, gain, bias)


def _rope_tables(pos):
    half = SWA_HD // 2
    inv = ROPE_THETA ** (-jnp.arange(half, dtype=F32) * 2.0 / SWA_HD)
    ang = pos.astype(F32)[:, None] * inv[None, :]
    cos = jnp.cos(ang)
    sin = jnp.sin(ang)
    reps = LANES // SWA_HD
    return (jnp.tile(jnp.concatenate([cos, cos], axis=1), (1, reps)),
            jnp.tile(jnp.concatenate([-sin, sin], axis=1), (1, reps)))


def _retention_tables(chunk, rows):
    lg = jnp.log1p(-jnp.exp2(-5.0 - jnp.arange(RET_HEADS, dtype=F32)))
    idx = jnp.arange(rows, dtype=F32)
    live = idx < chunk
    rel = idx[:, None] - idx[None, :]
    ok = (rel >= 0) & live[:, None] & live[None, :]
    dmat = jnp.where(ok[None], jnp.exp(jnp.maximum(rel, 0.0)[None] * lg[:, None, None]), 0.0)
    kdec = jnp.where(live[:, None], jnp.exp((chunk - 1.0 - idx)[:, None] * lg[None, :]), 0.0)
    qdec = jnp.where(live[:, None], jnp.exp((idx + 1.0)[:, None] * lg[None, :]), 0.0)
    g = jnp.exp(chunk * lg)
    kdec = jnp.repeat(kdec, RET_DK, axis=1)
    qdec = jnp.repeat(qdec, RET_DK, axis=1)
    gdec = jnp.broadcast_to(jnp.repeat(g, RET_DK)[:, None], (RET_HEADS * RET_DK, RET_DV))
    return dmat, kdec, qdec, gdec


def kernel(x_prompt, x_sample, state_ret, cache_swa_k, cache_swa_v, cache_mem_k, cache_mem_v,
           mem_prompt, w_in, w_mem_kv, w_out, ln_gain, ln_bias):
    depth = w_in.shape[0]
    assert depth == 1
    win_bf = w_in[0].astype(BF16)
    wmem_bf = w_mem_kv[0].astype(BF16)
    wout_bf = w_out[0].astype(BF16)
    gain = ln_gain[0].reshape(1, D_MODEL)
    bias = ln_bias[0].reshape(1, D_MODEL)

    xp = x_prompt.reshape(BATCH * SEQ, D_MODEL)
    cos_p, sin_p = _rope_tables(jnp.arange(SEQ))
    p_outs = (("rq", "rows", F32), ("rk", "rows", F32), ("rv", "rows", BF16), ("rg", "rows", BF16),
              ("sq", "perm", F32), ("sk", "cols", F32), ("sk", "perm", F32), ("sv", "cols", F32),
              ("sv", "perm", F32), ("sg", "rows", BF16), ("mq", "rows", BF16), ("mg", "rows", BF16))
    p_keys = ("rq", "rk", "rv", "rg", "sq4", "sk", "sk4", "sv", "sv4", "sg", "mq", "mg")
    pr = dict(zip(p_keys, _project(xp, win_bf, cos_p, sin_p, PERM_TILE, p_outs)))
    mk, mv = _memkv(mem_prompt.reshape(BATCH * N_MEM, D_MODEL), wmem_bf)
    slab = lambda a: a.reshape(SWA_W // LANES, BATCH, SEQ // PERM_TILE * PERM_ROWS, LANES)
    swa_o = _swa_prompt(slab(pr["sq4"]), slab(pr["sk4"]), slab(pr["sv4"]))
    swa_o = swa_o.reshape(SWA_W // LANES, BATCH * SEQ, LANES)
    tabs_p = _retention_tables(RET_CHUNK, RET_CHUNK)
    yp, ret_p = _mix_prompt(xp, pr, swa_o, mk, mv, wout_bf, tabs_p, gain, bias, 512)

    xs = jnp.pad(x_sample, ((0, 0), (0, DEC_PAD - DEC_SEQ), (0, 0))).reshape(DEC_BATCH * DEC_PAD, D_MODEL)
    pos_s = PAST_LEN + jnp.arange(DEC_BATCH * DEC_PAD) % DEC_PAD
    cos_s, sin_s = _rope_tables(pos_s)
    names = [c[0] for c in _PROJ_COLS]
    ps = dict(zip(names, _project(xs, win_bf, cos_s, sin_s, DEC_BATCH * DEC_PAD,
                                  [(k, "rows", F32) for k in names])))
    tabs_s = _retention_tables(DEC_SEQ, DEC_PAD)
    mix_s, ret_s = _mix_sample(
        ps, state_ret[0].reshape(DEC_BATCH, RET_HEADS * RET_DK, RET_DV),
        cache_swa_k[0].transpose(0, 2, 3, 1), cache_swa_v[0].transpose(0, 2, 3, 1),
        cache_mem_k[0].reshape(DEC_BATCH, N_MEM * MEM_HEADS, MEM_HD),
        cache_mem_v[0].reshape(DEC_BATCH, N_MEM * MEM_HEADS, MEM_HD),
        tabs_s)
    ys = _finish(xs, mix_s, wout_bf, gain, bias)

    take = lambda a, w: a.reshape(DEC_BATCH, DEC_PAD, w)[:, :DEC_SEQ]
    swa_rows = lambda a: a.reshape(BATCH, SWA_HEADS, SWA_HD, SEQ).transpose(0, 3, 1, 2)[None]
    return (
        yp.reshape(BATCH, SEQ, D_MODEL),
        take(ys, D_MODEL),
        ret_p.reshape(1, BATCH, RET_HEADS, RET_DK, RET_DV),
        ret_s.reshape(1, DEC_BATCH, RET_HEADS, RET_DK, RET_DV),
        swa_rows(pr["sk"]),
        swa_rows(pr["sv"]),
        take(ps["sk"], SWA_W).reshape(1, DEC_BATCH, DEC_SEQ, SWA_HEADS, SWA_HD),
        take(ps["sv"], SWA_W).reshape(1, DEC_BATCH, DEC_SEQ, SWA_HEADS, SWA_HD),
        mk.reshape(1, BATCH, N_MEM, MEM_HEADS, MEM_HD),
        mv.reshape(1, BATCH, N_MEM, MEM_HEADS, MEM_HD),
    )
```

```python
import functools

import jax
import jax.numpy as jnp
from jax import lax
from jax.experimental import pallas as pl
from jax.experimental.pallas import tpu as pltpu

F32 = jnp.float32
BF16 = jnp.bfloat16

D_MODEL = 1024
BATCH = 8
SEQ = 2048
DEC_BATCH = 32
DEC_SEQ = 4
PAST_LEN = 8192
N_MEM = 256
MEM_HEADS = 4
MEM_HD = 128
RET_HEADS = 4
RET_DK = 64
RET_DV = 128
RET_CHUNK = 128
SWA_HEADS = 8
SWA_HD = 64
SWA_DILATIONS = (1, 4, 16)
SWA_STEPS = 128
SWA_BLOCK = 128
ROPE_THETA = 10000.0
LN_EPS = 1e-5
GN_EPS = 1e-5
RET_W = RET_HEADS * RET_DV
SWA_W = SWA_HEADS * SWA_HD
MEM_W = MEM_HEADS * MEM_HD
D_MIX = RET_W + SWA_W + MEM_W
DEEPNORM_ALPHA = 2.0 ** 0.25
MEM_SCALE = MEM_HD ** -0.5
QK_SCALE = 0.125

LANES = 128
DEC_PAD = 16
PERM_TILE = 512
PERM_PITCH = 40
PERM_ROWS = 16 * PERM_PITCH
OUT_ROWS = 256
VMEM_LIMIT = 56 * 1024 * 1024
NEG = -1e30

_PROJ_COLS = (
    ("rq", RET_HEADS * RET_DK, "rope", 1.0),
    ("rk", RET_HEADS * RET_DK, "rope", QK_SCALE),
    ("rv", RET_W, "id", 1.0),
    ("rg", RET_W, "silu", 1.0),
    ("sq", SWA_W, "rope", QK_SCALE),
    ("sk", SWA_W, "rope", 1.0),
    ("sv", SWA_W, "id", 1.0),
    ("sg", SWA_W, "silu", 1.0),
    ("mq", MEM_W, "id", 1.0),
    ("mg", MEM_W, "silu", 1.0),
)


def _dot(a, b):
    return jnp.dot(a, b, preferred_element_type=F32)


def _dot_nt(a, b):
    return lax.dot_general(a, b, (((1,), (1,)), ((), ())), preferred_element_type=F32)


def _dot_tn(a, b):
    return lax.dot_general(a, b, (((0,), (0,)), ((), ())), preferred_element_type=F32)


def _params(n_axes):
    return pltpu.CompilerParams(dimension_semantics=("arbitrary",) * n_axes,
                                vmem_limit_bytes=VMEM_LIMIT)


def _proj_kernel(x_ref, w_ref, cos_ref, sin_ref, *out_refs, dests):
    xb = x_ref[...].astype(BF16)
    cos = cos_ref[...]
    sin = sin_ref[...]
    lane = lax.broadcasted_iota(jnp.int32, cos.shape, 1)
    first_half = (lane % 64) < 32
    col = 0
    for name, width, kind, scale in _PROJ_COLS:
        targets = [(o_ref, layout) for o_ref, (dname, layout) in zip(out_refs, dests) if dname == name]
        for c in range(0, width, 2 * LANES):
            h2 = _dot(xb, w_ref[:, col + c:col + c + 2 * LANES])
            for half in range(2):
                h = h2[:, half * LANES:(half + 1) * LANES]
                if kind == "rope":
                    swapped = jnp.where(first_half, pltpu.roll(h, 96, 1), pltpu.roll(h, 32, 1))
                    h = h * cos + swapped * sin
                    if scale != 1.0:
                        h = h * scale
                elif kind == "silu":
                    h = h * (1.0 / (1.0 + jnp.exp(-h)))
                lo = c + half * LANES
                for o_ref, layout in targets:
                    if layout == "rows":
                        o_ref[:, lo:lo + LANES] = h.astype(o_ref.dtype)
                    elif layout == "perm":
                        slab = lo // LANES
                        for g in range(h.shape[0] // 8):
                            base = (g % 2) * 8 * PERM_PITCH + g // 2
                            o_ref[slab, pl.ds(base, 8, stride=PERM_PITCH), :] = (
                                h[8 * g:8 * g + 8].astype(o_ref.dtype))
                        n_live = h.shape[0] // 16
                        for cls in range(16):
                            o_ref[slab, pl.ds(cls * PERM_PITCH + n_live, PERM_PITCH - n_live), :] = (
                                jnp.zeros((PERM_PITCH - n_live, LANES), o_ref.dtype))
                    else:
                        o_ref[lo:lo + LANES, :] = h.T.astype(o_ref.dtype)
        col += width


def _project(x2d, w_bf, cos_t, sin_t, tile, outs):
    n = x2d.shape[0]
    seq = cos_t.shape[0]
    n_tab = seq // tile
    d_in = w_bf.shape[1]
    widths = {name: w for name, w, _, _ in _PROJ_COLS}
    out_shape, out_specs = [], []
    for name, layout, dt in outs:
        w = widths[name]
        if layout == "rows":
            out_shape.append(jax.ShapeDtypeStruct((n, w), dt))
            out_specs.append(pl.BlockSpec((tile, w), lambda i: (i, 0)))
        elif layout == "perm":
            assert tile == PERM_TILE
            out_shape.append(jax.ShapeDtypeStruct((w // LANES, n // tile * PERM_ROWS, LANES), dt))
            out_specs.append(pl.BlockSpec((w // LANES, PERM_ROWS, LANES), lambda i: (0, i, 0)))
        else:
            out_shape.append(jax.ShapeDtypeStruct((n // seq, w, seq), dt))
            out_specs.append(pl.BlockSpec((None, w, tile), lambda i: (i // n_tab, 0, i % n_tab)))
    return pl.pallas_call(
        functools.partial(_proj_kernel, dests=tuple((name, layout) for name, layout, _ in outs)),
        grid=(n // tile,),
        in_specs=[
            pl.BlockSpec((tile, D_MODEL), lambda i: (i, 0)),
            pl.BlockSpec((D_MODEL, d_in), lambda i: (0, 0)),
            pl.BlockSpec((tile, LANES), lambda i: (i % n_tab, 0)),
            pl.BlockSpec((tile, LANES), lambda i: (i % n_tab, 0)),
        ],
        out_specs=out_specs,
        out_shape=out_shape,
        compiler_params=_params(1),
        name="proj",
    )(x2d, w_bf, cos_t, sin_t)


def _memkv_kernel(m_ref, w_ref, mk_ref, mv_ref):
    mb = m_ref[...].astype(BF16)
    for h in range(MEM_HEADS):
        rows = pl.ds(h, N_MEM, stride=MEM_HEADS)
        mk_ref[rows, :] = _dot(mb, w_ref[:, h * MEM_HD:(h + 1) * MEM_HD])
        mv_ref[rows, :] = _dot(mb, w_ref[:, MEM_W + h * MEM_HD:MEM_W + (h + 1) * MEM_HD])


def _memkv(mem2d, w_bf):
    n = mem2d.shape[0] // N_MEM
    out = pl.BlockSpec((None, N_MEM * MEM_HEADS, MEM_HD), lambda i: (i, 0, 0))
    return pl.pallas_call(
        _memkv_kernel,
        grid=(n,),
        in_specs=[pl.BlockSpec((N_MEM, D_MODEL), lambda i: (i, 0)),
                  pl.BlockSpec((D_MODEL, 2 * MEM_W), lambda i: (0, 0))],
        out_specs=[out, out],
        out_shape=[jax.ShapeDtypeStruct((n, N_MEM * MEM_HEADS, MEM_HD), F32)] * 2,
        compiler_params=_params(1),
        name="memkv",
    )(mem2d, w_bf)


def _swa_prompt_kernel(q_ref, k_ref, v_ref, o_ref, a_ref, m_ref, l_ref):
    blk = SWA_BLOCK
    pt = PERM_TILE
    grp = pt // 16
    lo = lax.broadcasted_iota(jnp.int32, (blk, blk), 1) < SWA_HD

    def biases(seq_of):
        rq = seq_of(lax.broadcasted_iota(jnp.int32, (blk, 2 * blk), 0))
        c2 = lax.broadcasted_iota(jnp.int32, (blk, 2 * blk), 1)
        rel = blk + rq - (seq_of(c2 % blk) + blk * (c2 // blk))
        prev = jnp.where((rel >= 0) & (rel <= SWA_STEPS), 0.0, NEG).astype(F32)
        r1 = seq_of(lax.broadcasted_iota(jnp.int32, (blk, blk), 0))
        c1 = seq_of(lax.broadcasted_iota(jnp.int32, (blk, blk), 1))
        return prev, jnp.where(r1 >= c1, 0.0, NEG).astype(F32)

    def gather(ref, pair, starts, n):
        return jnp.concatenate([ref[pair, pl.ds(s, n), :] for s in starts], axis=0)

    def aligned(x):
        return x if isinstance(x, int) else pl.multiple_of(x, 8)

    def block(chunks, prev_chunks, n, bias, first_pattern, last_pattern, out_start=None):
        n_pairs = SWA_HEADS // 2
        in_at = lambda c: aligned(c[0] * PERM_ROWS + c[1] * PERM_PITCH + c[2])
        starts = [aligned(c[0] * pt + c[1] * grp + c[2]) for c in chunks]
        q_starts = [in_at(c) for c in chunks]
        k_starts = ([] if prev_chunks is None else [in_at(c) for c in prev_chunks]) + q_starts
        olds = None
        if not first_pattern:
            olds = [(gather(m_ref, pair, starts, n), gather(l_ref, pair, starts, n),
                     gather(a_ref, pair, starts, n)) for pair in range(n_pairs)]
        heads = [(pair, hh) for pair in range(n_pairs) for hh in range(2)]
        vbs = [gather(v_ref, pair, k_starts, n).astype(BF16) for pair in range(n_pairs)]
        scores = []
        for pair in range(n_pairs):
            q = gather(q_ref, pair, q_starts, n)
            kb = gather(k_ref, pair, k_starts, n).astype(BF16)
            for hh in range(2):
                qm = jnp.where(lo if hh == 0 else ~lo, q, 0.0).astype(BF16)
                scores.append(_dot_nt(qm, kb) + bias)
        ms = [jnp.max(s, axis=-1, keepdims=True) for s in scores]
        ps = [jnp.exp(s - m) for s, m in zip(scores, ms)]
        ls = [jnp.sum(p, axis=-1, keepdims=True) for p in ps]
        pvs = [_dot(p.astype(BF16), vbs[pair]) for p, (pair, _) in zip(ps, heads)]
        news = []
        for pair in range(n_pairs):
            i0, i1 = 2 * pair, 2 * pair + 1
            m_g = jnp.where(lo, ms[i0], ms[i1])
            l_g = jnp.where(lo, ls[i0], ls[i1])
            a_g = jnp.where(lo, pvs[i0], pvs[i1])
            if first_pattern:
                m_new, l_new, a_new = m_g, l_g, a_g
            else:
                m_old, l_old, a_old = olds[pair]
                m_new = jnp.maximum(m_old, m_g)
                w_old = jnp.exp(m_old - m_new)
                w_g = jnp.exp(m_g - m_new)
                l_new = w_old * l_old + w_g * l_g
                a_new = w_old * a_old + w_g * a_g
            news.append((m_new, l_new, a_new))
        for pair, (m_new, l_new, a_new) in enumerate(news):
            if last_pattern:
                o = a_new / l_new
                for i in range(len(starts)):
                    o_ref[pair, pl.ds(out_start + i, n, stride=len(starts)), :] = o[i * n:(i + 1) * n]
            else:
                for i, s in enumerate(starts):
                    m_ref[pair, pl.ds(s, n), :] = m_new[i * n:(i + 1) * n]
                    l_ref[pair, pl.ds(s, n), :] = l_new[i * n:(i + 1) * n]
                    a_ref[pair, pl.ds(s, n), :] = a_new[i * n:(i + 1) * n]

    _, bias16 = biases(lambda i: i)

    def class16(r, carry):
        block([(t, r, 0) for t in range(SEQ // pt)], None, grp, bias16, True, False)
        return carry

    lax.fori_loop(0, 16, class16, 0)

    bias4_prev, bias4_own = biases(lambda i: 4 * (i % grp) + i // grp)

    def class4(r, carry):
        chunks_of = lambda tile: [(tile, 4 * a + r, 0) for a in range(4)]
        block(chunks_of(0), None, grp, bias4_own, False, False)

        def per_block(b, c2):
            block(chunks_of(b), chunks_of(b - 1), grp, bias4_prev, False, False)
            return c2

        lax.fori_loop(1, SEQ // pt, per_block, 0)
        return carry

    lax.fori_loop(0, 4, class4, 0)

    bias1_prev, bias1_own = biases(lambda i: 16 * (i % 8) + i // 8)
    per_tile = pt // blk
    chunks1 = lambda c: [(c // per_tile, i, (c % per_tile) * 8) for i in range(16)]
    block(chunks1(0), None, 8, bias1_own, False, True, out_start=0)

    def block1(c, carry):
        block(chunks1(c), chunks1(c - 1), 8, bias1_prev, False, True, out_start=pl.multiple_of(c * blk, blk))
        return carry

    lax.fori_loop(1, SEQ // blk, block1, 0)


def _swa_prompt(sq, sk, sv):
    slabs = SWA_W // LANES
    spec = pl.BlockSpec((slabs, None, SEQ, LANES), lambda b: (0, b, 0, 0))
    in_spec = pl.BlockSpec((slabs, None, sq.shape[2], LANES), lambda b: (0, b, 0, 0))
    return pl.pallas_call(
        _swa_prompt_kernel,
        grid=(BATCH,),
        in_specs=[in_spec, in_spec, in_spec],
        out_specs=spec,
        out_shape=jax.ShapeDtypeStruct((slabs, BATCH, SEQ, LANES), F32),
        scratch_shapes=[pltpu.VMEM((slabs, SEQ, LANES), F32)] * 3,
        compiler_params=_params(1),
        name="swa_prompt",
    )(sq, sk, sv)


def _head_norm(o):
    mu = jnp.mean(o, axis=-1, keepdims=True)
    d = o - mu
    var = jnp.mean(d * d, axis=-1, keepdims=True)
    return d * lax.rsqrt(var + GN_EPS)


def _deepnorm_ln(x, h, gain, bias):
    z = DEEPNORM_ALPHA * x + h
    mu = jnp.mean(z, axis=-1, keepdims=True)
    d = z - mu
    var = jnp.mean(d * d, axis=-1, keepdims=True)
    return d * lax.rsqrt(var + LN_EPS) * gain + bias


def _softmax_rows(s):
    m = jnp.max(s, axis=-1, keepdims=True)
    p = jnp.exp(s - m)
    return p * (1.0 / jnp.sum(p, axis=-1, keepdims=True))


def _mix_prompt_kernel(x_ref, rq_ref, rk_ref, rv_ref, rg_ref, so_ref, sg_ref, mq_ref, mg_ref,
                       mk_ref, mv_ref, wout_ref, dmat_ref, kdec_ref, qdec_ref, gdec_ref,
                       gain_ref, bias_ref, y_ref, state_out_ref,
                       state_ref, mix_ref, mkb_ref, mvb_ref, *, tile):
    t = pl.program_id(1)

    @pl.when(t == 0)
    def _():
        state_ref[...] = jnp.zeros_like(state_ref)
        for h in range(MEM_HEADS):
            rows = pl.ds(h, N_MEM, stride=MEM_HEADS)
            mkb_ref[:, pl.ds(h * MEM_HD, MEM_HD)] = mk_ref[rows, :].astype(BF16)
            mvb_ref[:, pl.ds(h * MEM_HD, MEM_HD)] = mv_ref[rows, :].astype(BF16)

    ck = RET_CHUNK
    lane = lax.broadcasted_iota(jnp.int32, (ck, LANES), 1)
    lo = lane < RET_DK
    top = lax.broadcasted_iota(jnp.int32, (LANES, LANES), 0) < RET_DK
    gain = gain_ref[...]
    bias = bias_ref[...]

    n_ck = tile // ck
    n_pairs = RET_HEADS // 2
    rows_of = [pl.ds(c * ck, ck) for c in range(n_ck)]
    items = [(c, pair, hh) for c in range(n_ck) for pair in range(n_pairs) for hh in range(2)]
    hs_of = lambda pair, hh: pl.ds((2 * pair + hh) * RET_DV, RET_DV)
    qk = {}
    for c in range(n_ck):
        for pair in range(n_pairs):
            cs = pl.ds(pair * LANES, LANES)
            q = rq_ref[rows_of[c], cs]
            k = rk_ref[rows_of[c], cs]
            qk[c, pair] = (q, k.astype(BF16), (k * kdec_ref[:, cs]).astype(BF16), q * qdec_ref[:, cs])
    sel = lambda x, hh: jnp.where(lo if hh == 0 else ~lo, x, 0.0).astype(BF16)
    s = {(c, pair, hh): _dot_nt(sel(qk[c, pair][0], hh), qk[c, pair][1]) * dmat_ref[2 * pair + hh]
         for c, pair, hh in items}
    kv = {(c, pair, hh): _dot_tn(qk[c, pair][2], rv_ref[rows_of[c], hs_of(pair, hh)]) for c, pair, hh in items}
    intra = {(c, pair, hh): _dot(s[c, pair, hh].astype(BF16), rv_ref[rows_of[c], hs_of(pair, hh)])
             for c, pair, hh in items}
    state = {(0, pair): state_ref[pair] for pair in range(n_pairs)}
    for c in range(n_ck):
        for pair in range(n_pairs):
            state[c + 1, pair] = (gdec_ref[pl.ds(pair * LANES, LANES), :] * state[c, pair]
                                  + jnp.where(top, kv[c, pair, 0], kv[c, pair, 1]))
    for pair in range(n_pairs):
        state_ref[pair] = state[n_ck, pair]
    cross = {(c, pair, hh): _dot(sel(qk[c, pair][3], hh), state[c, pair].astype(BF16)) for c, pair, hh in items}
    for c, pair, hh in items:
        hs = hs_of(pair, hh)
        o = intra[c, pair, hh] + cross[c, pair, hh]
        mix_ref[rows_of[c], hs] = (rg_ref[rows_of[c], hs].astype(F32) * _head_norm(o)).astype(BF16)
    for pair in range(SWA_W // LANES):
        cs = pl.ds(pair * LANES, LANES)
        mix_ref[:, pl.ds(RET_W + pair * LANES, LANES)] = (
            sg_ref[:, cs].astype(F32) * so_ref[pair]).astype(BF16)
    for c in range(n_ck):
        ps = [_softmax_rows(_dot_nt(mq_ref[rows_of[c], pl.ds(h * MEM_HD, MEM_HD)],
                                    mkb_ref[:, pl.ds(h * MEM_HD, MEM_HD)]) * MEM_SCALE)
              for h in range(MEM_HEADS)]
        for h in range(MEM_HEADS):
            hs = pl.ds(h * MEM_HD, MEM_HD)
            o = _dot(ps[h].astype(BF16), mvb_ref[:, hs])
            mix_ref[rows_of[c], pl.ds(RET_W + SWA_W + h * MEM_HD, MEM_HD)] = (
                mg_ref[rows_of[c], hs].astype(F32) * o).astype(BF16)
    ob = OUT_ROWS
    for r in range(tile // ob):
        rows = pl.ds(r * ob, ob)
        hout = _dot(mix_ref[rows, :], wout_ref[...])
        y_ref[rows, :] = _deepnorm_ln(x_ref[rows, :], hout, gain, bias)

    @pl.when(t == pl.num_programs(1) - 1)
    def _():
        state_out_ref[pl.ds(0, LANES), :] = state_ref[0]
        state_out_ref[pl.ds(LANES, LANES), :] = state_ref[1]


def _mix_prompt(x2d, pr, swa_o, mk, mv, wout_bf, tabs, gain, bias, tile):
    nt = SEQ // tile
    row = lambda w: pl.BlockSpec((tile, w), lambda b, t: (b * nt + t, 0))
    const2 = lambda a: pl.BlockSpec(a.shape, lambda b, t: (0, 0))
    dmat, kdec, qdec, gdec = tabs
    return pl.pallas_call(
        functools.partial(_mix_prompt_kernel, tile=tile),
        grid=(BATCH, nt),
        in_specs=[
            row(D_MODEL), row(RET_HEADS * RET_DK), row(RET_HEADS * RET_DK), row(RET_W), row(RET_W),
            pl.BlockSpec((SWA_W // LANES, tile, LANES), lambda b, t: (0, b * nt + t, 0)),
            row(SWA_W), row(MEM_W), row(MEM_W),
            pl.BlockSpec((None, N_MEM * MEM_HEADS, MEM_HD), lambda b, t: (b, 0, 0)),
            pl.BlockSpec((None, N_MEM * MEM_HEADS, MEM_HD), lambda b, t: (b, 0, 0)),
            const2(wout_bf),
            pl.BlockSpec(dmat.shape, lambda b, t: (0, 0, 0)),
            const2(kdec), const2(qdec), const2(gdec), const2(gain), const2(bias),
        ],
        out_specs=[row(D_MODEL),
                   pl.BlockSpec((None, RET_HEADS * RET_DK, RET_DV), lambda b, t: (b, 0, 0))],
        out_shape=[jax.ShapeDtypeStruct((BATCH * SEQ, D_MODEL), F32),
                   jax.ShapeDtypeStruct((BATCH, RET_HEADS * RET_DK, RET_DV), F32)],
        scratch_shapes=[pltpu.VMEM((RET_HEADS // 2, LANES, RET_DV), F32),
                        pltpu.VMEM((tile, D_MIX), BF16),
                        pltpu.VMEM((N_MEM, MEM_W), BF16),
                        pltpu.VMEM((N_MEM, MEM_W), BF16)],
        compiler_params=_params(2),
        name="mix_prompt",
    )(x2d, pr["rq"], pr["rk"], pr["rv"], pr["rg"], swa_o, pr["sg"], pr["mq"], pr["mg"],
      mk, mv, wout_bf, dmat, kdec, qdec, gdec, gain, bias)


def _mix_sample_kernel(rq_ref, rk_ref, rv_ref, rg_ref, sq_ref, sk_ref, sv_ref, sg_ref, mq_ref, mg_ref,
                       state_ref, ckt_ref, cvt_ref, cmk_ref, cmv_ref,
                       dmat_ref, kdec_ref, qdec_ref, gdec_ref,
                       mix_ref, state_out_ref):
    n = DEC_PAD
    q = rq_ref[...]
    k = rk_ref[...]
    kb = k.astype(BF16)
    kd = (k * kdec_ref[...]).astype(BF16)
    qd = q * qdec_ref[...]
    st = state_ref[...]
    stb = st.astype(BF16)
    vb = rv_ref[...].astype(BF16)
    kv = _dot_tn(kd, vb)
    lane_qk = lax.broadcasted_iota(jnp.int32, q.shape, 1) // RET_DK
    ret_s = [_dot_nt(jnp.where(lane_qk == h, q, 0.0).astype(BF16), kb) for h in range(RET_HEADS)]
    ret_cross = [_dot(jnp.where(lane_qk == h, qd, 0.0).astype(BF16), stb) for h in range(RET_HEADS)]
    mem_rows = [pl.ds(h, N_MEM, stride=MEM_HEADS) for h in range(MEM_HEADS)]
    mem_s = [_dot_nt(mq_ref[:, pl.ds(h * MEM_HD, MEM_HD)].astype(BF16), cmk_ref[mem_rows[h], :].astype(BF16))
             for h in range(MEM_HEADS)]

    blk = SWA_BLOCK
    wb = ckt_ref.shape[2]
    rt = 8
    lo = lax.broadcasted_iota(jnp.int32, (n, LANES), 1) < SWA_HD
    lo_t = lax.broadcasted_iota(jnp.int32, (rt, LANES), 1) < SWA_HD
    tok = lambda w: lax.broadcasted_iota(jnp.int32, (rt, w), 0)
    col = lambda w: lax.broadcasted_iota(jnp.int32, (rt, w), 1)
    windows = (
        (wb - blk, jnp.where(col(blk) >= tok(blk), 0.0, NEG), jnp.where(col(blk) <= tok(blk), 0.0, NEG)),
        (wb - 4 * blk, jnp.where(col(4 * blk) % 4 == tok(4 * blk), 0.0, NEG),
         jnp.where(col(blk) == tok(blk), 0.0, NEG)),
        (0, jnp.where(col(wb) % 16 == tok(wb), 0.0, NEG), jnp.where(col(blk) == tok(blk), 0.0, NEG)),
    )
    pad = jnp.zeros((blk - n, LANES), BF16)
    zero_c = jnp.zeros((rt, wb), F32)
    zero_n = jnp.zeros((rt, blk), F32)
    n_pairs = SWA_HEADS // 2
    n_win = len(windows)
    pair_cols = [pl.ds(pair * LANES, LANES) for pair in range(n_pairs)]
    vps = [cvt_ref[2 * pair:2 * pair + 2].reshape(2 * SWA_HD, wb).astype(BF16) for pair in range(n_pairs)]
    v_news = [jnp.concatenate([sv_ref[:, cs].astype(BF16), pad], axis=0) for cs in pair_cols]
    s_alls, sn_alls = [], []
    for pair, cs in enumerate(pair_cols):
        kp = ckt_ref[2 * pair:2 * pair + 2].reshape(2 * SWA_HD, wb).astype(BF16)
        k_new = jnp.concatenate([sk_ref[:, cs].astype(BF16), pad], axis=0)
        q = sq_ref[:, cs]
        qs = jnp.concatenate([jnp.where(lo, q, 0.0), jnp.where(lo, 0.0, q)], axis=0).astype(BF16)
        s_alls.append(_dot(qs, kp))
        sn_alls.append(_dot_nt(qs, k_new))
    p_rows = [[] for _ in range(n_pairs)]
    pn_rows = [[] for _ in range(n_pairs)]
    stats = [[] for _ in range(n_pairs)]
    for pair in range(n_pairs):
        for hh in range(2):
            s = s_alls[pair][hh * n:hh * n + rt]
            sn = sn_alls[pair][hh * n:hh * n + rt]
            for w0, bias_c, bias_n in windows:
                sc = s[:, w0:] + bias_c
                snb = sn + bias_n
                m = jnp.maximum(jnp.max(sc, axis=-1, keepdims=True), jnp.max(snb, axis=-1, keepdims=True))
                pc = jnp.exp(sc - m)
                pn = jnp.exp(snb - m)
                stats[pair].append((m, jnp.sum(pc, axis=-1, keepdims=True) + jnp.sum(pn, axis=-1, keepdims=True)))
                if w0:
                    pc = jnp.concatenate([jnp.zeros((rt, w0), F32), pc], axis=1)
                p_rows[pair].append(pc)
                pn_rows[pair].append(pn)
            p_rows[pair].append(zero_c)
            pn_rows[pair].append(zero_n)
    ret_sb = [(ret_s[h] * dmat_ref[h]).astype(BF16) for h in range(RET_HEADS)]
    mem_p = [_softmax_rows(s * MEM_SCALE).astype(BF16) for s in mem_s]
    pvs = [_dot_nt(jnp.concatenate(p_rows[pair], axis=0).astype(BF16), vps[pair])
           + _dot(jnp.concatenate(pn_rows[pair], axis=0).astype(BF16), v_news[pair])
           for pair in range(n_pairs)]
    ret_intra = [_dot(ret_sb[h], vb[:, h * RET_DV:(h + 1) * RET_DV]) for h in range(RET_HEADS)]
    mem_o = [_dot(mem_p[h], cmv_ref[mem_rows[h], :].astype(BF16)) for h in range(MEM_HEADS)]
    pieces = []
    for pair in range(n_pairs):
        heads = []
        for hh in range(2):
            parts = []
            for i in range(n_win):
                m, l = stats[pair][hh * n_win + i]
                r0 = (hh * (n_win + 1) + i) * rt
                parts.append((pvs[pair][r0:r0 + rt] / l, m, l))
            m_all = jnp.maximum(jnp.maximum(parts[0][1], parts[1][1]), parts[2][1])
            ws = [l * jnp.exp(m - m_all) for (_, m, l) in parts]
            heads.append((ws[0] * parts[0][0] + ws[1] * parts[1][0] + ws[2] * parts[2][0])
                         / (ws[0] + ws[1] + ws[2]))
        pieces.append(jnp.where(lo_t, heads[0], heads[1]))
    swa = jnp.concatenate(pieces, axis=1)
    swa = jnp.concatenate([swa, jnp.zeros((n - rt, SWA_W), F32)], axis=0)
    mix_ref[:, pl.ds(RET_W, SWA_W)] = sg_ref[...] * swa
    for h in range(RET_HEADS):
        hs = pl.ds(h * RET_DV, RET_DV)
        ks = pl.ds(h * RET_DK, RET_DK)
        mix_ref[:, hs] = rg_ref[:, hs] * _head_norm(ret_intra[h] + ret_cross[h])
        state_out_ref[ks, :] = (gdec_ref[ks, :] * st[h * RET_DK:(h + 1) * RET_DK, :]
                                + kv[h * RET_DK:(h + 1) * RET_DK, h * RET_DV:(h + 1) * RET_DV])
    for h in range(MEM_HEADS):
        hs = pl.ds(h * MEM_HD, MEM_HD)
        mix_ref[:, pl.ds(RET_W + SWA_W + h * MEM_HD, MEM_HD)] = mg_ref[:, hs] * mem_o[h]


def _mix_sample(ps, state, ckt, cvt, cmk, cmv, tabs):
    n = DEC_PAD
    wb = ckt.shape[3]
    row = lambda w: pl.BlockSpec((n, w), lambda b: (b, 0))
    st_spec = pl.BlockSpec((None, RET_HEADS * RET_DK, RET_DV), lambda b: (b, 0, 0))
    cache_spec = pl.BlockSpec((None, SWA_HEADS, SWA_HD, wb), lambda b: (b, 0, 0, 0))
    mem_spec = pl.BlockSpec((None, N_MEM * MEM_HEADS, MEM_HD), lambda b: (b, 0, 0))
    dmat, kdec, qdec, gdec = tabs
    return pl.pallas_call(
        _mix_sample_kernel,
        grid=(DEC_BATCH,),
        in_specs=[
            row(RET_HEADS * RET_DK), row(RET_HEADS * RET_DK), row(RET_W), row(RET_W),
            row(SWA_W), row(SWA_W), row(SWA_W), row(SWA_W), row(MEM_W), row(MEM_W),
            st_spec, cache_spec, cache_spec, mem_spec, mem_spec,
            pl.BlockSpec(dmat.shape, lambda b: (0, 0, 0)),
            pl.BlockSpec(kdec.shape, lambda b: (0, 0)),
            pl.BlockSpec(qdec.shape, lambda b: (0, 0)),
            pl.BlockSpec(gdec.shape, lambda b: (0, 0)),
        ],
        out_specs=[row(D_MIX), st_spec],
        out_shape=[jax.ShapeDtypeStruct((DEC_BATCH * n, D_MIX), F32),
                   jax.ShapeDtypeStruct((DEC_BATCH, RET_HEADS * RET_DK, RET_DV), F32)],
        compiler_params=_params(1),
        name="mix_sample",
    )(ps["rq"], ps["rk"], ps["rv"], ps["rg"], ps["sq"], ps["sk"], ps["sv"], ps["sg"], ps["mq"], ps["mg"],
      state, ckt, cvt, cmk, cmv, dmat, kdec, qdec, gdec)


def _finish_kernel(x_ref, mix_ref, wout_ref, gain_ref, bias_ref, y_ref):
    hout = _dot(mix_ref[...].astype(BF16), wout_ref[...])
    y_ref[...] = _deepnorm_ln(x_ref[...], hout, gain_ref[...], bias_ref[...])


def _finish(x2d, mix, wout_bf, gain, bias):
    n = x2d.shape[0]
    tile = 256
    const2 = lambda a: pl.BlockSpec(a.shape, lambda i: (0, 0))
    return pl.pallas_call(
        _finish_kernel,
        grid=(n // tile,),
        in_specs=[pl.BlockSpec((tile, D_MODEL), lambda i: (i, 0)),
                  pl.BlockSpec((tile, D_MIX), lambda i: (i, 0)),
                  const2(wout_bf), const2(gain), const2(bias)],
        out_specs=pl.BlockSpec((tile, D_MODEL), lambda i: (i, 0)),
        out_shape=jax.ShapeDtypeStruct((n, D_MODEL), F32),
        compiler_params=_params(1),
        name="finish",
    )(x2d, mix, wout_bf, gain, bias)


def _rope_tables(pos):
    half = SWA_HD // 2
    inv = ROPE_THETA ** (-jnp.arange(half, dtype=F32) * 2.0 / SWA_HD)
    ang = pos.astype(F32)[:, None] * inv[None, :]
    cos = jnp.cos(ang)
    sin = jnp.sin(ang)
    reps = LANES // SWA_HD
    return (jnp.tile(jnp.concatenate([cos, cos], axis=1), (1, reps)),
            jnp.tile(jnp.concatenate([-sin, sin], axis=1), (1, reps)))


def _retention_tables(chunk, rows):
    lg = jnp.log1p(-jnp.exp2(-5.0 - jnp.arange(RET_HEADS, dtype=F32)))
    idx = jnp.arange(rows, dtype=F32)
    live = idx < chunk
    rel = idx[:, None] - idx[None, :]
    ok = (rel >= 0) & live[:, None] & live[None, :]
    dmat = jnp.where(ok[None], jnp.exp(jnp.maximum(rel, 0.0)[None] * lg[:, None, None]), 0.0)
    kdec = jnp.where(live[:, None], jnp.exp((chunk - 1.0 - idx)[:, None] * lg[None, :]), 0.0)
    qdec = jnp.where(live[:, None], jnp.exp((idx + 1.0)[:, None] * lg[None, :]), 0.0)
    g = jnp.exp(chunk * lg)
    kdec = jnp.repeat(kdec, RET_DK, axis=1)
    qdec = jnp.repeat(qdec, RET_DK, axis=1)
    gdec = jnp.broadcast_to(jnp.repeat(g, RET_DK)[:, None], (RET_HEADS * RET_DK, RET_DV))
    return dmat, kdec, qdec, gdec


def kernel(x_prompt, x_sample, state_ret, cache_swa_k, cache_swa_v, cache_mem_k, cache_mem_v,
           mem_prompt, w_in, w_mem_kv, w_out, ln_gain, ln_bias):
    depth = w_in.shape[0]
    assert depth == 1
    win_bf = w_in[0].astype(BF16)
    wmem_bf = w_mem_kv[0].astype(BF16)
    wout_bf = w_out[0].astype(BF16)
    gain = ln_gain[0].reshape(1, D_MODEL)
    bias = ln_bias[0].reshape(1, D_MODEL)

    xp = x_prompt.reshape(BATCH * SEQ, D_MODEL)
    cos_p, sin_p = _rope_tables(jnp.arange(SEQ))
    p_outs = (("rq", "rows", F32), ("rk", "rows", F32), ("rv", "rows", BF16), ("rg", "rows", BF16),
              ("sq", "perm", F32), ("sk", "cols", F32), ("sk", "perm", F32), ("sv", "cols", F32),
              ("sv", "perm", F32), ("sg", "rows", BF16), ("mq", "rows", BF16), ("mg", "rows", BF16))
    p_keys = ("rq", "rk", "rv", "rg", "sq4", "sk", "sk4", "sv", "sv4", "sg", "mq", "mg")
    pr = dict(zip(p_keys, _project(xp, win_bf, cos_p, sin_p, PERM_TILE, p_outs)))
    mk, mv = _memkv(mem_prompt.reshape(BATCH * N_MEM, D_MODEL), wmem_bf)
    slab = lambda a: a.reshape(SWA_W // LANES, BATCH, SEQ // PERM_TILE * PERM_ROWS, LANES)
    swa_o = _swa_prompt(slab(pr["sq4"]), slab(pr["sk4"]), slab(pr["sv4"]))
    swa_o = swa_o.reshape(SWA_W // LANES, BATCH * SEQ, LANES)
    tabs_p = _retention_tables(RET_CHUNK, RET_CHUNK)
    yp, ret_p = _mix_prompt(xp, pr, swa_o, mk, mv, wout_bf, tabs_p, gain, bias, 512)

    xs = jnp.pad(x_sample, ((0, 0), (0, DEC_PAD - DEC_SEQ), (0, 0))).reshape(DEC_BATCH * DEC_PAD, D_MODEL)
    pos_s = PAST_LEN + jnp.arange(DEC_BATCH * DEC_PAD) % DEC_PAD
    cos_s, sin_s = _rope_tables(pos_s)
    names = [c[0] for c in _PROJ_COLS]
    ps = dict(zip(names, _project(xs, win_bf, cos_s, sin_s, DEC_BATCH * DEC_PAD,
                                  [(k, "rows", F32) for k in names])))
    tabs_s = _retention_tables(DEC_SEQ, DEC_PAD)
    mix_s, ret_s = _mix_sample(
        ps, state_ret[0].reshape(DEC_BATCH, RET_HEADS * RET_DK, RET_DV),
        cache_swa_k[0].transpose(0, 2, 3, 1), cache_swa_v[0].transpose(0, 2, 3, 1),
        cache_mem_k[0].reshape(DEC_BATCH, N_MEM * MEM_HEADS, MEM_HD),
        cache_mem_v[0].reshape(DEC_BATCH, N_MEM * MEM_HEADS, MEM_HD),
        tabs_s)
    ys = _finish(xs, mix_s, wout_bf, gain, bias)

    take = lambda a, w: a.reshape(DEC_BATCH, DEC_PAD, w)[:, :DEC_SEQ]
    swa_rows = lambda a: a.reshape(BATCH, SWA_HEADS, SWA_HD, SEQ).transpose(0, 3, 1, 2)[None]
    return (
        yp.reshape(BATCH, SEQ, D_MODEL),
        take(ys, D_MODEL),
        ret_p.reshape(1, BATCH, RET_HEADS, RET_DK, RET_DV),
        ret_s.reshape(1, DEC_BATCH, RET_HEADS, RET_DK, RET_DV),
        swa_rows(pr["sk"]),
        swa_rows(pr["sv"]),
        take(ps["sk"], SWA_W).reshape(1, DEC_BATCH, DEC_SEQ, SWA_HEADS, SWA_HD),
        take(ps["sv"], SWA_W).reshape(1, DEC_BATCH, DEC_SEQ, SWA_HEADS, SWA_HD),
        mk.reshape(1, BATCH, N_MEM, MEM_HEADS, MEM_HD),
        mv.reshape(1, BATCH, N_MEM, MEM_HEADS, MEM_HD),
    )
```

```python
import functools

import jax
import jax.numpy as jnp
from jax import lax
from jax.experimental import pallas as pl
from jax.experimental.pallas import tpu as pltpu

F32 = jnp.float32
BF16 = jnp.bfloat16

D_MODEL = 1024
BATCH = 8
SEQ = 2048
DEC_BATCH = 32
DEC_SEQ = 4
PAST_LEN = 8192
N_MEM = 256
MEM_HEADS = 4
MEM_HD = 128
RET_HEADS = 4
RET_DK = 64
RET_DV = 128
RET_CHUNK = 128
SWA_HEADS = 8
SWA_HD = 64
SWA_DILATIONS = (1, 4, 16)
SWA_STEPS = 128
SWA_BLOCK = 128
ROPE_THETA = 10000.0
LN_EPS = 1e-5
GN_EPS = 1e-5
RET_W = RET_HEADS * RET_DV
SWA_W = SWA_HEADS * SWA_HD
MEM_W = MEM_HEADS * MEM_HD
D_MIX = RET_W + SWA_W + MEM_W
DEEPNORM_ALPHA = 2.0 ** 0.25
MEM_SCALE = MEM_HD ** -0.5
QK_SCALE = 0.125

LANES = 128
DEC_PAD = 16
PERM_TILE = 512
PERM_PITCH = 40
PERM_ROWS = 16 * PERM_PITCH
OUT_ROWS = 256
VMEM_LIMIT = 56 * 1024 * 1024
NEG = -1e30

_PROJ_COLS = (
    ("rq", RET_HEADS * RET_DK, "rope", 1.0),
    ("rk", RET_HEADS * RET_DK, "rope", QK_SCALE),
    ("rv", RET_W, "id", 1.0),
    ("rg", RET_W, "silu", 1.0),
    ("sq", SWA_W, "rope", QK_SCALE),
    ("sk", SWA_W, "rope", 1.0),
    ("sv", SWA_W, "id", 1.0),
    ("sg", SWA_W, "silu", 1.0),
    ("mq", MEM_W, "id", 1.0),
    ("mg", MEM_W, "silu", 1.0),
)


def _dot(a, b):
    return jnp.dot(a, b, preferred_element_type=F32)


def _dot_nt(a, b):
    return lax.dot_general(a, b, (((1,), (1,)), ((), ())), preferred_element_type=F32)


def _dot_tn(a, b):
    return lax.dot_general(a, b, (((0,), (0,)), ((), ())), preferred_element_type=F32)


def _params(n_axes):
    return pltpu.CompilerParams(dimension_semantics=("arbitrary",) * n_axes,
                                vmem_limit_bytes=VMEM_LIMIT)


def _proj_kernel(x_ref, w_ref, cos_ref, sin_ref, *out_refs, dests):
    xb = x_ref[...].astype(BF16)
    cos = cos_ref[...]
    sin = sin_ref[...]
    lane = lax.broadcasted_iota(jnp.int32, cos.shape, 1)
    first_half = (lane % 64) < 32
    col = 0
    for name, width, kind, scale in _PROJ_COLS:
        targets = [(o_ref, layout) for o_ref, (dname, layout) in zip(out_refs, dests) if dname == name]
        for c in range(0, width, 2 * LANES):
            h2 = _dot(xb, w_ref[:, col + c:col + c + 2 * LANES])
            for half in range(2):
                h = h2[:, half * LANES:(half + 1) * LANES]
                if kind == "rope":
                    swapped = jnp.where(first_half, pltpu.roll(h, 96, 1), pltpu.roll(h, 32, 1))
                    h = h * cos + swapped * sin
                    if scale != 1.0:
                        h = h * scale
                elif kind == "silu":
                    h = h * (1.0 / (1.0 + jnp.exp(-h)))
                lo = c + half * LANES
                for o_ref, layout in targets:
                    if layout == "rows":
                        o_ref[:, lo:lo + LANES] = h.astype(o_ref.dtype)
                    elif layout == "perm":
                        slab = lo // LANES
                        for g in range(h.shape[0] // 8):
                            base = (g % 2) * 8 * PERM_PITCH + g // 2
                            o_ref[slab, pl.ds(base, 8, stride=PERM_PITCH), :] = (
                                h[8 * g:8 * g + 8].astype(o_ref.dtype))
                        n_live = h.shape[0] // 16
                        for cls in range(16):
                            o_ref[slab, pl.ds(cls * PERM_PITCH + n_live, PERM_PITCH - n_live), :] = (
                                jnp.zeros((PERM_PITCH - n_live, LANES), o_ref.dtype))
                    else:
                        o_ref[lo:lo + LANES, :] = h.T.astype(o_ref.dtype)
        col += width


def _project(x2d, w_bf, cos_t, sin_t, tile, outs):
    n = x2d.shape[0]
    seq = cos_t.shape[0]
    n_tab = seq // tile
    d_in = w_bf.shape[1]
    widths = {name: w for name, w, _, _ in _PROJ_COLS}
    out_shape, out_specs = [], []
    for name, layout, dt in outs:
        w = widths[name]
        if layout == "rows":
            out_shape.append(jax.ShapeDtypeStruct((n, w), dt))
            out_specs.append(pl.BlockSpec((tile, w), lambda i: (i, 0)))
        elif layout == "perm":
            assert tile == PERM_TILE
            out_shape.append(jax.ShapeDtypeStruct((w // LANES, n // tile * PERM_ROWS, LANES), dt))
            out_specs.append(pl.BlockSpec((w // LANES, PERM_ROWS, LANES), lambda i: (0, i, 0)))
        else:
            out_shape.append(jax.ShapeDtypeStruct((n // seq, w, seq), dt))
            out_specs.append(pl.BlockSpec((None, w, tile), lambda i: (i // n_tab, 0, i % n_tab)))
    return pl.pallas_call(
        functools.partial(_proj_kernel, dests=tuple((name, layout) for name, layout, _ in outs)),
        grid=(n // tile,),
        in_specs=[
            pl.BlockSpec((tile, D_MODEL), lambda i: (i, 0)),
            pl.BlockSpec((D_MODEL, d_in), lambda i: (0, 0)),
            pl.BlockSpec((tile, LANES), lambda i: (i % n_tab, 0)),
            pl.BlockSpec((tile, LANES), lambda i: (i % n_tab, 0)),
        ],
        out_specs=out_specs,
        out_shape=out_shape,
        compiler_params=_params(1),
        name="proj",
    )(x2d, w_bf, cos_t, sin_t)


def _memkv_kernel(m_ref, w_ref, mk_ref, mv_ref):
    mb = m_ref[...].astype(BF16)
    for h in range(MEM_HEADS):
        rows = pl.ds(h, N_MEM, stride=MEM_HEADS)
        mk_ref[rows, :] = _dot(mb, w_ref[:, h * MEM_HD:(h + 1) * MEM_HD])
        mv_ref[rows, :] = _dot(mb, w_ref[:, MEM_W + h * MEM_HD:MEM_W + (h + 1) * MEM_HD])


def _memkv(mem2d, w_bf):
    n = mem2d.shape[0] // N_MEM
    out = pl.BlockSpec((None, N_MEM * MEM_HEADS, MEM_HD), lambda i: (i, 0, 0))
    return pl.pallas_call(
        _memkv_kernel,
        grid=(n,),
        in_specs=[pl.BlockSpec((N_MEM, D_MODEL), lambda i: (i, 0)),
                  pl.BlockSpec((D_MODEL, 2 * MEM_W), lambda i: (0, 0))],
        out_specs=[out, out],
        out_shape=[jax.ShapeDtypeStruct((n, N_MEM * MEM_HEADS, MEM_HD), F32)] * 2,
        compiler_params=_params(1),
        name="memkv",
    )(mem2d, w_bf)


def _swa_prompt_kernel(q_ref, k_ref, v_ref, o_ref, a_ref, m_ref, l_ref):
    blk = SWA_BLOCK
    pt = PERM_TILE
    grp = pt // 16
    lo = lax.broadcasted_iota(jnp.int32, (blk, blk), 1) < SWA_HD

    def biases(seq_of):
        rq = seq_of(lax.broadcasted_iota(jnp.int32, (blk, 2 * blk), 0))
        c2 = lax.broadcasted_iota(jnp.int32, (blk, 2 * blk), 1)
        rel = blk + rq - (seq_of(c2 % blk) + blk * (c2 // blk))
        prev = jnp.where((rel >= 0) & (rel <= SWA_STEPS), 0.0, NEG).astype(F32)
        r1 = seq_of(lax.broadcasted_iota(jnp.int32, (blk, blk), 0))
        c1 = seq_of(lax.broadcasted_iota(jnp.int32, (blk, blk), 1))
        return prev, jnp.where(r1 >= c1, 0.0, NEG).astype(F32)

    def gather(ref, pair, starts, n):
        return jnp.concatenate([ref[pair, pl.ds(s, n), :] for s in starts], axis=0)

    def aligned(x):
        return x if isinstance(x, int) else pl.multiple_of(x, 8)

    def block(chunks, prev_chunks, n, bias, first_pattern, last_pattern, out_start=None):
        n_pairs = SWA_HEADS // 2
        in_at = lambda c: aligned(c[0] * PERM_ROWS + c[1] * PERM_PITCH + c[2])
        starts = [aligned(c[0] * pt + c[1] * grp + c[2]) for c in chunks]
        q_starts = [in_at(c) for c in chunks]
        k_starts = ([] if prev_chunks is None else [in_at(c) for c in prev_chunks]) + q_starts
        olds = None
        if not first_pattern:
            olds = [(gather(m_ref, pair, starts, n), gather(l_ref, pair, starts, n),
                     gather(a_ref, pair, starts, n)) for pair in range(n_pairs)]
        heads = [(pair, hh) for pair in range(n_pairs) for hh in range(2)]
        vbs = [gather(v_ref, pair, k_starts, n).astype(BF16) for pair in range(n_pairs)]
        scores = []
        for pair in range(n_pairs):
            q = gather(q_ref, pair, q_starts, n)
            kb = gather(k_ref, pair, k_starts, n).astype(BF16)
            for hh in range(2):
                qm = jnp.where(lo if hh == 0 else ~lo, q, 0.0).astype(BF16)
                scores.append(_dot_nt(qm, kb) + bias)
        ms = [jnp.max(s, axis=-1, keepdims=True) for s in scores]
        ps = [jnp.exp(s - m) for s, m in zip(scores, ms)]
        ls = [jnp.sum(p, axis=-1, keepdims=True) for p in ps]
        pvs = [_dot(p.astype(BF16), vbs[pair]) for p, (pair, _) in zip(ps, heads)]
        news = []
        for pair in range(n_pairs):
            i0, i1 = 2 * pair, 2 * pair + 1
            m_g = jnp.where(lo, ms[i0], ms[i1])
            l_g = jnp.where(lo, ls[i0], ls[i1])
            a_g = jnp.where(lo, pvs[i0], pvs[i1])
            if first_pattern:
                m_new, l_new, a_new = m_g, l_g, a_g
            else:
                m_old, l_old, a_old = olds[pair]
                m_new = jnp.maximum(m_old, m_g)
                w_old = jnp.exp(m_old - m_new)
                w_g = jnp.exp(m_g - m_new)
                l_new = w_old * l_old + w_g * l_g
                a_new = w_old * a_old + w_g * a_g
            news.append((m_new, l_new, a_new))
        for pair, (m_new, l_new, a_new) in enumerate(news):
            if last_pattern:
                o = a_new / l_new
                for i in range(len(starts)):
                    o_ref[pair, pl.ds(out_start + i, n, stride=len(starts)), :] = o[i * n:(i + 1) * n]
            else:
                for i, s in enumerate(starts):
                    m_ref[pair, pl.ds(s, n), :] = m_new[i * n:(i + 1) * n]
                    l_ref[pair, pl.ds(s, n), :] = l_new[i * n:(i + 1) * n]
                    a_ref[pair, pl.ds(s, n), :] = a_new[i * n:(i + 1) * n]

    _, bias16 = biases(lambda i: i)

    def class16(r, carry):
        block([(t, r, 0) for t in range(SEQ // pt)], None, grp, bias16, True, False)
        return carry

    lax.fori_loop(0, 16, class16, 0)

    bias4_prev, bias4_own = biases(lambda i: 4 * (i % grp) + i // grp)

    def class4(r, carry):
        chunks_of = lambda tile: [(tile, 4 * a + r, 0) for a in range(4)]
        block(chunks_of(0), None, grp, bias4_own, False, False)

        def per_block(b, c2):
            block(chunks_of(b), chunks_of(b - 1), grp, bias4_prev, False, False)
            return c2

        lax.fori_loop(1, SEQ // pt, per_block, 0)
        return carry

    lax.fori_loop(0, 4, class4, 0)

    bias1_prev, bias1_own = biases(lambda i: 16 * (i % 8) + i // 8)
    per_tile = pt // blk
    chunks1 = lambda c: [(c // per_tile, i, (c % per_tile) * 8) for i in range(16)]
    block(chunks1(0), None, 8, bias1_own, False, True, out_start=0)

    def block1(c, carry):
        block(chunks1(c), chunks1(c - 1), 8, bias1_prev, False, True, out_start=pl.multiple_of(c * blk, blk))
        return carry

    lax.fori_loop(1, SEQ // blk, block1, 0)


def _swa_prompt(sq, sk, sv):
    slabs = SWA_W // LANES
    spec = pl.BlockSpec((slabs, None, SEQ, LANES), lambda b: (0, b, 0, 0))
    in_spec = pl.BlockSpec((slabs, None, sq.shape[2], LANES), lambda b: (0, b, 0, 0))
    return pl.pallas_call(
        _swa_prompt_kernel,
        grid=(BATCH,),
        in_specs=[in_spec, in_spec, in_spec],
        out_specs=spec,
        out_shape=jax.ShapeDtypeStruct((slabs, BATCH, SEQ, LANES), F32),
        scratch_shapes=[pltpu.VMEM((slabs, SEQ, LANES), F32)] * 3,
        compiler_params=_params(1),
        name="swa_prompt",
    )(sq, sk, sv)


def _head_norm(o):
    mu = jnp.mean(o, axis=-1, keepdims=True)
    d = o - mu
    var = jnp.mean(d * d, axis=-1, keepdims=True)
    return d * lax.rsqrt(var + GN_EPS)


def _deepnorm_ln(x, h, gain, bias):
    z = DEEPNORM_ALPHA * x + h
    mu = jnp.mean(z, axis=-1, keepdims=True)
    d = z - mu
    var = jnp.mean(d * d, axis=-1, keepdims=True)
    return d * lax.rsqrt(var + LN_EPS) * gain + bias


def _softmax_rows(s):
    m = jnp.max(s, axis=-1, keepdims=True)
    p = jnp.exp(s - m)
    return p * (1.0 / jnp.sum(p, axis=-1, keepdims=True))


def _mix_prompt_kernel(x_ref, rq_ref, rk_ref, rv_ref, rg_ref, so_ref, sg_ref, mq_ref, mg_ref,
                       mk_ref, mv_ref, wout_ref, dmat_ref, kdec_ref, qdec_ref, gdec_ref,
                       gain_ref, bias_ref, y_ref, state_out_ref,
                       state_ref, mix_ref, mkb_ref, mvb_ref, *, tile):
    t = pl.program_id(1)

    @pl.when(t == 0)
    def _():
        state_ref[...] = jnp.zeros_like(state_ref)
        for h in range(MEM_HEADS):
            rows = pl.ds(h, N_MEM, stride=MEM_HEADS)
            mkb_ref[:, pl.ds(h * MEM_HD, MEM_HD)] = mk_ref[rows, :].astype(BF16)
            mvb_ref[:, pl.ds(h * MEM_HD, MEM_HD)] = mv_ref[rows, :].astype(BF16)

    ck = RET_CHUNK
    lane = lax.broadcasted_iota(jnp.int32, (ck, LANES), 1)
    lo = lane < RET_DK
    top = lax.broadcasted_iota(jnp.int32, (LANES, LANES), 0) < RET_DK
    gain = gain_ref[...]
    bias = bias_ref[...]

    n_ck = tile // ck
    n_pairs = RET_HEADS // 2
    rows_of = [pl.ds(c * ck, ck) for c in range(n_ck)]
    items = [(c, pair, hh) for c in range(n_ck) for pair in range(n_pairs) for hh in range(2)]
    hs_of = lambda pair, hh: pl.ds((2 * pair + hh) * RET_DV, RET_DV)
    qk = {}
    for c in range(n_ck):
        for pair in range(n_pairs):
            cs = pl.ds(pair * LANES, LANES)
            q = rq_ref[rows_of[c], cs]
            k = rk_ref[rows_of[c], cs]
            qk[c, pair] = (q, k.astype(BF16), (k * kdec_ref[:, cs]).astype(BF16), q * qdec_ref[:, cs])
    sel = lambda x, hh: jnp.where(lo if hh == 0 else ~lo, x, 0.0).astype(BF16)
    s = {(c, pair, hh): _dot_nt(sel(qk[c, pair][0], hh), qk[c, pair][1]) * dmat_ref[2 * pair + hh]
         for c, pair, hh in items}
    kv = {(c, pair, hh): _dot_tn(qk[c, pair][2], rv_ref[rows_of[c], hs_of(pair, hh)]) for c, pair, hh in items}
    intra = {(c, pair, hh): _dot(s[c, pair, hh].astype(BF16), rv_ref[rows_of[c], hs_of(pair, hh)])
             for c, pair, hh in items}
    state = {(0, pair): state_ref[pair] for pair in range(n_pairs)}
    for c in range(n_ck):
        for pair in range(n_pairs):
            state[c + 1, pair] = (gdec_ref[pl.ds(pair * LANES, LANES), :] * state[c, pair]
                                  + jnp.where(top, kv[c, pair, 0], kv[c, pair, 1]))
    for pair in range(n_pairs):
        state_ref[pair] = state[n_ck, pair]
    cross = {(c, pair, hh): _dot(sel(qk[c, pair][3], hh), state[c, pair].astype(BF16)) for c, pair, hh in items}
    for c, pair, hh in items:
        hs = hs_of(pair, hh)
        o = intra[c, pair, hh] + cross[c, pair, hh]
        mix_ref[rows_of[c], hs] = (rg_ref[rows_of[c], hs].astype(F32) * _head_norm(o)).astype(BF16)
    for pair in range(SWA_W // LANES):
        cs = pl.ds(pair * LANES, LANES)
        mix_ref[:, pl.ds(RET_W + pair * LANES, LANES)] = (
            sg_ref[:, cs].astype(F32) * so_ref[pair]).astype(BF16)
    for c in range(n_ck):
        ps = [_softmax_rows(_dot_nt(mq_ref[rows_of[c], pl.ds(h * MEM_HD, MEM_HD)],
                                    mkb_ref[:, pl.ds(h * MEM_HD, MEM_HD)]) * MEM_SCALE)
              for h in range(MEM_HEADS)]
        for h in range(MEM_HEADS):
            hs = pl.ds(h * MEM_HD, MEM_HD)
            o = _dot(ps[h].astype(BF16), mvb_ref[:, hs])
            mix_ref[rows_of[c], pl.ds(RET_W + SWA_W + h * MEM_HD, MEM_HD)] = (
                mg_ref[rows_of[c], hs].astype(F32) * o).astype(BF16)
    ob = OUT_ROWS
    for r in range(tile // ob):
        rows = pl.ds(r * ob, ob)
        hout = _dot(mix_ref[rows, :], wout_ref[...])
        y_ref[rows, :] = _deepnorm_ln(x_ref[rows, :], hout, gain, bias)

    @pl.when(t == pl.num_programs(1) - 1)
    def _():
        state_out_ref[pl.ds(0, LANES), :] = state_ref[0]
        state_out_ref[pl.ds(LANES, LANES), :] = state_ref[1]


def _mix_all_kernel(*refs, tile, n_in, n_out):
    p_in, s_in = refs[:n_in[0]], refs[n_in[0]:n_in[0] + n_in[1]]
    outs = refs[n_in[0] + n_in[1]:]
    p_out, s_out = outs[:n_out[0]], outs[n_out[0]:n_out[0] + n_out[1]]
    scratch = outs[n_out[0] + n_out[1]:]
    _mix_sample_kernel(*s_in, *s_out)
    _mix_prompt_kernel(*p_in, *p_out, *scratch, tile=tile)


def _mix_all(x2d, pr, swa_o, mk, mv, wout_bf, tabs_p, gain, bias, tile,
             ps, state, ckt, cvt, cmk, cmv, tabs_s):
    nt = SEQ // tile
    assert BATCH * nt == DEC_BATCH
    step = lambda b, t: b * nt + t
    row = lambda w: pl.BlockSpec((tile, w), lambda b, t: (step(b, t), 0))
    const = lambda a: pl.BlockSpec(a.shape, lambda b, t: (0,) * a.ndim)
    mem_p = pl.BlockSpec((None, N_MEM * MEM_HEADS, MEM_HD), lambda b, t: (b, 0, 0))
    st_p = pl.BlockSpec((None, RET_HEADS * RET_DK, RET_DV), lambda b, t: (b, 0, 0))
    p_args = (x2d, pr["rq"], pr["rk"], pr["rv"], pr["rg"], swa_o, pr["sg"], pr["mq"], pr["mg"],
              mk, mv, wout_bf, *tabs_p, gain, bias)
    p_specs = [row(D_MODEL), row(RET_HEADS * RET_DK), row(RET_HEADS * RET_DK), row(RET_W), row(RET_W),
               pl.BlockSpec((SWA_W // LANES, tile, LANES), lambda b, t: (0, step(b, t), 0)),
               row(SWA_W), row(MEM_W), row(MEM_W), mem_p, mem_p, const(wout_bf),
               *[const(a) for a in tabs_p], const(gain), const(bias)]
    n = DEC_PAD
    wb = ckt.shape[3]
    srow = lambda w: pl.BlockSpec((n, w), lambda b, t: (step(b, t), 0))
    st_s = pl.BlockSpec((None, RET_HEADS * RET_DK, RET_DV), lambda b, t: (step(b, t), 0, 0))
    cache_s = pl.BlockSpec((None, SWA_HEADS, SWA_HD, wb), lambda b, t: (step(b, t), 0, 0, 0))
    mem_s = pl.BlockSpec((None, N_MEM * MEM_HEADS, MEM_HD), lambda b, t: (step(b, t), 0, 0))
    s_names = ("rq", "rk", "rv", "rg", "sq", "sk", "sv", "sg", "mq", "mg")
    widths = {name: w for name, w, _, _ in _PROJ_COLS}
    s_args = (*[ps[k] for k in s_names], state, ckt, cvt, cmk, cmv, *tabs_s)
    s_specs = [*[srow(widths[k]) for k in s_names], st_s, cache_s, cache_s, mem_s, mem_s,
               *[const(a) for a in tabs_s]]
    return pl.pallas_call(
        functools.partial(_mix_all_kernel, tile=tile, n_in=(len(p_args), len(s_args)), n_out=(2, 2)),
        grid=(BATCH, nt),
        in_specs=p_specs + s_specs,
        out_specs=[row(D_MODEL), st_p, srow(D_MIX), st_s],
        out_shape=[jax.ShapeDtypeStruct((BATCH * SEQ, D_MODEL), F32),
                   jax.ShapeDtypeStruct((BATCH, RET_HEADS * RET_DK, RET_DV), F32),
                   jax.ShapeDtypeStruct((DEC_BATCH * n, D_MIX), F32),
                   jax.ShapeDtypeStruct((DEC_BATCH, RET_HEADS * RET_DK, RET_DV), F32)],
        scratch_shapes=[pltpu.VMEM((RET_HEADS // 2, LANES, RET_DV), F32),
                        pltpu.VMEM((tile, D_MIX), BF16),
                        pltpu.VMEM((N_MEM, MEM_W), BF16),
                        pltpu.VMEM((N_MEM, MEM_W), BF16)],
        compiler_params=_params(2),
        name="mix_all",
    )(*p_args, *s_args)


def _mix_sample_kernel(rq_ref, rk_ref, rv_ref, rg_ref, sq_ref, sk_ref, sv_ref, sg_ref, mq_ref, mg_ref,
                       state_ref, ckt_ref, cvt_ref, cmk_ref, cmv_ref,
                       dmat_ref, kdec_ref, qdec_ref, gdec_ref,
                       mix_ref, state_out_ref):
    n = DEC_PAD
    q = rq_ref[...]
    k = rk_ref[...]
    kb = k.astype(BF16)
    kd = (k * kdec_ref[...]).astype(BF16)
    qd = q * qdec_ref[...]
    st = state_ref[...]
    stb = st.astype(BF16)
    vb = rv_ref[...].astype(BF16)
    kv = _dot_tn(kd, vb)
    lane_qk = lax.broadcasted_iota(jnp.int32, q.shape, 1) // RET_DK
    ret_s = [_dot_nt(jnp.where(lane_qk == h, q, 0.0).astype(BF16), kb) for h in range(RET_HEADS)]
    ret_cross = [_dot(jnp.where(lane_qk == h, qd, 0.0).astype(BF16), stb) for h in range(RET_HEADS)]
    mem_rows = [pl.ds(h, N_MEM, stride=MEM_HEADS) for h in range(MEM_HEADS)]
    mem_s = [_dot_nt(mq_ref[:, pl.ds(h * MEM_HD, MEM_HD)].astype(BF16), cmk_ref[mem_rows[h], :].astype(BF16))
             for h in range(MEM_HEADS)]

    blk = SWA_BLOCK
    wb = ckt_ref.shape[2]
    rt = 8
    lo = lax.broadcasted_iota(jnp.int32, (n, LANES), 1) < SWA_HD
    lo_t = lax.broadcasted_iota(jnp.int32, (rt, LANES), 1) < SWA_HD
    tok = lambda w: lax.broadcasted_iota(jnp.int32, (rt, w), 0)
    col = lambda w: lax.broadcasted_iota(jnp.int32, (rt, w), 1)
    windows = (
        (wb - blk, jnp.where(col(blk) >= tok(blk), 0.0, NEG), jnp.where(col(blk) <= tok(blk), 0.0, NEG)),
        (wb - 4 * blk, jnp.where(col(4 * blk) % 4 == tok(4 * blk), 0.0, NEG),
         jnp.where(col(blk) == tok(blk), 0.0, NEG)),
        (0, jnp.where(col(wb) % 16 == tok(wb), 0.0, NEG), jnp.where(col(blk) == tok(blk), 0.0, NEG)),
    )
    pad = jnp.zeros((blk - n, LANES), BF16)
    zero_c = jnp.zeros((rt, wb), F32)
    zero_n = jnp.zeros((rt, blk), F32)
    n_pairs = SWA_HEADS // 2
    n_win = len(windows)
    pair_cols = [pl.ds(pair * LANES, LANES) for pair in range(n_pairs)]
    vps = [cvt_ref[2 * pair:2 * pair + 2].reshape(2 * SWA_HD, wb).astype(BF16) for pair in range(n_pairs)]
    v_news = [jnp.concatenate([sv_ref[:, cs].astype(BF16), pad], axis=0) for cs in pair_cols]
    s_alls, sn_alls = [], []
    for pair, cs in enumerate(pair_cols):
        kp = ckt_ref[2 * pair:2 * pair + 2].reshape(2 * SWA_HD, wb).astype(BF16)
        k_new = jnp.concatenate([sk_ref[:, cs].astype(BF16), pad], axis=0)
        q = sq_ref[:, cs]
        qs = jnp.concatenate([jnp.where(lo, q, 0.0), jnp.where(lo, 0.0, q)], axis=0).astype(BF16)
        s_alls.append(_dot(qs, kp))
        sn_alls.append(_dot_nt(qs, k_new))
    p_rows = [[] for _ in range(n_pairs)]
    pn_rows = [[] for _ in range(n_pairs)]
    stats = [[] for _ in range(n_pairs)]
    for pair in range(n_pairs):
        for hh in range(2):
            s = s_alls[pair][hh * n:hh * n + rt]
            sn = sn_alls[pair][hh * n:hh * n + rt]
            for w0, bias_c, bias_n in windows:
                sc = s[:, w0:] + bias_c
                snb = sn + bias_n
                m = jnp.maximum(jnp.max(sc, axis=-1, keepdims=True), jnp.max(snb, axis=-1, keepdims=True))
                pc = jnp.exp(sc - m)
                pn = jnp.exp(snb - m)
                stats[pair].append((m, jnp.sum(pc, axis=-1, keepdims=True) + jnp.sum(pn, axis=-1, keepdims=True)))
                if w0:
                    pc = jnp.concatenate([jnp.zeros((rt, w0), F32), pc], axis=1)
                p_rows[pair].append(pc)
                pn_rows[pair].append(pn)
            p_rows[pair].append(zero_c)
            pn_rows[pair].append(zero_n)
    ret_sb = [(ret_s[h] * dmat_ref[h]).astype(BF16) for h in range(RET_HEADS)]
    mem_p = [_softmax_rows(s * MEM_SCALE).astype(BF16) for s in mem_s]
    pvs = [_dot_nt(jnp.concatenate(p_rows[pair], axis=0).astype(BF16), vps[pair])
           + _dot(jnp.concatenate(pn_rows[pair], axis=0).astype(BF16), v_news[pair])
           for pair in range(n_pairs)]
    ret_intra = [_dot(ret_sb[h], vb[:, h * RET_DV:(h + 1) * RET_DV]) for h in range(RET_HEADS)]
    mem_o = [_dot(mem_p[h], cmv_ref[mem_rows[h], :].astype(BF16)) for h in range(MEM_HEADS)]
    pieces = []
    for pair in range(n_pairs):
        heads = []
        for hh in range(2):
            parts = []
            for i in range(n_win):
                m, l = stats[pair][hh * n_win + i]
                r0 = (hh * (n_win + 1) + i) * rt
                parts.append((pvs[pair][r0:r0 + rt] / l, m, l))
            m_all = jnp.maximum(jnp.maximum(parts[0][1], parts[1][1]), parts[2][1])
            ws = [l * jnp.exp(m - m_all) for (_, m, l) in parts]
            heads.append((ws[0] * parts[0][0] + ws[1] * parts[1][0] + ws[2] * parts[2][0])
                         / (ws[0] + ws[1] + ws[2]))
        pieces.append(jnp.where(lo_t, heads[0], heads[1]))
    swa = jnp.concatenate(pieces, axis=1)
    swa = jnp.concatenate([swa, jnp.zeros((n - rt, SWA_W), F32)], axis=0)
    mix_ref[:, pl.ds(RET_W, SWA_W)] = sg_ref[...] * swa
    for h in range(RET_HEADS):
        hs = pl.ds(h * RET_DV, RET_DV)
        ks = pl.ds(h * RET_DK, RET_DK)
        mix_ref[:, hs] = rg_ref[:, hs] * _head_norm(ret_intra[h] + ret_cross[h])
        state_out_ref[ks, :] = (gdec_ref[ks, :] * st[h * RET_DK:(h + 1) * RET_DK, :]
                                + kv[h * RET_DK:(h + 1) * RET_DK, h * RET_DV:(h + 1) * RET_DV])
    for h in range(MEM_HEADS):
        hs = pl.ds(h * MEM_HD, MEM_HD)
        mix_ref[:, pl.ds(RET_W + SWA_W + h * MEM_HD, MEM_HD)] = mg_ref[:, hs] * mem_o[h]


def _finish_kernel(x_ref, mix_ref, wout_ref, gain_ref, bias_ref, y_ref):
    hout = _dot(mix_ref[...].astype(BF16), wout_ref[...])
    y_ref[...] = _deepnorm_ln(x_ref[...], hout, gain_ref[...], bias_ref[...])


def _finish(x2d, mix, wout_bf, gain, bias):
    n = x2d.shape[0]
    tile = 256
    const2 = lambda a: pl.BlockSpec(a.shape, lambda i: (0, 0))
    return pl.pallas_call(
        _finish_kernel,
        grid=(n // tile,),
        in_specs=[pl.BlockSpec((tile, D_MODEL), lambda i: (i, 0)),
                  pl.BlockSpec((tile, D_MIX), lambda i: (i, 0)),
                  const2(wout_bf), const2(gain), const2(bias)],
        out_specs=pl.BlockSpec((tile, D_MODEL), lambda i: (i, 0)),
        out_shape=jax.ShapeDtypeStruct((n, D_MODEL), F32),
        compiler_params=_params(1),
        name="finish",
    )(x2d, mix, wout_bf, gain, bias)


def _rope_tables(pos):
    half = SWA_HD // 2
    inv = ROPE_THETA ** (-jnp.arange(half, dtype=F32) * 2.0 / SWA_HD)
    ang = pos.astype(F32)[:, None] * inv[None, :]
    cos = jnp.cos(ang)
    sin = jnp.sin(ang)
    reps = LANES // SWA_HD
    return (jnp.tile(jnp.concatenate([cos, cos], axis=1), (1, reps)),
            jnp.tile(jnp.concatenate([-sin, sin], axis=1), (1, reps)))


def _retention_tables(chunk, rows):
    lg = jnp.log1p(-jnp.exp2(-5.0 - jnp.arange(RET_HEADS, dtype=F32)))
    idx = jnp.arange(rows, dtype=F32)
    live = idx < chunk
    rel = idx[:, None] - idx[None, :]
    ok = (rel >= 0) & live[:, None] & live[None, :]
    dmat = jnp.where(ok[None], jnp.exp(jnp.maximum(rel, 0.0)[None] * lg[:, None, None]), 0.0)
    kdec = jnp.where(live[:, None], jnp.exp((chunk - 1.0 - idx)[:, None] * lg[None, :]), 0.0)
    qdec = jnp.where(live[:, None], jnp.exp((idx + 1.0)[:, None] * lg[None, :]), 0.0)
    g = jnp.exp(chunk * lg)
    kdec = jnp.repeat(kdec, RET_DK, axis=1)
    qdec = jnp.repeat(qdec, RET_DK, axis=1)
    gdec = jnp.broadcast_to(jnp.repeat(g, RET_DK)[:, None], (RET_HEADS * RET_DK, RET_DV))
    return dmat, kdec, qdec, gdec


def kernel(x_prompt, x_sample, state_ret, cache_swa_k, cache_swa_v, cache_mem_k, cache_mem_v,
           mem_prompt, w_in, w_mem_kv, w_out, ln_gain, ln_bias):
    depth = w_in.shape[0]
    assert depth == 1
    win_bf = w_in[0].astype(BF16)
    wmem_bf = w_mem_kv[0].astype(BF16)
    wout_bf = w_out[0].astype(BF16)
    gain = ln_gain[0].reshape(1, D_MODEL)
    bias = ln_bias[0].reshape(1, D_MODEL)

    xp = x_prompt.reshape(BATCH * SEQ, D_MODEL)
    cos_p, sin_p = _rope_tables(jnp.arange(SEQ))
    p_outs = (("rq", "rows", F32), ("rk", "rows", F32), ("rv", "rows", BF16), ("rg", "rows", BF16),
              ("sq", "perm", F32), ("sk", "cols", F32), ("sk", "perm", F32), ("sv", "cols", F32),
              ("sv", "perm", F32), ("sg", "rows", BF16), ("mq", "rows", BF16), ("mg", "rows", BF16))
    p_keys = ("rq", "rk", "rv", "rg", "sq4", "sk", "sk4", "sv", "sv4", "sg", "mq", "mg")
    pr = dict(zip(p_keys, _project(xp, win_bf, cos_p, sin_p, PERM_TILE, p_outs)))
    mk, mv = _memkv(mem_prompt.reshape(BATCH * N_MEM, D_MODEL), wmem_bf)
    slab = lambda a: a.reshape(SWA_W // LANES, BATCH, SEQ // PERM_TILE * PERM_ROWS, LANES)
    swa_o = _swa_prompt(slab(pr["sq4"]), slab(pr["sk4"]), slab(pr["sv4"]))
    swa_o = swa_o.reshape(SWA_W // LANES, BATCH * SEQ, LANES)
    tabs_p = _retention_tables(RET_CHUNK, RET_CHUNK)

    xs = jnp.pad(x_sample, ((0, 0), (0, DEC_PAD - DEC_SEQ), (0, 0))).reshape(DEC_BATCH * DEC_PAD, D_MODEL)
    pos_s = PAST_LEN + jnp.arange(DEC_BATCH * DEC_PAD) % DEC_PAD
    cos_s, sin_s = _rope_tables(pos_s)
    names = [c[0] for c in _PROJ_COLS]
    ps = dict(zip(names, _project(xs, win_bf, cos_s, sin_s, DEC_BATCH * DEC_PAD,
                                  [(k, "rows", F32) for k in names])))
    tabs_s = _retention_tables(DEC_SEQ, DEC_PAD)
    yp, ret_p, mix_s, ret_s = _mix_all(
        xp, pr, swa_o, mk, mv, wout_bf, tabs_p, gain, bias, 512,
        ps, state_ret[0].reshape(DEC_BATCH, RET_HEADS * RET_DK, RET_DV),
        cache_swa_k[0].transpose(0, 2, 3, 1), cache_swa_v[0].transpose(0, 2, 3, 1),
        cache_mem_k[0].reshape(DEC_BATCH, N_MEM * MEM_HEADS, MEM_HD),
        cache_mem_v[0].reshape(DEC_BATCH, N_MEM * MEM_HEADS, MEM_HD),
        tabs_s)
    ys = _finish(xs, mix_s, wout_bf, gain, bias)

    take = lambda a, w: a.reshape(DEC_BATCH, DEC_PAD, w)[:, :DEC_SEQ]
    swa_rows = lambda a: a.reshape(BATCH, SWA_HEADS, SWA_HD, SEQ).transpose(0, 3, 1, 2)[None]
    return (
        yp.reshape(BATCH, SEQ, D_MODEL),
        take(ys, D_MODEL),
        ret_p.reshape(1, BATCH, RET_HEADS, RET_DK, RET_DV),
        ret_s.reshape(1, DEC_BATCH, RET_HEADS, RET_DK, RET_DV),
        swa_rows(pr["sk"]),
        swa_rows(pr["sv"]),
        take(ps["sk"], SWA_W).reshape(1, DEC_BATCH, DEC_SEQ, SWA_HEADS, SWA_HD),
        take(ps["sv"], SWA_W).reshape(1, DEC_BATCH, DEC_SEQ, SWA_HEADS, SWA_HD),
        mk.reshape(1, BATCH, N_MEM, MEM_HEADS, MEM_HD),
        mv.reshape(1, BATCH, N_MEM, MEM_HEADS, MEM_HD),
    )
```

```python
import functools

import jax
import jax.numpy as jnp
import numpy as np
from jax import lax
from jax.experimental import pallas as pl
from jax.experimental.pallas import tpu as pltpu

F32 = jnp.float32
BF16 = jnp.bfloat16

D_MODEL = 1024
BATCH = 8
SEQ = 2048
DEC_BATCH = 32
DEC_SEQ = 4
PAST_LEN = 8192
N_MEM = 256
MEM_HEADS = 4
MEM_HD = 128
RET_HEADS = 4
RET_DK = 64
RET_DV = 128
RET_CHUNK = 128
SWA_HEADS = 8
SWA_HD = 64
SWA_DILATIONS = (1, 4, 16)
SWA_STEPS = 128
SWA_BLOCK = 128
ROPE_THETA = 10000.0
LN_EPS = 1e-5
GN_EPS = 1e-5
RET_W = RET_HEADS * RET_DV
SWA_W = SWA_HEADS * SWA_HD
MEM_W = MEM_HEADS * MEM_HD
D_MIX = RET_W + SWA_W + MEM_W
DEEPNORM_ALPHA = 2.0 ** 0.25
MEM_SCALE = MEM_HD ** -0.5
QK_SCALE = 0.125

LANES = 128
DEC_PAD = 16
PERM_TILE = 512
PERM_PITCH = 40
PERM_ROWS = 16 * PERM_PITCH
OUT_ROWS = 256
SWA_PAR = 2
VMEM_LIMIT = 56 * 1024 * 1024
NEG = -1e30

_PROJ_COLS = (
    ("rq", RET_HEADS * RET_DK, "rope", 1.0),
    ("rk", RET_HEADS * RET_DK, "rope", QK_SCALE),
    ("rv", RET_W, "id", 1.0),
    ("rg", RET_W, "silu", 1.0),
    ("sq", SWA_W, "rope", QK_SCALE),
    ("sk", SWA_W, "rope", 1.0),
    ("sv", SWA_W, "id", 1.0),
    ("sg", SWA_W, "silu", 1.0),
    ("mq", MEM_W, "id", 1.0),
    ("mg", MEM_W, "silu", 1.0),
)


def _dot(a, b):
    return jnp.dot(a, b, preferred_element_type=F32)


def _dot_nt(a, b):
    return lax.dot_general(a, b, (((1,), (1,)), ((), ())), preferred_element_type=F32)


def _dot_tn(a, b):
    return lax.dot_general(a, b, (((0,), (0,)), ((), ())), preferred_element_type=F32)


def _params(n_axes):
    return pltpu.CompilerParams(dimension_semantics=("arbitrary",) * n_axes,
                                vmem_limit_bytes=VMEM_LIMIT)


def _proj_kernel(x_ref, w_ref, cos_ref, sin_ref, *out_refs, dests):
    xb = x_ref[...].astype(BF16)
    cos = cos_ref[...]
    sin = sin_ref[...]
    lane = lax.broadcasted_iota(jnp.int32, cos.shape, 1)
    first_half = (lane % 64) < 32
    col = 0
    for name, width, kind, scale in _PROJ_COLS:
        targets = [(o_ref, layout) for o_ref, (dname, layout) in zip(out_refs, dests) if dname == name]
        for c in range(0, width, 2 * LANES):
            h2 = _dot(xb, w_ref[:, col + c:col + c + 2 * LANES])
            for half in range(2):
                h = h2[:, half * LANES:(half + 1) * LANES]
                if kind == "rope":
                    swapped = jnp.where(first_half, pltpu.roll(h, 96, 1), pltpu.roll(h, 32, 1))
                    h = h * cos + swapped * sin
                    if scale != 1.0:
                        h = h * scale
                elif kind == "silu":
                    h = h * (1.0 / (1.0 + jnp.exp(-h)))
                lo = c + half * LANES
                for o_ref, layout in targets:
                    if layout == "rows":
                        o_ref[:, lo:lo + LANES] = h.astype(o_ref.dtype)
                    elif layout == "perm":
                        slab = lo // LANES
                        for g in range(h.shape[0] // 8):
                            base = (g % 2) * 8 * PERM_PITCH + g // 2
                            o_ref[slab, pl.ds(base, 8, stride=PERM_PITCH), :] = (
                                h[8 * g:8 * g + 8].astype(o_ref.dtype))
                        n_live = h.shape[0] // 16
                        for cls in range(16):
                            o_ref[slab, pl.ds(cls * PERM_PITCH + n_live, PERM_PITCH - n_live), :] = (
                                jnp.zeros((PERM_PITCH - n_live, LANES), o_ref.dtype))
                    else:
                        o_ref[lo:lo + LANES, :] = h.T.astype(o_ref.dtype)
        col += width


def _project(x2d, w_bf, cos_t, sin_t, tile, outs):
    n = x2d.shape[0]
    seq = cos_t.shape[0]
    n_tab = seq // tile
    d_in = w_bf.shape[1]
    widths = {name: w for name, w, _, _ in _PROJ_COLS}
    out_shape, out_specs = [], []
    for name, layout, dt in outs:
        w = widths[name]
        if layout == "rows":
            out_shape.append(jax.ShapeDtypeStruct((n, w), dt))
            out_specs.append(pl.BlockSpec((tile, w), lambda i: (i, 0)))
        elif layout == "perm":
            assert tile == PERM_TILE
            out_shape.append(jax.ShapeDtypeStruct((w // LANES, n // tile * PERM_ROWS, LANES), dt))
            out_specs.append(pl.BlockSpec((w // LANES, PERM_ROWS, LANES), lambda i: (0, i, 0)))
        else:
            out_shape.append(jax.ShapeDtypeStruct((n // seq, w, seq), dt))
            out_specs.append(pl.BlockSpec((None, w, tile), lambda i: (i // n_tab, 0, i % n_tab)))
    return pl.pallas_call(
        functools.partial(_proj_kernel, dests=tuple((name, layout) for name, layout, _ in outs)),
        grid=(n // tile,),
        in_specs=[
            pl.BlockSpec((tile, D_MODEL), lambda i: (i, 0)),
            pl.BlockSpec((D_MODEL, d_in), lambda i: (0, 0)),
            pl.BlockSpec((tile, LANES), lambda i: (i % n_tab, 0)),
            pl.BlockSpec((tile, LANES), lambda i: (i % n_tab, 0)),
        ],
        out_specs=out_specs,
        out_shape=out_shape,
        compiler_params=_params(1),
        name="proj",
    )(x2d, w_bf, cos_t, sin_t)


def _memkv_kernel(m_ref, w_ref, mk_ref, mv_ref):
    mb = m_ref[...].astype(BF16)
    for h in range(MEM_HEADS):
        rows = pl.ds(h, N_MEM, stride=MEM_HEADS)
        mk_ref[rows, :] = _dot(mb, w_ref[:, h * MEM_HD:(h + 1) * MEM_HD])
        mv_ref[rows, :] = _dot(mb, w_ref[:, MEM_W + h * MEM_HD:MEM_W + (h + 1) * MEM_HD])


def _memkv(mem2d, w_bf):
    n = mem2d.shape[0] // N_MEM
    out = pl.BlockSpec((None, N_MEM * MEM_HEADS, MEM_HD), lambda i: (i, 0, 0))
    return pl.pallas_call(
        _memkv_kernel,
        grid=(n,),
        in_specs=[pl.BlockSpec((N_MEM, D_MODEL), lambda i: (i, 0)),
                  pl.BlockSpec((D_MODEL, 2 * MEM_W), lambda i: (0, 0))],
        out_specs=[out, out],
        out_shape=[jax.ShapeDtypeStruct((n, N_MEM * MEM_HEADS, MEM_HD), F32)] * 2,
        compiler_params=_params(1),
        name="memkv",
    )(mem2d, w_bf)


def _swa_prompt_kernel(q_ref, k_ref, v_ref, o_ref, a_ref, m_ref, l_ref):
    blk = SWA_BLOCK
    pt = PERM_TILE
    grp = pt // 16
    lo = lax.broadcasted_iota(jnp.int32, (blk, blk), 1) < SWA_HD

    def biases(seq_of):
        rq = seq_of(lax.broadcasted_iota(jnp.int32, (blk, 2 * blk), 0))
        c2 = lax.broadcasted_iota(jnp.int32, (blk, 2 * blk), 1)
        rel = blk + rq - (seq_of(c2 % blk) + blk * (c2 // blk))
        prev = jnp.where((rel >= 0) & (rel <= SWA_STEPS), 0.0, NEG).astype(F32)
        r1 = seq_of(lax.broadcasted_iota(jnp.int32, (blk, blk), 0))
        c1 = seq_of(lax.broadcasted_iota(jnp.int32, (blk, blk), 1))
        return prev, jnp.where(r1 >= c1, 0.0, NEG).astype(F32)

    def gather(ref, pair, starts, n):
        return jnp.concatenate([ref[pair, pl.ds(s, n), :] for s in starts], axis=0)

    def aligned(x):
        return x if isinstance(x, int) else pl.multiple_of(x, 8)

    def blocks(jobs, n, first_pattern, last_pattern):
        n_pairs = SWA_HEADS // 2
        in_at = lambda c: aligned(c[0] * PERM_ROWS + c[1] * PERM_PITCH + c[2])
        acc_starts, q_starts, k_starts = [], [], []
        for chunks, prev_chunks, _, _ in jobs:
            acc_starts.append([aligned(c[0] * pt + c[1] * grp + c[2]) for c in chunks])
            q_starts.append([in_at(c) for c in chunks])
            k_starts.append(([] if prev_chunks is None else [in_at(c) for c in prev_chunks]) + q_starts[-1])
        items = [(j, pair) for j in range(len(jobs)) for pair in range(n_pairs)]
        olds = {}
        if not first_pattern:
            for j, pair in items:
                olds[j, pair] = (gather(m_ref, pair, acc_starts[j], n), gather(l_ref, pair, acc_starts[j], n),
                                 gather(a_ref, pair, acc_starts[j], n))
        vbs = {(j, pair): gather(v_ref, pair, k_starts[j], n).astype(BF16) for j, pair in items}
        scores = {}
        for j, pair in items:
            q = gather(q_ref, pair, q_starts[j], n)
            kb = gather(k_ref, pair, k_starts[j], n).astype(BF16)
            for hh in range(2):
                qm = jnp.where(lo if hh == 0 else ~lo, q, 0.0).astype(BF16)
                scores[j, pair, hh] = _dot_nt(qm, kb) + jobs[j][2]
        ms = {key: jnp.max(s, axis=-1, keepdims=True) for key, s in scores.items()}
        ps = {key: jnp.exp(s - ms[key]) for key, s in scores.items()}
        def v_ext(v, hh):
            own = lax.broadcasted_iota(jnp.int32, v.shape, 1) < SWA_HD
            return jnp.where(own if hh == 0 else ~own, v, jnp.ones((), BF16))

        pvs = {(j, pair, hh): _dot(p.astype(BF16), v_ext(vbs[j, pair], hh)) for (j, pair, hh), p in ps.items()}
        news = {}
        for j, pair in items:
            m_g = jnp.where(lo, ms[j, pair, 0], ms[j, pair, 1])
            l_g = pltpu.roll(jnp.where(lo, pvs[j, pair, 1], pvs[j, pair, 0]), SWA_HD, 1)
            a_g = jnp.where(lo, pvs[j, pair, 0], pvs[j, pair, 1])
            if first_pattern:
                news[j, pair] = (m_g, l_g, a_g)
            else:
                m_old, l_old, a_old = olds[j, pair]
                m_new = jnp.maximum(m_old, m_g)
                w_old = jnp.exp(m_old - m_new)
                w_g = jnp.exp(m_g - m_new)
                news[j, pair] = (m_new, w_old * l_old + w_g * l_g, w_old * a_old + w_g * a_g)
        for (j, pair), (m_new, l_new, a_new) in news.items():
            starts = acc_starts[j]
            if last_pattern:
                o = a_new / l_new
                for i in range(len(starts)):
                    o_ref[pair, pl.ds(jobs[j][3] + i, n, stride=len(starts)), :] = o[i * n:(i + 1) * n]
            else:
                for i, s in enumerate(starts):
                    m_ref[pair, pl.ds(s, n), :] = m_new[i * n:(i + 1) * n]
                    l_ref[pair, pl.ds(s, n), :] = l_new[i * n:(i + 1) * n]
                    a_ref[pair, pl.ds(s, n), :] = a_new[i * n:(i + 1) * n]

    par = SWA_PAR
    _, bias16 = biases(lambda i: i)

    def class16(r0, carry):
        blocks([([(t, r0 + k * (16 // par), 0) for t in range(SEQ // pt)], None, bias16, None)
                for k in range(par)], grp, True, False)
        return carry

    lax.fori_loop(0, 16 // par, class16, 0)

    bias4_prev, bias4_own = biases(lambda i: 4 * (i % grp) + i // grp)

    def class4(r0, carry):
        chunks_of = lambda r, tile: [(tile, 4 * a + r, 0) for a in range(4)]
        classes = [r0 + k * (4 // par) for k in range(par)]
        blocks([(chunks_of(r, 0), None, bias4_own, None) for r in classes], grp, False, False)

        def per_block(b, c2):
            blocks([(chunks_of(r, b), chunks_of(r, b - 1), bias4_prev, None) for r in classes],
                   grp, False, False)
            return c2

        lax.fori_loop(1, SEQ // pt, per_block, 0)
        return carry

    lax.fori_loop(0, 4 // par, class4, 0)

    bias1_prev, bias1_own = biases(lambda i: 16 * (i % 8) + i // 8)
    per_tile = pt // blk
    chunks1 = lambda c: [(c // per_tile, i, (c % per_tile) * 8) for i in range(16)]
    job1 = lambda c: (chunks1(c), chunks1(c - 1), bias1_prev, aligned(c * blk))
    blocks([(chunks1(0), None, bias1_own, 0)] + [job1(c) for c in range(1, par)], 8, False, True)

    def block1(g, carry):
        blocks([job1(g * par + k) for k in range(par)], 8, False, True)
        return carry

    lax.fori_loop(1, SEQ // blk // par, block1, 0)


def _swa_prompt(sq, sk, sv):
    slabs = SWA_W // LANES
    spec = pl.BlockSpec((slabs, None, SEQ, LANES), lambda b: (0, b, 0, 0))
    in_spec = pl.BlockSpec((slabs, None, sq.shape[2], LANES), lambda b: (0, b, 0, 0))
    return pl.pallas_call(
        _swa_prompt_kernel,
        grid=(BATCH,),
        in_specs=[in_spec, in_spec, in_spec],
        out_specs=spec,
        out_shape=jax.ShapeDtypeStruct((slabs, BATCH, SEQ, LANES), F32),
        scratch_shapes=[pltpu.VMEM((slabs, SEQ, LANES), F32)] * 3,
        compiler_params=_params(1),
        name="swa_prompt",
    )(sq, sk, sv)


def _head_norm(o):
    mu = jnp.mean(o, axis=-1, keepdims=True)
    d = o - mu
    var = jnp.mean(d * d, axis=-1, keepdims=True)
    return d * lax.rsqrt(var + GN_EPS)


def _deepnorm_ln(x, h, gain, bias):
    z = DEEPNORM_ALPHA * x + h
    mu = jnp.mean(z, axis=-1, keepdims=True)
    d = z - mu
    var = jnp.mean(d * d, axis=-1, keepdims=True)
    return d * lax.rsqrt(var + LN_EPS) * gain + bias


def _softmax_rows(s):
    m = jnp.max(s, axis=-1, keepdims=True)
    p = jnp.exp(s - m)
    return p * (1.0 / jnp.sum(p, axis=-1, keepdims=True))


def _mix_prompt_kernel(x_ref, rq_ref, rk_ref, rv_ref, rg_ref, so_ref, sg_ref, mq_ref, mg_ref,
                       mk_ref, mv_ref, wout_ref, dmat_ref, kdec_ref, qdec_ref, gdec_ref,
                       gain_ref, bias_ref, y_ref, state_out_ref,
                       state_ref, mix_ref, mkb_ref, mvb_ref, *, tile):
    t = pl.program_id(1)

    @pl.when(t == 0)
    def _():
        state_ref[...] = jnp.zeros_like(state_ref)
        for h in range(MEM_HEADS):
            rows = pl.ds(h, N_MEM, stride=MEM_HEADS)
            mkb_ref[:, pl.ds(h * MEM_HD, MEM_HD)] = mk_ref[rows, :].astype(BF16)
            mvb_ref[:, pl.ds(h * MEM_HD, MEM_HD)] = mv_ref[rows, :].astype(BF16)

    ck = RET_CHUNK
    lane = lax.broadcasted_iota(jnp.int32, (ck, LANES), 1)
    lo = lane < RET_DK
    top = lax.broadcasted_iota(jnp.int32, (LANES, LANES), 0) < RET_DK
    gain = gain_ref[...]
    bias = bias_ref[...]

    n_ck = tile // ck
    n_pairs = RET_HEADS // 2
    rows_of = [pl.ds(c * ck, ck) for c in range(n_ck)]
    items = [(c, pair, hh) for c in range(n_ck) for pair in range(n_pairs) for hh in range(2)]
    hs_of = lambda pair, hh: pl.ds((2 * pair + hh) * RET_DV, RET_DV)
    qk = {}
    for c in range(n_ck):
        for pair in range(n_pairs):
            cs = pl.ds(pair * LANES, LANES)
            q = rq_ref[rows_of[c], cs]
            k = rk_ref[rows_of[c], cs]
            qk[c, pair] = (q, k.astype(BF16), (k * kdec_ref[:, cs]).astype(BF16), q * qdec_ref[:, cs])
    sel = lambda x, hh: jnp.where(lo if hh == 0 else ~lo, x, 0.0).astype(BF16)
    s = {(c, pair, hh): _dot_nt(sel(qk[c, pair][0], hh), qk[c, pair][1]) * dmat_ref[2 * pair + hh]
         for c, pair, hh in items}
    kv = {(c, pair, hh): _dot_tn(qk[c, pair][2], rv_ref[rows_of[c], hs_of(pair, hh)]) for c, pair, hh in items}
    intra = {(c, pair, hh): _dot(s[c, pair, hh].astype(BF16), rv_ref[rows_of[c], hs_of(pair, hh)])
             for c, pair, hh in items}
    state = {(0, pair): state_ref[pair] for pair in range(n_pairs)}
    for c in range(n_ck):
        for pair in range(n_pairs):
            state[c + 1, pair] = (gdec_ref[pl.ds(pair * LANES, LANES), :] * state[c, pair]
                                  + jnp.where(top, kv[c, pair, 0], kv[c, pair, 1]))
    for pair in range(n_pairs):
        state_ref[pair] = state[n_ck, pair]
    cross = {(c, pair, hh): _dot(sel(qk[c, pair][3], hh), state[c, pair].astype(BF16)) for c, pair, hh in items}
    for c, pair, hh in items:
        hs = hs_of(pair, hh)
        o = intra[c, pair, hh] + cross[c, pair, hh]
        mix_ref[rows_of[c], hs] = (rg_ref[rows_of[c], hs].astype(F32) * _head_norm(o)).astype(BF16)
    for pair in range(SWA_W // LANES):
        cs = pl.ds(pair * LANES, LANES)
        mix_ref[:, pl.ds(RET_W + pair * LANES, LANES)] = (
            sg_ref[:, cs].astype(F32) * so_ref[pair]).astype(BF16)
    for c in range(n_ck):
        ps = [_softmax_rows(_dot_nt(mq_ref[rows_of[c], pl.ds(h * MEM_HD, MEM_HD)],
                                    mkb_ref[:, pl.ds(h * MEM_HD, MEM_HD)]) * MEM_SCALE)
              for h in range(MEM_HEADS)]
        for h in range(MEM_HEADS):
            hs = pl.ds(h * MEM_HD, MEM_HD)
            o = _dot(ps[h].astype(BF16), mvb_ref[:, hs])
            mix_ref[rows_of[c], pl.ds(RET_W + SWA_W + h * MEM_HD, MEM_HD)] = (
                mg_ref[rows_of[c], hs].astype(F32) * o).astype(BF16)
    blocks = [pl.ds(r * OUT_ROWS, OUT_ROWS) for r in range(tile // OUT_ROWS)]
    hout = _dot(mix_ref[blocks[0], :], wout_ref[...])
    for r, rows in enumerate(blocks):
        nxt = _dot(mix_ref[blocks[r + 1], :], wout_ref[...]) if r + 1 < len(blocks) else None
        y_ref[rows, :] = _deepnorm_ln(x_ref[rows, :], hout, gain, bias)
        hout = nxt

    @pl.when(t == pl.num_programs(1) - 1)
    def _():
        state_out_ref[pl.ds(0, LANES), :] = state_ref[0]
        state_out_ref[pl.ds(LANES, LANES), :] = state_ref[1]


def _mix_all_kernel(*refs, tile, n_in, n_out):
    p_in, s_in = refs[:n_in[0]], refs[n_in[0]:n_in[0] + n_in[1]]
    outs = refs[n_in[0] + n_in[1]:]
    p_out, s_out = outs[:n_out[0]], outs[n_out[0]:n_out[0] + n_out[1]]
    scratch = outs[n_out[0] + n_out[1]:]
    _mix_sample_kernel(*s_in, *s_out)
    _mix_prompt_kernel(*p_in, *p_out, *scratch, tile=tile)


def _mix_all(x2d, pr, swa_o, mk, mv, wout_bf, tabs_p, gain, bias, tile,
             ps, state, ckt, cvt, cmk, cmv, tabs_s):
    nt = SEQ // tile
    assert BATCH * nt == DEC_BATCH
    step = lambda b, t: b * nt + t
    row = lambda w: pl.BlockSpec((tile, w), lambda b, t: (step(b, t), 0))
    const = lambda a: pl.BlockSpec(a.shape, lambda b, t: (0,) * a.ndim)
    mem_p = pl.BlockSpec((None, N_MEM * MEM_HEADS, MEM_HD), lambda b, t: (b, 0, 0))
    st_p = pl.BlockSpec((None, RET_HEADS * RET_DK, RET_DV), lambda b, t: (b, 0, 0))
    p_args = (x2d, pr["rq"], pr["rk"], pr["rv"], pr["rg"], swa_o, pr["sg"], pr["mq"], pr["mg"],
              mk, mv, wout_bf, *tabs_p, gain, bias)
    p_specs = [row(D_MODEL), row(RET_HEADS * RET_DK), row(RET_HEADS * RET_DK), row(RET_W), row(RET_W),
               pl.BlockSpec((SWA_W // LANES, tile, LANES), lambda b, t: (0, step(b, t), 0)),
               row(SWA_W), row(MEM_W), row(MEM_W), mem_p, mem_p, const(wout_bf),
               *[const(a) for a in tabs_p], const(gain), const(bias)]
    n = DEC_PAD
    wb = ckt.shape[3]
    srow = lambda w: pl.BlockSpec((n, w), lambda b, t: (step(b, t), 0))
    st_s = pl.BlockSpec((None, RET_HEADS * RET_DK, RET_DV), lambda b, t: (step(b, t), 0, 0))
    cache_s = pl.BlockSpec((None, SWA_HEADS, SWA_HD, wb), lambda b, t: (step(b, t), 0, 0, 0))
    mem_s = pl.BlockSpec((None, N_MEM * MEM_HEADS, MEM_HD), lambda b, t: (step(b, t), 0, 0))
    s_names = ("rq", "rk", "rv", "rg", "sq", "sk", "sv", "sg", "mq", "mg")
    widths = {name: w for name, w, _, _ in _PROJ_COLS}
    s_args = (*[ps[k] for k in s_names], state, ckt, cvt, cmk, cmv, *tabs_s)
    s_specs = [*[srow(widths[k]) for k in s_names], st_s, cache_s, cache_s, mem_s, mem_s,
               *[const(a) for a in tabs_s]]
    return pl.pallas_call(
        functools.partial(_mix_all_kernel, tile=tile, n_in=(len(p_args), len(s_args)), n_out=(2, 2)),
        grid=(BATCH, nt),
        in_specs=p_specs + s_specs,
        out_specs=[row(D_MODEL), st_p, srow(D_MIX), st_s],
        out_shape=[jax.ShapeDtypeStruct((BATCH * SEQ, D_MODEL), F32),
                   jax.ShapeDtypeStruct((BATCH, RET_HEADS * RET_DK, RET_DV), F32),
                   jax.ShapeDtypeStruct((DEC_BATCH * n, D_MIX), F32),
                   jax.ShapeDtypeStruct((DEC_BATCH, RET_HEADS * RET_DK, RET_DV), F32)],
        scratch_shapes=[pltpu.VMEM((RET_HEADS // 2, LANES, RET_DV), F32),
                        pltpu.VMEM((tile, D_MIX), BF16),
                        pltpu.VMEM((N_MEM, MEM_W), BF16),
                        pltpu.VMEM((N_MEM, MEM_W), BF16)],
        compiler_params=_params(2),
        name="mix_all",
    )(*p_args, *s_args)


def _mix_sample_kernel(rq_ref, rk_ref, rv_ref, rg_ref, sq_ref, sk_ref, sv_ref, sg_ref, mq_ref, mg_ref,
                       state_ref, ckt_ref, cvt_ref, cmk_ref, cmv_ref,
                       dmat_ref, kdec_ref, qdec_ref, gdec_ref,
                       mix_ref, state_out_ref):
    n = DEC_PAD
    q = rq_ref[...]
    k = rk_ref[...]
    kb = k.astype(BF16)
    kd = (k * kdec_ref[...]).astype(BF16)
    qd = q * qdec_ref[...]
    st = state_ref[...]
    stb = st.astype(BF16)
    vb = rv_ref[...].astype(BF16)
    kv = _dot_tn(kd, vb)
    lane_qk = lax.broadcasted_iota(jnp.int32, q.shape, 1) // RET_DK
    ret_s = [_dot_nt(jnp.where(lane_qk == h, q, 0.0).astype(BF16), kb) for h in range(RET_HEADS)]
    ret_cross = [_dot(jnp.where(lane_qk == h, qd, 0.0).astype(BF16), stb) for h in range(RET_HEADS)]
    mem_rows = [pl.ds(h, N_MEM, stride=MEM_HEADS) for h in range(MEM_HEADS)]
    mem_s = [_dot_nt(mq_ref[:, pl.ds(h * MEM_HD, MEM_HD)].astype(BF16), cmk_ref[mem_rows[h], :].astype(BF16))
             for h in range(MEM_HEADS)]

    blk = SWA_BLOCK
    wb = ckt_ref.shape[2]
    rt = 8
    lo = lax.broadcasted_iota(jnp.int32, (n, LANES), 1) < SWA_HD
    lo_t = lax.broadcasted_iota(jnp.int32, (rt, LANES), 1) < SWA_HD
    tok = lambda w: lax.broadcasted_iota(jnp.int32, (rt, w), 0)
    col = lambda w: lax.broadcasted_iota(jnp.int32, (rt, w), 1)
    windows = (
        (wb - blk, jnp.where(col(blk) >= tok(blk), 0.0, NEG), jnp.where(col(blk) <= tok(blk), 0.0, NEG)),
        (wb - 4 * blk, jnp.where(col(4 * blk) % 4 == tok(4 * blk), 0.0, NEG),
         jnp.where(col(blk) == tok(blk), 0.0, NEG)),
        (0, jnp.where(col(wb) % 16 == tok(wb), 0.0, NEG), jnp.where(col(blk) == tok(blk), 0.0, NEG)),
    )
    pad = jnp.zeros((blk - n, LANES), BF16)
    zero_c = jnp.zeros((rt, wb), F32)
    zero_n = jnp.zeros((rt, blk), F32)
    n_pairs = SWA_HEADS // 2
    n_win = len(windows)
    pair_cols = [pl.ds(pair * LANES, LANES) for pair in range(n_pairs)]
    vps = [cvt_ref[2 * pair:2 * pair + 2].reshape(2 * SWA_HD, wb).astype(BF16) for pair in range(n_pairs)]
    v_news = [jnp.concatenate([sv_ref[:, cs].astype(BF16), pad], axis=0) for cs in pair_cols]
    s_alls, sn_alls = [], []
    for pair, cs in enumerate(pair_cols):
        kp = ckt_ref[2 * pair:2 * pair + 2].reshape(2 * SWA_HD, wb).astype(BF16)
        k_new = jnp.concatenate([sk_ref[:, cs].astype(BF16), pad], axis=0)
        q = sq_ref[:, cs]
        qs = jnp.concatenate([jnp.where(lo, q, 0.0), jnp.where(lo, 0.0, q)], axis=0).astype(BF16)
        s_alls.append(_dot(qs, kp))
        sn_alls.append(_dot_nt(qs, k_new))
    p_rows = [[] for _ in range(n_pairs)]
    pn_rows = [[] for _ in range(n_pairs)]
    stats = [[] for _ in range(n_pairs)]
    for pair in range(n_pairs):
        for hh in range(2):
            s = s_alls[pair][hh * n:hh * n + rt]
            sn = sn_alls[pair][hh * n:hh * n + rt]
            for w0, bias_c, bias_n in windows:
                sc = s[:, w0:] + bias_c
                snb = sn + bias_n
                m = jnp.maximum(jnp.max(sc, axis=-1, keepdims=True), jnp.max(snb, axis=-1, keepdims=True))
                pc = jnp.exp(sc - m)
                pn = jnp.exp(snb - m)
                stats[pair].append((m, jnp.sum(pc, axis=-1, keepdims=True) + jnp.sum(pn, axis=-1, keepdims=True)))
                if w0:
                    pc = jnp.concatenate([jnp.zeros((rt, w0), F32), pc], axis=1)
                p_rows[pair].append(pc)
                pn_rows[pair].append(pn)
            p_rows[pair].append(zero_c)
            pn_rows[pair].append(zero_n)
    ret_sb = [(ret_s[h] * dmat_ref[h]).astype(BF16) for h in range(RET_HEADS)]
    mem_p = [_softmax_rows(s * MEM_SCALE).astype(BF16) for s in mem_s]
    pvs = [_dot_nt(jnp.concatenate(p_rows[pair], axis=0).astype(BF16), vps[pair])
           + _dot(jnp.concatenate(pn_rows[pair], axis=0).astype(BF16), v_news[pair])
           for pair in range(n_pairs)]
    ret_intra = [_dot(ret_sb[h], vb[:, h * RET_DV:(h + 1) * RET_DV]) for h in range(RET_HEADS)]
    mem_o = [_dot(mem_p[h], cmv_ref[mem_rows[h], :].astype(BF16)) for h in range(MEM_HEADS)]
    pieces = []
    for pair in range(n_pairs):
        heads = []
        for hh in range(2):
            parts = []
            for i in range(n_win):
                m, l = stats[pair][hh * n_win + i]
                r0 = (hh * (n_win + 1) + i) * rt
                parts.append((pvs[pair][r0:r0 + rt] / l, m, l))
            m_all = jnp.maximum(jnp.maximum(parts[0][1], parts[1][1]), parts[2][1])
            ws = [l * jnp.exp(m - m_all) for (_, m, l) in parts]
            heads.append((ws[0] * parts[0][0] + ws[1] * parts[1][0] + ws[2] * parts[2][0])
                         / (ws[0] + ws[1] + ws[2]))
        pieces.append(jnp.where(lo_t, heads[0], heads[1]))
    swa = jnp.concatenate(pieces, axis=1)
    swa = jnp.concatenate([swa, jnp.zeros((n - rt, SWA_W), F32)], axis=0)
    mix_ref[:, pl.ds(RET_W, SWA_W)] = sg_ref[...] * swa
    for h in range(RET_HEADS):
        hs = pl.ds(h * RET_DV, RET_DV)
        ks = pl.ds(h * RET_DK, RET_DK)
        mix_ref[:, hs] = rg_ref[:, hs] * _head_norm(ret_intra[h] + ret_cross[h])
        state_out_ref[ks, :] = (gdec_ref[ks, :] * st[h * RET_DK:(h + 1) * RET_DK, :]
                                + kv[h * RET_DK:(h + 1) * RET_DK, h * RET_DV:(h + 1) * RET_DV])
    for h in range(MEM_HEADS):
        hs = pl.ds(h * MEM_HD, MEM_HD)
        mix_ref[:, pl.ds(RET_W + SWA_W + h * MEM_HD, MEM_HD)] = mg_ref[:, hs] * mem_o[h]


def _finish_kernel(x_ref, mix_ref, wout_ref, gain_ref, bias_ref, y_ref):
    hout = _dot(mix_ref[...].astype(BF16), wout_ref[...])
    y_ref[...] = _deepnorm_ln(x_ref[...], hout, gain_ref[...], bias_ref[...])


def _finish(x2d, mix, wout_bf, gain, bias):
    n = x2d.shape[0]
    tile = 256
    const2 = lambda a: pl.BlockSpec(a.shape, lambda i: (0, 0))
    return pl.pallas_call(
        _finish_kernel,
        grid=(n // tile,),
        in_specs=[pl.BlockSpec((tile, D_MODEL), lambda i: (i, 0)),
                  pl.BlockSpec((tile, D_MIX), lambda i: (i, 0)),
                  const2(wout_bf), const2(gain), const2(bias)],
        out_specs=pl.BlockSpec((tile, D_MODEL), lambda i: (i, 0)),
        out_shape=jax.ShapeDtypeStruct((n, D_MODEL), F32),
        compiler_params=_params(1),
        name="finish",
    )(x2d, mix, wout_bf, gain, bias)


def _rope_tables(pos):
    half = SWA_HD // 2
    inv = ROPE_THETA ** (-np.arange(half, dtype=np.float64) * 2.0 / SWA_HD)
    ang = pos.astype(np.float64)[:, None] * inv[None, :]
    cos, sin = np.cos(ang), np.sin(ang)
    reps = LANES // SWA_HD
    return (jnp.asarray(np.tile(np.concatenate([cos, cos], axis=1), (1, reps)), F32),
            jnp.asarray(np.tile(np.concatenate([-sin, sin], axis=1), (1, reps)), F32))


def _retention_tables(chunk, rows):
    lg = np.log1p(-np.exp2(-5.0 - np.arange(RET_HEADS, dtype=np.float64)))
    idx = np.arange(rows, dtype=np.float64)
    live = idx < chunk
    rel = idx[:, None] - idx[None, :]
    ok = (rel >= 0) & live[:, None] & live[None, :]
    dmat = np.where(ok[None], np.exp(np.maximum(rel, 0.0)[None] * lg[:, None, None]), 0.0)
    kdec = np.where(live[:, None], np.exp((chunk - 1.0 - idx)[:, None] * lg[None, :]), 0.0)
    qdec = np.where(live[:, None], np.exp((idx + 1.0)[:, None] * lg[None, :]), 0.0)
    g = np.exp(chunk * lg)
    kdec = np.repeat(kdec, RET_DK, axis=1)
    qdec = np.repeat(qdec, RET_DK, axis=1)
    gdec = np.broadcast_to(np.repeat(g, RET_DK)[:, None], (RET_HEADS * RET_DK, RET_DV))
    return tuple(jnp.asarray(t, F32) for t in (dmat, kdec, qdec, gdec))


def kernel(x_prompt, x_sample, state_ret, cache_swa_k, cache_swa_v, cache_mem_k, cache_mem_v,
           mem_prompt, w_in, w_mem_kv, w_out, ln_gain, ln_bias):
    depth = w_in.shape[0]
    assert depth == 1
    win_bf = w_in[0].astype(BF16)
    wmem_bf = w_mem_kv[0].astype(BF16)
    wout_bf = w_out[0].astype(BF16)
    gain = ln_gain[0].reshape(1, D_MODEL)
    bias = ln_bias[0].reshape(1, D_MODEL)

    xp = x_prompt.reshape(BATCH * SEQ, D_MODEL)
    cos_p, sin_p = _rope_tables(np.arange(SEQ))
    p_outs = (("rq", "rows", F32), ("rk", "rows", F32), ("rv", "rows", BF16), ("rg", "rows", BF16),
              ("sq", "perm", F32), ("sk", "cols", F32), ("sk", "perm", F32), ("sv", "cols", F32),
              ("sv", "perm", F32), ("sg", "rows", BF16), ("mq", "rows", BF16), ("mg", "rows", BF16))
    p_keys = ("rq", "rk", "rv", "rg", "sq4", "sk", "sk4", "sv", "sv4", "sg", "mq", "mg")
    pr = dict(zip(p_keys, _project(xp, win_bf, cos_p, sin_p, PERM_TILE, p_outs)))
    mk, mv = _memkv(mem_prompt.reshape(BATCH * N_MEM, D_MODEL), wmem_bf)
    slab = lambda a: a.reshape(SWA_W // LANES, BATCH, SEQ // PERM_TILE * PERM_ROWS, LANES)
    swa_o = _swa_prompt(slab(pr["sq4"]), slab(pr["sk4"]), slab(pr["sv4"]))
    swa_o = swa_o.reshape(SWA_W // LANES, BATCH * SEQ, LANES)
    tabs_p = _retention_tables(RET_CHUNK, RET_CHUNK)

    xs = jnp.pad(x_sample, ((0, 0), (0, DEC_PAD - DEC_SEQ), (0, 0))).reshape(DEC_BATCH * DEC_PAD, D_MODEL)
    pos_s = PAST_LEN + np.arange(DEC_BATCH * DEC_PAD) % DEC_PAD
    cos_s, sin_s = _rope_tables(pos_s)
    names = [c[0] for c in _PROJ_COLS]
    ps = dict(zip(names, _project(xs, win_bf, cos_s, sin_s, DEC_BATCH * DEC_PAD,
                                  [(k, "rows", F32) for k in names])))
    tabs_s = _retention_tables(DEC_SEQ, DEC_PAD)
    yp, ret_p, mix_s, ret_s = _mix_all(
        xp, pr, swa_o, mk, mv, wout_bf, tabs_p, gain, bias, 512,
        ps, state_ret[0].reshape(DEC_BATCH, RET_HEADS * RET_DK, RET_DV),
        cache_swa_k[0].transpose(0, 2, 3, 1), cache_swa_v[0].transpose(0, 2, 3, 1),
        cache_mem_k[0].reshape(DEC_BATCH, N_MEM * MEM_HEADS, MEM_HD),
        cache_mem_v[0].reshape(DEC_BATCH, N_MEM * MEM_HEADS, MEM_HD),
        tabs_s)
    ys = _finish(xs, mix_s, wout_bf, gain, bias)

    take = lambda a, w: a.reshape(DEC_BATCH, DEC_PAD, w)[:, :DEC_SEQ]
    swa_rows = lambda a: a.reshape(BATCH, SWA_HEADS, SWA_HD, SEQ).transpose(0, 3, 1, 2)[None]
    return (
        yp.reshape(BATCH, SEQ, D_MODEL),
        take(ys, D_MODEL),
        ret_p.reshape(1, BATCH, RET_HEADS, RET_DK, RET_DV),
        ret_s.reshape(1, DEC_BATCH, RET_HEADS, RET_DK, RET_DV),
        swa_rows(pr["sk"]),
        swa_rows(pr["sv"]),
        take(ps["sk"], SWA_W).reshape(1, DEC_BATCH, DEC_SEQ, SWA_HEADS, SWA_HD),
        take(ps["sv"], SWA_W).reshape(1, DEC_BATCH, DEC_SEQ, SWA_HEADS, SWA_HD),
        mk.reshape(1, BATCH, N_MEM, MEM_HEADS, MEM_HD),
        mv.reshape(1, BATCH, N_MEM, MEM_HEADS, MEM_HD),
    )
```

```python
import functools

import jax
import jax.numpy as jnp
import numpy as np
from jax import lax
from jax.experimental import pallas as pl
from jax.experimental.pallas import tpu as pltpu

F32 = jnp.float32
BF16 = jnp.bfloat16

D_MODEL = 1024
BATCH = 8
SEQ = 2048
DEC_BATCH = 32
DEC_SEQ = 4
PAST_LEN = 8192
N_MEM = 256
MEM_HEADS = 4
MEM_HD = 128
RET_HEADS = 4
RET_DK = 64
RET_DV = 128
RET_CHUNK = 128
SWA_HEADS = 8
SWA_HD = 64
SWA_DILATIONS = (1, 4, 16)
SWA_STEPS = 128
SWA_BLOCK = 128
ROPE_THETA = 10000.0
LN_EPS = 1e-5
GN_EPS = 1e-5
RET_W = RET_HEADS * RET_DV
SWA_W = SWA_HEADS * SWA_HD
MEM_W = MEM_HEADS * MEM_HD
D_MIX = RET_W + SWA_W + MEM_W
DEEPNORM_ALPHA = 2.0 ** 0.25
MEM_SCALE = MEM_HD ** -0.5
LOG2_E = 1.4426950408889634
QK_SCALE = 0.125

LANES = 128
DEC_PAD = 16
PERM_TILE = 512
PERM_PITCH = 40
PERM_ROWS = 16 * PERM_PITCH
OUT_ROWS = 256
SWA_PAR = 2
VMEM_LIMIT = 56 * 1024 * 1024
NEG = -1e30

_PROJ_COLS = (
    ("rq", RET_HEADS * RET_DK, "rope", 1.0),
    ("rk", RET_HEADS * RET_DK, "rope", QK_SCALE),
    ("rv", RET_W, "id", 1.0),
    ("rg", RET_W, "silu", 1.0),
    ("sq", SWA_W, "rope", QK_SCALE),
    ("sk", SWA_W, "rope", 1.0),
    ("sv", SWA_W, "id", 1.0),
    ("sg", SWA_W, "silu", 1.0),
    ("mq", MEM_W, "id", 1.0),
    ("mg", MEM_W, "silu", 1.0),
)


def _dot(a, b):
    return jnp.dot(a, b, preferred_element_type=F32)


def _dot_nt(a, b):
    return lax.dot_general(a, b, (((1,), (1,)), ((), ())), preferred_element_type=F32)


def _dot_tn(a, b):
    return lax.dot_general(a, b, (((0,), (0,)), ((), ())), preferred_element_type=F32)


def _params(n_axes):
    return pltpu.CompilerParams(dimension_semantics=("arbitrary",) * n_axes,
                                vmem_limit_bytes=VMEM_LIMIT)


def _proj_kernel(x_ref, w_ref, cos_ref, sin_ref, *out_refs, dests):
    xb = x_ref[...].astype(BF16)
    cos = cos_ref[...]
    sin = sin_ref[...]
    lane = lax.broadcasted_iota(jnp.int32, cos.shape, 1)
    first_half = (lane % 64) < 32
    col = 0
    for name, width, kind, scale in _PROJ_COLS:
        targets = [(o_ref, layout) for o_ref, (dname, layout) in zip(out_refs, dests) if dname == name]
        for c in range(0, width, 2 * LANES):
            h2 = _dot(xb, w_ref[:, col + c:col + c + 2 * LANES])
            for half in range(2):
                h = h2[:, half * LANES:(half + 1) * LANES]
                if kind == "rope":
                    swapped = jnp.where(first_half, pltpu.roll(h, 96, 1), pltpu.roll(h, 32, 1))
                    h = h * cos + swapped * sin
                    if scale != 1.0:
                        h = h * scale
                elif kind == "silu":
                    h = h * (1.0 / (1.0 + jnp.exp(-h)))
                lo = c + half * LANES
                for o_ref, layout in targets:
                    if layout == "rows":
                        o_ref[:, lo:lo + LANES] = h.astype(o_ref.dtype)
                    elif layout == "perm":
                        slab = lo // LANES
                        for g in range(h.shape[0] // 8):
                            base = (g % 2) * 8 * PERM_PITCH + g // 2
                            o_ref[slab, pl.ds(base, 8, stride=PERM_PITCH), :] = (
                                h[8 * g:8 * g + 8].astype(o_ref.dtype))
                        n_live = h.shape[0] // 16
                        for cls in range(16):
                            o_ref[slab, pl.ds(cls * PERM_PITCH + n_live, PERM_PITCH - n_live), :] = (
                                jnp.zeros((PERM_PITCH - n_live, LANES), o_ref.dtype))
                    else:
                        o_ref[lo:lo + LANES, :] = h.T.astype(o_ref.dtype)
        col += width


def _project(x2d, w_bf, cos_t, sin_t, tile, outs):
    n = x2d.shape[0]
    seq = cos_t.shape[0]
    n_tab = seq // tile
    d_in = w_bf.shape[1]
    widths = {name: w for name, w, _, _ in _PROJ_COLS}
    out_shape, out_specs = [], []
    for name, layout, dt in outs:
        w = widths[name]
        if layout == "rows":
            out_shape.append(jax.ShapeDtypeStruct((n, w), dt))
            out_specs.append(pl.BlockSpec((tile, w), lambda i: (i, 0)))
        elif layout == "perm":
            assert tile == PERM_TILE
            out_shape.append(jax.ShapeDtypeStruct((w // LANES, n // tile * PERM_ROWS, LANES), dt))
            out_specs.append(pl.BlockSpec((w // LANES, PERM_ROWS, LANES), lambda i: (0, i, 0)))
        else:
            out_shape.append(jax.ShapeDtypeStruct((n // seq, w, seq), dt))
            out_specs.append(pl.BlockSpec((None, w, tile), lambda i: (i // n_tab, 0, i % n_tab)))
    return pl.pallas_call(
        functools.partial(_proj_kernel, dests=tuple((name, layout) for name, layout, _ in outs)),
        grid=(n // tile,),
        in_specs=[
            pl.BlockSpec((tile, D_MODEL), lambda i: (i, 0)),
            pl.BlockSpec((D_MODEL, d_in), lambda i: (0, 0)),
            pl.BlockSpec((tile, LANES), lambda i: (i % n_tab, 0)),
            pl.BlockSpec((tile, LANES), lambda i: (i % n_tab, 0)),
        ],
        out_specs=out_specs,
        out_shape=out_shape,
        compiler_params=_params(1),
        name="proj",
    )(x2d, w_bf, cos_t, sin_t)


def _memkv_kernel(m_ref, w_ref, mk_ref, mv_ref):
    mb = m_ref[...].astype(BF16)
    for h in range(MEM_HEADS):
        rows = pl.ds(h, N_MEM, stride=MEM_HEADS)
        mk_ref[rows, :] = _dot(mb, w_ref[:, h * MEM_HD:(h + 1) * MEM_HD])
        mv_ref[rows, :] = _dot(mb, w_ref[:, MEM_W + h * MEM_HD:MEM_W + (h + 1) * MEM_HD])


def _memkv(mem2d, w_bf):
    n = mem2d.shape[0] // N_MEM
    out = pl.BlockSpec((None, N_MEM * MEM_HEADS, MEM_HD), lambda i: (i, 0, 0))
    return pl.pallas_call(
        _memkv_kernel,
        grid=(n,),
        in_specs=[pl.BlockSpec((N_MEM, D_MODEL), lambda i: (i, 0)),
                  pl.BlockSpec((D_MODEL, 2 * MEM_W), lambda i: (0, 0))],
        out_specs=[out, out],
        out_shape=[jax.ShapeDtypeStruct((n, N_MEM * MEM_HEADS, MEM_HD), F32)] * 2,
        compiler_params=_params(1),
        name="memkv",
    )(mem2d, w_bf)


def _swa_prompt_kernel(q_ref, k_ref, v_ref, o_ref, a_ref, m_ref, l_ref):
    blk = SWA_BLOCK
    pt = PERM_TILE
    grp = pt // 16
    lo = lax.broadcasted_iota(jnp.int32, (blk, blk), 1) < SWA_HD

    def biases(seq_of):
        rq = seq_of(lax.broadcasted_iota(jnp.int32, (blk, 2 * blk), 0))
        c2 = lax.broadcasted_iota(jnp.int32, (blk, 2 * blk), 1)
        rel = blk + rq - (seq_of(c2 % blk) + blk * (c2 // blk))
        prev = jnp.where((rel >= 0) & (rel <= SWA_STEPS), 0.0, NEG).astype(F32)
        r1 = seq_of(lax.broadcasted_iota(jnp.int32, (blk, blk), 0))
        c1 = seq_of(lax.broadcasted_iota(jnp.int32, (blk, blk), 1))
        return prev, jnp.where(r1 >= c1, 0.0, NEG).astype(F32)

    def gather(ref, pair, starts, n):
        return jnp.concatenate([ref[pair, pl.ds(s, n), :] for s in starts], axis=0)

    def aligned(x):
        return x if isinstance(x, int) else pl.multiple_of(x, 8)

    def blocks(jobs, n, first_pattern, last_pattern):
        n_pairs = SWA_HEADS // 2
        in_at = lambda c: aligned(c[0] * PERM_ROWS + c[1] * PERM_PITCH + c[2])
        acc_starts, q_starts, k_starts = [], [], []
        for chunks, prev_chunks, _, _ in jobs:
            acc_starts.append([aligned(c[0] * pt + c[1] * grp + c[2]) for c in chunks])
            q_starts.append([in_at(c) for c in chunks])
            k_starts.append(([] if prev_chunks is None else [in_at(c) for c in prev_chunks]) + q_starts[-1])
        items = [(j, pair) for j in range(len(jobs)) for pair in range(n_pairs)]
        olds = {}
        if not first_pattern:
            for j, pair in items:
                olds[j, pair] = (gather(m_ref, pair, acc_starts[j], n), gather(l_ref, pair, acc_starts[j], n),
                                 gather(a_ref, pair, acc_starts[j], n))
        vbs = {(j, pair): gather(v_ref, pair, k_starts[j], n).astype(BF16) for j, pair in items}
        scores = {}
        for j, pair in items:
            q = gather(q_ref, pair, q_starts[j], n)
            kb = gather(k_ref, pair, k_starts[j], n).astype(BF16)
            for hh in range(2):
                qm = jnp.where(lo if hh == 0 else ~lo, q, 0.0).astype(BF16)
                scores[j, pair, hh] = _dot_nt(qm, kb) + jobs[j][2]
        ms = {key: jnp.max(s, axis=-1, keepdims=True) for key, s in scores.items()}
        ps = {key: jnp.exp(s - ms[key]) for key, s in scores.items()}
        def v_ext(v, hh):
            own = lax.broadcasted_iota(jnp.int32, v.shape, 1) < SWA_HD
            return jnp.where(own if hh == 0 else ~own, v, jnp.ones((), BF16))

        pvs = {(j, pair, hh): _dot(p.astype(BF16), v_ext(vbs[j, pair], hh)) for (j, pair, hh), p in ps.items()}
        news = {}
        for j, pair in items:
            m_g = jnp.where(lo, ms[j, pair, 0], ms[j, pair, 1])
            l_g = pltpu.roll(jnp.where(lo, pvs[j, pair, 1], pvs[j, pair, 0]), SWA_HD, 1)
            a_g = jnp.where(lo, pvs[j, pair, 0], pvs[j, pair, 1])
            if first_pattern:
                news[j, pair] = (m_g, l_g, a_g)
            else:
                m_old, l_old, a_old = olds[j, pair]
                m_new = jnp.maximum(m_old, m_g)
                w_old = jnp.exp(m_old - m_new)
                w_g = jnp.exp(m_g - m_new)
                news[j, pair] = (m_new, w_old * l_old + w_g * l_g, w_old * a_old + w_g * a_g)
        for (j, pair), (m_new, l_new, a_new) in news.items():
            starts = acc_starts[j]
            if last_pattern:
                o = a_new / l_new
                for i in range(len(starts)):
                    o_ref[pair, pl.ds(jobs[j][3] + i, n, stride=len(starts)), :] = o[i * n:(i + 1) * n]
            else:
                for i, s in enumerate(starts):
                    m_ref[pair, pl.ds(s, n), :] = m_new[i * n:(i + 1) * n]
                    l_ref[pair, pl.ds(s, n), :] = l_new[i * n:(i + 1) * n]
                    a_ref[pair, pl.ds(s, n), :] = a_new[i * n:(i + 1) * n]

    par = SWA_PAR
    _, bias16 = biases(lambda i: i)

    def class16(r0, carry):
        blocks([([(t, r0 + k * (16 // par), 0) for t in range(SEQ // pt)], None, bias16, None)
                for k in range(par)], grp, True, False)
        return carry

    lax.fori_loop(0, 16 // par, class16, 0)

    bias4_prev, bias4_own = biases(lambda i: 4 * (i % grp) + i // grp)

    def class4(r0, carry):
        chunks_of = lambda r, tile: [(tile, 4 * a + r, 0) for a in range(4)]
        classes = [r0 + k * (4 // par) for k in range(par)]
        blocks([(chunks_of(r, 0), None, bias4_own, None) for r in classes], grp, False, False)

        def per_block(b, c2):
            blocks([(chunks_of(r, b), chunks_of(r, b - 1), bias4_prev, None) for r in classes],
                   grp, False, False)
            return c2

        lax.fori_loop(1, SEQ // pt, per_block, 0)
        return carry

    lax.fori_loop(0, 4 // par, class4, 0)

    bias1_prev, bias1_own = biases(lambda i: 16 * (i % 8) + i // 8)
    per_tile = pt // blk
    chunks1 = lambda c: [(c // per_tile, i, (c % per_tile) * 8) for i in range(16)]
    job1 = lambda c: (chunks1(c), chunks1(c - 1), bias1_prev, aligned(c * blk))
    blocks([(chunks1(0), None, bias1_own, 0)] + [job1(c) for c in range(1, par)], 8, False, True)

    def block1(g, carry):
        blocks([job1(g * par + k) for k in range(par)], 8, False, True)
        return carry

    lax.fori_loop(1, SEQ // blk // par, block1, 0)


def _swa_prompt(sq, sk, sv):
    slabs = SWA_W // LANES
    spec = pl.BlockSpec((slabs, None, SEQ, LANES), lambda b: (0, b, 0, 0))
    in_spec = pl.BlockSpec((slabs, None, sq.shape[2], LANES), lambda b: (0, b, 0, 0))
    return pl.pallas_call(
        _swa_prompt_kernel,
        grid=(BATCH,),
        in_specs=[in_spec, in_spec, in_spec],
        out_specs=spec,
        out_shape=jax.ShapeDtypeStruct((slabs, BATCH, SEQ, LANES), F32),
        scratch_shapes=[pltpu.VMEM((slabs, SEQ, LANES), F32)] * 3,
        compiler_params=_params(1),
        name="swa_prompt",
    )(sq, sk, sv)


def _head_norm(o):
    mu = jnp.mean(o, axis=-1, keepdims=True)
    d = o - mu
    var = jnp.mean(d * d, axis=-1, keepdims=True)
    return d * lax.rsqrt(var + GN_EPS)


def _deepnorm_ln(x, h, gain, bias):
    z = DEEPNORM_ALPHA * x + h
    mu = jnp.mean(z, axis=-1, keepdims=True)
    d = z - mu
    var = jnp.mean(d * d, axis=-1, keepdims=True)
    return d * lax.rsqrt(var + LN_EPS) * gain + bias


def _softmax_rows(s):
    m = jnp.max(s, axis=-1, keepdims=True)
    p = jnp.exp(s - m)
    return p * (1.0 / jnp.sum(p, axis=-1, keepdims=True))


def _mix_prompt_kernel(x_ref, rq_ref, rk_ref, rv_ref, rg_ref, so_ref, sg_ref, mq_ref, mg_ref,
                       mk_ref, mv_ref, wout_ref, dmat_ref, kdec_ref, qdec_ref, gdec_ref,
                       gain_ref, bias_ref, y_ref, state_out_ref,
                       state_ref, mix_ref, mkb_ref, mvb_ref, *, tile):
    t = pl.program_id(1)

    @pl.when(t == 0)
    def _():
        state_ref[...] = jnp.zeros_like(state_ref)
        for h in range(MEM_HEADS):
            rows = pl.ds(h, N_MEM, stride=MEM_HEADS)
            mkb_ref[:, pl.ds(h * MEM_HD, MEM_HD)] = mk_ref[rows, :].astype(BF16)
            mvb_ref[:, pl.ds(h * MEM_HD, MEM_HD)] = mv_ref[rows, :].astype(BF16)

    ck = RET_CHUNK
    lane = lax.broadcasted_iota(jnp.int32, (ck, LANES), 1)
    lo = lane < RET_DK
    top = lax.broadcasted_iota(jnp.int32, (LANES, LANES), 0) < RET_DK
    gain = gain_ref[...]
    bias = bias_ref[...]

    n_ck = tile // ck
    n_pairs = RET_HEADS // 2
    rows_of = [pl.ds(c * ck, ck) for c in range(n_ck)]
    items = [(c, pair, hh) for c in range(n_ck) for pair in range(n_pairs) for hh in range(2)]
    hs_of = lambda pair, hh: pl.ds((2 * pair + hh) * RET_DV, RET_DV)
    sel = lambda x, hh: jnp.where(lo if hh == 0 else ~lo, x, 0.0).astype(BF16)
    pairs = [(c, pair) for c in range(n_ck) for pair in range(n_pairs)]
    qk = {}
    for c, pair in pairs:
        cs = pl.ds(pair * LANES, LANES)
        q = rq_ref[rows_of[c], cs]
        k = rk_ref[rows_of[c], cs]
        qk[c, pair] = (q.astype(BF16), jnp.concatenate([sel(k, 0), sel(k, 1)], axis=0),
                       (k * kdec_ref[:, cs]).astype(BF16), q * qdec_ref[:, cs])
    v_pair = lambda c, pair: rv_ref[rows_of[c], pl.ds(pair * 2 * RET_DV, 2 * RET_DV)]
    s2 = {key: _dot_nt(qk[key][0], qk[key][1]) for key in pairs}
    kv2 = {key: _dot_tn(qk[key][2], v_pair(*key)) for key in pairs}
    state = {(0, pair): state_ref[pair] for pair in range(n_pairs)}
    for c, pair in pairs:
        state[c + 1, pair] = (gdec_ref[pl.ds(pair * LANES, LANES), :] * state[c, pair]
                              + jnp.where(top, kv2[c, pair][:, :RET_DV], kv2[c, pair][:, RET_DV:]))
    for pair in range(n_pairs):
        state_ref[pair] = state[n_ck, pair]
    o = {}
    for c, pair, hh in items:
        sh = (s2[c, pair][:, hh * ck:(hh + 1) * ck] * dmat_ref[2 * pair + hh]).astype(BF16)
        lhs = jnp.concatenate([sh, sel(qk[c, pair][3], hh)], axis=1)
        rhs = jnp.concatenate([rv_ref[rows_of[c], hs_of(pair, hh)], state[c, pair].astype(BF16)], axis=0)
        o[c, pair, hh] = _dot(lhs, rhs)
    for c, pair, hh in items:
        hs = hs_of(pair, hh)
        mix_ref[rows_of[c], hs] = (rg_ref[rows_of[c], hs].astype(F32) * _head_norm(o[c, pair, hh])).astype(BF16)
    for pair in range(SWA_W // LANES):
        cs = pl.ds(pair * LANES, LANES)
        mix_ref[:, pl.ds(RET_W + pair * LANES, LANES)] = (
            sg_ref[:, cs].astype(F32) * so_ref[pair]).astype(BF16)
    for c in range(n_ck):
        ss = [_dot_nt(mq_ref[rows_of[c], pl.ds(h * MEM_HD, MEM_HD)], mkb_ref[:, pl.ds(h * MEM_HD, MEM_HD)])
              for h in range(MEM_HEADS)]
        ps = [jnp.exp2((s - jnp.max(s, axis=-1, keepdims=True)) * (MEM_SCALE * LOG2_E)) for s in ss]
        for h in range(MEM_HEADS):
            hs = pl.ds(h * MEM_HD, MEM_HD)
            o = _dot(ps[h].astype(BF16), mvb_ref[:, hs]) * (1.0 / jnp.sum(ps[h], axis=-1, keepdims=True))
            mix_ref[rows_of[c], pl.ds(RET_W + SWA_W + h * MEM_HD, MEM_HD)] = (
                mg_ref[rows_of[c], hs].astype(F32) * o).astype(BF16)
    blocks = [pl.ds(r * OUT_ROWS, OUT_ROWS) for r in range(tile // OUT_ROWS)]
    hout = _dot(mix_ref[blocks[0], :], wout_ref[...])
    for r, rows in enumerate(blocks):
        nxt = _dot(mix_ref[blocks[r + 1], :], wout_ref[...]) if r + 1 < len(blocks) else None
        y_ref[rows, :] = _deepnorm_ln(x_ref[rows, :], hout, gain, bias)
        hout = nxt

    @pl.when(t == pl.num_programs(1) - 1)
    def _():
        state_out_ref[pl.ds(0, LANES), :] = state_ref[0]
        state_out_ref[pl.ds(LANES, LANES), :] = state_ref[1]


def _mix_all_kernel(*refs, tile, n_in, n_out):
    p_in, s_in = refs[:n_in[0]], refs[n_in[0]:n_in[0] + n_in[1]]
    outs = refs[n_in[0] + n_in[1]:]
    p_out, s_out = outs[:n_out[0]], outs[n_out[0]:n_out[0] + n_out[1]]
    scratch = outs[n_out[0] + n_out[1]:]
    _mix_sample_kernel(*s_in, *s_out)
    _mix_prompt_kernel(*p_in, *p_out, *scratch, tile=tile)


def _mix_all(x2d, pr, swa_o, mk, mv, wout_bf, tabs_p, gain, bias, tile,
             ps, state, ckt, cvt, cmk, cmv, tabs_s):
    nt = SEQ // tile
    assert BATCH * nt == DEC_BATCH
    step = lambda b, t: b * nt + t
    row = lambda w: pl.BlockSpec((tile, w), lambda b, t: (step(b, t), 0))
    const = lambda a: pl.BlockSpec(a.shape, lambda b, t: (0,) * a.ndim)
    mem_p = pl.BlockSpec((None, N_MEM * MEM_HEADS, MEM_HD), lambda b, t: (b, 0, 0))
    st_p = pl.BlockSpec((None, RET_HEADS * RET_DK, RET_DV), lambda b, t: (b, 0, 0))
    p_args = (x2d, pr["rq"], pr["rk"], pr["rv"], pr["rg"], swa_o, pr["sg"], pr["mq"], pr["mg"],
              mk, mv, wout_bf, *tabs_p, gain, bias)
    p_specs = [row(D_MODEL), row(RET_HEADS * RET_DK), row(RET_HEADS * RET_DK), row(RET_W), row(RET_W),
               pl.BlockSpec((SWA_W // LANES, tile, LANES), lambda b, t: (0, step(b, t), 0)),
               row(SWA_W), row(MEM_W), row(MEM_W), mem_p, mem_p, const(wout_bf),
               *[const(a) for a in tabs_p], const(gain), const(bias)]
    n = DEC_PAD
    wb = ckt.shape[3]
    srow = lambda w: pl.BlockSpec((n, w), lambda b, t: (step(b, t), 0))
    st_s = pl.BlockSpec((None, RET_HEADS * RET_DK, RET_DV), lambda b, t: (step(b, t), 0, 0))
    cache_s = pl.BlockSpec((None, SWA_HEADS, SWA_HD, wb), lambda b, t: (step(b, t), 0, 0, 0))
    mem_s = pl.BlockSpec((None, N_MEM * MEM_HEADS, MEM_HD), lambda b, t: (step(b, t), 0, 0))
    s_names = ("rq", "rk", "rv", "rg", "sq", "sk", "sv", "sg", "mq", "mg")
    widths = {name: w for name, w, _, _ in _PROJ_COLS}
    s_args = (*[ps[k] for k in s_names], state, ckt, cvt, cmk, cmv, *tabs_s)
    s_specs = [*[srow(widths[k]) for k in s_names], st_s, cache_s, cache_s, mem_s, mem_s,
               *[const(a) for a in tabs_s]]
    return pl.pallas_call(
        functools.partial(_mix_all_kernel, tile=tile, n_in=(len(p_args), len(s_args)), n_out=(2, 2)),
        grid=(BATCH, nt),
        in_specs=p_specs + s_specs,
        out_specs=[row(D_MODEL), st_p, srow(D_MIX), st_s],
        out_shape=[jax.ShapeDtypeStruct((BATCH * SEQ, D_MODEL), F32),
                   jax.ShapeDtypeStruct((BATCH, RET_HEADS * RET_DK, RET_DV), F32),
                   jax.ShapeDtypeStruct((DEC_BATCH * n, D_MIX), F32),
                   jax.ShapeDtypeStruct((DEC_BATCH, RET_HEADS * RET_DK, RET_DV), F32)],
        scratch_shapes=[pltpu.VMEM((RET_HEADS // 2, LANES, RET_DV), F32),
                        pltpu.VMEM((tile, D_MIX), BF16),
                        pltpu.VMEM((N_MEM, MEM_W), BF16),
                        pltpu.VMEM((N_MEM, MEM_W), BF16)],
        compiler_params=_params(2),
        name="mix_all",
    )(*p_args, *s_args)


def _mix_sample_kernel(rq_ref, rk_ref, rv_ref, rg_ref, sq_ref, sk_ref, sv_ref, sg_ref, mq_ref, mg_ref,
                       state_ref, ckt_ref, cvt_ref, cmk_ref, cmv_ref,
                       dmat_ref, kdec_ref, qdec_ref, gdec_ref,
                       mix_ref, state_out_ref):
    n = DEC_PAD
    q = rq_ref[...]
    k = rk_ref[...]
    kb = k.astype(BF16)
    kd = (k * kdec_ref[...]).astype(BF16)
    qd = q * qdec_ref[...]
    st = state_ref[...]
    stb = st.astype(BF16)
    vb = rv_ref[...].astype(BF16)
    kv = _dot_tn(kd, vb)
    lane_qk = lax.broadcasted_iota(jnp.int32, q.shape, 1) // RET_DK
    ret_s = [_dot_nt(jnp.where(lane_qk == h, q, 0.0).astype(BF16), kb) for h in range(RET_HEADS)]
    ret_cross = [_dot(jnp.where(lane_qk == h, qd, 0.0).astype(BF16), stb) for h in range(RET_HEADS)]
    mem_rows = [pl.ds(h, N_MEM, stride=MEM_HEADS) for h in range(MEM_HEADS)]
    mem_s = [_dot_nt(mq_ref[:, pl.ds(h * MEM_HD, MEM_HD)].astype(BF16), cmk_ref[mem_rows[h], :].astype(BF16))
             for h in range(MEM_HEADS)]

    blk = SWA_BLOCK
    wb = ckt_ref.shape[2]
    rt = 8
    lo = lax.broadcasted_iota(jnp.int32, (n, LANES), 1) < SWA_HD
    lo_t = lax.broadcasted_iota(jnp.int32, (rt, LANES), 1) < SWA_HD
    tok = lambda w: lax.broadcasted_iota(jnp.int32, (rt, w), 0)
    col = lambda w: lax.broadcasted_iota(jnp.int32, (rt, w), 1)
    windows = (
        (wb - blk, jnp.where(col(blk) >= tok(blk), 0.0, NEG), jnp.where(col(blk) <= tok(blk), 0.0, NEG)),
        (wb - 4 * blk, jnp.where(col(4 * blk) % 4 == tok(4 * blk), 0.0, NEG),
         jnp.where(col(blk) == tok(blk), 0.0, NEG)),
        (0, jnp.where(col(wb) % 16 == tok(wb), 0.0, NEG), jnp.where(col(blk) == tok(blk), 0.0, NEG)),
    )
    pad = jnp.zeros((blk - n, LANES), BF16)
    zero_c = jnp.zeros((rt, wb), F32)
    zero_n = jnp.zeros((rt, blk), F32)
    n_pairs = SWA_HEADS // 2
    n_win = len(windows)
    pair_cols = [pl.ds(pair * LANES, LANES) for pair in range(n_pairs)]
    vps = [cvt_ref[2 * pair:2 * pair + 2].reshape(2 * SWA_HD, wb).astype(BF16) for pair in range(n_pairs)]
    v_news = [jnp.concatenate([sv_ref[:, cs].astype(BF16), pad], axis=0) for cs in pair_cols]
    s_alls, sn_alls = [], []
    for pair, cs in enumerate(pair_cols):
        kp = ckt_ref[2 * pair:2 * pair + 2].reshape(2 * SWA_HD, wb).astype(BF16)
        k_new = jnp.concatenate([sk_ref[:, cs].astype(BF16), pad], axis=0)
        q = sq_ref[:, cs]
        qs = jnp.concatenate([jnp.where(lo, q, 0.0), jnp.where(lo, 0.0, q)], axis=0).astype(BF16)
        s_alls.append(_dot(qs, kp))
        sn_alls.append(_dot_nt(qs, k_new))
    p_rows = [[] for _ in range(n_pairs)]
    pn_rows = [[] for _ in range(n_pairs)]
    stats = [[] for _ in range(n_pairs)]
    for pair in range(n_pairs):
        for hh in range(2):
            s = s_alls[pair][hh * n:hh * n + rt]
            sn = sn_alls[pair][hh * n:hh * n + rt]
            for w0, bias_c, bias_n in windows:
                sc = s[:, w0:] + bias_c
                snb = sn + bias_n
                m = jnp.maximum(jnp.max(sc, axis=-1, keepdims=True), jnp.max(snb, axis=-1, keepdims=True))
                pc = jnp.exp(sc - m)
                pn = jnp.exp(snb - m)
                stats[pair].append((m, jnp.sum(pc, axis=-1, keepdims=True) + jnp.sum(pn, axis=-1, keepdims=True)))
                if w0:
                    pc = jnp.concatenate([jnp.zeros((rt, w0), F32), pc], axis=1)
                p_rows[pair].append(pc)
                pn_rows[pair].append(pn)
            p_rows[pair].append(zero_c)
            pn_rows[pair].append(zero_n)
    ret_sb = [(ret_s[h] * dmat_ref[h]).astype(BF16) for h in range(RET_HEADS)]
    mem_p = [_softmax_rows(s * MEM_SCALE).astype(BF16) for s in mem_s]
    pvs = [_dot_nt(jnp.concatenate(p_rows[pair], axis=0).astype(BF16), vps[pair])
           + _dot(jnp.concatenate(pn_rows[pair], axis=0).astype(BF16), v_news[pair])
           for pair in range(n_pairs)]
    ret_intra = [_dot(ret_sb[h], vb[:, h * RET_DV:(h + 1) * RET_DV]) for h in range(RET_HEADS)]
    mem_o = [_dot(mem_p[h], cmv_ref[mem_rows[h], :].astype(BF16)) for h in range(MEM_HEADS)]
    pieces = []
    for pair in range(n_pairs):
        heads = []
        for hh in range(2):
            parts = []
            for i in range(n_win):
                m, l = stats[pair][hh * n_win + i]
                r0 = (hh * (n_win + 1) + i) * rt
                parts.append((pvs[pair][r0:r0 + rt] / l, m, l))
            m_all = jnp.maximum(jnp.maximum(parts[0][1], parts[1][1]), parts[2][1])
            ws = [l * jnp.exp(m - m_all) for (_, m, l) in parts]
            heads.append((ws[0] * parts[0][0] + ws[1] * parts[1][0] + ws[2] * parts[2][0])
                         / (ws[0] + ws[1] + ws[2]))
        pieces.append(jnp.where(lo_t, heads[0], heads[1]))
    swa = jnp.concatenate(pieces, axis=1)
    swa = jnp.concatenate([swa, jnp.zeros((n - rt, SWA_W), F32)], axis=0)
    mix_ref[:, pl.ds(RET_W, SWA_W)] = sg_ref[...] * swa
    for h in range(RET_HEADS):
        hs = pl.ds(h * RET_DV, RET_DV)
        ks = pl.ds(h * RET_DK, RET_DK)
        mix_ref[:, hs] = rg_ref[:, hs] * _head_norm(ret_intra[h] + ret_cross[h])
        state_out_ref[ks, :] = (gdec_ref[ks, :] * st[h * RET_DK:(h + 1) * RET_DK, :]
                                + kv[h * RET_DK:(h + 1) * RET_DK, h * RET_DV:(h + 1) * RET_DV])
    for h in range(MEM_HEADS):
        hs = pl.ds(h * MEM_HD, MEM_HD)
        mix_ref[:, pl.ds(RET_W + SWA_W + h * MEM_HD, MEM_HD)] = mg_ref[:, hs] * mem_o[h]


def _finish_kernel(x_ref, mix_ref, wout_ref, gain_ref, bias_ref, y_ref):
    hout = _dot(mix_ref[...].astype(BF16), wout_ref[...])
    y_ref[...] = _deepnorm_ln(x_ref[...], hout, gain_ref[...], bias_ref[...])


def _finish(x2d, mix, wout_bf, gain, bias):
    n = x2d.shape[0]
    tile = 256
    const2 = lambda a: pl.BlockSpec(a.shape, lambda i: (0, 0))
    return pl.pallas_call(
        _finish_kernel,
        grid=(n // tile,),
        in_specs=[pl.BlockSpec((tile, D_MODEL), lambda i: (i, 0)),
                  pl.BlockSpec((tile, D_MIX), lambda i: (i, 0)),
                  const2(wout_bf), const2(gain), const2(bias)],
        out_specs=pl.BlockSpec((tile, D_MODEL), lambda i: (i, 0)),
        out_shape=jax.ShapeDtypeStruct((n, D_MODEL), F32),
        compiler_params=_params(1),
        name="finish",
    )(x2d, mix, wout_bf, gain, bias)


def _rope_tables(pos):
    half = SWA_HD // 2
    inv = ROPE_THETA ** (-np.arange(half, dtype=np.float64) * 2.0 / SWA_HD)
    ang = pos.astype(np.float64)[:, None] * inv[None, :]
    cos, sin = np.cos(ang), np.sin(ang)
    reps = LANES // SWA_HD
    return (jnp.asarray(np.tile(np.concatenate([cos, cos], axis=1), (1, reps)), F32),
            jnp.asarray(np.tile(np.concatenate([-sin, sin], axis=1), (1, reps)), F32))


def _retention_tables(chunk, rows):
    lg = np.log1p(-np.exp2(-5.0 - np.arange(RET_HEADS, dtype=np.float64)))
    idx = np.arange(rows, dtype=np.float64)
    live = idx < chunk
    rel = idx[:, None] - idx[None, :]
    ok = (rel >= 0) & live[:, None] & live[None, :]
    dmat = np.where(ok[None], np.exp(np.maximum(rel, 0.0)[None] * lg[:, None, None]), 0.0)
    kdec = np.where(live[:, None], np.exp((chunk - 1.0 - idx)[:, None] * lg[None, :]), 0.0)
    qdec = np.where(live[:, None], np.exp((idx + 1.0)[:, None] * lg[None, :]), 0.0)
    g = np.exp(chunk * lg)
    kdec = np.repeat(kdec, RET_DK, axis=1)
    qdec = np.repeat(qdec, RET_DK, axis=1)
    gdec = np.broadcast_to(np.repeat(g, RET_DK)[:, None], (RET_HEADS * RET_DK, RET_DV))
    return tuple(jnp.asarray(t, F32) for t in (dmat, kdec, qdec, gdec))


def kernel(x_prompt, x_sample, state_ret, cache_swa_k, cache_swa_v, cache_mem_k, cache_mem_v,
           mem_prompt, w_in, w_mem_kv, w_out, ln_gain, ln_bias):
    depth = w_in.shape[0]
    assert depth == 1
    win_bf = w_in[0].astype(BF16)
    wmem_bf = w_mem_kv[0].astype(BF16)
    wout_bf = w_out[0].astype(BF16)
    gain = ln_gain[0].reshape(1, D_MODEL)
    bias = ln_bias[0].reshape(1, D_MODEL)

    xp = x_prompt.reshape(BATCH * SEQ, D_MODEL)
    cos_p, sin_p = _rope_tables(np.arange(SEQ))
    p_outs = (("rq", "rows", F32), ("rk", "rows", F32), ("rv", "rows", BF16), ("rg", "rows", BF16),
              ("sq", "perm", F32), ("sk", "cols", F32), ("sk", "perm", F32), ("sv", "cols", F32),
              ("sv", "perm", F32), ("sg", "rows", BF16), ("mq", "rows", BF16), ("mg", "rows", BF16))
    p_keys = ("rq", "rk", "rv", "rg", "sq4", "sk", "sk4", "sv", "sv4", "sg", "mq", "mg")
    pr = dict(zip(p_keys, _project(xp, win_bf, cos_p, sin_p, PERM_TILE, p_outs)))
    mk, mv = _memkv(mem_prompt.reshape(BATCH * N_MEM, D_MODEL), wmem_bf)
    slab = lambda a: a.reshape(SWA_W // LANES, BATCH, SEQ // PERM_TILE * PERM_ROWS, LANES)
    swa_o = _swa_prompt(slab(pr["sq4"]), slab(pr["sk4"]), slab(pr["sv4"]))
    swa_o = swa_o.reshape(SWA_W // LANES, BATCH * SEQ, LANES)
    tabs_p = _retention_tables(RET_CHUNK, RET_CHUNK)

    xs = jnp.pad(x_sample, ((0, 0), (0, DEC_PAD - DEC_SEQ), (0, 0))).reshape(DEC_BATCH * DEC_PAD, D_MODEL)
    pos_s = PAST_LEN + np.arange(DEC_BATCH * DEC_PAD) % DEC_PAD
    cos_s, sin_s = _rope_tables(pos_s)
    names = [c[0] for c in _PROJ_COLS]
    ps = dict(zip(names, _project(xs, win_bf, cos_s, sin_s, DEC_BATCH * DEC_PAD,
                                  [(k, "rows", F32) for k in names])))
    tabs_s = _retention_tables(DEC_SEQ, DEC_PAD)
    yp, ret_p, mix_s, ret_s = _mix_all(
        xp, pr, swa_o, mk, mv, wout_bf, tabs_p, gain, bias, 512,
        ps, state_ret[0].reshape(DEC_BATCH, RET_HEADS * RET_DK, RET_DV),
        cache_swa_k[0].transpose(0, 2, 3, 1), cache_swa_v[0].transpose(0, 2, 3, 1),
        cache_mem_k[0].reshape(DEC_BATCH, N_MEM * MEM_HEADS, MEM_HD),
        cache_mem_v[0].reshape(DEC_BATCH, N_MEM * MEM_HEADS, MEM_HD),
        tabs_s)
    ys = _finish(xs, mix_s, wout_bf, gain, bias)

    take = lambda a, w: a.reshape(DEC_BATCH, DEC_PAD, w)[:, :DEC_SEQ]
    swa_rows = lambda a: a.reshape(BATCH, SWA_HEADS, SWA_HD, SEQ).transpose(0, 3, 1, 2)[None]
    return (
        yp.reshape(BATCH, SEQ, D_MODEL),
        take(ys, D_MODEL),
        ret_p.reshape(1, BATCH, RET_HEADS, RET_DK, RET_DV),
        ret_s.reshape(1, DEC_BATCH, RET_HEADS, RET_DK, RET_DV),
        swa_rows(pr["sk"]),
        swa_rows(pr["sv"]),
        take(ps["sk"], SWA_W).reshape(1, DEC_BATCH, DEC_SEQ, SWA_HEADS, SWA_HD),
        take(ps["sv"], SWA_W).reshape(1, DEC_BATCH, DEC_SEQ, SWA_HEADS, SWA_HD),
        mk.reshape(1, BATCH, N_MEM, MEM_HEADS, MEM_HD),
        mv.reshape(1, BATCH, N_MEM, MEM_HEADS, MEM_HD),
    )
```

```python
import functools

import jax
import jax.numpy as jnp
import numpy as np
from jax import lax
from jax.experimental import pallas as pl
from jax.experimental.pallas import tpu as pltpu

F32 = jnp.float32
BF16 = jnp.bfloat16

D_MODEL = 1024
BATCH = 8
SEQ = 2048
DEC_BATCH = 32
DEC_SEQ = 4
PAST_LEN = 8192
N_MEM = 256
MEM_HEADS = 4
MEM_HD = 128
RET_HEADS = 4
RET_DK = 64
RET_DV = 128
RET_CHUNK = 128
SWA_HEADS = 8
SWA_HD = 64
SWA_DILATIONS = (1, 4, 16)
SWA_STEPS = 128
SWA_BLOCK = 128
ROPE_THETA = 10000.0
LN_EPS = 1e-5
GN_EPS = 1e-5
RET_W = RET_HEADS * RET_DV
SWA_W = SWA_HEADS * SWA_HD
MEM_W = MEM_HEADS * MEM_HD
D_MIX = RET_W + SWA_W + MEM_W
DEEPNORM_ALPHA = 2.0 ** 0.25
MEM_SCALE = MEM_HD ** -0.5
LOG2_E = 1.4426950408889634
QK_SCALE = 0.125

LANES = 128
DEC_PAD = 16
PERM_TILE = 512
PERM_PITCH = 40
PERM_ROWS = 16 * PERM_PITCH
OUT_ROWS = 256
SWA_PAR = 2
VMEM_LIMIT = 56 * 1024 * 1024
NEG = -1e30

_PROJ_COLS = (
    ("rq", RET_HEADS * RET_DK, "rope", 1.0),
    ("rk", RET_HEADS * RET_DK, "rope", QK_SCALE),
    ("rv", RET_W, "id", 1.0),
    ("rg", RET_W, "silu", 1.0),
    ("sq", SWA_W, "rope", QK_SCALE),
    ("sk", SWA_W, "rope", 1.0),
    ("sv", SWA_W, "id", 1.0),
    ("sg", SWA_W, "silu", 1.0),
    ("mq", MEM_W, "id", 1.0),
    ("mg", MEM_W, "silu", 1.0),
)


def _dot(a, b):
    return jnp.dot(a, b, preferred_element_type=F32)


def _dot_nt(a, b):
    return lax.dot_general(a, b, (((1,), (1,)), ((), ())), preferred_element_type=F32)


def _dot_tn(a, b):
    return lax.dot_general(a, b, (((0,), (0,)), ((), ())), preferred_element_type=F32)


def _params(n_axes):
    return pltpu.CompilerParams(dimension_semantics=("arbitrary",) * n_axes,
                                vmem_limit_bytes=VMEM_LIMIT)


def _proj_kernel(x_ref, w_ref, cos_ref, sin_ref, *out_refs, dests):
    xb = x_ref[...].astype(BF16)
    cos = cos_ref[...]
    sin = sin_ref[...]
    lane = lax.broadcasted_iota(jnp.int32, cos.shape, 1)
    first_half = (lane % 64) < 32
    col = 0
    for name, width, kind, scale in _PROJ_COLS:
        targets = [(o_ref, layout) for o_ref, (dname, layout) in zip(out_refs, dests) if dname == name]
        for c in range(0, width, 2 * LANES):
            h2 = _dot(xb, w_ref[:, col + c:col + c + 2 * LANES])
            for half in range(2):
                h = h2[:, half * LANES:(half + 1) * LANES]
                if kind == "rope":
                    swapped = jnp.where(first_half, pltpu.roll(h, 96, 1), pltpu.roll(h, 32, 1))
                    h = h * cos + swapped * sin
                    if scale != 1.0:
                        h = h * scale
                elif kind == "silu":
                    h = h * (1.0 / (1.0 + jnp.exp(-h)))
                lo = c + half * LANES
                for o_ref, layout in targets:
                    if layout == "rows":
                        o_ref[:, lo:lo + LANES] = h.astype(o_ref.dtype)
                    elif layout == "perm":
                        slab = lo // LANES
                        for g in range(h.shape[0] // 8):
                            base = (g % 2) * 8 * PERM_PITCH + g // 2
                            o_ref[slab, pl.ds(base, 8, stride=PERM_PITCH), :] = (
                                h[8 * g:8 * g + 8].astype(o_ref.dtype))
                        n_live = h.shape[0] // 16
                        for cls in range(16):
                            o_ref[slab, pl.ds(cls * PERM_PITCH + n_live, PERM_PITCH - n_live), :] = (
                                jnp.zeros((PERM_PITCH - n_live, LANES), o_ref.dtype))
                    else:
                        o_ref[lo:lo + LANES, :] = h.T.astype(o_ref.dtype)
        col += width


def _project(x2d, w_bf, cos_t, sin_t, tile, outs):
    n = x2d.shape[0]
    seq = cos_t.shape[0]
    n_tab = seq // tile
    d_in = w_bf.shape[1]
    widths = {name: w for name, w, _, _ in _PROJ_COLS}
    out_shape, out_specs = [], []
    for name, layout, dt in outs:
        w = widths[name]
        if layout == "rows":
            out_shape.append(jax.ShapeDtypeStruct((n, w), dt))
            out_specs.append(pl.BlockSpec((tile, w), lambda i: (i, 0)))
        elif layout == "perm":
            assert tile == PERM_TILE
            out_shape.append(jax.ShapeDtypeStruct((w // LANES, n // tile * PERM_ROWS, LANES), dt))
            out_specs.append(pl.BlockSpec((w // LANES, PERM_ROWS, LANES), lambda i: (0, i, 0)))
        else:
            out_shape.append(jax.ShapeDtypeStruct((n // seq, w, seq), dt))
            out_specs.append(pl.BlockSpec((None, w, tile), lambda i: (i // n_tab, 0, i % n_tab)))
    return pl.pallas_call(
        functools.partial(_proj_kernel, dests=tuple((name, layout) for name, layout, _ in outs)),
        grid=(n // tile,),
        in_specs=[
            pl.BlockSpec((tile, D_MODEL), lambda i: (i, 0)),
            pl.BlockSpec((D_MODEL, d_in), lambda i: (0, 0)),
            pl.BlockSpec((tile, LANES), lambda i: (i % n_tab, 0)),
            pl.BlockSpec((tile, LANES), lambda i: (i % n_tab, 0)),
        ],
        out_specs=out_specs,
        out_shape=out_shape,
        compiler_params=_params(1),
        name="proj",
    )(x2d, w_bf, cos_t, sin_t)


def _memkv_kernel(m_ref, w_ref, mk_ref, mv_ref):
    mb = m_ref[...].astype(BF16)
    for h in range(MEM_HEADS):
        rows = pl.ds(h, N_MEM, stride=MEM_HEADS)
        mk_ref[rows, :] = _dot(mb, w_ref[:, h * MEM_HD:(h + 1) * MEM_HD])
        mv_ref[rows, :] = _dot(mb, w_ref[:, MEM_W + h * MEM_HD:MEM_W + (h + 1) * MEM_HD])


def _memkv(mem2d, w_bf):
    n = mem2d.shape[0] // N_MEM
    out = pl.BlockSpec((None, N_MEM * MEM_HEADS, MEM_HD), lambda i: (i, 0, 0))
    return pl.pallas_call(
        _memkv_kernel,
        grid=(n,),
        in_specs=[pl.BlockSpec((N_MEM, D_MODEL), lambda i: (i, 0)),
                  pl.BlockSpec((D_MODEL, 2 * MEM_W), lambda i: (0, 0))],
        out_specs=[out, out],
        out_shape=[jax.ShapeDtypeStruct((n, N_MEM * MEM_HEADS, MEM_HD), F32)] * 2,
        compiler_params=_params(1),
        name="memkv",
    )(mem2d, w_bf)


def _swa_prompt_kernel(q_ref, k_ref, v_ref, o_ref, a_ref, m_ref, l_ref):
    blk = SWA_BLOCK
    pt = PERM_TILE
    grp = pt // 16
    lo = lax.broadcasted_iota(jnp.int32, (blk, blk), 1) < SWA_HD

    def biases(seq_of):
        rq = seq_of(lax.broadcasted_iota(jnp.int32, (blk, 2 * blk), 0))
        c2 = lax.broadcasted_iota(jnp.int32, (blk, 2 * blk), 1)
        rel = blk + rq - (seq_of(c2 % blk) + blk * (c2 // blk))
        prev = jnp.where((rel >= 0) & (rel <= SWA_STEPS), 0.0, NEG).astype(F32)
        r1 = seq_of(lax.broadcasted_iota(jnp.int32, (blk, blk), 0))
        c1 = seq_of(lax.broadcasted_iota(jnp.int32, (blk, blk), 1))
        return prev, jnp.where(r1 >= c1, 0.0, NEG).astype(F32)

    def gather(ref, pair, starts, n):
        return jnp.concatenate([ref[pair, pl.ds(s, n), :] for s in starts], axis=0)

    def aligned(x):
        return x if isinstance(x, int) else pl.multiple_of(x, 8)

    def blocks(jobs, n, first_pattern, last_pattern):
        n_pairs = SWA_HEADS // 2
        in_at = lambda c: aligned(c[0] * PERM_ROWS + c[1] * PERM_PITCH + c[2])
        acc_starts, q_starts, k_starts = [], [], []
        for chunks, prev_chunks, _, _ in jobs:
            acc_starts.append([aligned(c[0] * pt + c[1] * grp + c[2]) for c in chunks])
            q_starts.append([in_at(c) for c in chunks])
            k_starts.append(([] if prev_chunks is None else [in_at(c) for c in prev_chunks]) + q_starts[-1])
        items = [(j, pair) for j in range(len(jobs)) for pair in range(n_pairs)]
        olds = {}
        if not first_pattern:
            for j, pair in items:
                olds[j, pair] = (gather(m_ref, pair, acc_starts[j], n), gather(l_ref, pair, acc_starts[j], n),
                                 gather(a_ref, pair, acc_starts[j], n))
        vbs = {(j, pair): gather(v_ref, pair, k_starts[j], n).astype(BF16) for j, pair in items}
        scores = {}
        for j, pair in items:
            q = gather(q_ref, pair, q_starts[j], n)
            kb = gather(k_ref, pair, k_starts[j], n).astype(BF16)
            for hh in range(2):
                qm = jnp.where(lo if hh == 0 else ~lo, q, 0.0).astype(BF16)
                scores[j, pair, hh] = _dot_nt(qm, kb) + jobs[j][2]
        ms = {key: jnp.max(s, axis=-1, keepdims=True) for key, s in scores.items()}
        ps = {key: jnp.exp(s - ms[key]) for key, s in scores.items()}
        def v_ext(v, hh):
            own = lax.broadcasted_iota(jnp.int32, v.shape, 1) < SWA_HD
            return jnp.where(own if hh == 0 else ~own, v, jnp.ones((), BF16))

        pvs = {(j, pair, hh): _dot(p.astype(BF16), v_ext(vbs[j, pair], hh)) for (j, pair, hh), p in ps.items()}
        news = {}
        for j, pair in items:
            m_g = jnp.where(lo, ms[j, pair, 0], ms[j, pair, 1])
            l_g = pltpu.roll(jnp.where(lo, pvs[j, pair, 1], pvs[j, pair, 0]), SWA_HD, 1)
            a_g = jnp.where(lo, pvs[j, pair, 0], pvs[j, pair, 1])
            if first_pattern:
                news[j, pair] = (m_g, l_g, a_g)
            else:
                m_old, l_old, a_old = olds[j, pair]
                m_new = jnp.maximum(m_old, m_g)
                w_old = jnp.exp(m_old - m_new)
                w_g = jnp.exp(m_g - m_new)
                news[j, pair] = (m_new, w_old * l_old + w_g * l_g, w_old * a_old + w_g * a_g)
        for (j, pair), (m_new, l_new, a_new) in news.items():
            starts = acc_starts[j]
            if last_pattern:
                o = a_new / l_new
                for i in range(len(starts)):
                    o_ref[pair, pl.ds(jobs[j][3] + i, n, stride=len(starts)), :] = o[i * n:(i + 1) * n]
            else:
                for i, s in enumerate(starts):
                    m_ref[pair, pl.ds(s, n), :] = m_new[i * n:(i + 1) * n]
                    l_ref[pair, pl.ds(s, n), :] = l_new[i * n:(i + 1) * n]
                    a_ref[pair, pl.ds(s, n), :] = a_new[i * n:(i + 1) * n]

    par = SWA_PAR
    _, bias16 = biases(lambda i: i)

    def class16(r0, carry):
        blocks([([(t, r0 + k * (16 // par), 0) for t in range(SEQ // pt)], None, bias16, None)
                for k in range(par)], grp, True, False)
        return carry

    lax.fori_loop(0, 16 // par, class16, 0)

    bias4_prev, bias4_own = biases(lambda i: 4 * (i % grp) + i // grp)

    def class4(r0, carry):
        chunks_of = lambda r, tile: [(tile, 4 * a + r, 0) for a in range(4)]
        classes = [r0 + k * (4 // par) for k in range(par)]
        blocks([(chunks_of(r, 0), None, bias4_own, None) for r in classes], grp, False, False)

        def per_block(b, c2):
            blocks([(chunks_of(r, b), chunks_of(r, b - 1), bias4_prev, None) for r in classes],
                   grp, False, False)
            return c2

        lax.fori_loop(1, SEQ // pt, per_block, 0)
        return carry

    lax.fori_loop(0, 4 // par, class4, 0)

    bias1_prev, bias1_own = biases(lambda i: 16 * (i % 8) + i // 8)
    per_tile = pt // blk
    chunks1 = lambda c: [(c // per_tile, i, (c % per_tile) * 8) for i in range(16)]
    job1 = lambda c: (chunks1(c), chunks1(c - 1), bias1_prev, aligned(c * blk))
    blocks([(chunks1(0), None, bias1_own, 0)] + [job1(c) for c in range(1, par)], 8, False, True)

    def block1(g, carry):
        blocks([job1(g * par + k) for k in range(par)], 8, False, True)
        return carry

    lax.fori_loop(1, SEQ // blk // par, block1, 0)


def _swa_prompt(sq, sk, sv):
    slabs = SWA_W // LANES
    spec = pl.BlockSpec((slabs, None, SEQ, LANES), lambda b: (0, b, 0, 0))
    in_spec = pl.BlockSpec((slabs, None, sq.shape[2], LANES), lambda b: (0, b, 0, 0))
    return pl.pallas_call(
        _swa_prompt_kernel,
        grid=(BATCH,),
        in_specs=[in_spec, in_spec, in_spec],
        out_specs=spec,
        out_shape=jax.ShapeDtypeStruct((slabs, BATCH, SEQ, LANES), F32),
        scratch_shapes=[pltpu.VMEM((slabs, SEQ, LANES), F32)] * 3,
        compiler_params=_params(1),
        name="swa_prompt",
    )(sq, sk, sv)


def _head_norm(o):
    mu = jnp.mean(o, axis=-1, keepdims=True)
    d = o - mu
    var = jnp.mean(d * d, axis=-1, keepdims=True)
    return d * lax.rsqrt(var + GN_EPS)


def _deepnorm_ln(x, h, gain, bias):
    z = DEEPNORM_ALPHA * x + h
    mu = jnp.mean(z, axis=-1, keepdims=True)
    d = z - mu
    var = jnp.mean(d * d, axis=-1, keepdims=True)
    return d * lax.rsqrt(var + LN_EPS) * gain + bias


def _softmax_rows(s):
    m = jnp.max(s, axis=-1, keepdims=True)
    p = jnp.exp(s - m)
    return p * (1.0 / jnp.sum(p, axis=-1, keepdims=True))


def _mix_prompt_kernel(x_ref, rq_ref, rk_ref, rv_ref, rg_ref, so_ref, sg_ref, mq_ref, mg_ref,
                       mk_ref, mv_ref, wout_ref, dmat_ref, kdec_ref, qdec_ref, gdec_ref,
                       gain_ref, bias_ref, y_ref, state_out_ref,
                       state_ref, mix_ref, mkb_ref, mvb_ref, *, tile):
    t = pl.program_id(1)

    @pl.when(t == 0)
    def _():
        state_ref[...] = jnp.zeros_like(state_ref)
        for h in range(MEM_HEADS):
            rows = pl.ds(h, N_MEM, stride=MEM_HEADS)
            mkb_ref[:, pl.ds(h * MEM_HD, MEM_HD)] = mk_ref[rows, :].astype(BF16)
            mvb_ref[:, pl.ds(h * MEM_HD, MEM_HD)] = mv_ref[rows, :].astype(BF16)

    ck = RET_CHUNK
    lane = lax.broadcasted_iota(jnp.int32, (ck, LANES), 1)
    lo = lane < RET_DK
    top = lax.broadcasted_iota(jnp.int32, (LANES, LANES), 0) < RET_DK
    gain = gain_ref[...]
    bias = bias_ref[...]

    n_ck = tile // ck
    n_pairs = RET_HEADS // 2
    rows_of = [pl.ds(c * ck, ck) for c in range(n_ck)]
    items = [(c, pair, hh) for c in range(n_ck) for pair in range(n_pairs) for hh in range(2)]
    hs_of = lambda pair, hh: pl.ds((2 * pair + hh) * RET_DV, RET_DV)
    sel = lambda x, hh: jnp.where(lo if hh == 0 else ~lo, x, 0.0).astype(BF16)
    pairs = [(c, pair) for c in range(n_ck) for pair in range(n_pairs)]
    qk = {}
    for c, pair in pairs:
        cs = pl.ds(pair * LANES, LANES)
        q = rq_ref[rows_of[c], cs]
        k = rk_ref[rows_of[c], cs]
        qk[c, pair] = (q.astype(BF16), jnp.concatenate([sel(k, 0), sel(k, 1)], axis=0),
                       (k * kdec_ref[:, cs]).astype(BF16), q * qdec_ref[:, cs])
    v_pair = lambda c, pair: rv_ref[rows_of[c], pl.ds(pair * 2 * RET_DV, 2 * RET_DV)]
    s2 = {key: _dot_nt(qk[key][0], qk[key][1]) for key in pairs}
    kv2 = {key: _dot_tn(qk[key][2], v_pair(*key)) for key in pairs}
    yield
    state = {(0, pair): state_ref[pair] for pair in range(n_pairs)}
    for c, pair in pairs:
        state[c + 1, pair] = (gdec_ref[pl.ds(pair * LANES, LANES), :] * state[c, pair]
                              + jnp.where(top, kv2[c, pair][:, :RET_DV], kv2[c, pair][:, RET_DV:]))
    for pair in range(n_pairs):
        state_ref[pair] = state[n_ck, pair]
    o = {}
    for c, pair, hh in items:
        sh = (s2[c, pair][:, hh * ck:(hh + 1) * ck] * dmat_ref[2 * pair + hh]).astype(BF16)
        lhs = jnp.concatenate([sh, sel(qk[c, pair][3], hh)], axis=1)
        rhs = jnp.concatenate([rv_ref[rows_of[c], hs_of(pair, hh)], state[c, pair].astype(BF16)], axis=0)
        o[c, pair, hh] = _dot(lhs, rhs)
    yield
    for c, pair, hh in items:
        hs = hs_of(pair, hh)
        mix_ref[rows_of[c], hs] = (rg_ref[rows_of[c], hs].astype(F32) * _head_norm(o[c, pair, hh])).astype(BF16)
    for pair in range(SWA_W // LANES):
        cs = pl.ds(pair * LANES, LANES)
        mix_ref[:, pl.ds(RET_W + pair * LANES, LANES)] = (
            sg_ref[:, cs].astype(F32) * so_ref[pair]).astype(BF16)
    for c in range(n_ck):
        yield
        ss = [_dot_nt(mq_ref[rows_of[c], pl.ds(h * MEM_HD, MEM_HD)], mkb_ref[:, pl.ds(h * MEM_HD, MEM_HD)])
              for h in range(MEM_HEADS)]
        ps = [jnp.exp2((s - jnp.max(s, axis=-1, keepdims=True)) * (MEM_SCALE * LOG2_E)) for s in ss]
        for h in range(MEM_HEADS):
            hs = pl.ds(h * MEM_HD, MEM_HD)
            o = _dot(ps[h].astype(BF16), mvb_ref[:, hs]) * (1.0 / jnp.sum(ps[h], axis=-1, keepdims=True))
            mix_ref[rows_of[c], pl.ds(RET_W + SWA_W + h * MEM_HD, MEM_HD)] = (
                mg_ref[rows_of[c], hs].astype(F32) * o).astype(BF16)
    yield
    blocks = [pl.ds(r * OUT_ROWS, OUT_ROWS) for r in range(tile // OUT_ROWS)]
    hout = _dot(mix_ref[blocks[0], :], wout_ref[...])
    for r, rows in enumerate(blocks):
        nxt = _dot(mix_ref[blocks[r + 1], :], wout_ref[...]) if r + 1 < len(blocks) else None
        y_ref[rows, :] = _deepnorm_ln(x_ref[rows, :], hout, gain, bias)
        hout = nxt

    @pl.when(t == pl.num_programs(1) - 1)
    def _():
        state_out_ref[pl.ds(0, LANES), :] = state_ref[0]
        state_out_ref[pl.ds(LANES, LANES), :] = state_ref[1]


def _mix_all_kernel(*refs, tile, n_in, n_out):
    p_in, s_in = refs[:n_in[0]], refs[n_in[0]:n_in[0] + n_in[1]]
    outs = refs[n_in[0] + n_in[1]:]
    p_out, s_out = outs[:n_out[0]], outs[n_out[0]:n_out[0] + n_out[1]]
    scratch = outs[n_out[0] + n_out[1]:]
    stages = [_mix_prompt_kernel(*p_in, *p_out, *scratch, tile=tile), _mix_sample_kernel(*s_in, *s_out)]
    while stages:
        for g in list(stages):
            if next(g, "done") == "done":
                stages.remove(g)


def _mix_all(x2d, pr, swa_o, mk, mv, wout_bf, tabs_p, gain, bias, tile,
             ps, state, ckt, cvt, cmk, cmv, tabs_s):
    nt = SEQ // tile
    assert BATCH * nt == DEC_BATCH
    step = lambda b, t: b * nt + t
    row = lambda w: pl.BlockSpec((tile, w), lambda b, t: (step(b, t), 0))
    const = lambda a: pl.BlockSpec(a.shape, lambda b, t: (0,) * a.ndim)
    mem_p = pl.BlockSpec((None, N_MEM * MEM_HEADS, MEM_HD), lambda b, t: (b, 0, 0))
    st_p = pl.BlockSpec((None, RET_HEADS * RET_DK, RET_DV), lambda b, t: (b, 0, 0))
    p_args = (x2d, pr["rq"], pr["rk"], pr["rv"], pr["rg"], swa_o, pr["sg"], pr["mq"], pr["mg"],
              mk, mv, wout_bf, *tabs_p, gain, bias)
    p_specs = [row(D_MODEL), row(RET_HEADS * RET_DK), row(RET_HEADS * RET_DK), row(RET_W), row(RET_W),
               pl.BlockSpec((SWA_W // LANES, tile, LANES), lambda b, t: (0, step(b, t), 0)),
               row(SWA_W), row(MEM_W), row(MEM_W), mem_p, mem_p, const(wout_bf),
               *[const(a) for a in tabs_p], const(gain), const(bias)]
    n = DEC_PAD
    wb = ckt.shape[3]
    srow = lambda w: pl.BlockSpec((n, w), lambda b, t: (step(b, t), 0))
    st_s = pl.BlockSpec((None, RET_HEADS * RET_DK, RET_DV), lambda b, t: (step(b, t), 0, 0))
    cache_s = pl.BlockSpec((None, SWA_HEADS, SWA_HD, wb), lambda b, t: (step(b, t), 0, 0, 0))
    mem_s = pl.BlockSpec((None, N_MEM * MEM_HEADS, MEM_HD), lambda b, t: (step(b, t), 0, 0))
    s_names = ("rq", "rk", "rv", "rg", "sq", "sk", "sv", "sg", "mq", "mg")
    widths = {name: w for name, w, _, _ in _PROJ_COLS}
    s_args = (*[ps[k] for k in s_names], state, ckt, cvt, cmk, cmv, *tabs_s)
    s_specs = [*[srow(widths[k]) for k in s_names], st_s, cache_s, cache_s, mem_s, mem_s,
               *[const(a) for a in tabs_s]]
    return pl.pallas_call(
        functools.partial(_mix_all_kernel, tile=tile, n_in=(len(p_args), len(s_args)), n_out=(2, 2)),
        grid=(BATCH, nt),
        in_specs=p_specs + s_specs,
        out_specs=[row(D_MODEL), st_p, srow(D_MIX), st_s],
        out_shape=[jax.ShapeDtypeStruct((BATCH * SEQ, D_MODEL), F32),
                   jax.ShapeDtypeStruct((BATCH, RET_HEADS * RET_DK, RET_DV), F32),
                   jax.ShapeDtypeStruct((DEC_BATCH * n, D_MIX), F32),
                   jax.ShapeDtypeStruct((DEC_BATCH, RET_HEADS * RET_DK, RET_DV), F32)],
        scratch_shapes=[pltpu.VMEM((RET_HEADS // 2, LANES, RET_DV), F32),
                        pltpu.VMEM((tile, D_MIX), BF16),
                        pltpu.VMEM((N_MEM, MEM_W), BF16),
                        pltpu.VMEM((N_MEM, MEM_W), BF16)],
        compiler_params=_params(2),
        name="mix_all",
    )(*p_args, *s_args)


def _mix_sample_kernel(rq_ref, rk_ref, rv_ref, rg_ref, sq_ref, sk_ref, sv_ref, sg_ref, mq_ref, mg_ref,
                       state_ref, ckt_ref, cvt_ref, cmk_ref, cmv_ref,
                       dmat_ref, kdec_ref, qdec_ref, gdec_ref,
                       mix_ref, state_out_ref):
    n = DEC_PAD
    q = rq_ref[...]
    k = rk_ref[...]
    kb = k.astype(BF16)
    kd = (k * kdec_ref[...]).astype(BF16)
    qd = q * qdec_ref[...]
    st = state_ref[...]
    stb = st.astype(BF16)
    vb = rv_ref[...].astype(BF16)
    kv = _dot_tn(kd, vb)
    lane_qk = lax.broadcasted_iota(jnp.int32, q.shape, 1) // RET_DK
    ret_s = [_dot_nt(jnp.where(lane_qk == h, q, 0.0).astype(BF16), kb) for h in range(RET_HEADS)]
    ret_cross = [_dot(jnp.where(lane_qk == h, qd, 0.0).astype(BF16), stb) for h in range(RET_HEADS)]
    mem_rows = [pl.ds(h, N_MEM, stride=MEM_HEADS) for h in range(MEM_HEADS)]
    mem_s = [_dot_nt(mq_ref[:, pl.ds(h * MEM_HD, MEM_HD)].astype(BF16), cmk_ref[mem_rows[h], :].astype(BF16))
             for h in range(MEM_HEADS)]

    blk = SWA_BLOCK
    wb = ckt_ref.shape[2]
    rt = 8
    lo = lax.broadcasted_iota(jnp.int32, (n, LANES), 1) < SWA_HD
    lo_t = lax.broadcasted_iota(jnp.int32, (rt, LANES), 1) < SWA_HD
    tok = lambda w: lax.broadcasted_iota(jnp.int32, (rt, w), 0)
    col = lambda w: lax.broadcasted_iota(jnp.int32, (rt, w), 1)
    windows = (
        (wb - blk, jnp.where(col(blk) >= tok(blk), 0.0, NEG), jnp.where(col(blk) <= tok(blk), 0.0, NEG)),
        (wb - 4 * blk, jnp.where(col(4 * blk) % 4 == tok(4 * blk), 0.0, NEG),
         jnp.where(col(blk) == tok(blk), 0.0, NEG)),
        (0, jnp.where(col(wb) % 16 == tok(wb), 0.0, NEG), jnp.where(col(blk) == tok(blk), 0.0, NEG)),
    )
    pad = jnp.zeros((blk - n, LANES), BF16)
    zero_c = jnp.zeros((rt, wb), F32)
    zero_n = jnp.zeros((rt, blk), F32)
    n_pairs = SWA_HEADS // 2
    n_win = len(windows)
    pair_cols = [pl.ds(pair * LANES, LANES) for pair in range(n_pairs)]
    vps = [cvt_ref[2 * pair:2 * pair + 2].reshape(2 * SWA_HD, wb).astype(BF16) for pair in range(n_pairs)]
    v_news = [jnp.concatenate([sv_ref[:, cs].astype(BF16), pad], axis=0) for cs in pair_cols]
    s_alls, sn_alls = [], []
    for pair, cs in enumerate(pair_cols):
        kp = ckt_ref[2 * pair:2 * pair + 2].reshape(2 * SWA_HD, wb).astype(BF16)
        k_new = jnp.concatenate([sk_ref[:, cs].astype(BF16), pad], axis=0)
        q = sq_ref[:, cs]
        qs = jnp.concatenate([jnp.where(lo, q, 0.0), jnp.where(lo, 0.0, q)], axis=0).astype(BF16)
        s_alls.append(_dot(qs, kp))
        sn_alls.append(_dot_nt(qs, k_new))
    yield
    p_rows = [[] for _ in range(n_pairs)]
    pn_rows = [[] for _ in range(n_pairs)]
    stats = [[] for _ in range(n_pairs)]
    for pair in range(n_pairs):
        for hh in range(2):
            s = s_alls[pair][hh * n:hh * n + rt]
            sn = sn_alls[pair][hh * n:hh * n + rt]
            for w0, bias_c, bias_n in windows:
                sc = s[:, w0:] + bias_c
                snb = sn + bias_n
                m = jnp.maximum(jnp.max(sc, axis=-1, keepdims=True), jnp.max(snb, axis=-1, keepdims=True))
                pc = jnp.exp(sc - m)
                pn = jnp.exp(snb - m)
                stats[pair].append((m, jnp.sum(pc, axis=-1, keepdims=True) + jnp.sum(pn, axis=-1, keepdims=True)))
                if w0:
                    pc = jnp.concatenate([jnp.zeros((rt, w0), F32), pc], axis=1)
                p_rows[pair].append(pc)
                pn_rows[pair].append(pn)
            p_rows[pair].append(zero_c)
            pn_rows[pair].append(zero_n)
    ret_sb = [(ret_s[h] * dmat_ref[h]).astype(BF16) for h in range(RET_HEADS)]
    mem_p = [_softmax_rows(s * MEM_SCALE).astype(BF16) for s in mem_s]
    yield
    pvs = [_dot_nt(jnp.concatenate(p_rows[pair], axis=0).astype(BF16), vps[pair])
           + _dot(jnp.concatenate(pn_rows[pair], axis=0).astype(BF16), v_news[pair])
           for pair in range(n_pairs)]
    ret_intra = [_dot(ret_sb[h], vb[:, h * RET_DV:(h + 1) * RET_DV]) for h in range(RET_HEADS)]
    mem_o = [_dot(mem_p[h], cmv_ref[mem_rows[h], :].astype(BF16)) for h in range(MEM_HEADS)]
    yield
    pieces = []
    for pair in range(n_pairs):
        heads = []
        for hh in range(2):
            parts = []
            for i in range(n_win):
                m, l = stats[pair][hh * n_win + i]
                r0 = (hh * (n_win + 1) + i) * rt
                parts.append((pvs[pair][r0:r0 + rt] / l, m, l))
            m_all = jnp.maximum(jnp.maximum(parts[0][1], parts[1][1]), parts[2][1])
            ws = [l * jnp.exp(m - m_all) for (_, m, l) in parts]
            heads.append((ws[0] * parts[0][0] + ws[1] * parts[1][0] + ws[2] * parts[2][0])
                         / (ws[0] + ws[1] + ws[2]))
        pieces.append(jnp.where(lo_t, heads[0], heads[1]))
    swa = jnp.concatenate(pieces, axis=1)
    swa = jnp.concatenate([swa, jnp.zeros((n - rt, SWA_W), F32)], axis=0)
    mix_ref[:, pl.ds(RET_W, SWA_W)] = sg_ref[...] * swa
    for h in range(RET_HEADS):
        hs = pl.ds(h * RET_DV, RET_DV)
        ks = pl.ds(h * RET_DK, RET_DK)
        mix_ref[:, hs] = rg_ref[:, hs] * _head_norm(ret_intra[h] + ret_cross[h])
        state_out_ref[ks, :] = (gdec_ref[ks, :] * st[h * RET_DK:(h + 1) * RET_DK, :]
                                + kv[h * RET_DK:(h + 1) * RET_DK, h * RET_DV:(h + 1) * RET_DV])
    for h in range(MEM_HEADS):
        hs = pl.ds(h * MEM_HD, MEM_HD)
        mix_ref[:, pl.ds(RET_W + SWA_W + h * MEM_HD, MEM_HD)] = mg_ref[:, hs] * mem_o[h]


def _finish_kernel(x_ref, mix_ref, wout_ref, gain_ref, bias_ref, y_ref):
    hout = _dot(mix_ref[...].astype(BF16), wout_ref[...])
    y_ref[...] = _deepnorm_ln(x_ref[...], hout, gain_ref[...], bias_ref[...])


def _finish(x2d, mix, wout_bf, gain, bias):
    n = x2d.shape[0]
    tile = 256
    const2 = lambda a: pl.BlockSpec(a.shape, lambda i: (0, 0))
    return pl.pallas_call(
        _finish_kernel,
        grid=(n // tile,),
        in_specs=[pl.BlockSpec((tile, D_MODEL), lambda i: (i, 0)),
                  pl.BlockSpec((tile, D_MIX), lambda i: (i, 0)),
                  const2(wout_bf), const2(gain), const2(bias)],
        out_specs=pl.BlockSpec((tile, D_MODEL), lambda i: (i, 0)),
        out_shape=jax.ShapeDtypeStruct((n, D_MODEL), F32),
        compiler_params=_params(1),
        name="finish",
    )(x2d, mix, wout_bf, gain, bias)


def _rope_tables(pos):
    half = SWA_HD // 2
    inv = ROPE_THETA ** (-np.arange(half, dtype=np.float64) * 2.0 / SWA_HD)
    ang = pos.astype(np.float64)[:, None] * inv[None, :]
    cos, sin = np.cos(ang), np.sin(ang)
    reps = LANES // SWA_HD
    return (jnp.asarray(np.tile(np.concatenate([cos, cos], axis=1), (1, reps)), F32),
            jnp.asarray(np.tile(np.concatenate([-sin, sin], axis=1), (1, reps)), F32))


def _retention_tables(chunk, rows):
    lg = np.log1p(-np.exp2(-5.0 - np.arange(RET_HEADS, dtype=np.float64)))
    idx = np.arange(rows, dtype=np.float64)
    live = idx < chunk
    rel = idx[:, None] - idx[None, :]
    ok = (rel >= 0) & live[:, None] & live[None, :]
    dmat = np.where(ok[None], np.exp(np.maximum(rel, 0.0)[None] * lg[:, None, None]), 0.0)
    kdec = np.where(live[:, None], np.exp((chunk - 1.0 - idx)[:, None] * lg[None, :]), 0.0)
    qdec = np.where(live[:, None], np.exp((idx + 1.0)[:, None] * lg[None, :]), 0.0)
    g = np.exp(chunk * lg)
    kdec = np.repeat(kdec, RET_DK, axis=1)
    qdec = np.repeat(qdec, RET_DK, axis=1)
    gdec = np.broadcast_to(np.repeat(g, RET_DK)[:, None], (RET_HEADS * RET_DK, RET_DV))
    return tuple(jnp.asarray(t, F32) for t in (dmat, kdec, qdec, gdec))


def kernel(x_prompt, x_sample, state_ret, cache_swa_k, cache_swa_v, cache_mem_k, cache_mem_v,
           mem_prompt, w_in, w_mem_kv, w_out, ln_gain, ln_bias):
    depth = w_in.shape[0]
    assert depth == 1
    win_bf = w_in[0].astype(BF16)
    wmem_bf = w_mem_kv[0].astype(BF16)
    wout_bf = w_out[0].astype(BF16)
    gain = ln_gain[0].reshape(1, D_MODEL)
    bias = ln_bias[0].reshape(1, D_MODEL)

    xp = x_prompt.reshape(BATCH * SEQ, D_MODEL)
    cos_p, sin_p = _rope_tables(np.arange(SEQ))
    p_outs = (("rq", "rows", F32), ("rk", "rows", F32), ("rv", "rows", BF16), ("rg", "rows", BF16),
              ("sq", "perm", F32), ("sk", "cols", F32), ("sk", "perm", F32), ("sv", "cols", F32),
              ("sv", "perm", F32), ("sg", "rows", BF16), ("mq", "rows", BF16), ("mg", "rows", BF16))
    p_keys = ("rq", "rk", "rv", "rg", "sq4", "sk", "sk4", "sv", "sv4", "sg", "mq", "mg")
    pr = dict(zip(p_keys, _project(xp, win_bf, cos_p, sin_p, PERM_TILE, p_outs)))
    mk, mv = _memkv(mem_prompt.reshape(BATCH * N_MEM, D_MODEL), wmem_bf)
    slab = lambda a: a.reshape(SWA_W // LANES, BATCH, SEQ // PERM_TILE * PERM_ROWS, LANES)
    swa_o = _swa_prompt(slab(pr["sq4"]), slab(pr["sk4"]), slab(pr["sv4"]))
    swa_o = swa_o.reshape(SWA_W // LANES, BATCH * SEQ, LANES)
    tabs_p = _retention_tables(RET_CHUNK, RET_CHUNK)

    xs = jnp.pad(x_sample, ((0, 0), (0, DEC_PAD - DEC_SEQ), (0, 0))).reshape(DEC_BATCH * DEC_PAD, D_MODEL)
    pos_s = PAST_LEN + np.arange(DEC_BATCH * DEC_PAD) % DEC_PAD
    cos_s, sin_s = _rope_tables(pos_s)
    names = [c[0] for c in _PROJ_COLS]
    ps = dict(zip(names, _project(xs, win_bf, cos_s, sin_s, DEC_BATCH * DEC_PAD,
                                  [(k, "rows", F32) for k in names])))
    tabs_s = _retention_tables(DEC_SEQ, DEC_PAD)
    yp, ret_p, mix_s, ret_s = _mix_all(
        xp, pr, swa_o, mk, mv, wout_bf, tabs_p, gain, bias, 512,
        ps, state_ret[0].reshape(DEC_BATCH, RET_HEADS * RET_DK, RET_DV),
        cache_swa_k[0].transpose(0, 2, 3, 1), cache_swa_v[0].transpose(0, 2, 3, 1),
        cache_mem_k[0].reshape(DEC_BATCH, N_MEM * MEM_HEADS, MEM_HD),
        cache_mem_v[0].reshape(DEC_BATCH, N_MEM * MEM_HEADS, MEM_HD),
        tabs_s)
    ys = _finish(xs, mix_s, wout_bf, gain, bias)

    take = lambda a, w: a.reshape(DEC_BATCH, DEC_PAD, w)[:, :DEC_SEQ]
    swa_rows = lambda a: a.reshape(BATCH, SWA_HEADS, SWA_HD, SEQ).transpose(0, 3, 1, 2)[None]
    return (
        yp.reshape(BATCH, SEQ, D_MODEL),
        take(ys, D_MODEL),
        ret_p.reshape(1, BATCH, RET_HEADS, RET_DK, RET_DV),
        ret_s.reshape(1, DEC_BATCH, RET_HEADS, RET_DK, RET_DV),
        swa_rows(pr["sk"]),
        swa_rows(pr["sv"]),
        take(ps["sk"], SWA_W).reshape(1, DEC_BATCH, DEC_SEQ, SWA_HEADS, SWA_HD),
        take(ps["sv"], SWA_W).reshape(1, DEC_BATCH, DEC_SEQ, SWA_HEADS, SWA_HD),
        mk.reshape(1, BATCH, N_MEM, MEM_HEADS, MEM_HD),
        mv.reshape(1, BATCH, N_MEM, MEM_HEADS, MEM_HD),
    )
```

```python
import functools

import jax
import jax.numpy as jnp
import numpy as np
from jax import lax
from jax.experimental import pallas as pl
from jax.experimental.pallas import tpu as pltpu

F32 = jnp.float32
BF16 = jnp.bfloat16

D_MODEL = 1024
BATCH = 8
SEQ = 2048
DEC_BATCH = 32
DEC_SEQ = 4
PAST_LEN = 8192
N_MEM = 256
MEM_HEADS = 4
MEM_HD = 128
RET_HEADS = 4
RET_DK = 64
RET_DV = 128
RET_CHUNK = 128
SWA_HEADS = 8
SWA_HD = 64
SWA_DILATIONS = (1, 4, 16)
SWA_STEPS = 128
SWA_BLOCK = 128
ROPE_THETA = 10000.0
LN_EPS = 1e-5
GN_EPS = 1e-5
RET_W = RET_HEADS * RET_DV
SWA_W = SWA_HEADS * SWA_HD
MEM_W = MEM_HEADS * MEM_HD
D_MIX = RET_W + SWA_W + MEM_W
DEEPNORM_ALPHA = 2.0 ** 0.25
MEM_SCALE = MEM_HD ** -0.5
LOG2_E = 1.4426950408889634
QK_SCALE = 0.125

LANES = 128
DEC_PAD = 16
PERM_TILE = 512
PERM_PITCH = 40
PERM_ROWS = 16 * PERM_PITCH
OUT_ROWS = 256
SWA_PAR = 2
MEMKV_BATCHES = 4
VMEM_LIMIT = 56 * 1024 * 1024
NEG = -1e30

_PROJ_COLS = (
    ("rq", RET_HEADS * RET_DK, "rope", 1.0),
    ("rk", RET_HEADS * RET_DK, "rope", QK_SCALE),
    ("rv", RET_W, "id", 1.0),
    ("rg", RET_W, "silu", 1.0),
    ("sq", SWA_W, "rope", QK_SCALE),
    ("sk", SWA_W, "rope", 1.0),
    ("sv", SWA_W, "id", 1.0),
    ("sg", SWA_W, "silu", 1.0),
    ("mq", MEM_W, "id", 1.0),
    ("mg", MEM_W, "silu", 1.0),
)


def _dot(a, b):
    return jnp.dot(a, b, preferred_element_type=F32)


def _dot_nt(a, b):
    return lax.dot_general(a, b, (((1,), (1,)), ((), ())), preferred_element_type=F32)


def _dot_tn(a, b):
    return lax.dot_general(a, b, (((0,), (0,)), ((), ())), preferred_element_type=F32)


def _params(n_axes):
    return pltpu.CompilerParams(dimension_semantics=("arbitrary",) * n_axes,
                                vmem_limit_bytes=VMEM_LIMIT)


def _proj_kernel(x_ref, w_ref, cos_ref, sin_ref, *out_refs, dests):
    xb = x_ref[...].astype(BF16)
    cos = cos_ref[...]
    sin = sin_ref[...]
    lane = lax.broadcasted_iota(jnp.int32, cos.shape, 1)
    first_half = (lane % 64) < 32
    col = 0
    for name, width, kind, scale in _PROJ_COLS:
        targets = [(o_ref, layout) for o_ref, (dname, layout) in zip(out_refs, dests) if dname == name]
        for c in range(0, width, 2 * LANES):
            h2 = _dot(xb, w_ref[:, col + c:col + c + 2 * LANES])
            for half in range(2):
                h = h2[:, half * LANES:(half + 1) * LANES]
                if kind == "rope":
                    swapped = jnp.where(first_half, pltpu.roll(h, 96, 1), pltpu.roll(h, 32, 1))
                    h = h * cos + swapped * sin
                    if scale != 1.0:
                        h = h * scale
                elif kind == "silu":
                    h = h * (1.0 / (1.0 + jnp.exp(-h)))
                lo = c + half * LANES
                for o_ref, layout in targets:
                    if layout == "rows":
                        o_ref[:, lo:lo + LANES] = h.astype(o_ref.dtype)
                    elif layout == "perm":
                        slab = lo // LANES
                        for g in range(h.shape[0] // 8):
                            base = (g % 2) * 8 * PERM_PITCH + g // 2
                            o_ref[slab, pl.ds(base, 8, stride=PERM_PITCH), :] = (
                                h[8 * g:8 * g + 8].astype(o_ref.dtype))
                        n_live = h.shape[0] // 16
                        for cls in range(16):
                            o_ref[slab, pl.ds(cls * PERM_PITCH + n_live, PERM_PITCH - n_live), :] = (
                                jnp.zeros((PERM_PITCH - n_live, LANES), o_ref.dtype))
                    else:
                        o_ref[lo:lo + LANES, :] = h.T.astype(o_ref.dtype)
        col += width


def _project(x2d, w_bf, cos_t, sin_t, tile, outs):
    n = x2d.shape[0]
    seq = cos_t.shape[0]
    n_tab = seq // tile
    d_in = w_bf.shape[1]
    widths = {name: w for name, w, _, _ in _PROJ_COLS}
    out_shape, out_specs = [], []
    for name, layout, dt in outs:
        w = widths[name]
        if layout == "rows":
            out_shape.append(jax.ShapeDtypeStruct((n, w), dt))
            out_specs.append(pl.BlockSpec((tile, w), lambda i: (i, 0)))
        elif layout == "perm":
            assert tile == PERM_TILE
            out_shape.append(jax.ShapeDtypeStruct((w // LANES, n // tile * PERM_ROWS, LANES), dt))
            out_specs.append(pl.BlockSpec((w // LANES, PERM_ROWS, LANES), lambda i: (0, i, 0)))
        else:
            out_shape.append(jax.ShapeDtypeStruct((n // seq, w, seq), dt))
            out_specs.append(pl.BlockSpec((None, w, tile), lambda i: (i // n_tab, 0, i % n_tab)))
    return pl.pallas_call(
        functools.partial(_proj_kernel, dests=tuple((name, layout) for name, layout, _ in outs)),
        grid=(n // tile,),
        in_specs=[
            pl.BlockSpec((tile, D_MODEL), lambda i: (i, 0)),
            pl.BlockSpec((D_MODEL, d_in), lambda i: (0, 0)),
            pl.BlockSpec((tile, LANES), lambda i: (i % n_tab, 0)),
            pl.BlockSpec((tile, LANES), lambda i: (i % n_tab, 0)),
        ],
        out_specs=out_specs,
        out_shape=out_shape,
        compiler_params=_params(1),
        name="proj",
    )(x2d, w_bf, cos_t, sin_t)


def _memkv_kernel(m_ref, w_ref, mk_ref, mv_ref):
    for b in range(m_ref.shape[0] // N_MEM):
        mb = m_ref[pl.ds(b * N_MEM, N_MEM), :].astype(BF16)
        for h in range(MEM_HEADS):
            rows = pl.ds(h, N_MEM, stride=MEM_HEADS)
            mk_ref[b, rows, :] = _dot(mb, w_ref[:, h * MEM_HD:(h + 1) * MEM_HD])
            mv_ref[b, rows, :] = _dot(mb, w_ref[:, MEM_W + h * MEM_HD:MEM_W + (h + 1) * MEM_HD])


def _memkv(mem2d, w_bf):
    n = mem2d.shape[0] // N_MEM
    per = MEMKV_BATCHES
    out = pl.BlockSpec((per, N_MEM * MEM_HEADS, MEM_HD), lambda i: (i, 0, 0))
    return pl.pallas_call(
        _memkv_kernel,
        grid=(n // per,),
        in_specs=[pl.BlockSpec((per * N_MEM, D_MODEL), lambda i: (i, 0)),
                  pl.BlockSpec((D_MODEL, 2 * MEM_W), lambda i: (0, 0))],
        out_specs=[out, out],
        out_shape=[jax.ShapeDtypeStruct((n, N_MEM * MEM_HEADS, MEM_HD), F32)] * 2,
        compiler_params=_params(1),
        name="memkv",
    )(mem2d, w_bf)


def _swa_prompt_kernel(q_ref, k_ref, v_ref, o_ref, a_ref, m_ref, l_ref):
    blk = SWA_BLOCK
    pt = PERM_TILE
    grp = pt // 16
    lo = lax.broadcasted_iota(jnp.int32, (blk, blk), 1) < SWA_HD

    def biases(seq_of):
        rq = seq_of(lax.broadcasted_iota(jnp.int32, (blk, 2 * blk), 0))
        c2 = lax.broadcasted_iota(jnp.int32, (blk, 2 * blk), 1)
        rel = blk + rq - (seq_of(c2 % blk) + blk * (c2 // blk))
        prev = jnp.where((rel >= 0) & (rel <= SWA_STEPS), 0.0, NEG).astype(F32)
        r1 = seq_of(lax.broadcasted_iota(jnp.int32, (blk, blk), 0))
        c1 = seq_of(lax.broadcasted_iota(jnp.int32, (blk, blk), 1))
        return prev, jnp.where(r1 >= c1, 0.0, NEG).astype(F32)

    def gather(ref, pair, starts, n):
        return jnp.concatenate([ref[pair, pl.ds(s, n), :] for s in starts], axis=0)

    def aligned(x):
        return x if isinstance(x, int) else pl.multiple_of(x, 8)

    def blocks(jobs, n, first_pattern, last_pattern):
        n_pairs = SWA_HEADS // 2
        in_at = lambda c: aligned(c[0] * PERM_ROWS + c[1] * PERM_PITCH + c[2])
        acc_starts, q_starts, k_starts = [], [], []
        for chunks, prev_chunks, _, _ in jobs:
            acc_starts.append([aligned(c[0] * pt + c[1] * grp + c[2]) for c in chunks])
            q_starts.append([in_at(c) for c in chunks])
            k_starts.append(([] if prev_chunks is None else [in_at(c) for c in prev_chunks]) + q_starts[-1])
        items = [(j, pair) for j in range(len(jobs)) for pair in range(n_pairs)]
        olds = {}
        if not first_pattern:
            for j, pair in items:
                olds[j, pair] = (gather(m_ref, pair, acc_starts[j], n), gather(l_ref, pair, acc_starts[j], n),
                                 gather(a_ref, pair, acc_starts[j], n))
        vbs = {(j, pair): gather(v_ref, pair, k_starts[j], n).astype(BF16) for j, pair in items}
        scores = {}
        for j, pair in items:
            q = gather(q_ref, pair, q_starts[j], n)
            kb = gather(k_ref, pair, k_starts[j], n).astype(BF16)
            for hh in range(2):
                qm = jnp.where(lo if hh == 0 else ~lo, q, 0.0).astype(BF16)
                scores[j, pair, hh] = _dot_nt(qm, kb) + jobs[j][2]
        ms = {key: jnp.max(s, axis=-1, keepdims=True) for key, s in scores.items()}
        ps = {key: jnp.exp(s - ms[key]) for key, s in scores.items()}
        def v_ext(v, hh):
            own = lax.broadcasted_iota(jnp.int32, v.shape, 1) < SWA_HD
            return jnp.where(own if hh == 0 else ~own, v, jnp.ones((), BF16))

        pvs = {(j, pair, hh): _dot(p.astype(BF16), v_ext(vbs[j, pair], hh)) for (j, pair, hh), p in ps.items()}
        news = {}
        for j, pair in items:
            m_g = jnp.where(lo, ms[j, pair, 0], ms[j, pair, 1])
            l_g = pltpu.roll(jnp.where(lo, pvs[j, pair, 1], pvs[j, pair, 0]), SWA_HD, 1)
            a_g = jnp.where(lo, pvs[j, pair, 0], pvs[j, pair, 1])
            if first_pattern:
                news[j, pair] = (m_g, l_g, a_g)
            else:
                m_old, l_old, a_old = olds[j, pair]
                m_new = jnp.maximum(m_old, m_g)
                w_old = jnp.exp(m_old - m_new)
                w_g = jnp.exp(m_g - m_new)
                news[j, pair] = (m_new, w_old * l_old + w_g * l_g, w_old * a_old + w_g * a_g)
        for (j, pair), (m_new, l_new, a_new) in news.items():
            starts = acc_starts[j]
            if last_pattern:
                o = a_new / l_new
                for i in range(len(starts)):
                    o_ref[pair, pl.ds(jobs[j][3] + i, n, stride=len(starts)), :] = o[i * n:(i + 1) * n]
            else:
                for i, s in enumerate(starts):
                    m_ref[pair, pl.ds(s, n), :] = m_new[i * n:(i + 1) * n]
                    l_ref[pair, pl.ds(s, n), :] = l_new[i * n:(i + 1) * n]
                    a_ref[pair, pl.ds(s, n), :] = a_new[i * n:(i + 1) * n]

    par = SWA_PAR
    _, bias16 = biases(lambda i: i)

    par16 = 2 * par

    def class16(r0, carry):
        blocks([([(t, r0 + k * (16 // par16), 0) for t in range(SEQ // pt)], None, bias16, None)
                for k in range(par16)], grp, True, False)
        return carry

    lax.fori_loop(0, 16 // par16, class16, 0)

    bias4_prev, bias4_own = biases(lambda i: 4 * (i % grp) + i // grp)

    def class4(r0, carry):
        chunks_of = lambda r, tile: [(tile, 4 * a + r, 0) for a in range(4)]
        classes = [r0 + k * (4 // par) for k in range(par)]
        blocks([(chunks_of(r, 0), None, bias4_own, None) for r in classes], grp, False, False)

        def per_block(b, c2):
            blocks([(chunks_of(r, b), chunks_of(r, b - 1), bias4_prev, None) for r in classes],
                   grp, False, False)
            return c2

        lax.fori_loop(1, SEQ // pt, per_block, 0)
        return carry

    lax.fori_loop(0, 4 // par, class4, 0)

    bias1_prev, bias1_own = biases(lambda i: 16 * (i % 8) + i // 8)
    per_tile = pt // blk
    chunks1 = lambda c: [(c // per_tile, i, (c % per_tile) * 8) for i in range(16)]
    job1 = lambda c: (chunks1(c), chunks1(c - 1), bias1_prev, aligned(c * blk))
    blocks([(chunks1(0), None, bias1_own, 0)] + [job1(c) for c in range(1, par)], 8, False, True)

    def block1(g, carry):
        blocks([job1(g * par + k) for k in range(par)], 8, False, True)
        return carry

    lax.fori_loop(1, SEQ // blk // par, block1, 0)


def _swa_prompt(sq, sk, sv):
    slabs = SWA_W // LANES
    spec = pl.BlockSpec((slabs, None, SEQ, LANES), lambda b: (0, b, 0, 0))
    in_spec = pl.BlockSpec((slabs, None, sq.shape[2], LANES), lambda b: (0, b, 0, 0))
    return pl.pallas_call(
        _swa_prompt_kernel,
        grid=(BATCH,),
        in_specs=[in_spec, in_spec, in_spec],
        out_specs=spec,
        out_shape=jax.ShapeDtypeStruct((slabs, BATCH, SEQ, LANES), F32),
        scratch_shapes=[pltpu.VMEM((slabs, SEQ, LANES), F32)] * 3,
        compiler_params=_params(1),
        name="swa_prompt",
    )(sq, sk, sv)


def _head_norm(o):
    mu = jnp.mean(o, axis=-1, keepdims=True)
    d = o - mu
    var = jnp.mean(d * d, axis=-1, keepdims=True)
    return d * lax.rsqrt(var + GN_EPS)


def _deepnorm_ln(x, h, gain, bias):
    z = DEEPNORM_ALPHA * x + h
    mu = jnp.mean(z, axis=-1, keepdims=True)
    d = z - mu
    var = jnp.mean(d * d, axis=-1, keepdims=True)
    return d * lax.rsqrt(var + LN_EPS) * gain + bias


def _softmax_rows(s):
    m = jnp.max(s, axis=-1, keepdims=True)
    p = jnp.exp(s - m)
    return p * (1.0 / jnp.sum(p, axis=-1, keepdims=True))


def _mix_prompt_kernel(x_ref, rq_ref, rk_ref, rv_ref, rg_ref, so_ref, sg_ref, mq_ref, mg_ref,
                       mk_ref, mv_ref, wout_ref, dmat_ref, kdec_ref, qdec_ref, gdec_ref,
                       gain_ref, bias_ref, y_ref, state_out_ref,
                       state_ref, mix_ref, mkb_ref, mvb_ref, *, tile):
    t = pl.program_id(1)

    @pl.when(t == 0)
    def _():
        state_ref[...] = jnp.zeros_like(state_ref)
        for h in range(MEM_HEADS):
            rows = pl.ds(h, N_MEM, stride=MEM_HEADS)
            mkb_ref[:, pl.ds(h * MEM_HD, MEM_HD)] = mk_ref[rows, :].astype(BF16)
            mvb_ref[:, pl.ds(h * MEM_HD, MEM_HD)] = mv_ref[rows, :].astype(BF16)

    ck = RET_CHUNK
    lane = lax.broadcasted_iota(jnp.int32, (ck, LANES), 1)
    lo = lane < RET_DK
    top = lax.broadcasted_iota(jnp.int32, (LANES, LANES), 0) < RET_DK
    gain = gain_ref[...]
    bias = bias_ref[...]

    n_ck = tile // ck
    n_pairs = RET_HEADS // 2
    rows_of = [pl.ds(c * ck, ck) for c in range(n_ck)]
    items = [(c, pair, hh) for c in range(n_ck) for pair in range(n_pairs) for hh in range(2)]
    hs_of = lambda pair, hh: pl.ds((2 * pair + hh) * RET_DV, RET_DV)
    sel = lambda x, hh: jnp.where(lo if hh == 0 else ~lo, x, 0.0).astype(BF16)
    pairs = [(c, pair) for c in range(n_ck) for pair in range(n_pairs)]
    qk = {}
    for c, pair in pairs:
        cs = pl.ds(pair * LANES, LANES)
        q = rq_ref[rows_of[c], cs]
        k = rk_ref[rows_of[c], cs]
        qk[c, pair] = (q.astype(BF16), jnp.concatenate([sel(k, 0), sel(k, 1)], axis=0),
                       (k * kdec_ref[:, cs]).astype(BF16), q * qdec_ref[:, cs])
    v_pair = lambda c, pair: rv_ref[rows_of[c], pl.ds(pair * 2 * RET_DV, 2 * RET_DV)]
    s2 = {key: _dot_nt(qk[key][0], qk[key][1]) for key in pairs}
    kv2 = {key: _dot_tn(qk[key][2], v_pair(*key)) for key in pairs}
    yield
    state = {(0, pair): state_ref[pair] for pair in range(n_pairs)}
    for c, pair in pairs:
        state[c + 1, pair] = (gdec_ref[pl.ds(pair * LANES, LANES), :] * state[c, pair]
                              + jnp.where(top, kv2[c, pair][:, :RET_DV], kv2[c, pair][:, RET_DV:]))
    for pair in range(n_pairs):
        state_ref[pair] = state[n_ck, pair]
    o = {}
    for c, pair, hh in items:
        sh = (s2[c, pair][:, hh * ck:(hh + 1) * ck] * dmat_ref[2 * pair + hh]).astype(BF16)
        lhs = jnp.concatenate([sh, sel(qk[c, pair][3], hh)], axis=1)
        rhs = jnp.concatenate([rv_ref[rows_of[c], hs_of(pair, hh)], state[c, pair].astype(BF16)], axis=0)
        o[c, pair, hh] = _dot(lhs, rhs)
    yield
    for c, pair, hh in items:
        hs = hs_of(pair, hh)
        mix_ref[rows_of[c], hs] = (rg_ref[rows_of[c], hs].astype(F32) * _head_norm(o[c, pair, hh])).astype(BF16)
    for pair in range(SWA_W // LANES):
        cs = pl.ds(pair * LANES, LANES)
        mix_ref[:, pl.ds(RET_W + pair * LANES, LANES)] = (
            sg_ref[:, cs].astype(F32) * so_ref[pair]).astype(BF16)
    for c in range(n_ck):
        yield
        ss = [_dot_nt(mq_ref[rows_of[c], pl.ds(h * MEM_HD, MEM_HD)], mkb_ref[:, pl.ds(h * MEM_HD, MEM_HD)])
              for h in range(MEM_HEADS)]
        ps = [jnp.exp2((s - jnp.max(s, axis=-1, keepdims=True)) * (MEM_SCALE * LOG2_E)) for s in ss]
        for h in range(MEM_HEADS):
            hs = pl.ds(h * MEM_HD, MEM_HD)
            o = _dot(ps[h].astype(BF16), mvb_ref[:, hs]) * (1.0 / jnp.sum(ps[h], axis=-1, keepdims=True))
            mix_ref[rows_of[c], pl.ds(RET_W + SWA_W + h * MEM_HD, MEM_HD)] = (
                mg_ref[rows_of[c], hs].astype(F32) * o).astype(BF16)
    yield
    blocks = [pl.ds(r * OUT_ROWS, OUT_ROWS) for r in range(tile // OUT_ROWS)]
    hout = _dot(mix_ref[blocks[0], :], wout_ref[...])
    for r, rows in enumerate(blocks):
        nxt = _dot(mix_ref[blocks[r + 1], :], wout_ref[...]) if r + 1 < len(blocks) else None
        y_ref[rows, :] = _deepnorm_ln(x_ref[rows, :], hout, gain, bias)
        hout = nxt

    @pl.when(t == pl.num_programs(1) - 1)
    def _():
        state_out_ref[pl.ds(0, LANES), :] = state_ref[0]
        state_out_ref[pl.ds(LANES, LANES), :] = state_ref[1]


def _mix_all_kernel(*refs, tile, n_in, n_out):
    p_in, s_in = refs[:n_in[0]], refs[n_in[0]:n_in[0] + n_in[1]]
    outs = refs[n_in[0] + n_in[1]:]
    p_out, s_out = outs[:n_out[0]], outs[n_out[0]:n_out[0] + n_out[1]]
    scratch = outs[n_out[0] + n_out[1]:]
    stages = [_mix_prompt_kernel(*p_in, *p_out, *scratch, tile=tile), _mix_sample_kernel(*s_in, *s_out)]
    while stages:
        for g in list(stages):
            if next(g, "done") == "done":
                stages.remove(g)


def _mix_all(x2d, pr, swa_o, mk, mv, wout_bf, tabs_p, gain, bias, tile,
             ps, state, ckt, cvt, cmk, cmv, tabs_s):
    nt = SEQ // tile
    assert BATCH * nt == DEC_BATCH
    step = lambda b, t: b * nt + t
    row = lambda w: pl.BlockSpec((tile, w), lambda b, t: (step(b, t), 0))
    const = lambda a: pl.BlockSpec(a.shape, lambda b, t: (0,) * a.ndim)
    mem_p = pl.BlockSpec((None, N_MEM * MEM_HEADS, MEM_HD), lambda b, t: (b, 0, 0))
    st_p = pl.BlockSpec((None, RET_HEADS * RET_DK, RET_DV), lambda b, t: (b, 0, 0))
    p_args = (x2d, pr["rq"], pr["rk"], pr["rv"], pr["rg"], swa_o, pr["sg"], pr["mq"], pr["mg"],
              mk, mv, wout_bf, *tabs_p, gain, bias)
    p_specs = [row(D_MODEL), row(RET_HEADS * RET_DK), row(RET_HEADS * RET_DK), row(RET_W), row(RET_W),
               pl.BlockSpec((SWA_W // LANES, tile, LANES), lambda b, t: (0, step(b, t), 0)),
               row(SWA_W), row(MEM_W), row(MEM_W), mem_p, mem_p, const(wout_bf),
               *[const(a) for a in tabs_p], const(gain), const(bias)]
    n = DEC_PAD
    wb = ckt.shape[3]
    srow = lambda w: pl.BlockSpec((n, w), lambda b, t: (step(b, t), 0))
    st_s = pl.BlockSpec((None, RET_HEADS * RET_DK, RET_DV), lambda b, t: (step(b, t), 0, 0))
    cache_s = pl.BlockSpec((None, SWA_HEADS, SWA_HD, wb), lambda b, t: (step(b, t), 0, 0, 0))
    mem_s = pl.BlockSpec((None, N_MEM * MEM_HEADS, MEM_HD), lambda b, t: (step(b, t), 0, 0))
    s_names = ("rq", "rk", "rv", "rg", "sq", "sk", "sv", "sg", "mq", "mg")
    widths = {name: w for name, w, _, _ in _PROJ_COLS}
    s_args = (*[ps[k] for k in s_names], state, ckt, cvt, cmk, cmv, *tabs_s)
    s_specs = [*[srow(widths[k]) for k in s_names], st_s, cache_s, cache_s, mem_s, mem_s,
               *[const(a) for a in tabs_s]]
    return pl.pallas_call(
        functools.partial(_mix_all_kernel, tile=tile, n_in=(len(p_args), len(s_args)), n_out=(2, 2)),
        grid=(BATCH, nt),
        in_specs=p_specs + s_specs,
        out_specs=[row(D_MODEL), st_p, srow(D_MIX), st_s],
        out_shape=[jax.ShapeDtypeStruct((BATCH * SEQ, D_MODEL), F32),
                   jax.ShapeDtypeStruct((BATCH, RET_HEADS * RET_DK, RET_DV), F32),
                   jax.ShapeDtypeStruct((DEC_BATCH * n, D_MIX), F32),
                   jax.ShapeDtypeStruct((DEC_BATCH, RET_HEADS * RET_DK, RET_DV), F32)],
        scratch_shapes=[pltpu.VMEM((RET_HEADS // 2, LANES, RET_DV), F32),
                        pltpu.VMEM((tile, D_MIX), BF16),
                        pltpu.VMEM((N_MEM, MEM_W), BF16),
                        pltpu.VMEM((N_MEM, MEM_W), BF16)],
        compiler_params=_params(2),
        name="mix_all",
    )(*p_args, *s_args)


def _mix_sample_kernel(rq_ref, rk_ref, rv_ref, rg_ref, sq_ref, sk_ref, sv_ref, sg_ref, mq_ref, mg_ref,
                       state_ref, ckt_ref, cvt_ref, cmk_ref, cmv_ref,
                       dmat_ref, kdec_ref, qdec_ref, gdec_ref,
                       mix_ref, state_out_ref):
    n = DEC_PAD
    q = rq_ref[...]
    k = rk_ref[...]
    kb = k.astype(BF16)
    kd = (k * kdec_ref[...]).astype(BF16)
    qd = q * qdec_ref[...]
    st = state_ref[...]
    stb = st.astype(BF16)
    vb = rv_ref[...].astype(BF16)
    kv = _dot_tn(kd, vb)
    lane_qk = lax.broadcasted_iota(jnp.int32, q.shape, 1) // RET_DK
    ret_s = [_dot_nt(jnp.where(lane_qk == h, q, 0.0).astype(BF16), kb) for h in range(RET_HEADS)]
    ret_cross = [_dot(jnp.where(lane_qk == h, qd, 0.0).astype(BF16), stb) for h in range(RET_HEADS)]
    mem_rows = [pl.ds(h, N_MEM, stride=MEM_HEADS) for h in range(MEM_HEADS)]
    mem_s = [_dot_nt(mq_ref[:, pl.ds(h * MEM_HD, MEM_HD)].astype(BF16), cmk_ref[mem_rows[h], :].astype(BF16))
             for h in range(MEM_HEADS)]

    blk = SWA_BLOCK
    wb = ckt_ref.shape[2]
    rt = 8
    lo = lax.broadcasted_iota(jnp.int32, (n, LANES), 1) < SWA_HD
    lo_t = lax.broadcasted_iota(jnp.int32, (rt, LANES), 1) < SWA_HD
    tok = lambda w: lax.broadcasted_iota(jnp.int32, (rt, w), 0)
    col = lambda w: lax.broadcasted_iota(jnp.int32, (rt, w), 1)
    windows = (
        (wb - blk, jnp.where(col(blk) >= tok(blk), 0.0, NEG), jnp.where(col(blk) <= tok(blk), 0.0, NEG)),
        (wb - 4 * blk, jnp.where(col(4 * blk) % 4 == tok(4 * blk), 0.0, NEG),
         jnp.where(col(blk) == tok(blk), 0.0, NEG)),
        (0, jnp.where(col(wb) % 16 == tok(wb), 0.0, NEG), jnp.where(col(blk) == tok(blk), 0.0, NEG)),
    )
    pad = jnp.zeros((blk - n, LANES), BF16)
    zero_c = jnp.zeros((rt, wb), F32)
    zero_n = jnp.zeros((rt, blk), F32)
    n_pairs = SWA_HEADS // 2
    n_win = len(windows)
    pair_cols = [pl.ds(pair * LANES, LANES) for pair in range(n_pairs)]
    vps = [cvt_ref[2 * pair:2 * pair + 2].reshape(2 * SWA_HD, wb).astype(BF16) for pair in range(n_pairs)]
    v_news = [jnp.concatenate([sv_ref[:, cs].astype(BF16), pad], axis=0) for cs in pair_cols]
    s_alls, sn_alls = [], []
    for pair, cs in enumerate(pair_cols):
        kp = ckt_ref[2 * pair:2 * pair + 2].reshape(2 * SWA_HD, wb).astype(BF16)
        k_new = jnp.concatenate([sk_ref[:, cs].astype(BF16), pad], axis=0)
        q = sq_ref[:, cs]
        qs = jnp.concatenate([jnp.where(lo, q, 0.0), jnp.where(lo, 0.0, q)], axis=0).astype(BF16)
        s_alls.append(_dot(qs, kp))
        sn_alls.append(_dot_nt(qs, k_new))
    yield
    p_rows = [[] for _ in range(n_pairs)]
    pn_rows = [[] for _ in range(n_pairs)]
    stats = [[] for _ in range(n_pairs)]
    for pair in range(n_pairs):
        for hh in range(2):
            s = s_alls[pair][hh * n:hh * n + rt]
            sn = sn_alls[pair][hh * n:hh * n + rt]
            for w0, bias_c, bias_n in windows:
                sc = s[:, w0:] + bias_c
                snb = sn + bias_n
                m = jnp.maximum(jnp.max(sc, axis=-1, keepdims=True), jnp.max(snb, axis=-1, keepdims=True))
                pc = jnp.exp(sc - m)
                pn = jnp.exp(snb - m)
                stats[pair].append((m, jnp.sum(pc, axis=-1, keepdims=True) + jnp.sum(pn, axis=-1, keepdims=True)))
                if w0:
                    pc = jnp.concatenate([jnp.zeros((rt, w0), F32), pc], axis=1)
                p_rows[pair].append(pc)
                pn_rows[pair].append(pn)
            p_rows[pair].append(zero_c)
            pn_rows[pair].append(zero_n)
    ret_sb = [(ret_s[h] * dmat_ref[h]).astype(BF16) for h in range(RET_HEADS)]
    mem_p = [_softmax_rows(s * MEM_SCALE).astype(BF16) for s in mem_s]
    yield
    pvs = [_dot_nt(jnp.concatenate(p_rows[pair], axis=0).astype(BF16), vps[pair])
           + _dot(jnp.concatenate(pn_rows[pair], axis=0).astype(BF16), v_news[pair])
           for pair in range(n_pairs)]
    ret_intra = [_dot(ret_sb[h], vb[:, h * RET_DV:(h + 1) * RET_DV]) for h in range(RET_HEADS)]
    mem_o = [_dot(mem_p[h], cmv_ref[mem_rows[h], :].astype(BF16)) for h in range(MEM_HEADS)]
    yield
    pieces = []
    for pair in range(n_pairs):
        heads = []
        for hh in range(2):
            parts = []
            for i in range(n_win):
                m, l = stats[pair][hh * n_win + i]
                r0 = (hh * (n_win + 1) + i) * rt
                parts.append((pvs[pair][r0:r0 + rt] / l, m, l))
            m_all = jnp.maximum(jnp.maximum(parts[0][1], parts[1][1]), parts[2][1])
            ws = [l * jnp.exp(m - m_all) for (_, m, l) in parts]
            heads.append((ws[0] * parts[0][0] + ws[1] * parts[1][0] + ws[2] * parts[2][0])
                         / (ws[0] + ws[1] + ws[2]))
        pieces.append(jnp.where(lo_t, heads[0], heads[1]))
    swa = jnp.concatenate(pieces, axis=1)
    swa = jnp.concatenate([swa, jnp.zeros((n - rt, SWA_W), F32)], axis=0)
    mix_ref[:, pl.ds(RET_W, SWA_W)] = sg_ref[...] * swa
    for h in range(RET_HEADS):
        hs = pl.ds(h * RET_DV, RET_DV)
        ks = pl.ds(h * RET_DK, RET_DK)
        mix_ref[:, hs] = rg_ref[:, hs] * _head_norm(ret_intra[h] + ret_cross[h])
        state_out_ref[ks, :] = (gdec_ref[ks, :] * st[h * RET_DK:(h + 1) * RET_DK, :]
                                + kv[h * RET_DK:(h + 1) * RET_DK, h * RET_DV:(h + 1) * RET_DV])
    for h in range(MEM_HEADS):
        hs = pl.ds(h * MEM_HD, MEM_HD)
        mix_ref[:, pl.ds(RET_W + SWA_W + h * MEM_HD, MEM_HD)] = mg_ref[:, hs] * mem_o[h]


def _finish_kernel(x_ref, mix_ref, wout_ref, gain_ref, bias_ref, y_ref):
    hout = _dot(mix_ref[...].astype(BF16), wout_ref[...])
    y_ref[...] = _deepnorm_ln(x_ref[...], hout, gain_ref[...], bias_ref[...])


def _finish(x2d, mix, wout_bf, gain, bias):
    n = x2d.shape[0]
    tile = 256
    const2 = lambda a: pl.BlockSpec(a.shape, lambda i: (0, 0))
    return pl.pallas_call(
        _finish_kernel,
        grid=(n // tile,),
        in_specs=[pl.BlockSpec((tile, D_MODEL), lambda i: (i, 0)),
                  pl.BlockSpec((tile, D_MIX), lambda i: (i, 0)),
                  const2(wout_bf), const2(gain), const2(bias)],
        out_specs=pl.BlockSpec((tile, D_MODEL), lambda i: (i, 0)),
        out_shape=jax.ShapeDtypeStruct((n, D_MODEL), F32),
        compiler_params=_params(1),
        name="finish",
    )(x2d, mix, wout_bf, gain, bias)


def _rope_tables(pos):
    half = SWA_HD // 2
    inv = ROPE_THETA ** (-np.arange(half, dtype=np.float64) * 2.0 / SWA_HD)
    ang = pos.astype(np.float64)[:, None] * inv[None, :]
    cos, sin = np.cos(ang), np.sin(ang)
    reps = LANES // SWA_HD
    return (jnp.asarray(np.tile(np.concatenate([cos, cos], axis=1), (1, reps)), F32),
            jnp.asarray(np.tile(np.concatenate([-sin, sin], axis=1), (1, reps)), F32))


def _retention_tables(chunk, rows):
    lg = np.log1p(-np.exp2(-5.0 - np.arange(RET_HEADS, dtype=np.float64)))
    idx = np.arange(rows, dtype=np.float64)
    live = idx < chunk
    rel = idx[:, None] - idx[None, :]
    ok = (rel >= 0) & live[:, None] & live[None, :]
    dmat = np.where(ok[None], np.exp(np.maximum(rel, 0.0)[None] * lg[:, None, None]), 0.0)
    kdec = np.where(live[:, None], np.exp((chunk - 1.0 - idx)[:, None] * lg[None, :]), 0.0)
    qdec = np.where(live[:, None], np.exp((idx + 1.0)[:, None] * lg[None, :]), 0.0)
    g = np.exp(chunk * lg)
    kdec = np.repeat(kdec, RET_DK, axis=1)
    qdec = np.repeat(qdec, RET_DK, axis=1)
    gdec = np.broadcast_to(np.repeat(g, RET_DK)[:, None], (RET_HEADS * RET_DK, RET_DV))
    return tuple(jnp.asarray(t, F32) for t in (dmat, kdec, qdec, gdec))


def kernel(x_prompt, x_sample, state_ret, cache_swa_k, cache_swa_v, cache_mem_k, cache_mem_v,
           mem_prompt, w_in, w_mem_kv, w_out, ln_gain, ln_bias):
    depth = w_in.shape[0]
    assert depth == 1
    win_bf = w_in[0].astype(BF16)
    wmem_bf = w_mem_kv[0].astype(BF16)
    wout_bf = w_out[0].astype(BF16)
    gain = ln_gain[0].reshape(1, D_MODEL)
    bias = ln_bias[0].reshape(1, D_MODEL)

    xp = x_prompt.reshape(BATCH * SEQ, D_MODEL)
    cos_p, sin_p = _rope_tables(np.arange(SEQ))
    p_outs = (("rq", "rows", F32), ("rk", "rows", F32), ("rv", "rows", BF16), ("rg", "rows", BF16),
              ("sq", "perm", F32), ("sk", "cols", F32), ("sk", "perm", F32), ("sv", "cols", F32),
              ("sv", "perm", F32), ("sg", "rows", BF16), ("mq", "rows", BF16), ("mg", "rows", BF16))
    p_keys = ("rq", "rk", "rv", "rg", "sq4", "sk", "sk4", "sv", "sv4", "sg", "mq", "mg")
    pr = dict(zip(p_keys, _project(xp, win_bf, cos_p, sin_p, PERM_TILE, p_outs)))
    mk, mv = _memkv(mem_prompt.reshape(BATCH * N_MEM, D_MODEL), wmem_bf)
    slab = lambda a: a.reshape(SWA_W // LANES, BATCH, SEQ // PERM_TILE * PERM_ROWS, LANES)
    swa_o = _swa_prompt(slab(pr["sq4"]), slab(pr["sk4"]), slab(pr["sv4"]))
    swa_o = swa_o.reshape(SWA_W // LANES, BATCH * SEQ, LANES)
    tabs_p = _retention_tables(RET_CHUNK, RET_CHUNK)

    xs = jnp.pad(x_sample, ((0, 0), (0, DEC_PAD - DEC_SEQ), (0, 0))).reshape(DEC_BATCH * DEC_PAD, D_MODEL)
    pos_s = PAST_LEN + np.arange(DEC_BATCH * DEC_PAD) % DEC_PAD
    cos_s, sin_s = _rope_tables(pos_s)
    names = [c[0] for c in _PROJ_COLS]
    ps = dict(zip(names, _project(xs, win_bf, cos_s, sin_s, DEC_BATCH * DEC_PAD,
                                  [(k, "rows", F32) for k in names])))
    tabs_s = _retention_tables(DEC_SEQ, DEC_PAD)
    yp, ret_p, mix_s, ret_s = _mix_all(
        xp, pr, swa_o, mk, mv, wout_bf, tabs_p, gain, bias, 512,
        ps, state_ret[0].reshape(DEC_BATCH, RET_HEADS * RET_DK, RET_DV),
        cache_swa_k[0].transpose(0, 2, 3, 1), cache_swa_v[0].transpose(0, 2, 3, 1),
        cache_mem_k[0].reshape(DEC_BATCH, N_MEM * MEM_HEADS, MEM_HD),
        cache_mem_v[0].reshape(DEC_BATCH, N_MEM * MEM_HEADS, MEM_HD),
        tabs_s)
    ys = _finish(xs, mix_s, wout_bf, gain, bias)

    take = lambda a, w: a.reshape(DEC_BATCH, DEC_PAD, w)[:, :DEC_SEQ]
    swa_rows = lambda a: a.reshape(BATCH, SWA_HEADS, SWA_HD, SEQ).transpose(0, 3, 1, 2)[None]
    return (
        yp.reshape(BATCH, SEQ, D_MODEL),
        take(ys, D_MODEL),
        ret_p.reshape(1, BATCH, RET_HEADS, RET_DK, RET_DV),
        ret_s.reshape(1, DEC_BATCH, RET_HEADS, RET_DK, RET_DV),
        swa_rows(pr["sk"]),
        swa_rows(pr["sv"]),
        take(ps["sk"], SWA_W).reshape(1, DEC_BATCH, DEC_SEQ, SWA_HEADS, SWA_HD),
        take(ps["sv"], SWA_W).reshape(1, DEC_BATCH, DEC_SEQ, SWA_HEADS, SWA_HD),
        mk.reshape(1, BATCH, N_MEM, MEM_HEADS, MEM_HD),
        mv.reshape(1, BATCH, N_MEM, MEM_HEADS, MEM_HD),
    )
```

```python
import functools

import jax
import jax.numpy as jnp
import numpy as np
from jax import lax
from jax.experimental import pallas as pl
from jax.experimental.pallas import tpu as pltpu

F32 = jnp.float32
BF16 = jnp.bfloat16

D_MODEL = 1024
BATCH = 8
SEQ = 2048
DEC_BATCH = 32
DEC_SEQ = 4
PAST_LEN = 8192
N_MEM = 256
MEM_HEADS = 4
MEM_HD = 128
RET_HEADS = 4
RET_DK = 64
RET_DV = 128
RET_CHUNK = 128
SWA_HEADS = 8
SWA_HD = 64
SWA_DILATIONS = (1, 4, 16)
SWA_STEPS = 128
SWA_BLOCK = 128
ROPE_THETA = 10000.0
LN_EPS = 1e-5
GN_EPS = 1e-5
RET_W = RET_HEADS * RET_DV
SWA_W = SWA_HEADS * SWA_HD
MEM_W = MEM_HEADS * MEM_HD
D_MIX = RET_W + SWA_W + MEM_W
DEEPNORM_ALPHA = 2.0 ** 0.25
MEM_SCALE = MEM_HD ** -0.5
LOG2_E = 1.4426950408889634
QK_SCALE = 0.125

LANES = 128
DEC_PAD = 16
PERM_TILE = 512
PERM_PITCH = 40
PERM_ROWS = 16 * PERM_PITCH
OUT_ROWS = 256
SWA_PAR = 2
MEMKV_BATCHES = 4
VMEM_LIMIT = 56 * 1024 * 1024
NEG = -1e30

_PROJ_COLS = (
    ("rq", RET_HEADS * RET_DK, "rope", 1.0),
    ("rk", RET_HEADS * RET_DK, "rope", QK_SCALE),
    ("rv", RET_W, "id", 1.0),
    ("rg", RET_W, "silu", 1.0),
    ("sq", SWA_W, "rope", QK_SCALE),
    ("sk", SWA_W, "rope", 1.0),
    ("sv", SWA_W, "id", 1.0),
    ("sg", SWA_W, "silu", 1.0),
    ("mq", MEM_W, "id", 1.0),
    ("mg", MEM_W, "silu", 1.0),
)


def _dot(a, b):
    return jnp.dot(a, b, preferred_element_type=F32)


def _dot_nt(a, b):
    return lax.dot_general(a, b, (((1,), (1,)), ((), ())), preferred_element_type=F32)


def _dot_tn(a, b):
    return lax.dot_general(a, b, (((0,), (0,)), ((), ())), preferred_element_type=F32)


def _params(n_axes):
    return pltpu.CompilerParams(dimension_semantics=("arbitrary",) * n_axes,
                                vmem_limit_bytes=VMEM_LIMIT)


def _proj_kernel(x_ref, w_ref, cos_ref, sin_ref, *out_refs, dests):
    xb = x_ref[...].astype(BF16)
    cos = cos_ref[...]
    sin = sin_ref[...]
    lane = lax.broadcasted_iota(jnp.int32, cos.shape, 1)
    first_half = (lane % 64) < 32
    col = 0
    for name, width, kind, scale in _PROJ_COLS:
        targets = [(o_ref, layout) for o_ref, (dname, layout) in zip(out_refs, dests) if dname == name]
        for c in range(0, width, 2 * LANES):
            h2 = _dot(xb, w_ref[:, col + c:col + c + 2 * LANES])
            for half in range(2):
                h = h2[:, half * LANES:(half + 1) * LANES]
                if kind == "rope":
                    swapped = jnp.where(first_half, pltpu.roll(h, 96, 1), pltpu.roll(h, 32, 1))
                    h = h * cos + swapped * sin
                    if scale != 1.0:
                        h = h * scale
                elif kind == "silu":
                    h = h * (1.0 / (1.0 + jnp.exp(-h)))
                lo = c + half * LANES
                for o_ref, layout in targets:
                    if layout == "rows":
                        o_ref[:, lo:lo + LANES] = h.astype(o_ref.dtype)
                    elif layout == "perm":
                        slab = lo // LANES
                        for g in range(h.shape[0] // 8):
                            base = (g % 2) * 8 * PERM_PITCH + g // 2
                            o_ref[slab, pl.ds(base, 8, stride=PERM_PITCH), :] = (
                                h[8 * g:8 * g + 8].astype(o_ref.dtype))
                        n_live = h.shape[0] // 16
                        for cls in range(16):
                            o_ref[slab, pl.ds(cls * PERM_PITCH + n_live, PERM_PITCH - n_live), :] = (
                                jnp.zeros((PERM_PITCH - n_live, LANES), o_ref.dtype))
                    else:
                        o_ref[lo:lo + LANES, :] = h.T.astype(o_ref.dtype)
        col += width


def _project(x2d, w_bf, cos_t, sin_t, tile, outs):
    n = x2d.shape[0]
    seq = cos_t.shape[0]
    n_tab = seq // tile
    d_in = w_bf.shape[1]
    widths = {name: w for name, w, _, _ in _PROJ_COLS}
    out_shape, out_specs = [], []
    for name, layout, dt in outs:
        w = widths[name]
        if layout == "rows":
            out_shape.append(jax.ShapeDtypeStruct((n, w), dt))
            out_specs.append(pl.BlockSpec((tile, w), lambda i: (i, 0)))
        elif layout == "perm":
            assert tile == PERM_TILE
            out_shape.append(jax.ShapeDtypeStruct((w // LANES, n // tile * PERM_ROWS, LANES), dt))
            out_specs.append(pl.BlockSpec((w // LANES, PERM_ROWS, LANES), lambda i: (0, i, 0)))
        else:
            out_shape.append(jax.ShapeDtypeStruct((n // seq, w, seq), dt))
            out_specs.append(pl.BlockSpec((None, w, tile), lambda i: (i // n_tab, 0, i % n_tab)))
    return pl.pallas_call(
        functools.partial(_proj_kernel, dests=tuple((name, layout) for name, layout, _ in outs)),
        grid=(n // tile,),
        in_specs=[
            pl.BlockSpec((tile, D_MODEL), lambda i: (i, 0)),
            pl.BlockSpec((D_MODEL, d_in), lambda i: (0, 0)),
            pl.BlockSpec((tile, LANES), lambda i: (i % n_tab, 0)),
            pl.BlockSpec((tile, LANES), lambda i: (i % n_tab, 0)),
        ],
        out_specs=out_specs,
        out_shape=out_shape,
        compiler_params=_params(1),
        name="proj",
    )(x2d, w_bf, cos_t, sin_t)


def _memkv_kernel(m_ref, w_ref, mk_ref, mv_ref):
    for b in range(m_ref.shape[0] // N_MEM):
        mb = m_ref[pl.ds(b * N_MEM, N_MEM), :].astype(BF16)
        for h in range(MEM_HEADS):
            rows = pl.ds(h, N_MEM, stride=MEM_HEADS)
            mk_ref[b, rows, :] = _dot(mb, w_ref[:, h * MEM_HD:(h + 1) * MEM_HD])
            mv_ref[b, rows, :] = _dot(mb, w_ref[:, MEM_W + h * MEM_HD:MEM_W + (h + 1) * MEM_HD])


def _memkv(mem2d, w_bf):
    n = mem2d.shape[0] // N_MEM
    per = MEMKV_BATCHES
    out = pl.BlockSpec((per, N_MEM * MEM_HEADS, MEM_HD), lambda i: (i, 0, 0))
    return pl.pallas_call(
        _memkv_kernel,
        grid=(n // per,),
        in_specs=[pl.BlockSpec((per * N_MEM, D_MODEL), lambda i: (i, 0)),
                  pl.BlockSpec((D_MODEL, 2 * MEM_W), lambda i: (0, 0))],
        out_specs=[out, out],
        out_shape=[jax.ShapeDtypeStruct((n, N_MEM * MEM_HEADS, MEM_HD), F32)] * 2,
        compiler_params=_params(1),
        name="memkv",
    )(mem2d, w_bf)


def _swa_prompt_kernel(q_ref, k_ref, v_ref, o_ref, a_ref, m_ref, l_ref):
    blk = SWA_BLOCK
    pt = PERM_TILE
    grp = pt // 16
    lo = lax.broadcasted_iota(jnp.int32, (blk, blk), 1) < SWA_HD

    def biases(seq_of):
        rq = seq_of(lax.broadcasted_iota(jnp.int32, (blk, 2 * blk), 0))
        c2 = lax.broadcasted_iota(jnp.int32, (blk, 2 * blk), 1)
        rel = blk + rq - (seq_of(c2 % blk) + blk * (c2 // blk))
        prev = jnp.where((rel >= 0) & (rel <= SWA_STEPS), 0.0, NEG).astype(F32)
        r1 = seq_of(lax.broadcasted_iota(jnp.int32, (blk, blk), 0))
        c1 = seq_of(lax.broadcasted_iota(jnp.int32, (blk, blk), 1))
        return prev, jnp.where(r1 >= c1, 0.0, NEG).astype(F32)

    def gather(ref, pair, starts, n):
        return jnp.concatenate([ref[pair, pl.ds(s, n), :] for s in starts], axis=0)

    def aligned(x):
        return x if isinstance(x, int) else pl.multiple_of(x, 8)

    def blocks(jobs, n, first_pattern, last_pattern):
        n_pairs = SWA_HEADS // 2
        in_at = lambda c: aligned(c[0] * PERM_ROWS + c[1] * PERM_PITCH + c[2])
        acc_starts, q_starts, k_starts = [], [], []
        for chunks, prev_chunks, _, _ in jobs:
            acc_starts.append([aligned(c[0] * pt + c[1] * grp + c[2]) for c in chunks])
            q_starts.append([in_at(c) for c in chunks])
            k_starts.append(([] if prev_chunks is None else [in_at(c) for c in prev_chunks]) + q_starts[-1])
        items = [(j, pair) for j in range(len(jobs)) for pair in range(n_pairs)]
        olds = {}
        if not first_pattern:
            for j, pair in items:
                olds[j, pair] = (gather(m_ref, pair, acc_starts[j], n), gather(l_ref, pair, acc_starts[j], n),
                                 gather(a_ref, pair, acc_starts[j], n))
        vbs = {(j, pair): gather(v_ref, pair, k_starts[j], n).astype(BF16) for j, pair in items}
        scores = {}
        for j, pair in items:
            q = gather(q_ref, pair, q_starts[j], n)
            kb = gather(k_ref, pair, k_starts[j], n).astype(BF16)
            for hh in range(2):
                qm = jnp.where(lo if hh == 0 else ~lo, q, 0.0).astype(BF16)
                scores[j, pair, hh] = _dot_nt(qm, kb) + jobs[j][2]
        ms = {key: jnp.max(s, axis=-1, keepdims=True) for key, s in scores.items()}
        ps = {key: jnp.exp(s - ms[key]) for key, s in scores.items()}
        def v_ext(v):
            lane = lax.broadcasted_iota(jnp.int32, v.shape, 1)
            own = lane < SWA_HD
            zero = jnp.zeros((), BF16)
            ones_lo = jnp.where(lane < SWA_HD, 1.0, 0.0).astype(BF16)
            ones_hi = jnp.where(lane < SWA_HD, 0.0, 1.0).astype(BF16)
            return jnp.concatenate(
                [jnp.concatenate([jnp.where(own, v, zero), ones_lo], axis=1),
                 jnp.concatenate([jnp.where(own, zero, v), ones_hi], axis=1)], axis=0)

        pvs = {(j, pair): _dot(jnp.concatenate([ps[j, pair, 0].astype(BF16), ps[j, pair, 1].astype(BF16)], axis=1),
                               v_ext(vbs[j, pair])) for j, pair in items}
        news = {}
        for j, pair in items:
            m_g = jnp.where(lo, ms[j, pair, 0], ms[j, pair, 1])
            a_g = pvs[j, pair][:, :LANES]
            l_g = pvs[j, pair][:, LANES:]
            if first_pattern:
                news[j, pair] = (m_g, l_g, a_g)
            else:
                m_old, l_old, a_old = olds[j, pair]
                m_new = jnp.maximum(m_old, m_g)
                w_old = jnp.exp(m_old - m_new)
                w_g = jnp.exp(m_g - m_new)
                news[j, pair] = (m_new, w_old * l_old + w_g * l_g, w_old * a_old + w_g * a_g)
        for (j, pair), (m_new, l_new, a_new) in news.items():
            starts = acc_starts[j]
            if last_pattern:
                o = a_new / l_new
                for i in range(len(starts)):
                    o_ref[pair, pl.ds(jobs[j][3] + i, n, stride=len(starts)), :] = o[i * n:(i + 1) * n]
            else:
                for i, s in enumerate(starts):
                    m_ref[pair, pl.ds(s, n), :] = m_new[i * n:(i + 1) * n]
                    l_ref[pair, pl.ds(s, n), :] = l_new[i * n:(i + 1) * n]
                    a_ref[pair, pl.ds(s, n), :] = a_new[i * n:(i + 1) * n]

    par = SWA_PAR
    _, bias16 = biases(lambda i: i)

    par16 = 2 * par

    def class16(r0, carry):
        blocks([([(t, r0 + k * (16 // par16), 0) for t in range(SEQ // pt)], None, bias16, None)
                for k in range(par16)], grp, True, False)
        return carry

    lax.fori_loop(0, 16 // par16, class16, 0)

    bias4_prev, bias4_own = biases(lambda i: 4 * (i % grp) + i // grp)

    def class4(r0, carry):
        chunks_of = lambda r, tile: [(tile, 4 * a + r, 0) for a in range(4)]
        classes = [r0 + k * (4 // par) for k in range(par)]
        blocks([(chunks_of(r, 0), None, bias4_own, None) for r in classes], grp, False, False)

        def per_block(b, c2):
            blocks([(chunks_of(r, b), chunks_of(r, b - 1), bias4_prev, None) for r in classes],
                   grp, False, False)
            return c2

        lax.fori_loop(1, SEQ // pt, per_block, 0)
        return carry

    lax.fori_loop(0, 4 // par, class4, 0)

    bias1_prev, bias1_own = biases(lambda i: 16 * (i % 8) + i // 8)
    per_tile = pt // blk
    chunks1 = lambda c: [(c // per_tile, i, (c % per_tile) * 8) for i in range(16)]
    job1 = lambda c: (chunks1(c), chunks1(c - 1), bias1_prev, aligned(c * blk))
    blocks([(chunks1(0), None, bias1_own, 0)] + [job1(c) for c in range(1, par)], 8, False, True)

    def block1(g, carry):
        blocks([job1(g * par + k) for k in range(par)], 8, False, True)
        return carry

    lax.fori_loop(1, SEQ // blk // par, block1, 0)


def _swa_prompt(sq, sk, sv):
    slabs = SWA_W // LANES
    spec = pl.BlockSpec((slabs, None, SEQ, LANES), lambda b: (0, b, 0, 0))
    in_spec = pl.BlockSpec((slabs, None, sq.shape[2], LANES), lambda b: (0, b, 0, 0))
    return pl.pallas_call(
        _swa_prompt_kernel,
        grid=(BATCH,),
        in_specs=[in_spec, in_spec, in_spec],
        out_specs=spec,
        out_shape=jax.ShapeDtypeStruct((slabs, BATCH, SEQ, LANES), F32),
        scratch_shapes=[pltpu.VMEM((slabs, SEQ, LANES), F32)] * 3,
        compiler_params=_params(1),
        name="swa_prompt",
    )(sq, sk, sv)


def _head_norm(o):
    mu = jnp.mean(o, axis=-1, keepdims=True)
    d = o - mu
    var = jnp.mean(d * d, axis=-1, keepdims=True)
    return d * lax.rsqrt(var + GN_EPS)


def _deepnorm_ln(x, h, gain, bias):
    z = DEEPNORM_ALPHA * x + h
    mu = jnp.mean(z, axis=-1, keepdims=True)
    d = z - mu
    var = jnp.mean(d * d, axis=-1, keepdims=True)
    return d * lax.rsqrt(var + LN_EPS) * gain + bias


def _softmax_rows(s):
    m = jnp.max(s, axis=-1, keepdims=True)
    p = jnp.exp(s - m)
    return p * (1.0 / jnp.sum(p, axis=-1, keepdims=True))


def _mix_prompt_kernel(x_ref, rq_ref, rk_ref, rv_ref, rg_ref, so_ref, sg_ref, mq_ref, mg_ref,
                       mk_ref, mv_ref, wout_ref, dmat_ref, kdec_ref, qdec_ref, gdec_ref,
                       gain_ref, bias_ref, y_ref, state_out_ref,
                       state_ref, mix_ref, mkb_ref, mvb_ref, *, tile):
    t = pl.program_id(1)

    @pl.when(t == 0)
    def _():
        state_ref[...] = jnp.zeros_like(state_ref)
        for h in range(MEM_HEADS):
            rows = pl.ds(h, N_MEM, stride=MEM_HEADS)
            mkb_ref[:, pl.ds(h * MEM_HD, MEM_HD)] = mk_ref[rows, :].astype(BF16)
            mvb_ref[:, pl.ds(h * MEM_HD, MEM_HD)] = mv_ref[rows, :].astype(BF16)

    ck = RET_CHUNK
    lane = lax.broadcasted_iota(jnp.int32, (ck, LANES), 1)
    lo = lane < RET_DK
    top = lax.broadcasted_iota(jnp.int32, (LANES, LANES), 0) < RET_DK
    gain = gain_ref[...]
    bias = bias_ref[...]

    n_ck = tile // ck
    n_pairs = RET_HEADS // 2
    rows_of = [pl.ds(c * ck, ck) for c in range(n_ck)]
    items = [(c, pair, hh) for c in range(n_ck) for pair in range(n_pairs) for hh in range(2)]
    hs_of = lambda pair, hh: pl.ds((2 * pair + hh) * RET_DV, RET_DV)
    sel = lambda x, hh: jnp.where(lo if hh == 0 else ~lo, x, 0.0).astype(BF16)
    pairs = [(c, pair) for c in range(n_ck) for pair in range(n_pairs)]
    qk = {}
    for c, pair in pairs:
        cs = pl.ds(pair * LANES, LANES)
        q = rq_ref[rows_of[c], cs]
        k = rk_ref[rows_of[c], cs]
        qk[c, pair] = (q.astype(BF16), jnp.concatenate([sel(k, 0), sel(k, 1)], axis=0),
                       (k * kdec_ref[:, cs]).astype(BF16), q * qdec_ref[:, cs])
    v_pair = lambda c, pair: rv_ref[rows_of[c], pl.ds(pair * 2 * RET_DV, 2 * RET_DV)]
    s2 = {key: _dot_nt(qk[key][0], qk[key][1]) for key in pairs}
    kv2 = {key: _dot_tn(qk[key][2], v_pair(*key)) for key in pairs}
    yield
    state = {(0, pair): state_ref[pair] for pair in range(n_pairs)}
    for c, pair in pairs:
        state[c + 1, pair] = (gdec_ref[pl.ds(pair * LANES, LANES), :] * state[c, pair]
                              + jnp.where(top, kv2[c, pair][:, :RET_DV], kv2[c, pair][:, RET_DV:]))
    for pair in range(n_pairs):
        state_ref[pair] = state[n_ck, pair]
    o = {}
    for c, pair, hh in items:
        sh = (s2[c, pair][:, hh * ck:(hh + 1) * ck] * dmat_ref[2 * pair + hh]).astype(BF16)
        lhs = jnp.concatenate([sh, sel(qk[c, pair][3], hh)], axis=1)
        rhs = jnp.concatenate([rv_ref[rows_of[c], hs_of(pair, hh)], state[c, pair].astype(BF16)], axis=0)
        o[c, pair, hh] = _dot(lhs, rhs)
    yield
    for c, pair, hh in items:
        hs = hs_of(pair, hh)
        mix_ref[rows_of[c], hs] = (rg_ref[rows_of[c], hs].astype(F32) * _head_norm(o[c, pair, hh])).astype(BF16)
    for pair in range(SWA_W // LANES):
        cs = pl.ds(pair * LANES, LANES)
        mix_ref[:, pl.ds(RET_W + pair * LANES, LANES)] = (
            sg_ref[:, cs].astype(F32) * so_ref[pair]).astype(BF16)
    for c in range(n_ck):
        yield
        ss = [_dot_nt(mq_ref[rows_of[c], pl.ds(h * MEM_HD, MEM_HD)], mkb_ref[:, pl.ds(h * MEM_HD, MEM_HD)])
              for h in range(MEM_HEADS)]
        ps = [jnp.exp2((s - jnp.max(s, axis=-1, keepdims=True)) * (MEM_SCALE * LOG2_E)) for s in ss]
        for h in range(MEM_HEADS):
            hs = pl.ds(h * MEM_HD, MEM_HD)
            o = _dot(ps[h].astype(BF16), mvb_ref[:, hs]) * (1.0 / jnp.sum(ps[h], axis=-1, keepdims=True))
            mix_ref[rows_of[c], pl.ds(RET_W + SWA_W + h * MEM_HD, MEM_HD)] = (
                mg_ref[rows_of[c], hs].astype(F32) * o).astype(BF16)
    yield
    blocks = [pl.ds(r * OUT_ROWS, OUT_ROWS) for r in range(tile // OUT_ROWS)]
    hout = _dot(mix_ref[blocks[0], :], wout_ref[...])
    for r, rows in enumerate(blocks):
        nxt = _dot(mix_ref[blocks[r + 1], :], wout_ref[...]) if r + 1 < len(blocks) else None
        y_ref[rows, :] = _deepnorm_ln(x_ref[rows, :], hout, gain, bias)
        hout = nxt

    @pl.when(t == pl.num_programs(1) - 1)
    def _():
        state_out_ref[pl.ds(0, LANES), :] = state_ref[0]
        state_out_ref[pl.ds(LANES, LANES), :] = state_ref[1]


def _mix_all_kernel(*refs, tile, n_in, n_out):
    p_in, s_in = refs[:n_in[0]], refs[n_in[0]:n_in[0] + n_in[1]]
    outs = refs[n_in[0] + n_in[1]:]
    p_out, s_out = outs[:n_out[0]], outs[n_out[0]:n_out[0] + n_out[1]]
    scratch = outs[n_out[0] + n_out[1]:]
    stages = [_mix_prompt_kernel(*p_in, *p_out, *scratch, tile=tile), _mix_sample_kernel(*s_in, *s_out)]
    while stages:
        for g in list(stages):
            if next(g, "done") == "done":
                stages.remove(g)


def _mix_all(x2d, pr, swa_o, mk, mv, wout_bf, tabs_p, gain, bias, tile,
             ps, state, ckt, cvt, cmk, cmv, tabs_s):
    nt = SEQ // tile
    assert BATCH * nt == DEC_BATCH
    step = lambda b, t: b * nt + t
    row = lambda w: pl.BlockSpec((tile, w), lambda b, t: (step(b, t), 0))
    const = lambda a: pl.BlockSpec(a.shape, lambda b, t: (0,) * a.ndim)
    mem_p = pl.BlockSpec((None, N_MEM * MEM_HEADS, MEM_HD), lambda b, t: (b, 0, 0))
    st_p = pl.BlockSpec((None, RET_HEADS * RET_DK, RET_DV), lambda b, t: (b, 0, 0))
    p_args = (x2d, pr["rq"], pr["rk"], pr["rv"], pr["rg"], swa_o, pr["sg"], pr["mq"], pr["mg"],
              mk, mv, wout_bf, *tabs_p, gain, bias)
    p_specs = [row(D_MODEL), row(RET_HEADS * RET_DK), row(RET_HEADS * RET_DK), row(RET_W), row(RET_W),
               pl.BlockSpec((SWA_W // LANES, tile, LANES), lambda b, t: (0, step(b, t), 0)),
               row(SWA_W), row(MEM_W), row(MEM_W), mem_p, mem_p, const(wout_bf),
               *[const(a) for a in tabs_p], const(gain), const(bias)]
    n = DEC_PAD
    wb = ckt.shape[3]
    srow = lambda w: pl.BlockSpec((n, w), lambda b, t: (step(b, t), 0))
    st_s = pl.BlockSpec((None, RET_HEADS * RET_DK, RET_DV), lambda b, t: (step(b, t), 0, 0))
    cache_s = pl.BlockSpec((None, SWA_HEADS, SWA_HD, wb), lambda b, t: (step(b, t), 0, 0, 0))
    mem_s = pl.BlockSpec((None, N_MEM * MEM_HEADS, MEM_HD), lambda b, t: (step(b, t), 0, 0))
    s_names = ("rq", "rk", "rv", "rg", "sq", "sk", "sv", "sg", "mq", "mg")
    widths = {name: w for name, w, _, _ in _PROJ_COLS}
    s_args = (*[ps[k] for k in s_names], state, ckt, cvt, cmk, cmv, *tabs_s)
    s_specs = [*[srow(widths[k]) for k in s_names], st_s, cache_s, cache_s, mem_s, mem_s,
               *[const(a) for a in tabs_s]]
    return pl.pallas_call(
        functools.partial(_mix_all_kernel, tile=tile, n_in=(len(p_args), len(s_args)), n_out=(2, 2)),
        grid=(BATCH, nt),
        in_specs=p_specs + s_specs,
        out_specs=[row(D_MODEL), st_p, srow(D_MIX), st_s],
        out_shape=[jax.ShapeDtypeStruct((BATCH * SEQ, D_MODEL), F32),
                   jax.ShapeDtypeStruct((BATCH, RET_HEADS * RET_DK, RET_DV), F32),
                   jax.ShapeDtypeStruct((DEC_BATCH * n, D_MIX), F32),
                   jax.ShapeDtypeStruct((DEC_BATCH, RET_HEADS * RET_DK, RET_DV), F32)],
        scratch_shapes=[pltpu.VMEM((RET_HEADS // 2, LANES, RET_DV), F32),
                        pltpu.VMEM((tile, D_MIX), BF16),
                        pltpu.VMEM((N_MEM, MEM_W), BF16),
                        pltpu.VMEM((N_MEM, MEM_W), BF16)],
        compiler_params=_params(2),
        name="mix_all",
    )(*p_args, *s_args)


def _mix_sample_kernel(rq_ref, rk_ref, rv_ref, rg_ref, sq_ref, sk_ref, sv_ref, sg_ref, mq_ref, mg_ref,
                       state_ref, ckt_ref, cvt_ref, cmk_ref, cmv_ref,
                       dmat_ref, kdec_ref, qdec_ref, gdec_ref,
                       mix_ref, state_out_ref):
    n = DEC_PAD
    q = rq_ref[...]
    k = rk_ref[...]
    kb = k.astype(BF16)
    kd = (k * kdec_ref[...]).astype(BF16)
    qd = q * qdec_ref[...]
    st = state_ref[...]
    stb = st.astype(BF16)
    vb = rv_ref[...].astype(BF16)
    kv = _dot_tn(kd, vb)
    lane_qk = lax.broadcasted_iota(jnp.int32, q.shape, 1) // RET_DK
    ret_s = [_dot_nt(jnp.where(lane_qk == h, q, 0.0).astype(BF16), kb) for h in range(RET_HEADS)]
    ret_cross = [_dot(jnp.where(lane_qk == h, qd, 0.0).astype(BF16), stb) for h in range(RET_HEADS)]
    mem_rows = [pl.ds(h, N_MEM, stride=MEM_HEADS) for h in range(MEM_HEADS)]
    mem_s = [_dot_nt(mq_ref[:, pl.ds(h * MEM_HD, MEM_HD)].astype(BF16), cmk_ref[mem_rows[h], :].astype(BF16))
             for h in range(MEM_HEADS)]

    blk = SWA_BLOCK
    wb = ckt_ref.shape[2]
    rt = 8
    lo = lax.broadcasted_iota(jnp.int32, (n, LANES), 1) < SWA_HD
    lo_t = lax.broadcasted_iota(jnp.int32, (rt, LANES), 1) < SWA_HD
    tok = lambda w: lax.broadcasted_iota(jnp.int32, (rt, w), 0)
    col = lambda w: lax.broadcasted_iota(jnp.int32, (rt, w), 1)
    windows = (
        (wb - blk, jnp.where(col(blk) >= tok(blk), 0.0, NEG), jnp.where(col(blk) <= tok(blk), 0.0, NEG)),
        (wb - 4 * blk, jnp.where(col(4 * blk) % 4 == tok(4 * blk), 0.0, NEG),
         jnp.where(col(blk) == tok(blk), 0.0, NEG)),
        (0, jnp.where(col(wb) % 16 == tok(wb), 0.0, NEG), jnp.where(col(blk) == tok(blk), 0.0, NEG)),
    )
    pad = jnp.zeros((blk - n, LANES), BF16)
    zero_c = jnp.zeros((rt, wb), F32)
    zero_n = jnp.zeros((rt, blk), F32)
    n_pairs = SWA_HEADS // 2
    n_win = len(windows)
    pair_cols = [pl.ds(pair * LANES, LANES) for pair in range(n_pairs)]
    vps = [cvt_ref[2 * pair:2 * pair + 2].reshape(2 * SWA_HD, wb).astype(BF16) for pair in range(n_pairs)]
    v_news = [jnp.concatenate([sv_ref[:, cs].astype(BF16), pad], axis=0) for cs in pair_cols]
    s_alls, sn_alls = [], []
    for pair, cs in enumerate(pair_cols):
        kp = ckt_ref[2 * pair:2 * pair + 2].reshape(2 * SWA_HD, wb).astype(BF16)
        k_new = jnp.concatenate([sk_ref[:, cs].astype(BF16), pad], axis=0)
        q = sq_ref[:, cs]
        qs = jnp.concatenate([jnp.where(lo, q, 0.0), jnp.where(lo, 0.0, q)], axis=0).astype(BF16)
        s_alls.append(_dot(qs, kp))
        sn_alls.append(_dot_nt(qs, k_new))
    yield
    p_rows = [[] for _ in range(n_pairs)]
    pn_rows = [[] for _ in range(n_pairs)]
    stats = [[] for _ in range(n_pairs)]
    for pair in range(n_pairs):
        for hh in range(2):
            s = s_alls[pair][hh * n:hh * n + rt]
            sn = sn_alls[pair][hh * n:hh * n + rt]
            for w0, bias_c, bias_n in windows:
                sc = s[:, w0:] + bias_c
                snb = sn + bias_n
                m = jnp.maximum(jnp.max(sc, axis=-1, keepdims=True), jnp.max(snb, axis=-1, keepdims=True))
                pc = jnp.exp(sc - m)
                pn = jnp.exp(snb - m)
                stats[pair].append((m, jnp.sum(pc, axis=-1, keepdims=True) + jnp.sum(pn, axis=-1, keepdims=True)))
                if w0:
                    pc = jnp.concatenate([jnp.zeros((rt, w0), F32), pc], axis=1)
                p_rows[pair].append(pc)
                pn_rows[pair].append(pn)
            p_rows[pair].append(zero_c)
            pn_rows[pair].append(zero_n)
    ret_sb = [(ret_s[h] * dmat_ref[h]).astype(BF16) for h in range(RET_HEADS)]
    mem_p = [_softmax_rows(s * MEM_SCALE).astype(BF16) for s in mem_s]
    yield
    pvs = [_dot_nt(jnp.concatenate(p_rows[pair], axis=0).astype(BF16), vps[pair])
           + _dot(jnp.concatenate(pn_rows[pair], axis=0).astype(BF16), v_news[pair])
           for pair in range(n_pairs)]
    ret_intra = [_dot(ret_sb[h], vb[:, h * RET_DV:(h + 1) * RET_DV]) for h in range(RET_HEADS)]
    mem_o = [_dot(mem_p[h], cmv_ref[mem_rows[h], :].astype(BF16)) for h in range(MEM_HEADS)]
    yield
    pieces = []
    for pair in range(n_pairs):
        heads = []
        for hh in range(2):
            parts = []
            for i in range(n_win):
                m, l = stats[pair][hh * n_win + i]
                r0 = (hh * (n_win + 1) + i) * rt
                parts.append((pvs[pair][r0:r0 + rt] / l, m, l))
            m_all = jnp.maximum(jnp.maximum(parts[0][1], parts[1][1]), parts[2][1])
            ws = [l * jnp.exp(m - m_all) for (_, m, l) in parts]
            heads.append((ws[0] * parts[0][0] + ws[1] * parts[1][0] + ws[2] * parts[2][0])
                         / (ws[0] + ws[1] + ws[2]))
        pieces.append(jnp.where(lo_t, heads[0], heads[1]))
    swa = jnp.concatenate(pieces, axis=1)
    swa = jnp.concatenate([swa, jnp.zeros((n - rt, SWA_W), F32)], axis=0)
    mix_ref[:, pl.ds(RET_W, SWA_W)] = sg_ref[...] * swa
    for h in range(RET_HEADS):
        hs = pl.ds(h * RET_DV, RET_DV)
        ks = pl.ds(h * RET_DK, RET_DK)
        mix_ref[:, hs] = rg_ref[:, hs] * _head_norm(ret_intra[h] + ret_cross[h])
        state_out_ref[ks, :] = (gdec_ref[ks, :] * st[h * RET_DK:(h + 1) * RET_DK, :]
                                + kv[h * RET_DK:(h + 1) * RET_DK, h * RET_DV:(h + 1) * RET_DV])
    for h in range(MEM_HEADS):
        hs = pl.ds(h * MEM_HD, MEM_HD)
        mix_ref[:, pl.ds(RET_W + SWA_W + h * MEM_HD, MEM_HD)] = mg_ref[:, hs] * mem_o[h]


def _finish_kernel(x_ref, mix_ref, wout_ref, gain_ref, bias_ref, y_ref):
    hout = _dot(mix_ref[...].astype(BF16), wout_ref[...])
    y_ref[...] = _deepnorm_ln(x_ref[...], hout, gain_ref[...], bias_ref[...])


def _finish(x2d, mix, wout_bf, gain, bias):
    n = x2d.shape[0]
    tile = 256
    const2 = lambda a: pl.BlockSpec(a.shape, lambda i: (0, 0))
    return pl.pallas_call(
        _finish_kernel,
        grid=(n // tile,),
        in_specs=[pl.BlockSpec((tile, D_MODEL), lambda i: (i, 0)),
                  pl.BlockSpec((tile, D_MIX), lambda i: (i, 0)),
                  const2(wout_bf), const2(gain), const2(bias)],
        out_specs=pl.BlockSpec((tile, D_MODEL), lambda i: (i, 0)),
        out_shape=jax.ShapeDtypeStruct((n, D_MODEL), F32),
        compiler_params=_params(1),
        name="finish",
    )(x2d, mix, wout_bf, gain, bias)


def _rope_tables(pos):
    half = SWA_HD // 2
    inv = ROPE_THETA ** (-np.arange(half, dtype=np.float64) * 2.0 / SWA_HD)
    ang = pos.astype(np.float64)[:, None] * inv[None, :]
    cos, sin = np.cos(ang), np.sin(ang)
    reps = LANES // SWA_HD
    return (jnp.asarray(np.tile(np.concatenate([cos, cos], axis=1), (1, reps)), F32),
            jnp.asarray(np.tile(np.concatenate([-sin, sin], axis=1), (1, reps)), F32))


def _retention_tables(chunk, rows):
    lg = np.log1p(-np.exp2(-5.0 - np.arange(RET_HEADS, dtype=np.float64)))
    idx = np.arange(rows, dtype=np.float64)
    live = idx < chunk
    rel = idx[:, None] - idx[None, :]
    ok = (rel >= 0) & live[:, None] & live[None, :]
    dmat = np.where(ok[None], np.exp(np.maximum(rel, 0.0)[None] * lg[:, None, None]), 0.0)
    kdec = np.where(live[:, None], np.exp((chunk - 1.0 - idx)[:, None] * lg[None, :]), 0.0)
    qdec = np.where(live[:, None], np.exp((idx + 1.0)[:, None] * lg[None, :]), 0.0)
    g = np.exp(chunk * lg)
    kdec = np.repeat(kdec, RET_DK, axis=1)
    qdec = np.repeat(qdec, RET_DK, axis=1)
    gdec = np.broadcast_to(np.repeat(g, RET_DK)[:, None], (RET_HEADS * RET_DK, RET_DV))
    return tuple(jnp.asarray(t, F32) for t in (dmat, kdec, qdec, gdec))


def kernel(x_prompt, x_sample, state_ret, cache_swa_k, cache_swa_v, cache_mem_k, cache_mem_v,
           mem_prompt, w_in, w_mem_kv, w_out, ln_gain, ln_bias):
    depth = w_in.shape[0]
    assert depth == 1
    win_bf = w_in[0].astype(BF16)
    wmem_bf = w_mem_kv[0].astype(BF16)
    wout_bf = w_out[0].astype(BF16)
    gain = ln_gain[0].reshape(1, D_MODEL)
    bias = ln_bias[0].reshape(1, D_MODEL)

    xp = x_prompt.reshape(BATCH * SEQ, D_MODEL)
    cos_p, sin_p = _rope_tables(np.arange(SEQ))
    p_outs = (("rq", "rows", F32), ("rk", "rows", F32), ("rv", "rows", BF16), ("rg", "rows", BF16),
              ("sq", "perm", F32), ("sk", "cols", F32), ("sk", "perm", F32), ("sv", "cols", F32),
              ("sv", "perm", F32), ("sg", "rows", BF16), ("mq", "rows", BF16), ("mg", "rows", BF16))
    p_keys = ("rq", "rk", "rv", "rg", "sq4", "sk", "sk4", "sv", "sv4", "sg", "mq", "mg")
    pr = dict(zip(p_keys, _project(xp, win_bf, cos_p, sin_p, PERM_TILE, p_outs)))
    mk, mv = _memkv(mem_prompt.reshape(BATCH * N_MEM, D_MODEL), wmem_bf)
    slab = lambda a: a.reshape(SWA_W // LANES, BATCH, SEQ // PERM_TILE * PERM_ROWS, LANES)
    swa_o = _swa_prompt(slab(pr["sq4"]), slab(pr["sk4"]), slab(pr["sv4"]))
    swa_o = swa_o.reshape(SWA_W // LANES, BATCH * SEQ, LANES)
    tabs_p = _retention_tables(RET_CHUNK, RET_CHUNK)

    xs = jnp.pad(x_sample, ((0, 0), (0, DEC_PAD - DEC_SEQ), (0, 0))).reshape(DEC_BATCH * DEC_PAD, D_MODEL)
    pos_s = PAST_LEN + np.arange(DEC_BATCH * DEC_PAD) % DEC_PAD
    cos_s, sin_s = _rope_tables(pos_s)
    names = [c[0] for c in _PROJ_COLS]
    ps = dict(zip(names, _project(xs, win_bf, cos_s, sin_s, DEC_BATCH * DEC_PAD,
                                  [(k, "rows", F32) for k in names])))
    tabs_s = _retention_tables(DEC_SEQ, DEC_PAD)
    yp, ret_p, mix_s, ret_s = _mix_all(
        xp, pr, swa_o, mk, mv, wout_bf, tabs_p, gain, bias, 512,
        ps, state_ret[0].reshape(DEC_BATCH, RET_HEADS * RET_DK, RET_DV),
        cache_swa_k[0].transpose(0, 2, 3, 1), cache_swa_v[0].transpose(0, 2, 3, 1),
        cache_mem_k[0].reshape(DEC_BATCH, N_MEM * MEM_HEADS, MEM_HD),
        cache_mem_v[0].reshape(DEC_BATCH, N_MEM * MEM_HEADS, MEM_HD),
        tabs_s)
    ys = _finish(xs, mix_s, wout_bf, gain, bias)

    take = lambda a, w: a.reshape(DEC_BATCH, DEC_PAD, w)[:, :DEC_SEQ]
    swa_rows = lambda a: a.reshape(BATCH, SWA_HEADS, SWA_HD, SEQ).transpose(0, 3, 1, 2)[None]
    return (
        yp.reshape(BATCH, SEQ, D_MODEL),
        take(ys, D_MODEL),
        ret_p.reshape(1, BATCH, RET_HEADS, RET_DK, RET_DV),
        ret_s.reshape(1, DEC_BATCH, RET_HEADS, RET_DK, RET_DV),
        swa_rows(pr["sk"]),
        swa_rows(pr["sv"]),
        take(ps["sk"], SWA_W).reshape(1, DEC_BATCH, DEC_SEQ, SWA_HEADS, SWA_HD),
        take(ps["sv"], SWA_W).reshape(1, DEC_BATCH, DEC_SEQ, SWA_HEADS, SWA_HD),
        mk.reshape(1, BATCH, N_MEM, MEM_HEADS, MEM_HD),
        mv.reshape(1, BATCH, N_MEM, MEM_HEADS, MEM_HD),
    )
```

```python
import functools

import jax
import jax.numpy as jnp
import numpy as np
from jax import lax
from jax.experimental import pallas as pl
from jax.experimental.pallas import tpu as pltpu

F32 = jnp.float32
BF16 = jnp.bfloat16

D_MODEL = 1024
BATCH = 8
SEQ = 2048
DEC_BATCH = 32
DEC_SEQ = 4
PAST_LEN = 8192
N_MEM = 256
MEM_HEADS = 4
MEM_HD = 128
RET_HEADS = 4
RET_DK = 64
RET_DV = 128
RET_CHUNK = 128
SWA_HEADS = 8
SWA_HD = 64
SWA_DILATIONS = (1, 4, 16)
SWA_STEPS = 128
SWA_BLOCK = 128
ROPE_THETA = 10000.0
LN_EPS = 1e-5
GN_EPS = 1e-5
RET_W = RET_HEADS * RET_DV
SWA_W = SWA_HEADS * SWA_HD
MEM_W = MEM_HEADS * MEM_HD
D_MIX = RET_W + SWA_W + MEM_W
DEEPNORM_ALPHA = 2.0 ** 0.25
MEM_SCALE = MEM_HD ** -0.5
LOG2_E = 1.4426950408889634
QK_SCALE = 0.125

LANES = 128
DEC_PAD = 16
PERM_TILE = 512
PERM_PITCH = 40
PERM_ROWS = 16 * PERM_PITCH
OUT_ROWS = 256
SWA_PAR = 2
MEMKV_BATCHES = 4
VMEM_LIMIT = 56 * 1024 * 1024
NEG = -1e30

_PROJ_COLS = (
    ("rq", RET_HEADS * RET_DK, "rope", 1.0),
    ("rk", RET_HEADS * RET_DK, "rope", QK_SCALE),
    ("rv", RET_W, "id", 1.0),
    ("rg", RET_W, "silu", 1.0),
    ("sq", SWA_W, "rope", QK_SCALE),
    ("sk", SWA_W, "rope", 1.0),
    ("sv", SWA_W, "id", 1.0),
    ("sg", SWA_W, "silu", 1.0),
    ("mq", MEM_W, "id", 1.0),
    ("mg", MEM_W, "silu", 1.0),
)


def _dot(a, b):
    return jnp.dot(a, b, preferred_element_type=F32)


def _dot_nt(a, b):
    return lax.dot_general(a, b, (((1,), (1,)), ((), ())), preferred_element_type=F32)


def _dot_tn(a, b):
    return lax.dot_general(a, b, (((0,), (0,)), ((), ())), preferred_element_type=F32)


def _params(n_axes):
    return pltpu.CompilerParams(dimension_semantics=("arbitrary",) * n_axes,
                                vmem_limit_bytes=VMEM_LIMIT)


def _run_interleaved(stages):
    stages = list(stages)
    while stages:
        for g in list(stages):
            if next(g, "done") == "done":
                stages.remove(g)


def _proj_kernel(*refs, dests, n_decode_in=0):
    n_out = len(dests)
    p_in, s_in = refs[:4], refs[4:4 + n_decode_in]
    p_out, s_out = refs[4 + n_decode_in:4 + n_decode_in + n_out], refs[4 + n_decode_in + n_out:]
    stages = [_proj_stages(*p_in, *p_out, dests=dests)]
    if n_decode_in:
        stages.append(_mix_sample_kernel(*s_in, *s_out))
    _run_interleaved(stages)


def _proj_stages(x_ref, w_ref, cos_ref, sin_ref, *out_refs, dests):
    xb = x_ref[...].astype(BF16)
    cos = cos_ref[...]
    sin = sin_ref[...]
    lane = lax.broadcasted_iota(jnp.int32, cos.shape, 1)
    first_half = (lane % 64) < 32
    col = 0
    for name, width, kind, scale in _PROJ_COLS:
        targets = [(o_ref, layout) for o_ref, (dname, layout) in zip(out_refs, dests) if dname == name]
        for c in range(0, width, 2 * LANES):
            h2 = _dot(xb, w_ref[:, col + c:col + c + 2 * LANES])
            for half in range(2):
                h = h2[:, half * LANES:(half + 1) * LANES]
                if kind == "rope":
                    swapped = jnp.where(first_half, pltpu.roll(h, 96, 1), pltpu.roll(h, 32, 1))
                    h = h * cos + swapped * sin
                    if scale != 1.0:
                        h = h * scale
                elif kind == "silu":
                    h = h * (1.0 / (1.0 + jnp.exp(-h)))
                lo = c + half * LANES
                for o_ref, layout in targets:
                    if layout == "rows":
                        o_ref[:, lo:lo + LANES] = h.astype(o_ref.dtype)
                    elif layout == "perm":
                        slab = lo // LANES
                        for g in range(h.shape[0] // 8):
                            base = (g % 2) * 8 * PERM_PITCH + g // 2
                            o_ref[slab, pl.ds(base, 8, stride=PERM_PITCH), :] = (
                                h[8 * g:8 * g + 8].astype(o_ref.dtype))
                        n_live = h.shape[0] // 16
                        for cls in range(16):
                            o_ref[slab, pl.ds(cls * PERM_PITCH + n_live, PERM_PITCH - n_live), :] = (
                                jnp.zeros((PERM_PITCH - n_live, LANES), o_ref.dtype))
                    else:
                        o_ref[lo:lo + LANES, :] = h.T.astype(o_ref.dtype)
        col += width
        yield


def _project(x2d, w_bf, cos_t, sin_t, tile, outs, decode=None):
    n = x2d.shape[0]
    seq = cos_t.shape[0]
    n_tab = seq // tile
    d_in = w_bf.shape[1]
    widths = {name: w for name, w, _, _ in _PROJ_COLS}
    out_shape, out_specs = [], []
    for name, layout, dt in outs:
        w = widths[name]
        if layout == "rows":
            out_shape.append(jax.ShapeDtypeStruct((n, w), dt))
            out_specs.append(pl.BlockSpec((tile, w), lambda i: (i, 0)))
        elif layout == "perm":
            assert tile == PERM_TILE
            out_shape.append(jax.ShapeDtypeStruct((w // LANES, n // tile * PERM_ROWS, LANES), dt))
            out_specs.append(pl.BlockSpec((w // LANES, PERM_ROWS, LANES), lambda i: (0, i, 0)))
        else:
            out_shape.append(jax.ShapeDtypeStruct((n // seq, w, seq), dt))
            out_specs.append(pl.BlockSpec((None, w, tile), lambda i: (i // n_tab, 0, i % n_tab)))
    d_args, d_in_specs, d_out_specs, d_out_shape = decode if decode else ((), [], [], [])
    return pl.pallas_call(
        functools.partial(_proj_kernel, dests=tuple((name, layout) for name, layout, _ in outs),
                          n_decode_in=len(d_args)),
        grid=(n // tile,),
        in_specs=[
            pl.BlockSpec((tile, D_MODEL), lambda i: (i, 0)),
            pl.BlockSpec((D_MODEL, d_in), lambda i: (0, 0)),
            pl.BlockSpec((tile, LANES), lambda i: (i % n_tab, 0)),
            pl.BlockSpec((tile, LANES), lambda i: (i % n_tab, 0)),
        ] + list(d_in_specs),
        out_specs=out_specs + list(d_out_specs),
        out_shape=out_shape + list(d_out_shape),
        compiler_params=_params(1),
        name="proj_decode" if decode else "proj",
    )(x2d, w_bf, cos_t, sin_t, *d_args)


def _decode_operands(ps, state, ckt, cvt, cmk, cmv, tabs):
    n = DEC_PAD
    wb = ckt.shape[3]
    const = lambda a: pl.BlockSpec(a.shape, lambda i: (0,) * a.ndim)
    srow = lambda w: pl.BlockSpec((n, w), lambda i: (i, 0))
    st_s = pl.BlockSpec((None, RET_HEADS * RET_DK, RET_DV), lambda i: (i, 0, 0))
    cache_s = pl.BlockSpec((None, SWA_HEADS, SWA_HD, wb), lambda i: (i, 0, 0, 0))
    mem_s = pl.BlockSpec((None, N_MEM * MEM_HEADS, MEM_HD), lambda i: (i, 0, 0))
    s_names = ("rq", "rk", "rv", "rg", "sq", "sk", "sv", "sg", "mq", "mg")
    widths = {name: w for name, w, _, _ in _PROJ_COLS}
    args = (*[ps[k] for k in s_names], state, ckt, cvt, cmk, cmv, *tabs)
    in_specs = [*[srow(widths[k]) for k in s_names], st_s, cache_s, cache_s, mem_s, mem_s,
                *[const(a) for a in tabs]]
    out_shape = [jax.ShapeDtypeStruct((DEC_BATCH * n, D_MIX), F32),
                 jax.ShapeDtypeStruct((DEC_BATCH, RET_HEADS * RET_DK, RET_DV), F32)]
    return args, in_specs, [srow(D_MIX), st_s], out_shape


def _memkv_kernel(m_ref, w_ref, mk_ref, mv_ref):
    for b in range(m_ref.shape[0] // N_MEM):
        mb = m_ref[pl.ds(b * N_MEM, N_MEM), :].astype(BF16)
        for h in range(MEM_HEADS):
            rows = pl.ds(h, N_MEM, stride=MEM_HEADS)
            mk_ref[b, rows, :] = _dot(mb, w_ref[:, h * MEM_HD:(h + 1) * MEM_HD])
            mv_ref[b, rows, :] = _dot(mb, w_ref[:, MEM_W + h * MEM_HD:MEM_W + (h + 1) * MEM_HD])


def _memkv(mem2d, w_bf):
    n = mem2d.shape[0] // N_MEM
    per = MEMKV_BATCHES
    out = pl.BlockSpec((per, N_MEM * MEM_HEADS, MEM_HD), lambda i: (i, 0, 0))
    return pl.pallas_call(
        _memkv_kernel,
        grid=(n // per,),
        in_specs=[pl.BlockSpec((per * N_MEM, D_MODEL), lambda i: (i, 0)),
                  pl.BlockSpec((D_MODEL, 2 * MEM_W), lambda i: (0, 0))],
        out_specs=[out, out],
        out_shape=[jax.ShapeDtypeStruct((n, N_MEM * MEM_HEADS, MEM_HD), F32)] * 2,
        compiler_params=_params(1),
        name="memkv",
    )(mem2d, w_bf)


def _swa_prompt_kernel(q_ref, k_ref, v_ref, o_ref, a_ref, m_ref, l_ref):
    blk = SWA_BLOCK
    pt = PERM_TILE
    grp = pt // 16
    lo = lax.broadcasted_iota(jnp.int32, (blk, blk), 1) < SWA_HD

    def biases(seq_of):
        rq = seq_of(lax.broadcasted_iota(jnp.int32, (blk, 2 * blk), 0))
        c2 = lax.broadcasted_iota(jnp.int32, (blk, 2 * blk), 1)
        rel = blk + rq - (seq_of(c2 % blk) + blk * (c2 // blk))
        prev = jnp.where((rel >= 0) & (rel <= SWA_STEPS), 0.0, NEG).astype(F32)
        r1 = seq_of(lax.broadcasted_iota(jnp.int32, (blk, blk), 0))
        c1 = seq_of(lax.broadcasted_iota(jnp.int32, (blk, blk), 1))
        return prev, jnp.where(r1 >= c1, 0.0, NEG).astype(F32)

    def gather(ref, pair, starts, n):
        return jnp.concatenate([ref[pair, pl.ds(s, n), :] for s in starts], axis=0)

    def aligned(x):
        return x if isinstance(x, int) else pl.multiple_of(x, 8)

    def blocks(jobs, n, first_pattern, last_pattern):
        n_pairs = SWA_HEADS // 2
        in_at = lambda c: aligned(c[0] * PERM_ROWS + c[1] * PERM_PITCH + c[2])
        acc_starts, q_starts, k_starts = [], [], []
        for chunks, prev_chunks, _, _ in jobs:
            acc_starts.append([aligned(c[0] * pt + c[1] * grp + c[2]) for c in chunks])
            q_starts.append([in_at(c) for c in chunks])
            k_starts.append(([] if prev_chunks is None else [in_at(c) for c in prev_chunks]) + q_starts[-1])
        items = [(j, pair) for j in range(len(jobs)) for pair in range(n_pairs)]
        olds = {}
        if not first_pattern:
            for j, pair in items:
                olds[j, pair] = (gather(m_ref, pair, acc_starts[j], n), gather(l_ref, pair, acc_starts[j], n),
                                 gather(a_ref, pair, acc_starts[j], n))
        vbs = {(j, pair): gather(v_ref, pair, k_starts[j], n).astype(BF16) for j, pair in items}
        scores = {}
        for j, pair in items:
            q = gather(q_ref, pair, q_starts[j], n)
            kb = gather(k_ref, pair, k_starts[j], n).astype(BF16)
            for hh in range(2):
                qm = jnp.where(lo if hh == 0 else ~lo, q, 0.0).astype(BF16)
                scores[j, pair, hh] = _dot_nt(qm, kb) + jobs[j][2]
        ms = {key: jnp.max(s, axis=-1, keepdims=True) for key, s in scores.items()}
        ps = {key: jnp.exp(s - ms[key]) for key, s in scores.items()}
        def v_ext(v):
            lane = lax.broadcasted_iota(jnp.int32, v.shape, 1)
            own = lane < SWA_HD
            zero = jnp.zeros((), BF16)
            ones_lo = jnp.where(lane < SWA_HD, 1.0, 0.0).astype(BF16)
            ones_hi = jnp.where(lane < SWA_HD, 0.0, 1.0).astype(BF16)
            return jnp.concatenate(
                [jnp.concatenate([jnp.where(own, v, zero), ones_lo], axis=1),
                 jnp.concatenate([jnp.where(own, zero, v), ones_hi], axis=1)], axis=0)

        pvs = {(j, pair): _dot(jnp.concatenate([ps[j, pair, 0].astype(BF16), ps[j, pair, 1].astype(BF16)], axis=1),
                               v_ext(vbs[j, pair])) for j, pair in items}
        news = {}
        for j, pair in items:
            m_g = jnp.where(lo, ms[j, pair, 0], ms[j, pair, 1])
            a_g = pvs[j, pair][:, :LANES]
            l_g = pvs[j, pair][:, LANES:]
            if first_pattern:
                news[j, pair] = (m_g, l_g, a_g)
            else:
                m_old, l_old, a_old = olds[j, pair]
                m_new = jnp.maximum(m_old, m_g)
                w_old = jnp.exp(m_old - m_new)
                w_g = jnp.exp(m_g - m_new)
                news[j, pair] = (m_new, w_old * l_old + w_g * l_g, w_old * a_old + w_g * a_g)
        for (j, pair), (m_new, l_new, a_new) in news.items():
            starts = acc_starts[j]
            if last_pattern:
                o = a_new / l_new
                for i in range(len(starts)):
                    o_ref[pair, pl.ds(jobs[j][3] + i, n, stride=len(starts)), :] = o[i * n:(i + 1) * n]
            else:
                for i, s in enumerate(starts):
                    m_ref[pair, pl.ds(s, n), :] = m_new[i * n:(i + 1) * n]
                    l_ref[pair, pl.ds(s, n), :] = l_new[i * n:(i + 1) * n]
                    a_ref[pair, pl.ds(s, n), :] = a_new[i * n:(i + 1) * n]

    par = SWA_PAR
    _, bias16 = biases(lambda i: i)

    par16 = 2 * par

    def class16(r0, carry):
        blocks([([(t, r0 + k * (16 // par16), 0) for t in range(SEQ // pt)], None, bias16, None)
                for k in range(par16)], grp, True, False)
        return carry

    lax.fori_loop(0, 16 // par16, class16, 0)

    bias4_prev, bias4_own = biases(lambda i: 4 * (i % grp) + i // grp)

    def class4(r0, carry):
        chunks_of = lambda r, tile: [(tile, 4 * a + r, 0) for a in range(4)]
        classes = [r0 + k * (4 // par) for k in range(par)]
        blocks([(chunks_of(r, 0), None, bias4_own, None) for r in classes], grp, False, False)

        def per_block(b, c2):
            blocks([(chunks_of(r, b), chunks_of(r, b - 1), bias4_prev, None) for r in classes],
                   grp, False, False)
            return c2

        lax.fori_loop(1, SEQ // pt, per_block, 0)
        return carry

    lax.fori_loop(0, 4 // par, class4, 0)

    bias1_prev, bias1_own = biases(lambda i: 16 * (i % 8) + i // 8)
    per_tile = pt // blk
    chunks1 = lambda c: [(c // per_tile, i, (c % per_tile) * 8) for i in range(16)]
    job1 = lambda c: (chunks1(c), chunks1(c - 1), bias1_prev, aligned(c * blk))
    blocks([(chunks1(0), None, bias1_own, 0)] + [job1(c) for c in range(1, par)], 8, False, True)

    def block1(g, carry):
        blocks([job1(g * par + k) for k in range(par)], 8, False, True)
        return carry

    lax.fori_loop(1, SEQ // blk // par, block1, 0)


def _swa_prompt(sq, sk, sv):
    slabs = SWA_W // LANES
    spec = pl.BlockSpec((slabs, None, SEQ, LANES), lambda b: (0, b, 0, 0))
    in_spec = pl.BlockSpec((slabs, None, sq.shape[2], LANES), lambda b: (0, b, 0, 0))
    return pl.pallas_call(
        _swa_prompt_kernel,
        grid=(BATCH,),
        in_specs=[in_spec, in_spec, in_spec],
        out_specs=spec,
        out_shape=jax.ShapeDtypeStruct((slabs, BATCH, SEQ, LANES), F32),
        scratch_shapes=[pltpu.VMEM((slabs, SEQ, LANES), F32)] * 3,
        compiler_params=_params(1),
        name="swa_prompt",
    )(sq, sk, sv)


def _head_norm(o):
    mu = jnp.mean(o, axis=-1, keepdims=True)
    d = o - mu
    var = jnp.mean(d * d, axis=-1, keepdims=True)
    return d * lax.rsqrt(var + GN_EPS)


def _deepnorm_ln(x, h, gain, bias):
    z = DEEPNORM_ALPHA * x + h
    mu = jnp.mean(z, axis=-1, keepdims=True)
    d = z - mu
    var = jnp.mean(d * d, axis=-1, keepdims=True)
    return d * lax.rsqrt(var + LN_EPS) * gain + bias


def _softmax_rows(s):
    m = jnp.max(s, axis=-1, keepdims=True)
    p = jnp.exp(s - m)
    return p * (1.0 / jnp.sum(p, axis=-1, keepdims=True))


def _mix_prompt_kernel(x_ref, rq_ref, rk_ref, rv_ref, rg_ref, so_ref, sg_ref, mq_ref, mg_ref,
                       mk_ref, mv_ref, wout_ref, dmat_ref, kdec_ref, qdec_ref, gdec_ref,
                       gain_ref, bias_ref, y_ref, state_out_ref,
                       state_ref, mix_ref, mkb_ref, mvb_ref, *, tile):
    t = pl.program_id(1)

    @pl.when(t == 0)
    def _():
        state_ref[...] = jnp.zeros_like(state_ref)
        for h in range(MEM_HEADS):
            rows = pl.ds(h, N_MEM, stride=MEM_HEADS)
            mkb_ref[:, pl.ds(h * MEM_HD, MEM_HD)] = mk_ref[rows, :].astype(BF16)
            mvb_ref[:, pl.ds(h * MEM_HD, MEM_HD)] = mv_ref[rows, :].astype(BF16)

    ck = RET_CHUNK
    lane = lax.broadcasted_iota(jnp.int32, (ck, LANES), 1)
    lo = lane < RET_DK
    top = lax.broadcasted_iota(jnp.int32, (LANES, LANES), 0) < RET_DK
    gain = gain_ref[...]
    bias = bias_ref[...]

    n_ck = tile // ck
    n_pairs = RET_HEADS // 2
    rows_of = [pl.ds(c * ck, ck) for c in range(n_ck)]
    items = [(c, pair, hh) for c in range(n_ck) for pair in range(n_pairs) for hh in range(2)]
    hs_of = lambda pair, hh: pl.ds((2 * pair + hh) * RET_DV, RET_DV)
    sel = lambda x, hh: jnp.where(lo if hh == 0 else ~lo, x, 0.0).astype(BF16)
    pairs = [(c, pair) for c in range(n_ck) for pair in range(n_pairs)]
    qk = {}
    for c, pair in pairs:
        cs = pl.ds(pair * LANES, LANES)
        q = rq_ref[rows_of[c], cs]
        k = rk_ref[rows_of[c], cs]
        qk[c, pair] = (q.astype(BF16), jnp.concatenate([sel(k, 0), sel(k, 1)], axis=0),
                       (k * kdec_ref[:, cs]).astype(BF16), q * qdec_ref[:, cs])
    v_pair = lambda c, pair: rv_ref[rows_of[c], pl.ds(pair * 2 * RET_DV, 2 * RET_DV)]
    s2 = {key: _dot_nt(qk[key][0], qk[key][1]) for key in pairs}
    kv2 = {key: _dot_tn(qk[key][2], v_pair(*key)) for key in pairs}
    yield
    state = {(0, pair): state_ref[pair] for pair in range(n_pairs)}
    for c, pair in pairs:
        state[c + 1, pair] = (gdec_ref[pl.ds(pair * LANES, LANES), :] * state[c, pair]
                              + jnp.where(top, kv2[c, pair][:, :RET_DV], kv2[c, pair][:, RET_DV:]))
    for pair in range(n_pairs):
        state_ref[pair] = state[n_ck, pair]
    o = {}
    for c, pair, hh in items:
        sh = (s2[c, pair][:, hh * ck:(hh + 1) * ck] * dmat_ref[2 * pair + hh]).astype(BF16)
        lhs = jnp.concatenate([sh, sel(qk[c, pair][3], hh)], axis=1)
        rhs = jnp.concatenate([rv_ref[rows_of[c], hs_of(pair, hh)], state[c, pair].astype(BF16)], axis=0)
        o[c, pair, hh] = _dot(lhs, rhs)
    yield
    for c, pair, hh in items:
        hs = hs_of(pair, hh)
        mix_ref[rows_of[c], hs] = (rg_ref[rows_of[c], hs].astype(F32) * _head_norm(o[c, pair, hh])).astype(BF16)
    for pair in range(SWA_W // LANES):
        cs = pl.ds(pair * LANES, LANES)
        mix_ref[:, pl.ds(RET_W + pair * LANES, LANES)] = (
            sg_ref[:, cs].astype(F32) * so_ref[pair]).astype(BF16)
    for c in range(n_ck):
        yield
        ss = [_dot_nt(mq_ref[rows_of[c], pl.ds(h * MEM_HD, MEM_HD)], mkb_ref[:, pl.ds(h * MEM_HD, MEM_HD)])
              for h in range(MEM_HEADS)]
        ps = [jnp.exp2((s - jnp.max(s, axis=-1, keepdims=True)) * (MEM_SCALE * LOG2_E)) for s in ss]
        for h in range(MEM_HEADS):
            hs = pl.ds(h * MEM_HD, MEM_HD)
            o = _dot(ps[h].astype(BF16), mvb_ref[:, hs]) * (1.0 / jnp.sum(ps[h], axis=-1, keepdims=True))
            mix_ref[rows_of[c], pl.ds(RET_W + SWA_W + h * MEM_HD, MEM_HD)] = (
                mg_ref[rows_of[c], hs].astype(F32) * o).astype(BF16)
    yield
    blocks = [pl.ds(r * OUT_ROWS, OUT_ROWS) for r in range(tile // OUT_ROWS)]
    hout = _dot(mix_ref[blocks[0], :], wout_ref[...])
    for r, rows in enumerate(blocks):
        nxt = _dot(mix_ref[blocks[r + 1], :], wout_ref[...]) if r + 1 < len(blocks) else None
        y_ref[rows, :] = _deepnorm_ln(x_ref[rows, :], hout, gain, bias)
        hout = nxt

    @pl.when(t == pl.num_programs(1) - 1)
    def _():
        state_out_ref[pl.ds(0, LANES), :] = state_ref[0]
        state_out_ref[pl.ds(LANES, LANES), :] = state_ref[1]


def _mix_prompt_body(*refs, tile):
    _run_interleaved([_mix_prompt_kernel(*refs, tile=tile)])


def _mix_prompt(x2d, pr, swa_o, mk, mv, wout_bf, tabs_p, gain, bias, tile):
    nt = SEQ // tile
    step = lambda b, t: b * nt + t
    row = lambda w: pl.BlockSpec((tile, w), lambda b, t: (step(b, t), 0))
    const = lambda a: pl.BlockSpec(a.shape, lambda b, t: (0,) * a.ndim)
    mem_p = pl.BlockSpec((None, N_MEM * MEM_HEADS, MEM_HD), lambda b, t: (b, 0, 0))
    st_p = pl.BlockSpec((None, RET_HEADS * RET_DK, RET_DV), lambda b, t: (b, 0, 0))
    p_args = (x2d, pr["rq"], pr["rk"], pr["rv"], pr["rg"], swa_o, pr["sg"], pr["mq"], pr["mg"],
              mk, mv, wout_bf, *tabs_p, gain, bias)
    p_specs = [row(D_MODEL), row(RET_HEADS * RET_DK), row(RET_HEADS * RET_DK), row(RET_W), row(RET_W),
               pl.BlockSpec((SWA_W // LANES, tile, LANES), lambda b, t: (0, step(b, t), 0)),
               row(SWA_W), row(MEM_W), row(MEM_W), mem_p, mem_p, const(wout_bf),
               *[const(a) for a in tabs_p], const(gain), const(bias)]
    return pl.pallas_call(
        functools.partial(_mix_prompt_body, tile=tile),
        grid=(BATCH, nt),
        in_specs=p_specs,
        out_specs=[row(D_MODEL), st_p],
        out_shape=[jax.ShapeDtypeStruct((BATCH * SEQ, D_MODEL), F32),
                   jax.ShapeDtypeStruct((BATCH, RET_HEADS * RET_DK, RET_DV), F32)],
        scratch_shapes=[pltpu.VMEM((RET_HEADS // 2, LANES, RET_DV), F32),
                        pltpu.VMEM((tile, D_MIX), BF16),
                        pltpu.VMEM((N_MEM, MEM_W), BF16),
                        pltpu.VMEM((N_MEM, MEM_W), BF16)],
        compiler_params=_params(2),
        name="mix_prompt",
    )(*p_args)


def _mix_sample_kernel(rq_ref, rk_ref, rv_ref, rg_ref, sq_ref, sk_ref, sv_ref, sg_ref, mq_ref, mg_ref,
                       state_ref, ckt_ref, cvt_ref, cmk_ref, cmv_ref,
                       dmat_ref, kdec_ref, qdec_ref, gdec_ref,
                       mix_ref, state_out_ref):
    n = DEC_PAD
    q = rq_ref[...]
    k = rk_ref[...]
    kb = k.astype(BF16)
    kd = (k * kdec_ref[...]).astype(BF16)
    qd = q * qdec_ref[...]
    st = state_ref[...]
    stb = st.astype(BF16)
    vb = rv_ref[...].astype(BF16)
    kv = _dot_tn(kd, vb)
    lane_qk = lax.broadcasted_iota(jnp.int32, q.shape, 1) // RET_DK
    ret_s = [_dot_nt(jnp.where(lane_qk == h, q, 0.0).astype(BF16), kb) for h in range(RET_HEADS)]
    ret_cross = [_dot(jnp.where(lane_qk == h, qd, 0.0).astype(BF16), stb) for h in range(RET_HEADS)]
    mem_rows = [pl.ds(h, N_MEM, stride=MEM_HEADS) for h in range(MEM_HEADS)]
    mem_s = [_dot_nt(mq_ref[:, pl.ds(h * MEM_HD, MEM_HD)].astype(BF16), cmk_ref[mem_rows[h], :].astype(BF16))
             for h in range(MEM_HEADS)]

    blk = SWA_BLOCK
    wb = ckt_ref.shape[2]
    rt = 8
    lo = lax.broadcasted_iota(jnp.int32, (n, LANES), 1) < SWA_HD
    lo_t = lax.broadcasted_iota(jnp.int32, (rt, LANES), 1) < SWA_HD
    tok = lambda w: lax.broadcasted_iota(jnp.int32, (rt, w), 0)
    col = lambda w: lax.broadcasted_iota(jnp.int32, (rt, w), 1)
    windows = (
        (wb - blk, jnp.where(col(blk) >= tok(blk), 0.0, NEG), jnp.where(col(blk) <= tok(blk), 0.0, NEG)),
        (wb - 4 * blk, jnp.where(col(4 * blk) % 4 == tok(4 * blk), 0.0, NEG),
         jnp.where(col(blk) == tok(blk), 0.0, NEG)),
        (0, jnp.where(col(wb) % 16 == tok(wb), 0.0, NEG), jnp.where(col(blk) == tok(blk), 0.0, NEG)),
    )
    pad = jnp.zeros((blk - n, LANES), BF16)
    zero_c = jnp.zeros((rt, wb), F32)
    zero_n = jnp.zeros((rt, blk), F32)
    n_pairs = SWA_HEADS // 2
    n_win = len(windows)
    pair_cols = [pl.ds(pair * LANES, LANES) for pair in range(n_pairs)]
    vps = [cvt_ref[2 * pair:2 * pair + 2].reshape(2 * SWA_HD, wb).astype(BF16) for pair in range(n_pairs)]
    v_news = [jnp.concatenate([sv_ref[:, cs].astype(BF16), pad], axis=0) for cs in pair_cols]
    s_alls, sn_alls = [], []
    for pair, cs in enumerate(pair_cols):
        kp = ckt_ref[2 * pair:2 * pair + 2].reshape(2 * SWA_HD, wb).astype(BF16)
        k_new = jnp.concatenate([sk_ref[:, cs].astype(BF16), pad], axis=0)
        q = sq_ref[:, cs]
        qs = jnp.concatenate([jnp.where(lo, q, 0.0), jnp.where(lo, 0.0, q)], axis=0).astype(BF16)
        s_alls.append(_dot(qs, kp))
        sn_alls.append(_dot_nt(qs, k_new))
    yield
    p_rows = [[] for _ in range(n_pairs)]
    pn_rows = [[] for _ in range(n_pairs)]
    stats = [[] for _ in range(n_pairs)]
    for pair in range(n_pairs):
        for hh in range(2):
            s = s_alls[pair][hh * n:hh * n + rt]
            sn = sn_alls[pair][hh * n:hh * n + rt]
            for w0, bias_c, bias_n in windows:
                sc = s[:, w0:] + bias_c
                snb = sn + bias_n
                m = jnp.maximum(jnp.max(sc, axis=-1, keepdims=True), jnp.max(snb, axis=-1, keepdims=True))
                pc = jnp.exp(sc - m)
                pn = jnp.exp(snb - m)
                stats[pair].append((m, jnp.sum(pc, axis=-1, keepdims=True) + jnp.sum(pn, axis=-1, keepdims=True)))
                if w0:
                    pc = jnp.concatenate([jnp.zeros((rt, w0), F32), pc], axis=1)
                p_rows[pair].append(pc)
                pn_rows[pair].append(pn)
            p_rows[pair].append(zero_c)
            pn_rows[pair].append(zero_n)
    ret_sb = [(ret_s[h] * dmat_ref[h]).astype(BF16) for h in range(RET_HEADS)]
    mem_p = [_softmax_rows(s * MEM_SCALE).astype(BF16) for s in mem_s]
    yield
    pvs = [_dot_nt(jnp.concatenate(p_rows[pair], axis=0).astype(BF16), vps[pair])
           + _dot(jnp.concatenate(pn_rows[pair], axis=0).astype(BF16), v_news[pair])
           for pair in range(n_pairs)]
    ret_intra = [_dot(ret_sb[h], vb[:, h * RET_DV:(h + 1) * RET_DV]) for h in range(RET_HEADS)]
    mem_o = [_dot(mem_p[h], cmv_ref[mem_rows[h], :].astype(BF16)) for h in range(MEM_HEADS)]
    yield
    pieces = []
    for pair in range(n_pairs):
        heads = []
        for hh in range(2):
            parts = []
            for i in range(n_win):
                m, l = stats[pair][hh * n_win + i]
                r0 = (hh * (n_win + 1) + i) * rt
                parts.append((pvs[pair][r0:r0 + rt] / l, m, l))
            m_all = jnp.maximum(jnp.maximum(parts[0][1], parts[1][1]), parts[2][1])
            ws = [l * jnp.exp(m - m_all) for (_, m, l) in parts]
            heads.append((ws[0] * parts[0][0] + ws[1] * parts[1][0] + ws[2] * parts[2][0])
                         / (ws[0] + ws[1] + ws[2]))
        pieces.append(jnp.where(lo_t, heads[0], heads[1]))
    swa = jnp.concatenate(pieces, axis=1)
    swa = jnp.concatenate([swa, jnp.zeros((n - rt, SWA_W), F32)], axis=0)
    mix_ref[:, pl.ds(RET_W, SWA_W)] = sg_ref[...] * swa
    for h in range(RET_HEADS):
        hs = pl.ds(h * RET_DV, RET_DV)
        ks = pl.ds(h * RET_DK, RET_DK)
        mix_ref[:, hs] = rg_ref[:, hs] * _head_norm(ret_intra[h] + ret_cross[h])
        state_out_ref[ks, :] = (gdec_ref[ks, :] * st[h * RET_DK:(h + 1) * RET_DK, :]
                                + kv[h * RET_DK:(h + 1) * RET_DK, h * RET_DV:(h + 1) * RET_DV])
    for h in range(MEM_HEADS):
        hs = pl.ds(h * MEM_HD, MEM_HD)
        mix_ref[:, pl.ds(RET_W + SWA_W + h * MEM_HD, MEM_HD)] = mg_ref[:, hs] * mem_o[h]


def _finish_kernel(x_ref, mix_ref, wout_ref, gain_ref, bias_ref, y_ref):
    hout = _dot(mix_ref[...].astype(BF16), wout_ref[...])
    y_ref[...] = _deepnorm_ln(x_ref[...], hout, gain_ref[...], bias_ref[...])


def _finish(x2d, mix, wout_bf, gain, bias):
    n = x2d.shape[0]
    tile = 256
    const2 = lambda a: pl.BlockSpec(a.shape, lambda i: (0, 0))
    return pl.pallas_call(
        _finish_kernel,
        grid=(n // tile,),
        in_specs=[pl.BlockSpec((tile, D_MODEL), lambda i: (i, 0)),
                  pl.BlockSpec((tile, D_MIX), lambda i: (i, 0)),
                  const2(wout_bf), const2(gain), const2(bias)],
        out_specs=pl.BlockSpec((tile, D_MODEL), lambda i: (i, 0)),
        out_shape=jax.ShapeDtypeStruct((n, D_MODEL), F32),
        compiler_params=_params(1),
        name="finish",
    )(x2d, mix, wout_bf, gain, bias)


def _rope_tables(pos):
    half = SWA_HD // 2
    inv = ROPE_THETA ** (-np.arange(half, dtype=np.float64) * 2.0 / SWA_HD)
    ang = pos.astype(np.float64)[:, None] * inv[None, :]
    cos, sin = np.cos(ang), np.sin(ang)
    reps = LANES // SWA_HD
    return (jnp.asarray(np.tile(np.concatenate([cos, cos], axis=1), (1, reps)), F32),
            jnp.asarray(np.tile(np.concatenate([-sin, sin], axis=1), (1, reps)), F32))


def _retention_tables(chunk, rows):
    lg = np.log1p(-np.exp2(-5.0 - np.arange(RET_HEADS, dtype=np.float64)))
    idx = np.arange(rows, dtype=np.float64)
    live = idx < chunk
    rel = idx[:, None] - idx[None, :]
    ok = (rel >= 0) & live[:, None] & live[None, :]
    dmat = np.where(ok[None], np.exp(np.maximum(rel, 0.0)[None] * lg[:, None, None]), 0.0)
    kdec = np.where(live[:, None], np.exp((chunk - 1.0 - idx)[:, None] * lg[None, :]), 0.0)
    qdec = np.where(live[:, None], np.exp((idx + 1.0)[:, None] * lg[None, :]), 0.0)
    g = np.exp(chunk * lg)
    kdec = np.repeat(kdec, RET_DK, axis=1)
    qdec = np.repeat(qdec, RET_DK, axis=1)
    gdec = np.broadcast_to(np.repeat(g, RET_DK)[:, None], (RET_HEADS * RET_DK, RET_DV))
    return tuple(jnp.asarray(t, F32) for t in (dmat, kdec, qdec, gdec))


def kernel(x_prompt, x_sample, state_ret, cache_swa_k, cache_swa_v, cache_mem_k, cache_mem_v,
           mem_prompt, w_in, w_mem_kv, w_out, ln_gain, ln_bias):
    depth = w_in.shape[0]
    assert depth == 1
    win_bf = w_in[0].astype(BF16)
    wmem_bf = w_mem_kv[0].astype(BF16)
    wout_bf = w_out[0].astype(BF16)
    gain = ln_gain[0].reshape(1, D_MODEL)
    bias = ln_bias[0].reshape(1, D_MODEL)

    xs = jnp.pad(x_sample, ((0, 0), (0, DEC_PAD - DEC_SEQ), (0, 0))).reshape(DEC_BATCH * DEC_PAD, D_MODEL)
    pos_s = PAST_LEN + np.arange(DEC_BATCH * DEC_PAD) % DEC_PAD
    cos_s, sin_s = _rope_tables(pos_s)
    names = [c[0] for c in _PROJ_COLS]
    ps = dict(zip(names, _project(xs, win_bf, cos_s, sin_s, DEC_BATCH * DEC_PAD,
                                  [(k, "rows", F32) for k in names])))
    decode = _decode_operands(
        ps, state_ret[0].reshape(DEC_BATCH, RET_HEADS * RET_DK, RET_DV),
        cache_swa_k[0].transpose(0, 2, 3, 1), cache_swa_v[0].transpose(0, 2, 3, 1),
        cache_mem_k[0].reshape(DEC_BATCH, N_MEM * MEM_HEADS, MEM_HD),
        cache_mem_v[0].reshape(DEC_BATCH, N_MEM * MEM_HEADS, MEM_HD),
        _retention_tables(DEC_SEQ, DEC_PAD))

    xp = x_prompt.reshape(BATCH * SEQ, D_MODEL)
    assert xp.shape[0] // PERM_TILE == DEC_BATCH
    cos_p, sin_p = _rope_tables(np.arange(SEQ))
    p_outs = (("rq", "rows", F32), ("rk", "rows", F32), ("rv", "rows", BF16), ("rg", "rows", BF16),
              ("sq", "perm", F32), ("sk", "cols", F32), ("sk", "perm", F32), ("sv", "cols", F32),
              ("sv", "perm", F32), ("sg", "rows", BF16), ("mq", "rows", BF16), ("mg", "rows", BF16))
    p_keys = ("rq", "rk", "rv", "rg", "sq4", "sk", "sk4", "sv", "sv4", "sg", "mq", "mg")
    *p_res, mix_s, ret_s = _project(xp, win_bf, cos_p, sin_p, PERM_TILE, p_outs, decode)
    pr = dict(zip(p_keys, p_res))
    mk, mv = _memkv(mem_prompt.reshape(BATCH * N_MEM, D_MODEL), wmem_bf)
    slab = lambda a: a.reshape(SWA_W // LANES, BATCH, SEQ // PERM_TILE * PERM_ROWS, LANES)
    swa_o = _swa_prompt(slab(pr["sq4"]), slab(pr["sk4"]), slab(pr["sv4"]))
    swa_o = swa_o.reshape(SWA_W // LANES, BATCH * SEQ, LANES)
    yp, ret_p = _mix_prompt(xp, pr, swa_o, mk, mv, wout_bf, _retention_tables(RET_CHUNK, RET_CHUNK),
                            gain, bias, 512)
    ys = _finish(xs, mix_s, wout_bf, gain, bias)

    take = lambda a, w: a.reshape(DEC_BATCH, DEC_PAD, w)[:, :DEC_SEQ]
    swa_rows = lambda a: a.reshape(BATCH, SWA_HEADS, SWA_HD, SEQ).transpose(0, 3, 1, 2)[None]
    return (
        yp.reshape(BATCH, SEQ, D_MODEL),
        take(ys, D_MODEL),
        ret_p.reshape(1, BATCH, RET_HEADS, RET_DK, RET_DV),
        ret_s.reshape(1, DEC_BATCH, RET_HEADS, RET_DK, RET_DV),
        swa_rows(pr["sk"]),
        swa_rows(pr["sv"]),
        take(ps["sk"], SWA_W).reshape(1, DEC_BATCH, DEC_SEQ, SWA_HEADS, SWA_HD),
        take(ps["sv"], SWA_W).reshape(1, DEC_BATCH, DEC_SEQ, SWA_HEADS, SWA_HD),
        mk.reshape(1, BATCH, N_MEM, MEM_HEADS, MEM_HD),
        mv.reshape(1, BATCH, N_MEM, MEM_HEADS, MEM_HD),
    )
```

```python
import functools

import jax
import jax.numpy as jnp
import numpy as np
from jax import lax
from jax.experimental import pallas as pl
from jax.experimental.pallas import tpu as pltpu

F32 = jnp.float32
BF16 = jnp.bfloat16

D_MODEL = 1024
BATCH = 8
SEQ = 2048
DEC_BATCH = 32
DEC_SEQ = 4
PAST_LEN = 8192
N_MEM = 256
MEM_HEADS = 4
MEM_HD = 128
RET_HEADS = 4
RET_DK = 64
RET_DV = 128
RET_CHUNK = 128
SWA_HEADS = 8
SWA_HD = 64
SWA_DILATIONS = (1, 4, 16)
SWA_STEPS = 128
SWA_BLOCK = 128
ROPE_THETA = 10000.0
LN_EPS = 1e-5
GN_EPS = 1e-5
RET_W = RET_HEADS * RET_DV
SWA_W = SWA_HEADS * SWA_HD
MEM_W = MEM_HEADS * MEM_HD
D_MIX = RET_W + SWA_W + MEM_W
DEEPNORM_ALPHA = 2.0 ** 0.25
MEM_SCALE = MEM_HD ** -0.5
LOG2_E = 1.4426950408889634
QK_SCALE = 0.125

LANES = 128
DEC_PAD = 16
PERM_TILE = 512
PERM_PITCH = 40
PERM_ROWS = 16 * PERM_PITCH
OUT_ROWS = 256
SWA_PAR = 2
MEMKV_BATCHES = 4
DECODE_EVERY = 2
VMEM_LIMIT = 56 * 1024 * 1024
NEG = -1e30

_PROJ_COLS = (
    ("rq", RET_HEADS * RET_DK, "rope", 1.0),
    ("rk", RET_HEADS * RET_DK, "rope", QK_SCALE),
    ("rv", RET_W, "id", 1.0),
    ("rg", RET_W, "silu", 1.0),
    ("sq", SWA_W, "rope", QK_SCALE),
    ("sk", SWA_W, "rope", 1.0),
    ("sv", SWA_W, "id", 1.0),
    ("sg", SWA_W, "silu", 1.0),
    ("mq", MEM_W, "id", 1.0),
    ("mg", MEM_W, "silu", 1.0),
)


def _dot(a, b):
    return jnp.dot(a, b, preferred_element_type=F32)


def _dot_nt(a, b):
    return lax.dot_general(a, b, (((1,), (1,)), ((), ())), preferred_element_type=F32)


def _dot_tn(a, b):
    return lax.dot_general(a, b, (((0,), (0,)), ((), ())), preferred_element_type=F32)


def _params(n_axes):
    return pltpu.CompilerParams(dimension_semantics=("arbitrary",) * n_axes,
                                vmem_limit_bytes=VMEM_LIMIT)


def _run_interleaved(stages):
    stages = list(stages)
    while stages:
        for g in list(stages):
            if next(g, "done") == "done":
                stages.remove(g)


def _decode_on_step(step, s_in, s_out):
    @pl.when(step % DECODE_EVERY == 0)
    def _():
        _run_interleaved([_mix_sample_kernel(*s_in, *s_out)])


def _proj_kernel(*refs, dests, n_decode_in=0):
    n_out = len(dests)
    p_in, s_in = refs[:4], refs[4:4 + n_decode_in]
    p_out, s_out = refs[4 + n_decode_in:4 + n_decode_in + n_out], refs[4 + n_decode_in + n_out:]
    if n_decode_in:
        _decode_on_step(pl.program_id(0), s_in, s_out)
    _run_interleaved([_proj_stages(*p_in, *p_out, dests=dests)])


def _proj_stages(x_ref, w_ref, cos_ref, sin_ref, *out_refs, dests):
    xb = x_ref[...].astype(BF16)
    cos = cos_ref[...]
    sin = sin_ref[...]
    lane = lax.broadcasted_iota(jnp.int32, cos.shape, 1)
    first_half = (lane % 64) < 32
    col = 0
    for name, width, kind, scale in _PROJ_COLS:
        targets = [(o_ref, layout) for o_ref, (dname, layout) in zip(out_refs, dests) if dname == name]
        for c in range(0, width, 2 * LANES):
            h2 = _dot(xb, w_ref[:, col + c:col + c + 2 * LANES])
            for half in range(2):
                h = h2[:, half * LANES:(half + 1) * LANES]
                if kind == "rope":
                    swapped = jnp.where(first_half, pltpu.roll(h, 96, 1), pltpu.roll(h, 32, 1))
                    h = h * cos + swapped * sin
                    if scale != 1.0:
                        h = h * scale
                elif kind == "silu":
                    h = h * (1.0 / (1.0 + jnp.exp(-h)))
                lo = c + half * LANES
                for o_ref, layout in targets:
                    if layout == "rows":
                        o_ref[:, lo:lo + LANES] = h.astype(o_ref.dtype)
                    elif layout == "perm":
                        slab = lo // LANES
                        for g in range(h.shape[0] // 8):
                            base = (g % 2) * 8 * PERM_PITCH + g // 2
                            o_ref[slab, pl.ds(base, 8, stride=PERM_PITCH), :] = (
                                h[8 * g:8 * g + 8].astype(o_ref.dtype))
                        n_live = h.shape[0] // 16
                        for cls in range(16):
                            o_ref[slab, pl.ds(cls * PERM_PITCH + n_live, PERM_PITCH - n_live), :] = (
                                jnp.zeros((PERM_PITCH - n_live, LANES), o_ref.dtype))
                    else:
                        o_ref[lo:lo + LANES, :] = h.T.astype(o_ref.dtype)
        col += width
        yield


def _project(x2d, w_bf, cos_t, sin_t, tile, outs, decode=None):
    n = x2d.shape[0]
    seq = cos_t.shape[0]
    n_tab = seq // tile
    d_in = w_bf.shape[1]
    widths = {name: w for name, w, _, _ in _PROJ_COLS}
    out_shape, out_specs = [], []
    for name, layout, dt in outs:
        w = widths[name]
        if layout == "rows":
            out_shape.append(jax.ShapeDtypeStruct((n, w), dt))
            out_specs.append(pl.BlockSpec((tile, w), lambda i: (i, 0)))
        elif layout == "perm":
            assert tile == PERM_TILE
            out_shape.append(jax.ShapeDtypeStruct((w // LANES, n // tile * PERM_ROWS, LANES), dt))
            out_specs.append(pl.BlockSpec((w // LANES, PERM_ROWS, LANES), lambda i: (0, i, 0)))
        else:
            out_shape.append(jax.ShapeDtypeStruct((n // seq, w, seq), dt))
            out_specs.append(pl.BlockSpec((None, w, tile), lambda i: (i // n_tab, 0, i % n_tab)))
    d_args, d_in_specs, d_out_specs, d_out_shape = decode if decode else ((), [], [], [])
    return pl.pallas_call(
        functools.partial(_proj_kernel, dests=tuple((name, layout) for name, layout, _ in outs),
                          n_decode_in=len(d_args)),
        grid=(n // tile,),
        in_specs=[
            pl.BlockSpec((tile, D_MODEL), lambda i: (i, 0)),
            pl.BlockSpec((D_MODEL, d_in), lambda i: (0, 0)),
            pl.BlockSpec((tile, LANES), lambda i: (i % n_tab, 0)),
            pl.BlockSpec((tile, LANES), lambda i: (i % n_tab, 0)),
        ] + list(d_in_specs),
        out_specs=out_specs + list(d_out_specs),
        out_shape=out_shape + list(d_out_shape),
        compiler_params=_params(1),
        name="proj_decode" if decode else "proj",
    )(x2d, w_bf, cos_t, sin_t, *d_args)


def _decode_operands(ps, state, ckt, cvt, cmk, cmv, tabs, step_of, first, count):
    n = DEC_PAD
    wb = ckt.shape[3]
    local = lambda *g: step_of(*g) // DECODE_EVERY
    batch = lambda *g: first + local(*g)
    const = lambda a: pl.BlockSpec(a.shape, lambda *g: (0,) * a.ndim)
    srow = lambda w: pl.BlockSpec((n, w), lambda *g: (batch(*g), 0))
    st_s = pl.BlockSpec((None, RET_HEADS * RET_DK, RET_DV), lambda *g: (batch(*g), 0, 0))
    cache_s = pl.BlockSpec((None, SWA_HEADS, SWA_HD, wb), lambda *g: (batch(*g), 0, 0, 0))
    mem_s = pl.BlockSpec((None, N_MEM * MEM_HEADS, MEM_HD), lambda *g: (batch(*g), 0, 0))
    s_names = ("rq", "rk", "rv", "rg", "sq", "sk", "sv", "sg", "mq", "mg")
    widths = {name: w for name, w, _, _ in _PROJ_COLS}
    args = (*[ps[k] for k in s_names], state, ckt, cvt, cmk, cmv, *tabs)
    in_specs = [*[srow(widths[k]) for k in s_names], st_s, cache_s, cache_s, mem_s, mem_s,
                *[const(a) for a in tabs]]
    out_specs = [pl.BlockSpec((n, D_MIX), lambda *g: (local(*g), 0)),
                 pl.BlockSpec((None, RET_HEADS * RET_DK, RET_DV), lambda *g: (local(*g), 0, 0))]
    out_shape = [jax.ShapeDtypeStruct((count * n, D_MIX), F32),
                 jax.ShapeDtypeStruct((count, RET_HEADS * RET_DK, RET_DV), F32)]
    return args, in_specs, out_specs, out_shape


def _memkv_kernel(m_ref, w_ref, mk_ref, mv_ref):
    for b in range(m_ref.shape[0] // N_MEM):
        mb = m_ref[pl.ds(b * N_MEM, N_MEM), :].astype(BF16)
        for h in range(MEM_HEADS):
            rows = pl.ds(h, N_MEM, stride=MEM_HEADS)
            mk_ref[b, rows, :] = _dot(mb, w_ref[:, h * MEM_HD:(h + 1) * MEM_HD])
            mv_ref[b, rows, :] = _dot(mb, w_ref[:, MEM_W + h * MEM_HD:MEM_W + (h + 1) * MEM_HD])


def _memkv(mem2d, w_bf):
    n = mem2d.shape[0] // N_MEM
    per = MEMKV_BATCHES
    out = pl.BlockSpec((per, N_MEM * MEM_HEADS, MEM_HD), lambda i: (i, 0, 0))
    return pl.pallas_call(
        _memkv_kernel,
        grid=(n // per,),
        in_specs=[pl.BlockSpec((per * N_MEM, D_MODEL), lambda i: (i, 0)),
                  pl.BlockSpec((D_MODEL, 2 * MEM_W), lambda i: (0, 0))],
        out_specs=[out, out],
        out_shape=[jax.ShapeDtypeStruct((n, N_MEM * MEM_HEADS, MEM_HD), F32)] * 2,
        compiler_params=_params(1),
        name="memkv",
    )(mem2d, w_bf)


def _swa_prompt_kernel(q_ref, k_ref, v_ref, o_ref, a_ref, m_ref, l_ref):
    blk = SWA_BLOCK
    pt = PERM_TILE
    grp = pt // 16
    lo = lax.broadcasted_iota(jnp.int32, (blk, blk), 1) < SWA_HD

    def biases(seq_of):
        rq = seq_of(lax.broadcasted_iota(jnp.int32, (blk, 2 * blk), 0))
        c2 = lax.broadcasted_iota(jnp.int32, (blk, 2 * blk), 1)
        rel = blk + rq - (seq_of(c2 % blk) + blk * (c2 // blk))
        prev = jnp.where((rel >= 0) & (rel <= SWA_STEPS), 0.0, NEG).astype(F32)
        r1 = seq_of(lax.broadcasted_iota(jnp.int32, (blk, blk), 0))
        c1 = seq_of(lax.broadcasted_iota(jnp.int32, (blk, blk), 1))
        return prev, jnp.where(r1 >= c1, 0.0, NEG).astype(F32)

    def gather(ref, pair, starts, n):
        return jnp.concatenate([ref[pair, pl.ds(s, n), :] for s in starts], axis=0)

    def aligned(x):
        return x if isinstance(x, int) else pl.multiple_of(x, 8)

    def blocks(jobs, n, first_pattern, last_pattern):
        n_pairs = SWA_HEADS // 2
        in_at = lambda c: aligned(c[0] * PERM_ROWS + c[1] * PERM_PITCH + c[2])
        acc_starts, q_starts, k_starts = [], [], []
        for chunks, prev_chunks, _, _ in jobs:
            acc_starts.append([aligned(c[0] * pt + c[1] * grp + c[2]) for c in chunks])
            q_starts.append([in_at(c) for c in chunks])
            k_starts.append(([] if prev_chunks is None else [in_at(c) for c in prev_chunks]) + q_starts[-1])
        items = [(j, pair) for j in range(len(jobs)) for pair in range(n_pairs)]
        olds = {}
        if not first_pattern:
            for j, pair in items:
                olds[j, pair] = (gather(m_ref, pair, acc_starts[j], n), gather(l_ref, pair, acc_starts[j], n),
                                 gather(a_ref, pair, acc_starts[j], n))
        vbs = {(j, pair): gather(v_ref, pair, k_starts[j], n).astype(BF16) for j, pair in items}
        scores = {}
        for j, pair in items:
            q = gather(q_ref, pair, q_starts[j], n)
            kb = gather(k_ref, pair, k_starts[j], n).astype(BF16)
            for hh in range(2):
                qm = jnp.where(lo if hh == 0 else ~lo, q, 0.0).astype(BF16)
                scores[j, pair, hh] = _dot_nt(qm, kb) + jobs[j][2]
        ms = {key: jnp.max(s, axis=-1, keepdims=True) for key, s in scores.items()}
        ps = {key: jnp.exp(s - ms[key]) for key, s in scores.items()}
        def v_ext(v):
            lane = lax.broadcasted_iota(jnp.int32, v.shape, 1)
            own = lane < SWA_HD
            zero = jnp.zeros((), BF16)
            ones_lo = jnp.where(lane < SWA_HD, 1.0, 0.0).astype(BF16)
            ones_hi = jnp.where(lane < SWA_HD, 0.0, 1.0).astype(BF16)
            return jnp.concatenate(
                [jnp.concatenate([jnp.where(own, v, zero), ones_lo], axis=1),
                 jnp.concatenate([jnp.where(own, zero, v), ones_hi], axis=1)], axis=0)

        pvs = {(j, pair): _dot(jnp.concatenate([ps[j, pair, 0].astype(BF16), ps[j, pair, 1].astype(BF16)], axis=1),
                               v_ext(vbs[j, pair])) for j, pair in items}
        news = {}
        for j, pair in items:
            m_g = jnp.where(lo, ms[j, pair, 0], ms[j, pair, 1])
            a_g = pvs[j, pair][:, :LANES]
            l_g = pvs[j, pair][:, LANES:]
            if first_pattern:
                news[j, pair] = (m_g, l_g, a_g)
            else:
                m_old, l_old, a_old = olds[j, pair]
                m_new = jnp.maximum(m_old, m_g)
                w_old = jnp.exp(m_old - m_new)
                w_g = jnp.exp(m_g - m_new)
                news[j, pair] = (m_new, w_old * l_old + w_g * l_g, w_old * a_old + w_g * a_g)
        for (j, pair), (m_new, l_new, a_new) in news.items():
            starts = acc_starts[j]
            if last_pattern:
                o = a_new / l_new
                for i in range(len(starts)):
                    o_ref[pair, pl.ds(jobs[j][3] + i, n, stride=len(starts)), :] = o[i * n:(i + 1) * n]
            else:
                for i, s in enumerate(starts):
                    m_ref[pair, pl.ds(s, n), :] = m_new[i * n:(i + 1) * n]
                    l_ref[pair, pl.ds(s, n), :] = l_new[i * n:(i + 1) * n]
                    a_ref[pair, pl.ds(s, n), :] = a_new[i * n:(i + 1) * n]

    par = SWA_PAR
    _, bias16 = biases(lambda i: i)

    par16 = 2 * par

    def class16(r0, carry):
        blocks([([(t, r0 + k * (16 // par16), 0) for t in range(SEQ // pt)], None, bias16, None)
                for k in range(par16)], grp, True, False)
        return carry

    lax.fori_loop(0, 16 // par16, class16, 0)

    bias4_prev, bias4_own = biases(lambda i: 4 * (i % grp) + i // grp)

    def class4(r0, carry):
        chunks_of = lambda r, tile: [(tile, 4 * a + r, 0) for a in range(4)]
        classes = [r0 + k * (4 // par) for k in range(par)]
        blocks([(chunks_of(r, 0), None, bias4_own, None) for r in classes], grp, False, False)

        def per_block(b, c2):
            blocks([(chunks_of(r, b), chunks_of(r, b - 1), bias4_prev, None) for r in classes],
                   grp, False, False)
            return c2

        lax.fori_loop(1, SEQ // pt, per_block, 0)
        return carry

    lax.fori_loop(0, 4 // par, class4, 0)

    bias1_prev, bias1_own = biases(lambda i: 16 * (i % 8) + i // 8)
    per_tile = pt // blk
    chunks1 = lambda c: [(c // per_tile, i, (c % per_tile) * 8) for i in range(16)]
    job1 = lambda c: (chunks1(c), chunks1(c - 1), bias1_prev, aligned(c * blk))
    blocks([(chunks1(0), None, bias1_own, 0)] + [job1(c) for c in range(1, par)], 8, False, True)

    def block1(g, carry):
        blocks([job1(g * par + k) for k in range(par)], 8, False, True)
        return carry

    lax.fori_loop(1, SEQ // blk // par, block1, 0)


def _swa_prompt(sq, sk, sv):
    slabs = SWA_W // LANES
    spec = pl.BlockSpec((slabs, None, SEQ, LANES), lambda b: (0, b, 0, 0))
    in_spec = pl.BlockSpec((slabs, None, sq.shape[2], LANES), lambda b: (0, b, 0, 0))
    return pl.pallas_call(
        _swa_prompt_kernel,
        grid=(BATCH,),
        in_specs=[in_spec, in_spec, in_spec],
        out_specs=spec,
        out_shape=jax.ShapeDtypeStruct((slabs, BATCH, SEQ, LANES), F32),
        scratch_shapes=[pltpu.VMEM((slabs, SEQ, LANES), F32)] * 3,
        compiler_params=_params(1),
        name="swa_prompt",
    )(sq, sk, sv)


def _head_norm(o):
    mu = jnp.mean(o, axis=-1, keepdims=True)
    d = o - mu
    var = jnp.mean(d * d, axis=-1, keepdims=True)
    return d * lax.rsqrt(var + GN_EPS)


def _deepnorm_ln(x, h, gain, bias):
    z = DEEPNORM_ALPHA * x + h
    mu = jnp.mean(z, axis=-1, keepdims=True)
    d = z - mu
    var = jnp.mean(d * d, axis=-1, keepdims=True)
    return d * lax.rsqrt(var + LN_EPS) * gain + bias


def _softmax_rows(s):
    m = jnp.max(s, axis=-1, keepdims=True)
    p = jnp.exp(s - m)
    return p * (1.0 / jnp.sum(p, axis=-1, keepdims=True))


def _mix_prompt_kernel(x_ref, rq_ref, rk_ref, rv_ref, rg_ref, so_ref, sg_ref, mq_ref, mg_ref,
                       mk_ref, mv_ref, wout_ref, dmat_ref, kdec_ref, qdec_ref, gdec_ref,
                       gain_ref, bias_ref, y_ref, state_out_ref,
                       state_ref, mix_ref, mkb_ref, mvb_ref, *, tile):
    t = pl.program_id(1)

    @pl.when(t == 0)
    def _():
        state_ref[...] = jnp.zeros_like(state_ref)
        for h in range(MEM_HEADS):
            rows = pl.ds(h, N_MEM, stride=MEM_HEADS)
            mkb_ref[:, pl.ds(h * MEM_HD, MEM_HD)] = mk_ref[rows, :].astype(BF16)
            mvb_ref[:, pl.ds(h * MEM_HD, MEM_HD)] = mv_ref[rows, :].astype(BF16)

    ck = RET_CHUNK
    lane = lax.broadcasted_iota(jnp.int32, (ck, LANES), 1)
    lo = lane < RET_DK
    top = lax.broadcasted_iota(jnp.int32, (LANES, LANES), 0) < RET_DK
    gain = gain_ref[...]
    bias = bias_ref[...]

    n_ck = tile // ck
    n_pairs = RET_HEADS // 2
    rows_of = [pl.ds(c * ck, ck) for c in range(n_ck)]
    items = [(c, pair, hh) for c in range(n_ck) for pair in range(n_pairs) for hh in range(2)]
    hs_of = lambda pair, hh: pl.ds((2 * pair + hh) * RET_DV, RET_DV)
    sel = lambda x, hh: jnp.where(lo if hh == 0 else ~lo, x, 0.0).astype(BF16)
    pairs = [(c, pair) for c in range(n_ck) for pair in range(n_pairs)]
    qk = {}
    for c, pair in pairs:
        cs = pl.ds(pair * LANES, LANES)
        q = rq_ref[rows_of[c], cs]
        k = rk_ref[rows_of[c], cs]
        qk[c, pair] = (q.astype(BF16), jnp.concatenate([sel(k, 0), sel(k, 1)], axis=0),
                       (k * kdec_ref[:, cs]).astype(BF16), q * qdec_ref[:, cs])
    v_pair = lambda c, pair: rv_ref[rows_of[c], pl.ds(pair * 2 * RET_DV, 2 * RET_DV)]
    s2 = {key: _dot_nt(qk[key][0], qk[key][1]) for key in pairs}
    kv2 = {key: _dot_tn(qk[key][2], v_pair(*key)) for key in pairs}
    yield
    state = {(0, pair): state_ref[pair] for pair in range(n_pairs)}
    for c, pair in pairs:
        state[c + 1, pair] = (gdec_ref[pl.ds(pair * LANES, LANES), :] * state[c, pair]
                              + jnp.where(top, kv2[c, pair][:, :RET_DV], kv2[c, pair][:, RET_DV:]))
    for pair in range(n_pairs):
        state_ref[pair] = state[n_ck, pair]
    o = {}
    for c, pair, hh in items:
        sh = (s2[c, pair][:, hh * ck:(hh + 1) * ck] * dmat_ref[2 * pair + hh]).astype(BF16)
        lhs = jnp.concatenate([sh, sel(qk[c, pair][3], hh)], axis=1)
        rhs = jnp.concatenate([rv_ref[rows_of[c], hs_of(pair, hh)], state[c, pair].astype(BF16)], axis=0)
        o[c, pair, hh] = _dot(lhs, rhs)
    yield
    for c, pair, hh in items:
        hs = hs_of(pair, hh)
        mix_ref[rows_of[c], hs] = (rg_ref[rows_of[c], hs].astype(F32) * _head_norm(o[c, pair, hh])).astype(BF16)
    for pair in range(SWA_W // LANES):
        cs = pl.ds(pair * LANES, LANES)
        mix_ref[:, pl.ds(RET_W + pair * LANES, LANES)] = (
            sg_ref[:, cs].astype(F32) * so_ref[pair]).astype(BF16)
    for c in range(n_ck):
        yield
        ss = [_dot_nt(mq_ref[rows_of[c], pl.ds(h * MEM_HD, MEM_HD)], mkb_ref[:, pl.ds(h * MEM_HD, MEM_HD)])
              for h in range(MEM_HEADS)]
        ps = [jnp.exp2((s - jnp.max(s, axis=-1, keepdims=True)) * (MEM_SCALE * LOG2_E)) for s in ss]
        for h in range(MEM_HEADS):
            hs = pl.ds(h * MEM_HD, MEM_HD)
            o = _dot(ps[h].astype(BF16), mvb_ref[:, hs]) * (1.0 / jnp.sum(ps[h], axis=-1, keepdims=True))
            mix_ref[rows_of[c], pl.ds(RET_W + SWA_W + h * MEM_HD, MEM_HD)] = (
                mg_ref[rows_of[c], hs].astype(F32) * o).astype(BF16)
    yield
    blocks = [pl.ds(r * OUT_ROWS, OUT_ROWS) for r in range(tile // OUT_ROWS)]
    hout = _dot(mix_ref[blocks[0], :], wout_ref[...])
    for r, rows in enumerate(blocks):
        nxt = _dot(mix_ref[blocks[r + 1], :], wout_ref[...]) if r + 1 < len(blocks) else None
        y_ref[rows, :] = _deepnorm_ln(x_ref[rows, :], hout, gain, bias)
        hout = nxt

    @pl.when(t == pl.num_programs(1) - 1)
    def _():
        state_out_ref[pl.ds(0, LANES), :] = state_ref[0]
        state_out_ref[pl.ds(LANES, LANES), :] = state_ref[1]


def _mix_prompt_body(*refs, tile, n_in, n_decode_in):
    p_in, s_in = refs[:n_in], refs[n_in:n_in + n_decode_in]
    rest = refs[n_in + n_decode_in:]
    p_out, rest = rest[:2], rest[2:]
    if n_decode_in:
        s_out, rest = rest[:2], rest[2:]
        _decode_on_step(pl.program_id(0) * pl.num_programs(1) + pl.program_id(1), s_in, s_out)
    _run_interleaved([_mix_prompt_kernel(*p_in, *p_out, *rest, tile=tile)])


def _mix_prompt(x2d, pr, swa_o, mk, mv, wout_bf, tabs_p, gain, bias, tile, decode_of=None):
    nt = SEQ // tile
    step = lambda b, t: b * nt + t
    d_args, d_in_specs, d_out_specs, d_out_shape = decode_of(step) if decode_of else ((), [], [], [])
    row = lambda w: pl.BlockSpec((tile, w), lambda b, t: (step(b, t), 0))
    const = lambda a: pl.BlockSpec(a.shape, lambda b, t: (0,) * a.ndim)
    mem_p = pl.BlockSpec((None, N_MEM * MEM_HEADS, MEM_HD), lambda b, t: (b, 0, 0))
    st_p = pl.BlockSpec((None, RET_HEADS * RET_DK, RET_DV), lambda b, t: (b, 0, 0))
    p_args = (x2d, pr["rq"], pr["rk"], pr["rv"], pr["rg"], swa_o, pr["sg"], pr["mq"], pr["mg"],
              mk, mv, wout_bf, *tabs_p, gain, bias)
    p_specs = [row(D_MODEL), row(RET_HEADS * RET_DK), row(RET_HEADS * RET_DK), row(RET_W), row(RET_W),
               pl.BlockSpec((SWA_W // LANES, tile, LANES), lambda b, t: (0, step(b, t), 0)),
               row(SWA_W), row(MEM_W), row(MEM_W), mem_p, mem_p, const(wout_bf),
               *[const(a) for a in tabs_p], const(gain), const(bias)]
    return pl.pallas_call(
        functools.partial(_mix_prompt_body, tile=tile, n_in=len(p_args), n_decode_in=len(d_args)),
        grid=(BATCH, nt),
        in_specs=p_specs + list(d_in_specs),
        out_specs=[row(D_MODEL), st_p] + list(d_out_specs),
        out_shape=[jax.ShapeDtypeStruct((BATCH * SEQ, D_MODEL), F32),
                   jax.ShapeDtypeStruct((BATCH, RET_HEADS * RET_DK, RET_DV), F32)] + list(d_out_shape),
        scratch_shapes=[pltpu.VMEM((RET_HEADS // 2, LANES, RET_DV), F32),
                        pltpu.VMEM((tile, D_MIX), BF16),
                        pltpu.VMEM((N_MEM, MEM_W), BF16),
                        pltpu.VMEM((N_MEM, MEM_W), BF16)],
        compiler_params=_params(2),
        name="mix_decode" if decode_of else "mix_prompt",
    )(*p_args, *d_args)


def _mix_sample_kernel(rq_ref, rk_ref, rv_ref, rg_ref, sq_ref, sk_ref, sv_ref, sg_ref, mq_ref, mg_ref,
                       state_ref, ckt_ref, cvt_ref, cmk_ref, cmv_ref,
                       dmat_ref, kdec_ref, qdec_ref, gdec_ref,
                       mix_ref, state_out_ref):
    n = DEC_PAD
    q = rq_ref[...]
    k = rk_ref[...]
    kb = k.astype(BF16)
    kd = (k * kdec_ref[...]).astype(BF16)
    qd = q * qdec_ref[...]
    st = state_ref[...]
    stb = st.astype(BF16)
    vb = rv_ref[...].astype(BF16)
    kv = _dot_tn(kd, vb)
    lane_qk = lax.broadcasted_iota(jnp.int32, q.shape, 1) // RET_DK
    ret_s = [_dot_nt(jnp.where(lane_qk == h, q, 0.0).astype(BF16), kb) for h in range(RET_HEADS)]
    ret_cross = [_dot(jnp.where(lane_qk == h, qd, 0.0).astype(BF16), stb) for h in range(RET_HEADS)]
    mem_rows = [pl.ds(h, N_MEM, stride=MEM_HEADS) for h in range(MEM_HEADS)]
    mem_s = [_dot_nt(mq_ref[:, pl.ds(h * MEM_HD, MEM_HD)].astype(BF16), cmk_ref[mem_rows[h], :].astype(BF16))
             for h in range(MEM_HEADS)]

    blk = SWA_BLOCK
    wb = ckt_ref.shape[2]
    rt = 8
    lo = lax.broadcasted_iota(jnp.int32, (n, LANES), 1) < SWA_HD
    lo_t = lax.broadcasted_iota(jnp.int32, (rt, LANES), 1) < SWA_HD
    tok = lambda w: lax.broadcasted_iota(jnp.int32, (rt, w), 0)
    col = lambda w: lax.broadcasted_iota(jnp.int32, (rt, w), 1)
    windows = (
        (wb - blk, jnp.where(col(blk) >= tok(blk), 0.0, NEG), jnp.where(col(blk) <= tok(blk), 0.0, NEG)),
        (wb - 4 * blk, jnp.where(col(4 * blk) % 4 == tok(4 * blk), 0.0, NEG),
         jnp.where(col(blk) == tok(blk), 0.0, NEG)),
        (0, jnp.where(col(wb) % 16 == tok(wb), 0.0, NEG), jnp.where(col(blk) == tok(blk), 0.0, NEG)),
    )
    pad = jnp.zeros((blk - n, LANES), BF16)
    zero_c = jnp.zeros((rt, wb), F32)
    zero_n = jnp.zeros((rt, blk), F32)
    n_pairs = SWA_HEADS // 2
    n_win = len(windows)
    pair_cols = [pl.ds(pair * LANES, LANES) for pair in range(n_pairs)]
    vps = [cvt_ref[2 * pair:2 * pair + 2].reshape(2 * SWA_HD, wb).astype(BF16) for pair in range(n_pairs)]
    v_news = [jnp.concatenate([sv_ref[:, cs].astype(BF16), pad], axis=0) for cs in pair_cols]
    s_alls, sn_alls = [], []
    for pair, cs in enumerate(pair_cols):
        kp = ckt_ref[2 * pair:2 * pair + 2].reshape(2 * SWA_HD, wb).astype(BF16)
        k_new = jnp.concatenate([sk_ref[:, cs].astype(BF16), pad], axis=0)
        q = sq_ref[:, cs]
        qs = jnp.concatenate([jnp.where(lo, q, 0.0), jnp.where(lo, 0.0, q)], axis=0).astype(BF16)
        s_alls.append(_dot(qs, kp))
        sn_alls.append(_dot_nt(qs, k_new))
    yield
    p_rows = [[] for _ in range(n_pairs)]
    pn_rows = [[] for _ in range(n_pairs)]
    stats = [[] for _ in range(n_pairs)]
    for pair in range(n_pairs):
        for hh in range(2):
            s = s_alls[pair][hh * n:hh * n + rt]
            sn = sn_alls[pair][hh * n:hh * n + rt]
            for w0, bias_c, bias_n in windows:
                sc = s[:, w0:] + bias_c
                snb = sn + bias_n
                m = jnp.maximum(jnp.max(sc, axis=-1, keepdims=True), jnp.max(snb, axis=-1, keepdims=True))
                pc = jnp.exp(sc - m)
                pn = jnp.exp(snb - m)
                stats[pair].append((m, jnp.sum(pc, axis=-1, keepdims=True) + jnp.sum(pn, axis=-1, keepdims=True)))
                if w0:
                    pc = jnp.concatenate([jnp.zeros((rt, w0), F32), pc], axis=1)
                p_rows[pair].append(pc)
                pn_rows[pair].append(pn)
            p_rows[pair].append(zero_c)
            pn_rows[pair].append(zero_n)
    ret_sb = [(ret_s[h] * dmat_ref[h]).astype(BF16) for h in range(RET_HEADS)]
    mem_p = [_softmax_rows(s * MEM_SCALE).astype(BF16) for s in mem_s]
    yield
    pvs = [_dot_nt(jnp.concatenate(p_rows[pair], axis=0).astype(BF16), vps[pair])
           + _dot(jnp.concatenate(pn_rows[pair], axis=0).astype(BF16), v_news[pair])
           for pair in range(n_pairs)]
    ret_intra = [_dot(ret_sb[h], vb[:, h * RET_DV:(h + 1) * RET_DV]) for h in range(RET_HEADS)]
    mem_o = [_dot(mem_p[h], cmv_ref[mem_rows[h], :].astype(BF16)) for h in range(MEM_HEADS)]
    yield
    pieces = []
    for pair in range(n_pairs):
        heads = []
        for hh in range(2):
            parts = []
            for i in range(n_win):
                m, l = stats[pair][hh * n_win + i]
                r0 = (hh * (n_win + 1) + i) * rt
                parts.append((pvs[pair][r0:r0 + rt] / l, m, l))
            m_all = jnp.maximum(jnp.maximum(parts[0][1], parts[1][1]), parts[2][1])
            ws = [l * jnp.exp(m - m_all) for (_, m, l) in parts]
            heads.append((ws[0] * parts[0][0] + ws[1] * parts[1][0] + ws[2] * parts[2][0])
                         / (ws[0] + ws[1] + ws[2]))
        pieces.append(jnp.where(lo_t, heads[0], heads[1]))
    swa = jnp.concatenate(pieces, axis=1)
    swa = jnp.concatenate([swa, jnp.zeros((n - rt, SWA_W), F32)], axis=0)
    mix_ref[:, pl.ds(RET_W, SWA_W)] = sg_ref[...] * swa
    for h in range(RET_HEADS):
        hs = pl.ds(h * RET_DV, RET_DV)
        ks = pl.ds(h * RET_DK, RET_DK)
        mix_ref[:, hs] = rg_ref[:, hs] * _head_norm(ret_intra[h] + ret_cross[h])
        state_out_ref[ks, :] = (gdec_ref[ks, :] * st[h * RET_DK:(h + 1) * RET_DK, :]
                                + kv[h * RET_DK:(h + 1) * RET_DK, h * RET_DV:(h + 1) * RET_DV])
    for h in range(MEM_HEADS):
        hs = pl.ds(h * MEM_HD, MEM_HD)
        mix_ref[:, pl.ds(RET_W + SWA_W + h * MEM_HD, MEM_HD)] = mg_ref[:, hs] * mem_o[h]


def _finish_kernel(x_ref, mix_ref, wout_ref, gain_ref, bias_ref, y_ref):
    hout = _dot(mix_ref[...].astype(BF16), wout_ref[...])
    y_ref[...] = _deepnorm_ln(x_ref[...], hout, gain_ref[...], bias_ref[...])


def _finish(x2d, mix, wout_bf, gain, bias):
    n = x2d.shape[0]
    tile = 256
    const2 = lambda a: pl.BlockSpec(a.shape, lambda i: (0, 0))
    return pl.pallas_call(
        _finish_kernel,
        grid=(n // tile,),
        in_specs=[pl.BlockSpec((tile, D_MODEL), lambda i: (i, 0)),
                  pl.BlockSpec((tile, D_MIX), lambda i: (i, 0)),
                  const2(wout_bf), const2(gain), const2(bias)],
        out_specs=pl.BlockSpec((tile, D_MODEL), lambda i: (i, 0)),
        out_shape=jax.ShapeDtypeStruct((n, D_MODEL), F32),
        compiler_params=_params(1),
        name="finish",
    )(x2d, mix, wout_bf, gain, bias)


def _rope_tables(pos):
    half = SWA_HD // 2
    inv = ROPE_THETA ** (-np.arange(half, dtype=np.float64) * 2.0 / SWA_HD)
    ang = pos.astype(np.float64)[:, None] * inv[None, :]
    cos, sin = np.cos(ang), np.sin(ang)
    reps = LANES // SWA_HD
    return (jnp.asarray(np.tile(np.concatenate([cos, cos], axis=1), (1, reps)), F32),
            jnp.asarray(np.tile(np.concatenate([-sin, sin], axis=1), (1, reps)), F32))


def _retention_tables(chunk, rows):
    lg = np.log1p(-np.exp2(-5.0 - np.arange(RET_HEADS, dtype=np.float64)))
    idx = np.arange(rows, dtype=np.float64)
    live = idx < chunk
    rel = idx[:, None] - idx[None, :]
    ok = (rel >= 0) & live[:, None] & live[None, :]
    dmat = np.where(ok[None], np.exp(np.maximum(rel, 0.0)[None] * lg[:, None, None]), 0.0)
    kdec = np.where(live[:, None], np.exp((chunk - 1.0 - idx)[:, None] * lg[None, :]), 0.0)
    qdec = np.where(live[:, None], np.exp((idx + 1.0)[:, None] * lg[None, :]), 0.0)
    g = np.exp(chunk * lg)
    kdec = np.repeat(kdec, RET_DK, axis=1)
    qdec = np.repeat(qdec, RET_DK, axis=1)
    gdec = np.broadcast_to(np.repeat(g, RET_DK)[:, None], (RET_HEADS * RET_DK, RET_DV))
    return tuple(jnp.asarray(t, F32) for t in (dmat, kdec, qdec, gdec))


def kernel(x_prompt, x_sample, state_ret, cache_swa_k, cache_swa_v, cache_mem_k, cache_mem_v,
           mem_prompt, w_in, w_mem_kv, w_out, ln_gain, ln_bias):
    depth = w_in.shape[0]
    assert depth == 1
    win_bf = w_in[0].astype(BF16)
    wmem_bf = w_mem_kv[0].astype(BF16)
    wout_bf = w_out[0].astype(BF16)
    gain = ln_gain[0].reshape(1, D_MODEL)
    bias = ln_bias[0].reshape(1, D_MODEL)

    xs = jnp.pad(x_sample, ((0, 0), (0, DEC_PAD - DEC_SEQ), (0, 0))).reshape(DEC_BATCH * DEC_PAD, D_MODEL)
    pos_s = PAST_LEN + np.arange(DEC_BATCH * DEC_PAD) % DEC_PAD
    cos_s, sin_s = _rope_tables(pos_s)
    names = [c[0] for c in _PROJ_COLS]
    ps = dict(zip(names, _project(xs, win_bf, cos_s, sin_s, DEC_BATCH * DEC_PAD,
                                  [(k, "rows", F32) for k in names])))
    n_steps = BATCH * SEQ // PERM_TILE
    per_call = n_steps // DECODE_EVERY
    assert 2 * per_call == DEC_BATCH
    decode_of = lambda first: functools.partial(
        _decode_operands, ps, state_ret[0].reshape(DEC_BATCH, RET_HEADS * RET_DK, RET_DV),
        cache_swa_k[0].transpose(0, 2, 3, 1), cache_swa_v[0].transpose(0, 2, 3, 1),
        cache_mem_k[0].reshape(DEC_BATCH, N_MEM * MEM_HEADS, MEM_HD),
        cache_mem_v[0].reshape(DEC_BATCH, N_MEM * MEM_HEADS, MEM_HD),
        _retention_tables(DEC_SEQ, DEC_PAD), first=first, count=per_call)
    decode = decode_of(0)(step_of=lambda i: i)

    xp = x_prompt.reshape(BATCH * SEQ, D_MODEL)
    cos_p, sin_p = _rope_tables(np.arange(SEQ))
    p_outs = (("rq", "rows", F32), ("rk", "rows", F32), ("rv", "rows", BF16), ("rg", "rows", BF16),
              ("sq", "perm", F32), ("sk", "cols", F32), ("sk", "perm", F32), ("sv", "cols", F32),
              ("sv", "perm", F32), ("sg", "rows", BF16), ("mq", "rows", BF16), ("mg", "rows", BF16))
    p_keys = ("rq", "rk", "rv", "rg", "sq4", "sk", "sk4", "sv", "sv4", "sg", "mq", "mg")
    *p_res, mix_s0, ret_s0 = _project(xp, win_bf, cos_p, sin_p, PERM_TILE, p_outs, decode)
    pr = dict(zip(p_keys, p_res))
    mk, mv = _memkv(mem_prompt.reshape(BATCH * N_MEM, D_MODEL), wmem_bf)
    slab = lambda a: a.reshape(SWA_W // LANES, BATCH, SEQ // PERM_TILE * PERM_ROWS, LANES)
    swa_o = _swa_prompt(slab(pr["sq4"]), slab(pr["sk4"]), slab(pr["sv4"]))
    swa_o = swa_o.reshape(SWA_W // LANES, BATCH * SEQ, LANES)
    yp, ret_p, mix_s1, ret_s1 = _mix_prompt(
        xp, pr, swa_o, mk, mv, wout_bf, _retention_tables(RET_CHUNK, RET_CHUNK), gain, bias, PERM_TILE,
        decode_of=lambda step_of: decode_of(per_call)(step_of=step_of))
    mix_s = jnp.concatenate([mix_s0, mix_s1], axis=0)
    ret_s = jnp.concatenate([ret_s0, ret_s1], axis=0)
    ys = _finish(xs, mix_s, wout_bf, gain, bias)

    take = lambda a, w: a.reshape(DEC_BATCH, DEC_PAD, w)[:, :DEC_SEQ]
    swa_rows = lambda a: a.reshape(BATCH, SWA_HEADS, SWA_HD, SEQ).transpose(0, 3, 1, 2)[None]
    return (
        yp.reshape(BATCH, SEQ, D_MODEL),
        take(ys, D_MODEL),
        ret_p.reshape(1, BATCH, RET_HEADS, RET_DK, RET_DV),
        ret_s.reshape(1, DEC_BATCH, RET_HEADS, RET_DK, RET_DV),
        swa_rows(pr["sk"]),
        swa_rows(pr["sv"]),
        take(ps["sk"], SWA_W).reshape(1, DEC_BATCH, DEC_SEQ, SWA_HEADS, SWA_HD),
        take(ps["sv"], SWA_W).reshape(1, DEC_BATCH, DEC_SEQ, SWA_HEADS, SWA_HD),
        mk.reshape(1, BATCH, N_MEM, MEM_HEADS, MEM_HD),
        mv.reshape(1, BATCH, N_MEM, MEM_HEADS, MEM_HD),
    )
```

```python
import functools

import jax
import jax.numpy as jnp
import numpy as np
from jax import lax
from jax.experimental import pallas as pl
from jax.experimental.pallas import tpu as pltpu

F32 = jnp.float32
BF16 = jnp.bfloat16

D_MODEL = 1024
BATCH = 8
SEQ = 2048
DEC_BATCH = 32
DEC_SEQ = 4
PAST_LEN = 8192
N_MEM = 256
MEM_HEADS = 4
MEM_HD = 128
RET_HEADS = 4
RET_DK = 64
RET_DV = 128
RET_CHUNK = 128
SWA_HEADS = 8
SWA_HD = 64
SWA_DILATIONS = (1, 4, 16)
SWA_STEPS = 128
SWA_BLOCK = 128
ROPE_THETA = 10000.0
LN_EPS = 1e-5
GN_EPS = 1e-5
RET_W = RET_HEADS * RET_DV
SWA_W = SWA_HEADS * SWA_HD
MEM_W = MEM_HEADS * MEM_HD
D_MIX = RET_W + SWA_W + MEM_W
DEEPNORM_ALPHA = 2.0 ** 0.25
MEM_SCALE = MEM_HD ** -0.5
LOG2_E = 1.4426950408889634
QK_SCALE = 0.125

LANES = 128
DEC_PAD = 16
PERM_TILE = 512
PERM_PITCH = 40
PERM_ROWS = 16 * PERM_PITCH
OUT_ROWS = 256
SWA_PAR = 2
MEMKV_BATCHES = 4
VMEM_LIMIT = 56 * 1024 * 1024
NEG = -1e30

_PROJ_COLS = (
    ("rq", RET_HEADS * RET_DK, "rope", 1.0),
    ("rk", RET_HEADS * RET_DK, "rope", QK_SCALE),
    ("rv", RET_W, "id", 1.0),
    ("rg", RET_W, "silu", 1.0),
    ("sq", SWA_W, "rope", QK_SCALE),
    ("sk", SWA_W, "rope", 1.0),
    ("sv", SWA_W, "id", 1.0),
    ("sg", SWA_W, "silu", 1.0),
    ("mq", MEM_W, "id", 1.0),
    ("mg", MEM_W, "silu", 1.0),
)


def _dot(a, b):
    return jnp.dot(a, b, preferred_element_type=F32)


def _dot_nt(a, b):
    return lax.dot_general(a, b, (((1,), (1,)), ((), ())), preferred_element_type=F32)


def _dot_tn(a, b):
    return lax.dot_general(a, b, (((0,), (0,)), ((), ())), preferred_element_type=F32)


def _params(n_axes):
    return pltpu.CompilerParams(dimension_semantics=("arbitrary",) * n_axes,
                                vmem_limit_bytes=VMEM_LIMIT)


def _run_interleaved(stages):
    stages = list(stages)
    while stages:
        for g in list(stages):
            if next(g, "done") == "done":
                stages.remove(g)


def _proj_kernel(*refs, dests, n_decode_in=0):
    n_out = len(dests)
    p_in, s_in = refs[:4], refs[4:4 + n_decode_in]
    p_out, s_out = refs[4 + n_decode_in:4 + n_decode_in + n_out], refs[4 + n_decode_in + n_out:]
    stages = [_proj_stages(*p_in, *p_out, dests=dests)]
    if n_decode_in:
        stages.append(_mix_sample_kernel(*s_in, *s_out))
    _run_interleaved(stages)


def _proj_stages(x_ref, w_ref, cos_ref, sin_ref, *out_refs, dests):
    xb = x_ref[...].astype(BF16)
    cos = cos_ref[...]
    sin = sin_ref[...]
    lane = lax.broadcasted_iota(jnp.int32, cos.shape, 1)
    first_half = (lane % 64) < 32
    col = 0
    for name, width, kind, scale in _PROJ_COLS:
        targets = [(o_ref, layout) for o_ref, (dname, layout) in zip(out_refs, dests) if dname == name]
        for c in range(0, width, 2 * LANES):
            h2 = _dot(xb, w_ref[:, col + c:col + c + 2 * LANES])
            for half in range(2):
                h = h2[:, half * LANES:(half + 1) * LANES]
                if kind == "rope":
                    swapped = jnp.where(first_half, pltpu.roll(h, 96, 1), pltpu.roll(h, 32, 1))
                    h = h * cos + swapped * sin
                    if scale != 1.0:
                        h = h * scale
                elif kind == "silu":
                    h = h * (1.0 / (1.0 + jnp.exp(-h)))
                lo = c + half * LANES
                for o_ref, layout in targets:
                    if layout == "rows":
                        o_ref[:, lo:lo + LANES] = h.astype(o_ref.dtype)
                    elif layout == "perm":
                        slab = lo // LANES
                        for g in range(h.shape[0] // 8):
                            base = (g % 2) * 8 * PERM_PITCH + g // 2
                            o_ref[slab, pl.ds(base, 8, stride=PERM_PITCH), :] = (
                                h[8 * g:8 * g + 8].astype(o_ref.dtype))
                        n_live = h.shape[0] // 16
                        for cls in range(16):
                            o_ref[slab, pl.ds(cls * PERM_PITCH + n_live, PERM_PITCH - n_live), :] = (
                                jnp.zeros((PERM_PITCH - n_live, LANES), o_ref.dtype))
                    else:
                        o_ref[lo:lo + LANES, :] = h.T.astype(o_ref.dtype)
        col += width
        yield


def _project(x2d, w_bf, cos_t, sin_t, tile, outs, decode=None):
    n = x2d.shape[0]
    seq = cos_t.shape[0]
    n_tab = seq // tile
    d_in = w_bf.shape[1]
    widths = {name: w for name, w, _, _ in _PROJ_COLS}
    out_shape, out_specs = [], []
    for name, layout, dt in outs:
        w = widths[name]
        if layout == "rows":
            out_shape.append(jax.ShapeDtypeStruct((n, w), dt))
            out_specs.append(pl.BlockSpec((tile, w), lambda i: (i, 0)))
        elif layout == "perm":
            assert tile == PERM_TILE
            out_shape.append(jax.ShapeDtypeStruct((w // LANES, n // tile * PERM_ROWS, LANES), dt))
            out_specs.append(pl.BlockSpec((w // LANES, PERM_ROWS, LANES), lambda i: (0, i, 0)))
        else:
            out_shape.append(jax.ShapeDtypeStruct((n // seq, w, seq), dt))
            out_specs.append(pl.BlockSpec((None, w, tile), lambda i: (i // n_tab, 0, i % n_tab)))
    d_args, d_in_specs, d_out_specs, d_out_shape = decode if decode else ((), [], [], [])
    return pl.pallas_call(
        functools.partial(_proj_kernel, dests=tuple((name, layout) for name, layout, _ in outs),
                          n_decode_in=len(d_args)),
        grid=(n // tile,),
        in_specs=[
            pl.BlockSpec((tile, D_MODEL), lambda i: (i, 0)),
            pl.BlockSpec((D_MODEL, d_in), lambda i: (0, 0)),
            pl.BlockSpec((tile, LANES), lambda i: (i % n_tab, 0)),
            pl.BlockSpec((tile, LANES), lambda i: (i % n_tab, 0)),
        ] + list(d_in_specs),
        out_specs=out_specs + list(d_out_specs),
        out_shape=out_shape + list(d_out_shape),
        compiler_params=_params(1),
        name="proj_decode" if decode else "proj",
    )(x2d, w_bf, cos_t, sin_t, *d_args)


def _decode_operands(ps, state, ckt, cvt, cmk, cmv, tabs):
    n = DEC_PAD
    wb = ckt.shape[3]
    const = lambda a: pl.BlockSpec(a.shape, lambda i: (0,) * a.ndim)
    srow = lambda w: pl.BlockSpec((n, w), lambda i: (i, 0))
    st_s = pl.BlockSpec((None, RET_HEADS * RET_DK, RET_DV), lambda i: (i, 0, 0))
    cache_s = pl.BlockSpec((None, SWA_HEADS, SWA_HD, wb), lambda i: (i, 0, 0, 0))
    mem_s = pl.BlockSpec((None, N_MEM * MEM_HEADS, MEM_HD), lambda i: (i, 0, 0))
    s_names = ("rq", "rk", "rv", "rg", "sq", "sk", "sv", "sg", "mq", "mg")
    widths = {name: w for name, w, _, _ in _PROJ_COLS}
    args = (*[ps[k] for k in s_names], state, ckt, cvt, cmk, cmv, *tabs)
    in_specs = [*[srow(widths[k]) for k in s_names], st_s, cache_s, cache_s, mem_s, mem_s,
                *[const(a) for a in tabs]]
    out_shape = [jax.ShapeDtypeStruct((DEC_BATCH * n, D_MIX), F32),
                 jax.ShapeDtypeStruct((DEC_BATCH, RET_HEADS * RET_DK, RET_DV), F32)]
    return args, in_specs, [srow(D_MIX), st_s], out_shape


def _memkv_kernel(m_ref, w_ref, mk_ref, mv_ref):
    for b in range(m_ref.shape[0] // N_MEM):
        mb = m_ref[pl.ds(b * N_MEM, N_MEM), :].astype(BF16)
        for h in range(MEM_HEADS):
            rows = pl.ds(h, N_MEM, stride=MEM_HEADS)
            mk_ref[b, rows, :] = _dot(mb, w_ref[:, h * MEM_HD:(h + 1) * MEM_HD])
            mv_ref[b, rows, :] = _dot(mb, w_ref[:, MEM_W + h * MEM_HD:MEM_W + (h + 1) * MEM_HD])


def _memkv(mem2d, w_bf):
    n = mem2d.shape[0] // N_MEM
    per = MEMKV_BATCHES
    out = pl.BlockSpec((per, N_MEM * MEM_HEADS, MEM_HD), lambda i: (i, 0, 0))
    return pl.pallas_call(
        _memkv_kernel,
        grid=(n // per,),
        in_specs=[pl.BlockSpec((per * N_MEM, D_MODEL), lambda i: (i, 0)),
                  pl.BlockSpec((D_MODEL, 2 * MEM_W), lambda i: (0, 0))],
        out_specs=[out, out],
        out_shape=[jax.ShapeDtypeStruct((n, N_MEM * MEM_HEADS, MEM_HD), F32)] * 2,
        compiler_params=_params(1),
        name="memkv",
    )(mem2d, w_bf)


def _swa_prompt_kernel(q_ref, k_ref, v_ref, o_ref, a_ref, m_ref, l_ref):
    blk = SWA_BLOCK
    pt = PERM_TILE
    grp = pt // 16
    lo = lax.broadcasted_iota(jnp.int32, (blk, blk), 1) < SWA_HD

    def biases(seq_of):
        rq = seq_of(lax.broadcasted_iota(jnp.int32, (blk, 2 * blk), 0))
        c2 = lax.broadcasted_iota(jnp.int32, (blk, 2 * blk), 1)
        rel = blk + rq - (seq_of(c2 % blk) + blk * (c2 // blk))
        prev = jnp.where((rel >= 0) & (rel <= SWA_STEPS), 0.0, NEG).astype(F32)
        r1 = seq_of(lax.broadcasted_iota(jnp.int32, (blk, blk), 0))
        c1 = seq_of(lax.broadcasted_iota(jnp.int32, (blk, blk), 1))
        return prev, jnp.where(r1 >= c1, 0.0, NEG).astype(F32)

    def gather(ref, pair, starts, n):
        return jnp.concatenate([ref[pair, pl.ds(s, n), :] for s in starts], axis=0)

    def aligned(x):
        return x if isinstance(x, int) else pl.multiple_of(x, 8)

    def blocks(jobs, n, first_pattern, last_pattern):
        n_pairs = SWA_HEADS // 2
        in_at = lambda c: aligned(c[0] * PERM_ROWS + c[1] * PERM_PITCH + c[2])
        acc_starts, q_starts, k_starts = [], [], []
        for chunks, prev_chunks, _, _ in jobs:
            acc_starts.append([aligned(c[0] * pt + c[1] * grp + c[2]) for c in chunks])
            q_starts.append([in_at(c) for c in chunks])
            k_starts.append(([] if prev_chunks is None else [in_at(c) for c in prev_chunks]) + q_starts[-1])
        items = [(j, pair) for j in range(len(jobs)) for pair in range(n_pairs)]
        olds = {}
        if not first_pattern:
            for j, pair in items:
                olds[j, pair] = (gather(m_ref, pair, acc_starts[j], n), gather(l_ref, pair, acc_starts[j], n),
                                 gather(a_ref, pair, acc_starts[j], n))
        vbs = {(j, pair): gather(v_ref, pair, k_starts[j], n).astype(BF16) for j, pair in items}
        scores = {}
        for j, pair in items:
            q = gather(q_ref, pair, q_starts[j], n)
            kb = gather(k_ref, pair, k_starts[j], n).astype(BF16)
            for hh in range(2):
                qm = jnp.where(lo if hh == 0 else ~lo, q, 0.0).astype(BF16)
                scores[j, pair, hh] = _dot_nt(qm, kb) + jobs[j][2]
        ms = {key: jnp.max(s, axis=-1, keepdims=True) for key, s in scores.items()}
        ps = {key: jnp.exp(s - ms[key]) for key, s in scores.items()}
        def v_ext(v):
            lane = lax.broadcasted_iota(jnp.int32, v.shape, 1)
            own = lane < SWA_HD
            zero = jnp.zeros((), BF16)
            ones_lo = jnp.where(lane < SWA_HD, 1.0, 0.0).astype(BF16)
            ones_hi = jnp.where(lane < SWA_HD, 0.0, 1.0).astype(BF16)
            return jnp.concatenate(
                [jnp.concatenate([jnp.where(own, v, zero), ones_lo], axis=1),
                 jnp.concatenate([jnp.where(own, zero, v), ones_hi], axis=1)], axis=0)

        pvs = {(j, pair): _dot(jnp.concatenate([ps[j, pair, 0].astype(BF16), ps[j, pair, 1].astype(BF16)], axis=1),
                               v_ext(vbs[j, pair])) for j, pair in items}
        news = {}
        for j, pair in items:
            m_g = jnp.where(lo, ms[j, pair, 0], ms[j, pair, 1])
            a_g = pvs[j, pair][:, :LANES]
            l_g = pvs[j, pair][:, LANES:]
            if first_pattern:
                news[j, pair] = (m_g, l_g, a_g)
            else:
                m_old, l_old, a_old = olds[j, pair]
                m_new = jnp.maximum(m_old, m_g)
                w_old = jnp.exp(m_old - m_new)
                w_g = jnp.exp(m_g - m_new)
                news[j, pair] = (m_new, w_old * l_old + w_g * l_g, w_old * a_old + w_g * a_g)
        for (j, pair), (m_new, l_new, a_new) in news.items():
            starts = acc_starts[j]
            if last_pattern:
                o = a_new / l_new
                for i in range(len(starts)):
                    o_ref[pair, pl.ds(jobs[j][3] + i, n, stride=len(starts)), :] = o[i * n:(i + 1) * n]
            else:
                for i, s in enumerate(starts):
                    m_ref[pair, pl.ds(s, n), :] = m_new[i * n:(i + 1) * n]
                    l_ref[pair, pl.ds(s, n), :] = l_new[i * n:(i + 1) * n]
                    a_ref[pair, pl.ds(s, n), :] = a_new[i * n:(i + 1) * n]

    par = SWA_PAR
    _, bias16 = biases(lambda i: i)

    par16 = 2 * par

    def class16(r0, carry):
        blocks([([(t, r0 + k * (16 // par16), 0) for t in range(SEQ // pt)], None, bias16, None)
                for k in range(par16)], grp, True, False)
        return carry

    lax.fori_loop(0, 16 // par16, class16, 0)

    bias4_prev, bias4_own = biases(lambda i: 4 * (i % grp) + i // grp)

    def class4(r0, carry):
        chunks_of = lambda r, tile: [(tile, 4 * a + r, 0) for a in range(4)]
        classes = [r0 + k * (4 // par) for k in range(par)]
        blocks([(chunks_of(r, 0), None, bias4_own, None) for r in classes], grp, False, False)

        def per_block(b, c2):
            blocks([(chunks_of(r, b), chunks_of(r, b - 1), bias4_prev, None) for r in classes],
                   grp, False, False)
            return c2

        lax.fori_loop(1, SEQ // pt, per_block, 0)
        return carry

    lax.fori_loop(0, 4 // par, class4, 0)

    bias1_prev, bias1_own = biases(lambda i: 16 * (i % 8) + i // 8)
    per_tile = pt // blk
    chunks1 = lambda c: [(c // per_tile, i, (c % per_tile) * 8) for i in range(16)]
    job1 = lambda c: (chunks1(c), chunks1(c - 1), bias1_prev, aligned(c * blk))
    blocks([(chunks1(0), None, bias1_own, 0)] + [job1(c) for c in range(1, par)], 8, False, True)

    def block1(g, carry):
        blocks([job1(g * par + k) for k in range(par)], 8, False, True)
        return carry

    lax.fori_loop(1, SEQ // blk // par, block1, 0)


def _swa_prompt(sq, sk, sv):
    slabs = SWA_W // LANES
    spec = pl.BlockSpec((slabs, None, SEQ, LANES), lambda b: (0, b, 0, 0))
    in_spec = pl.BlockSpec((slabs, None, sq.shape[2], LANES), lambda b: (0, b, 0, 0))
    return pl.pallas_call(
        _swa_prompt_kernel,
        grid=(BATCH,),
        in_specs=[in_spec, in_spec, in_spec],
        out_specs=spec,
        out_shape=jax.ShapeDtypeStruct((slabs, BATCH, SEQ, LANES), F32),
        scratch_shapes=[pltpu.VMEM((slabs, SEQ, LANES), F32)] * 3,
        compiler_params=_params(1),
        name="swa_prompt",
    )(sq, sk, sv)


def _head_norm(o):
    mu = jnp.mean(o, axis=-1, keepdims=True)
    d = o - mu
    var = jnp.mean(d * d, axis=-1, keepdims=True)
    return d * lax.rsqrt(var + GN_EPS)


def _deepnorm_ln(x, h, gain, bias):
    z = DEEPNORM_ALPHA * x + h
    mu = jnp.mean(z, axis=-1, keepdims=True)
    d = z - mu
    var = jnp.mean(d * d, axis=-1, keepdims=True)
    return d * lax.rsqrt(var + LN_EPS) * gain + bias


def _softmax_rows(s):
    m = jnp.max(s, axis=-1, keepdims=True)
    p = jnp.exp(s - m)
    return p * (1.0 / jnp.sum(p, axis=-1, keepdims=True))


def _mix_prompt_kernel(x_ref, rq_ref, rk_ref, rv_ref, rg_ref, so_ref, sg_ref, mq_ref, mg_ref,
                       mk_ref, mv_ref, wout_ref, dmat_ref, kdec_ref, qdec_ref, gdec_ref,
                       gain_ref, bias_ref, y_ref, state_out_ref,
                       state_ref, mix_ref, mkb_ref, mvb_ref, *, tile):
    t = pl.program_id(1)

    @pl.when(t == 0)
    def _():
        state_ref[...] = jnp.zeros_like(state_ref)
        for h in range(MEM_HEADS):
            rows = pl.ds(h, N_MEM, stride=MEM_HEADS)
            mkb_ref[:, pl.ds(h * MEM_HD, MEM_HD)] = mk_ref[rows, :].astype(BF16)
            mvb_ref[:, pl.ds(h * MEM_HD, MEM_HD)] = mv_ref[rows, :].astype(BF16)

    ck = RET_CHUNK
    lane = lax.broadcasted_iota(jnp.int32, (ck, LANES), 1)
    lo = lane < RET_DK
    top = lax.broadcasted_iota(jnp.int32, (LANES, LANES), 0) < RET_DK
    gain = gain_ref[...]
    bias = bias_ref[...]

    n_ck = tile // ck
    n_pairs = RET_HEADS // 2
    rows_of = [pl.ds(c * ck, ck) for c in range(n_ck)]
    items = [(c, pair, hh) for c in range(n_ck) for pair in range(n_pairs) for hh in range(2)]
    hs_of = lambda pair, hh: pl.ds((2 * pair + hh) * RET_DV, RET_DV)
    sel = lambda x, hh: jnp.where(lo if hh == 0 else ~lo, x, 0.0).astype(BF16)
    pairs = [(c, pair) for c in range(n_ck) for pair in range(n_pairs)]
    qk = {}
    for c, pair in pairs:
        cs = pl.ds(pair * LANES, LANES)
        q = rq_ref[rows_of[c], cs]
        k = rk_ref[rows_of[c], cs]
        qk[c, pair] = (q.astype(BF16), jnp.concatenate([sel(k, 0), sel(k, 1)], axis=0),
                       (k * kdec_ref[:, cs]).astype(BF16), q * qdec_ref[:, cs])
    v_pair = lambda c, pair: rv_ref[rows_of[c], pl.ds(pair * 2 * RET_DV, 2 * RET_DV)]
    s2 = {key: _dot_nt(qk[key][0], qk[key][1]) for key in pairs}
    kv2 = {key: _dot_tn(qk[key][2], v_pair(*key)) for key in pairs}
    yield
    state = {(0, pair): state_ref[pair] for pair in range(n_pairs)}
    for c, pair in pairs:
        state[c + 1, pair] = (gdec_ref[pl.ds(pair * LANES, LANES), :] * state[c, pair]
                              + jnp.where(top, kv2[c, pair][:, :RET_DV], kv2[c, pair][:, RET_DV:]))
    for pair in range(n_pairs):
        state_ref[pair] = state[n_ck, pair]
    o = {}
    for c, pair, hh in items:
        sh = (s2[c, pair][:, hh * ck:(hh + 1) * ck] * dmat_ref[2 * pair + hh]).astype(BF16)
        lhs = jnp.concatenate([sh, sel(qk[c, pair][3], hh)], axis=1)
        rhs = jnp.concatenate([rv_ref[rows_of[c], hs_of(pair, hh)], state[c, pair].astype(BF16)], axis=0)
        o[c, pair, hh] = _dot(lhs, rhs)
    yield
    for c, pair, hh in items:
        hs = hs_of(pair, hh)
        mix_ref[rows_of[c], hs] = (rg_ref[rows_of[c], hs].astype(F32) * _head_norm(o[c, pair, hh])).astype(BF16)
    for pair in range(SWA_W // LANES):
        cs = pl.ds(pair * LANES, LANES)
        mix_ref[:, pl.ds(RET_W + pair * LANES, LANES)] = (
            sg_ref[:, cs].astype(F32) * so_ref[pair]).astype(BF16)
    for c in range(n_ck):
        yield
        ss = [_dot_nt(mq_ref[rows_of[c], pl.ds(h * MEM_HD, MEM_HD)], mkb_ref[:, pl.ds(h * MEM_HD, MEM_HD)])
              for h in range(MEM_HEADS)]
        ps = [jnp.exp2((s - jnp.max(s, axis=-1, keepdims=True)) * (MEM_SCALE * LOG2_E)) for s in ss]
        for h in range(MEM_HEADS):
            hs = pl.ds(h * MEM_HD, MEM_HD)
            o = _dot(ps[h].astype(BF16), mvb_ref[:, hs]) * (1.0 / jnp.sum(ps[h], axis=-1, keepdims=True))
            mix_ref[rows_of[c], pl.ds(RET_W + SWA_W + h * MEM_HD, MEM_HD)] = (
                mg_ref[rows_of[c], hs].astype(F32) * o).astype(BF16)
    yield
    blocks = [pl.ds(r * OUT_ROWS, OUT_ROWS) for r in range(tile // OUT_ROWS)]
    hout = _dot(mix_ref[blocks[0], :], wout_ref[...])
    for r, rows in enumerate(blocks):
        nxt = _dot(mix_ref[blocks[r + 1], :], wout_ref[...]) if r + 1 < len(blocks) else None
        y_ref[rows, :] = _deepnorm_ln(x_ref[rows, :], hout, gain, bias)
        hout = nxt

    @pl.when(t == pl.num_programs(1) - 1)
    def _():
        state_out_ref[pl.ds(0, LANES), :] = state_ref[0]
        state_out_ref[pl.ds(LANES, LANES), :] = state_ref[1]


def _mix_prompt_body(*refs, tile):
    _run_interleaved([_mix_prompt_kernel(*refs, tile=tile)])


def _mix_prompt(x2d, pr, swa_o, mk, mv, wout_bf, tabs_p, gain, bias, tile):
    nt = SEQ // tile
    step = lambda b, t: b * nt + t
    row = lambda w: pl.BlockSpec((tile, w), lambda b, t: (step(b, t), 0))
    const = lambda a: pl.BlockSpec(a.shape, lambda b, t: (0,) * a.ndim)
    mem_p = pl.BlockSpec((None, N_MEM * MEM_HEADS, MEM_HD), lambda b, t: (b, 0, 0))
    st_p = pl.BlockSpec((None, RET_HEADS * RET_DK, RET_DV), lambda b, t: (b, 0, 0))
    p_args = (x2d, pr["rq"], pr["rk"], pr["rv"], pr["rg"], swa_o, pr["sg"], pr["mq"], pr["mg"],
              mk, mv, wout_bf, *tabs_p, gain, bias)
    p_specs = [row(D_MODEL), row(RET_HEADS * RET_DK), row(RET_HEADS * RET_DK), row(RET_W), row(RET_W),
               pl.BlockSpec((SWA_W // LANES, tile, LANES), lambda b, t: (0, step(b, t), 0)),
               row(SWA_W), row(MEM_W), row(MEM_W), mem_p, mem_p, const(wout_bf),
               *[const(a) for a in tabs_p], const(gain), const(bias)]
    return pl.pallas_call(
        functools.partial(_mix_prompt_body, tile=tile),
        grid=(BATCH, nt),
        in_specs=p_specs,
        out_specs=[row(D_MODEL), st_p],
        out_shape=[jax.ShapeDtypeStruct((BATCH * SEQ, D_MODEL), F32),
                   jax.ShapeDtypeStruct((BATCH, RET_HEADS * RET_DK, RET_DV), F32)],
        scratch_shapes=[pltpu.VMEM((RET_HEADS // 2, LANES, RET_DV), F32),
                        pltpu.VMEM((tile, D_MIX), BF16),
                        pltpu.VMEM((N_MEM, MEM_W), BF16),
                        pltpu.VMEM((N_MEM, MEM_W), BF16)],
        compiler_params=_params(2),
        name="mix_prompt",
    )(*p_args)


def _mix_sample_kernel(rq_ref, rk_ref, rv_ref, rg_ref, sq_ref, sk_ref, sv_ref, sg_ref, mq_ref, mg_ref,
                       state_ref, ckt_ref, cvt_ref, cmk_ref, cmv_ref,
                       dmat_ref, kdec_ref, qdec_ref, gdec_ref,
                       mix_ref, state_out_ref):
    n = DEC_PAD
    q = rq_ref[...]
    k = rk_ref[...]
    kb = k.astype(BF16)
    kd = (k * kdec_ref[...]).astype(BF16)
    qd = q * qdec_ref[...]
    st = state_ref[...]
    stb = st.astype(BF16)
    vb = rv_ref[...].astype(BF16)
    kv = _dot_tn(kd, vb)
    lane_qk = lax.broadcasted_iota(jnp.int32, q.shape, 1) // RET_DK
    ret_s = [_dot_nt(jnp.where(lane_qk == h, q, 0.0).astype(BF16), kb) for h in range(RET_HEADS)]
    ret_cross = [_dot(jnp.where(lane_qk == h, qd, 0.0).astype(BF16), stb) for h in range(RET_HEADS)]
    mem_rows = [pl.ds(h, N_MEM, stride=MEM_HEADS) for h in range(MEM_HEADS)]
    mem_s = [_dot_nt(mq_ref[:, pl.ds(h * MEM_HD, MEM_HD)].astype(BF16), cmk_ref[mem_rows[h], :].astype(BF16))
             for h in range(MEM_HEADS)]

    blk = SWA_BLOCK
    wb = ckt_ref.shape[2]
    rt = 8
    lo = lax.broadcasted_iota(jnp.int32, (n, LANES), 1) < SWA_HD
    lo_t = lax.broadcasted_iota(jnp.int32, (rt, LANES), 1) < SWA_HD
    tok = lambda w: lax.broadcasted_iota(jnp.int32, (rt, w), 0)
    col = lambda w: lax.broadcasted_iota(jnp.int32, (rt, w), 1)
    windows = (
        (wb - blk, jnp.where(col(blk) >= tok(blk), 0.0, NEG), jnp.where(col(blk) <= tok(blk), 0.0, NEG)),
        (wb - 4 * blk, jnp.where(col(4 * blk) % 4 == tok(4 * blk), 0.0, NEG),
         jnp.where(col(blk) == tok(blk), 0.0, NEG)),
        (0, jnp.where(col(wb) % 16 == tok(wb), 0.0, NEG), jnp.where(col(blk) == tok(blk), 0.0, NEG)),
    )
    pad = jnp.zeros((blk - n, LANES), BF16)
    zero_c = jnp.zeros((rt, wb), F32)
    zero_n = jnp.zeros((rt, blk), F32)
    n_pairs = SWA_HEADS // 2
    n_win = len(windows)
    pair_cols = [pl.ds(pair * LANES, LANES) for pair in range(n_pairs)]
    vps = [cvt_ref[2 * pair:2 * pair + 2].reshape(2 * SWA_HD, wb).astype(BF16) for pair in range(n_pairs)]
    v_news = [jnp.concatenate([sv_ref[:, cs].astype(BF16), pad], axis=0) for cs in pair_cols]
    s_alls, sn_alls = [], []
    for pair, cs in enumerate(pair_cols):
        kp = ckt_ref[2 * pair:2 * pair + 2].reshape(2 * SWA_HD, wb).astype(BF16)
        k_new = jnp.concatenate([sk_ref[:, cs].astype(BF16), pad], axis=0)
        q = sq_ref[:, cs]
        qs = jnp.concatenate([jnp.where(lo, q, 0.0), jnp.where(lo, 0.0, q)], axis=0).astype(BF16)
        s_alls.append(_dot(qs, kp))
        sn_alls.append(_dot_nt(qs, k_new))
    yield
    p_rows = [[] for _ in range(n_pairs)]
    pn_rows = [[] for _ in range(n_pairs)]
    stats = [[] for _ in range(n_pairs)]
    for pair in range(n_pairs):
        for hh in range(2):
            s = s_alls[pair][hh * n:hh * n + rt]
            sn = sn_alls[pair][hh * n:hh * n + rt]
            for w0, bias_c, bias_n in windows:
                sc = s[:, w0:] + bias_c
                snb = sn + bias_n
                m = jnp.maximum(jnp.max(sc, axis=-1, keepdims=True), jnp.max(snb, axis=-1, keepdims=True))
                pc = jnp.exp(sc - m)
                pn = jnp.exp(snb - m)
                stats[pair].append((m, jnp.sum(pc, axis=-1, keepdims=True) + jnp.sum(pn, axis=-1, keepdims=True)))
                if w0:
                    pc = jnp.concatenate([jnp.zeros((rt, w0), F32), pc], axis=1)
                p_rows[pair].append(pc)
                pn_rows[pair].append(pn)
            p_rows[pair].append(zero_c)
            pn_rows[pair].append(zero_n)
    ret_sb = [(ret_s[h] * dmat_ref[h]).astype(BF16) for h in range(RET_HEADS)]
    mem_p = [_softmax_rows(s * MEM_SCALE).astype(BF16) for s in mem_s]
    yield
    pvs = [_dot_nt(jnp.concatenate(p_rows[pair], axis=0).astype(BF16), vps[pair])
           + _dot(jnp.concatenate(pn_rows[pair], axis=0).astype(BF16), v_news[pair])
           for pair in range(n_pairs)]
    ret_intra = [_dot(ret_sb[h], vb[:, h * RET_DV:(h + 1) * RET_DV]) for h in range(RET_HEADS)]
    mem_o = [_dot(mem_p[h], cmv_ref[mem_rows[h], :].astype(BF16)) for h in range(MEM_HEADS)]
    yield
    pieces = []
    for pair in range(n_pairs):
        heads = []
        for hh in range(2):
            parts = []
            for i in range(n_win):
                m, l = stats[pair][hh * n_win + i]
                r0 = (hh * (n_win + 1) + i) * rt
                parts.append((pvs[pair][r0:r0 + rt] / l, m, l))
            m_all = jnp.maximum(jnp.maximum(parts[0][1], parts[1][1]), parts[2][1])
            ws = [l * jnp.exp(m - m_all) for (_, m, l) in parts]
            heads.append((ws[0] * parts[0][0] + ws[1] * parts[1][0] + ws[2] * parts[2][0])
                         / (ws[0] + ws[1] + ws[2]))
        pieces.append(jnp.where(lo_t, heads[0], heads[1]))
    swa = jnp.concatenate(pieces, axis=1)
    swa = jnp.concatenate([swa, jnp.zeros((n - rt, SWA_W), F32)], axis=0)
    mix_ref[:, pl.ds(RET_W, SWA_W)] = sg_ref[...] * swa
    for h in range(RET_HEADS):
        hs = pl.ds(h * RET_DV, RET_DV)
        ks = pl.ds(h * RET_DK, RET_DK)
        mix_ref[:, hs] = rg_ref[:, hs] * _head_norm(ret_intra[h] + ret_cross[h])
        state_out_ref[ks, :] = (gdec_ref[ks, :] * st[h * RET_DK:(h + 1) * RET_DK, :]
                                + kv[h * RET_DK:(h + 1) * RET_DK, h * RET_DV:(h + 1) * RET_DV])
    for h in range(MEM_HEADS):
        hs = pl.ds(h * MEM_HD, MEM_HD)
        mix_ref[:, pl.ds(RET_W + SWA_W + h * MEM_HD, MEM_HD)] = mg_ref[:, hs] * mem_o[h]


def _finish_kernel(x_ref, mix_ref, wout_ref, gain_ref, bias_ref, y_ref):
    hout = _dot(mix_ref[...].astype(BF16), wout_ref[...])
    y_ref[...] = _deepnorm_ln(x_ref[...], hout, gain_ref[...], bias_ref[...])


def _finish(x2d, mix, wout_bf, gain, bias):
    n = x2d.shape[0]
    tile = 256
    const2 = lambda a: pl.BlockSpec(a.shape, lambda i: (0, 0))
    return pl.pallas_call(
        _finish_kernel,
        grid=(n // tile,),
        in_specs=[pl.BlockSpec((tile, D_MODEL), lambda i: (i, 0)),
                  pl.BlockSpec((tile, D_MIX), lambda i: (i, 0)),
                  const2(wout_bf), const2(gain), const2(bias)],
        out_specs=pl.BlockSpec((tile, D_MODEL), lambda i: (i, 0)),
        out_shape=jax.ShapeDtypeStruct((n, D_MODEL), F32),
        compiler_params=_params(1),
        name="finish",
    )(x2d, mix, wout_bf, gain, bias)


def _rope_tables(pos):
    half = SWA_HD // 2
    inv = ROPE_THETA ** (-np.arange(half, dtype=np.float64) * 2.0 / SWA_HD)
    ang = pos.astype(np.float64)[:, None] * inv[None, :]
    cos, sin = np.cos(ang), np.sin(ang)
    reps = LANES // SWA_HD
    return (jnp.asarray(np.tile(np.concatenate([cos, cos], axis=1), (1, reps)), F32),
            jnp.asarray(np.tile(np.concatenate([-sin, sin], axis=1), (1, reps)), F32))


def _retention_tables(chunk, rows):
    lg = np.log1p(-np.exp2(-5.0 - np.arange(RET_HEADS, dtype=np.float64)))
    idx = np.arange(rows, dtype=np.float64)
    live = idx < chunk
    rel = idx[:, None] - idx[None, :]
    ok = (rel >= 0) & live[:, None] & live[None, :]
    dmat = np.where(ok[None], np.exp(np.maximum(rel, 0.0)[None] * lg[:, None, None]), 0.0)
    kdec = np.where(live[:, None], np.exp((chunk - 1.0 - idx)[:, None] * lg[None, :]), 0.0)
    qdec = np.where(live[:, None], np.exp((idx + 1.0)[:, None] * lg[None, :]), 0.0)
    g = np.exp(chunk * lg)
    kdec = np.repeat(kdec, RET_DK, axis=1)
    qdec = np.repeat(qdec, RET_DK, axis=1)
    gdec = np.broadcast_to(np.repeat(g, RET_DK)[:, None], (RET_HEADS * RET_DK, RET_DV))
    return tuple(jnp.asarray(t, F32) for t in (dmat, kdec, qdec, gdec))


def kernel(x_prompt, x_sample, state_ret, cache_swa_k, cache_swa_v, cache_mem_k, cache_mem_v,
           mem_prompt, w_in, w_mem_kv, w_out, ln_gain, ln_bias):
    depth = w_in.shape[0]
    assert depth == 1
    win_bf = w_in[0].astype(BF16)
    wmem_bf = w_mem_kv[0].astype(BF16)
    wout_bf = w_out[0].astype(BF16)
    gain = ln_gain[0].reshape(1, D_MODEL)
    bias = ln_bias[0].reshape(1, D_MODEL)

    xs = jnp.pad(x_sample, ((0, 0), (0, DEC_PAD - DEC_SEQ), (0, 0))).reshape(DEC_BATCH * DEC_PAD, D_MODEL)
    pos_s = PAST_LEN + np.arange(DEC_BATCH * DEC_PAD) % DEC_PAD
    cos_s, sin_s = _rope_tables(pos_s)
    names = [c[0] for c in _PROJ_COLS]
    ps = dict(zip(names, _project(xs, win_bf, cos_s, sin_s, DEC_BATCH * DEC_PAD,
                                  [(k, "rows", F32) for k in names])))
    decode = _decode_operands(
        ps, state_ret[0].reshape(DEC_BATCH, RET_HEADS * RET_DK, RET_DV),
        cache_swa_k[0].transpose(0, 2, 3, 1), cache_swa_v[0].transpose(0, 2, 3, 1),
        cache_mem_k[0].reshape(DEC_BATCH, N_MEM * MEM_HEADS, MEM_HD),
        cache_mem_v[0].reshape(DEC_BATCH, N_MEM * MEM_HEADS, MEM_HD),
        _retention_tables(DEC_SEQ, DEC_PAD))

    xp = x_prompt.reshape(BATCH * SEQ, D_MODEL)
    assert xp.shape[0] // PERM_TILE == DEC_BATCH
    cos_p, sin_p = _rope_tables(np.arange(SEQ))
    p_outs = (("rq", "rows", F32), ("rk", "rows", F32), ("rv", "rows", BF16), ("rg", "rows", BF16),
              ("sq", "perm", F32), ("sk", "cols", F32), ("sk", "perm", F32), ("sv", "cols", F32),
              ("sv", "perm", F32), ("sg", "rows", BF16), ("mq", "rows", BF16), ("mg", "rows", BF16))
    p_keys = ("rq", "rk", "rv", "rg", "sq4", "sk", "sk4", "sv", "sv4", "sg", "mq", "mg")
    *p_res, mix_s, ret_s = _project(xp, win_bf, cos_p, sin_p, PERM_TILE, p_outs, decode)
    pr = dict(zip(p_keys, p_res))
    mk, mv = _memkv(mem_prompt.reshape(BATCH * N_MEM, D_MODEL), wmem_bf)
    slab = lambda a: a.reshape(SWA_W // LANES, BATCH, SEQ // PERM_TILE * PERM_ROWS, LANES)
    swa_o = _swa_prompt(slab(pr["sq4"]), slab(pr["sk4"]), slab(pr["sv4"]))
    swa_o = swa_o.reshape(SWA_W // LANES, BATCH * SEQ, LANES)
    yp, ret_p = _mix_prompt(xp, pr, swa_o, mk, mv, wout_bf, _retention_tables(RET_CHUNK, RET_CHUNK),
                            gain, bias, 512)
    ys = _finish(xs, mix_s, wout_bf, gain, bias)

    take = lambda a, w: a.reshape(DEC_BATCH, DEC_PAD, w)[:, :DEC_SEQ]
    swa_rows = lambda a: a.reshape(BATCH, SWA_HEADS, SWA_HD, SEQ).transpose(0, 3, 1, 2)[None]
    return (
        yp.reshape(BATCH, SEQ, D_MODEL),
        take(ys, D_MODEL),
        ret_p.reshape(1, BATCH, RET_HEADS, RET_DK, RET_DV),
        ret_s.reshape(1, DEC_BATCH, RET_HEADS, RET_DK, RET_DV),
        swa_rows(pr["sk"]),
        swa_rows(pr["sv"]),
        take(ps["sk"], SWA_W).reshape(1, DEC_BATCH, DEC_SEQ, SWA_HEADS, SWA_HD),
        take(ps["sv"], SWA_W).reshape(1, DEC_BATCH, DEC_SEQ, SWA_HEADS, SWA_HD),
        mk.reshape(1, BATCH, N_MEM, MEM_HEADS, MEM_HD),
        mv.reshape(1, BATCH, N_MEM, MEM_HEADS, MEM_HD),
    )
```

```python
import functools

import jax
import jax.numpy as jnp
import numpy as np
from jax import lax
from jax.experimental import pallas as pl
from jax.experimental.pallas import tpu as pltpu

F32 = jnp.float32
BF16 = jnp.bfloat16

D_MODEL = 1024
BATCH = 8
SEQ = 2048
DEC_BATCH = 32
DEC_SEQ = 4
PAST_LEN = 8192
N_MEM = 256
MEM_HEADS = 4
MEM_HD = 128
RET_HEADS = 4
RET_DK = 64
RET_DV = 128
RET_CHUNK = 128
SWA_HEADS = 8
SWA_HD = 64
SWA_DILATIONS = (1, 4, 16)
SWA_STEPS = 128
SWA_BLOCK = 128
ROPE_THETA = 10000.0
LN_EPS = 1e-5
GN_EPS = 1e-5
RET_W = RET_HEADS * RET_DV
SWA_W = SWA_HEADS * SWA_HD
MEM_W = MEM_HEADS * MEM_HD
D_MIX = RET_W + SWA_W + MEM_W
DEEPNORM_ALPHA = 2.0 ** 0.25
MEM_SCALE = MEM_HD ** -0.5
LOG2_E = 1.4426950408889634
QK_SCALE = 0.125

LANES = 128
DEC_PAD = 16
PERM_TILE = 512
PERM_PITCH = 40
PERM_ROWS = 16 * PERM_PITCH
OUT_ROWS = 256
SWA_PAR = 4
MEMKV_BATCHES = 4
VMEM_LIMIT = 56 * 1024 * 1024
NEG = -1e30

_PROJ_COLS = (
    ("rq", RET_HEADS * RET_DK, "rope", 1.0),
    ("rk", RET_HEADS * RET_DK, "rope", QK_SCALE),
    ("rv", RET_W, "id", 1.0),
    ("rg", RET_W, "silu", 1.0),
    ("sq", SWA_W, "rope", QK_SCALE),
    ("sk", SWA_W, "rope", 1.0),
    ("sv", SWA_W, "id", 1.0),
    ("sg", SWA_W, "silu", 1.0),
    ("mq", MEM_W, "id", 1.0),
    ("mg", MEM_W, "silu", 1.0),
)


def _dot(a, b):
    return jnp.dot(a, b, preferred_element_type=F32)


def _dot_nt(a, b):
    return lax.dot_general(a, b, (((1,), (1,)), ((), ())), preferred_element_type=F32)


def _dot_tn(a, b):
    return lax.dot_general(a, b, (((0,), (0,)), ((), ())), preferred_element_type=F32)


def _params(n_axes):
    return pltpu.CompilerParams(dimension_semantics=("arbitrary",) * n_axes,
                                vmem_limit_bytes=VMEM_LIMIT)


def _run_interleaved(stages):
    stages = list(stages)
    while stages:
        for g in list(stages):
            if next(g, "done") == "done":
                stages.remove(g)


def _proj_kernel(*refs, dests, n_decode_in=0):
    n_out = len(dests)
    p_in, s_in = refs[:4], refs[4:4 + n_decode_in]
    p_out = refs[4 + n_decode_in:4 + n_decode_in + n_out]
    s_out = refs[4 + n_decode_in + n_out:4 + n_decode_in + n_out + (2 if n_decode_in else 0)]
    stages = [_proj_stages(*p_in, *p_out, dests=dests, stage_ref=refs[-1])]
    if n_decode_in:
        stages.append(_mix_sample_kernel(*s_in, *s_out))
    _run_interleaved(stages)


def _proj_stages(x_ref, w_ref, cos_ref, sin_ref, *out_refs, dests, stage_ref):
    xb = x_ref[...].astype(BF16)
    cos = cos_ref[...]
    sin = sin_ref[...]
    lane = lax.broadcasted_iota(jnp.int32, cos.shape, 1)
    first_half = (lane % 64) < 32
    col = 0
    for name, width, kind, scale in _PROJ_COLS:
        targets = [(o_ref, layout) for o_ref, (dname, layout) in zip(out_refs, dests) if dname == name]
        for c in range(0, width, 2 * LANES):
            h2 = _dot(xb, w_ref[:, col + c:col + c + 2 * LANES])
            for half in range(2):
                h = h2[:, half * LANES:(half + 1) * LANES]
                if kind == "rope":
                    swapped = jnp.where(first_half, pltpu.roll(h, 96, 1), pltpu.roll(h, 32, 1))
                    h = h * cos + swapped * sin
                    if scale != 1.0:
                        h = h * scale
                elif kind == "silu":
                    h = h * (1.0 / (1.0 + jnp.exp(-h)))
                lo = c + half * LANES
                for o_ref, layout in targets:
                    if layout == "rows":
                        o_ref[:, lo:lo + LANES] = h.astype(o_ref.dtype)
                    elif layout == "perm":
                        grp = h.shape[0] // 16
                        for g in range(h.shape[0] // 8):
                            base = (g % 2) * 8 * PERM_PITCH + g // 2
                            stage_ref[pl.ds(base, 8, stride=PERM_PITCH), :] = h[8 * g:8 * g + 8]
                        for cls in range(16):
                            o_ref[lo // LANES, pl.ds(cls * grp, grp), :] = (
                                stage_ref[pl.ds(cls * PERM_PITCH, grp), :].astype(o_ref.dtype))
                    else:
                        o_ref[lo:lo + LANES, :] = h.T.astype(o_ref.dtype)
        col += width
        yield


def _project(x2d, w_bf, cos_t, sin_t, tile, outs, decode=None):
    n = x2d.shape[0]
    seq = cos_t.shape[0]
    n_tab = seq // tile
    d_in = w_bf.shape[1]
    widths = {name: w for name, w, _, _ in _PROJ_COLS}
    out_shape, out_specs = [], []
    for name, layout, dt in outs:
        w = widths[name]
        if layout == "rows":
            out_shape.append(jax.ShapeDtypeStruct((n, w), dt))
            out_specs.append(pl.BlockSpec((tile, w), lambda i: (i, 0)))
        elif layout == "perm":
            assert tile == PERM_TILE
            out_shape.append(jax.ShapeDtypeStruct((w // LANES, n, LANES), dt))
            out_specs.append(pl.BlockSpec((w // LANES, tile, LANES), lambda i: (0, i, 0)))
        else:
            out_shape.append(jax.ShapeDtypeStruct((n // seq, w, seq), dt))
            out_specs.append(pl.BlockSpec((None, w, tile), lambda i: (i // n_tab, 0, i % n_tab)))
    d_args, d_in_specs, d_out_specs, d_out_shape = decode if decode else ((), [], [], [])
    return pl.pallas_call(
        functools.partial(_proj_kernel, dests=tuple((name, layout) for name, layout, _ in outs),
                          n_decode_in=len(d_args)),
        grid=(n // tile,),
        in_specs=[
            pl.BlockSpec((tile, D_MODEL), lambda i: (i, 0)),
            pl.BlockSpec((D_MODEL, d_in), lambda i: (0, 0)),
            pl.BlockSpec((tile, LANES), lambda i: (i % n_tab, 0)),
            pl.BlockSpec((tile, LANES), lambda i: (i % n_tab, 0)),
        ] + list(d_in_specs),
        out_specs=out_specs + list(d_out_specs),
        out_shape=out_shape + list(d_out_shape),
        scratch_shapes=[pltpu.VMEM((PERM_ROWS, LANES), F32)],
        compiler_params=_params(1),
        name="proj_decode" if decode else "proj",
    )(x2d, w_bf, cos_t, sin_t, *d_args)


def _decode_operands(ps, state, ckt, cvt, cmk, cmv, tabs):
    n = DEC_PAD
    wb = ckt.shape[3]
    const = lambda a: pl.BlockSpec(a.shape, lambda i: (0,) * a.ndim)
    srow = lambda w: pl.BlockSpec((n, w), lambda i: (i, 0))
    st_s = pl.BlockSpec((None, RET_HEADS * RET_DK, RET_DV), lambda i: (i, 0, 0))
    cache_s = pl.BlockSpec((None, SWA_HEADS, SWA_HD, wb), lambda i: (i, 0, 0, 0))
    mem_s = pl.BlockSpec((None, N_MEM * MEM_HEADS, MEM_HD), lambda i: (i, 0, 0))
    s_names = ("rq", "rk", "rv", "rg", "sq", "sk", "sv", "sg", "mq", "mg")
    widths = {name: w for name, w, _, _ in _PROJ_COLS}
    args = (*[ps[k] for k in s_names], state, ckt, cvt, cmk, cmv, *tabs)
    in_specs = [*[srow(widths[k]) for k in s_names], st_s, cache_s, cache_s, mem_s, mem_s,
                *[const(a) for a in tabs]]
    out_shape = [jax.ShapeDtypeStruct((DEC_BATCH * n, D_MIX), F32),
                 jax.ShapeDtypeStruct((DEC_BATCH, RET_HEADS * RET_DK, RET_DV), F32)]
    return args, in_specs, [srow(D_MIX), st_s], out_shape


def _memkv_kernel(m_ref, w_ref, mk_ref, mv_ref):
    for b in range(m_ref.shape[0] // N_MEM):
        mb = m_ref[pl.ds(b * N_MEM, N_MEM), :].astype(BF16)
        for h in range(MEM_HEADS):
            rows = pl.ds(h, N_MEM, stride=MEM_HEADS)
            mk_ref[b, rows, :] = _dot(mb, w_ref[:, h * MEM_HD:(h + 1) * MEM_HD])
            mv_ref[b, rows, :] = _dot(mb, w_ref[:, MEM_W + h * MEM_HD:MEM_W + (h + 1) * MEM_HD])


def _memkv(mem2d, w_bf):
    n = mem2d.shape[0] // N_MEM
    per = MEMKV_BATCHES
    out = pl.BlockSpec((per, N_MEM * MEM_HEADS, MEM_HD), lambda i: (i, 0, 0))
    return pl.pallas_call(
        _memkv_kernel,
        grid=(n // per,),
        in_specs=[pl.BlockSpec((per * N_MEM, D_MODEL), lambda i: (i, 0)),
                  pl.BlockSpec((D_MODEL, 2 * MEM_W), lambda i: (0, 0))],
        out_specs=[out, out],
        out_shape=[jax.ShapeDtypeStruct((n, N_MEM * MEM_HEADS, MEM_HD), F32)] * 2,
        compiler_params=_params(1),
        name="memkv",
    )(mem2d, w_bf)


def _swa_prompt_kernel(q_ref, k_ref, v_ref, o_ref, a_ref, m_ref, l_ref):
    blk = SWA_BLOCK
    pt = PERM_TILE
    grp = pt // 16
    lo = lax.broadcasted_iota(jnp.int32, (blk, blk), 1) < SWA_HD

    def biases(seq_of):
        rq = seq_of(lax.broadcasted_iota(jnp.int32, (blk, 2 * blk), 0))
        c2 = lax.broadcasted_iota(jnp.int32, (blk, 2 * blk), 1)
        rel = blk + rq - (seq_of(c2 % blk) + blk * (c2 // blk))
        prev = jnp.where((rel >= 0) & (rel <= SWA_STEPS), 0.0, NEG).astype(F32)
        r1 = seq_of(lax.broadcasted_iota(jnp.int32, (blk, blk), 0))
        c1 = seq_of(lax.broadcasted_iota(jnp.int32, (blk, blk), 1))
        return prev, jnp.where(r1 >= c1, 0.0, NEG).astype(F32)

    def gather(ref, pair, starts, n):
        return jnp.concatenate([ref[pair, pl.ds(s, n), :] for s in starts], axis=0)

    def aligned(x):
        return x if isinstance(x, int) else pl.multiple_of(x, 8)

    def blocks(jobs, n, first_pattern, last_pattern):
        n_pairs = SWA_HEADS // 2
        in_at = lambda c: aligned(c[0] * pt + c[1] * grp + c[2])
        acc_starts, q_starts, k_starts = [], [], []
        for chunks, prev_chunks, _, _ in jobs:
            acc_starts.append([aligned(c[0] * pt + c[1] * grp + c[2]) for c in chunks])
            q_starts.append([in_at(c) for c in chunks])
            k_starts.append(([] if prev_chunks is None else [in_at(c) for c in prev_chunks]) + q_starts[-1])
        items = [(j, pair) for j in range(len(jobs)) for pair in range(n_pairs)]
        olds = {}
        if not first_pattern:
            for j, pair in items:
                olds[j, pair] = (gather(m_ref, pair, acc_starts[j], n), gather(l_ref, pair, acc_starts[j], n),
                                 gather(a_ref, pair, acc_starts[j], n))
        vbs = {(j, pair): gather(v_ref, pair, k_starts[j], n).astype(BF16) for j, pair in items}
        scores = {}
        for j, pair in items:
            q = gather(q_ref, pair, q_starts[j], n)
            kb = gather(k_ref, pair, k_starts[j], n).astype(BF16)
            for hh in range(2):
                qm = jnp.where(lo if hh == 0 else ~lo, q, 0.0).astype(BF16)
                scores[j, pair, hh] = _dot_nt(qm, kb) + jobs[j][2]
        ms = {key: jnp.max(s, axis=-1, keepdims=True) for key, s in scores.items()}
        ps = {key: jnp.exp(s - ms[key]) for key, s in scores.items()}
        def v_ext(v):
            lane = lax.broadcasted_iota(jnp.int32, v.shape, 1)
            own = lane < SWA_HD
            zero = jnp.zeros((), BF16)
            ones_lo = jnp.where(lane < SWA_HD, 1.0, 0.0).astype(BF16)
            ones_hi = jnp.where(lane < SWA_HD, 0.0, 1.0).astype(BF16)
            return jnp.concatenate(
                [jnp.concatenate([jnp.where(own, v, zero), ones_lo], axis=1),
                 jnp.concatenate([jnp.where(own, zero, v), ones_hi], axis=1)], axis=0)

        pvs = {(j, pair): _dot(jnp.concatenate([ps[j, pair, 0].astype(BF16), ps[j, pair, 1].astype(BF16)], axis=1),
                               v_ext(vbs[j, pair])) for j, pair in items}
        news = {}
        for j, pair in items:
            m_g = jnp.where(lo, ms[j, pair, 0], ms[j, pair, 1])
            a_g = pvs[j, pair][:, :LANES]
            l_g = pvs[j, pair][:, LANES:]
            if first_pattern:
                news[j, pair] = (m_g, l_g, a_g)
            else:
                m_old, l_old, a_old = olds[j, pair]
                m_new = jnp.maximum(m_old, m_g)
                w_old = jnp.exp(m_old - m_new)
                w_g = jnp.exp(m_g - m_new)
                news[j, pair] = (m_new, w_old * l_old + w_g * l_g, w_old * a_old + w_g * a_g)
        for (j, pair), (m_new, l_new, a_new) in news.items():
            starts = acc_starts[j]
            if last_pattern:
                o = a_new / l_new
                for i in range(len(starts)):
                    o_ref[pair, pl.ds(jobs[j][3] + i, n, stride=len(starts)), :] = o[i * n:(i + 1) * n]
            else:
                for i, s in enumerate(starts):
                    m_ref[pair, pl.ds(s, n), :] = m_new[i * n:(i + 1) * n]
                    l_ref[pair, pl.ds(s, n), :] = l_new[i * n:(i + 1) * n]
                    a_ref[pair, pl.ds(s, n), :] = a_new[i * n:(i + 1) * n]

    par = SWA_PAR
    _, bias16 = biases(lambda i: i)

    par16 = 2 * par

    def class16(r0, carry):
        blocks([([(t, r0 + k * (16 // par16), 0) for t in range(SEQ // pt)], None, bias16, None)
                for k in range(par16)], grp, True, False)
        return carry

    lax.fori_loop(0, 16 // par16, class16, 0)

    bias4_prev, bias4_own = biases(lambda i: 4 * (i % grp) + i // grp)

    def class4(r0, carry):
        chunks_of = lambda r, tile: [(tile, 4 * a + r, 0) for a in range(4)]
        classes = [r0 + k * (4 // par) for k in range(par)]
        blocks([(chunks_of(r, 0), None, bias4_own, None) for r in classes], grp, False, False)

        def per_block(b, c2):
            blocks([(chunks_of(r, b), chunks_of(r, b - 1), bias4_prev, None) for r in classes],
                   grp, False, False)
            return c2

        lax.fori_loop(1, SEQ // pt, per_block, 0)
        return carry

    lax.fori_loop(0, 4 // par, class4, 0)

    bias1_prev, bias1_own = biases(lambda i: 16 * (i % 8) + i // 8)
    per_tile = pt // blk
    chunks1 = lambda c: [(c // per_tile, i, (c % per_tile) * 8) for i in range(16)]
    job1 = lambda c: (chunks1(c), chunks1(c - 1), bias1_prev, aligned(c * blk))
    blocks([(chunks1(0), None, bias1_own, 0)] + [job1(c) for c in range(1, par)], 8, False, True)

    def block1(g, carry):
        blocks([job1(g * par + k) for k in range(par)], 8, False, True)
        return carry

    lax.fori_loop(1, SEQ // blk // par, block1, 0)


def _swa_prompt(sq, sk, sv):
    slabs = SWA_W // LANES
    spec = pl.BlockSpec((slabs, None, SEQ, LANES), lambda b: (0, b, 0, 0))
    in_spec = pl.BlockSpec((slabs, None, sq.shape[2], LANES), lambda b: (0, b, 0, 0))
    return pl.pallas_call(
        _swa_prompt_kernel,
        grid=(BATCH,),
        in_specs=[in_spec, in_spec, in_spec],
        out_specs=spec,
        out_shape=jax.ShapeDtypeStruct((slabs, BATCH, SEQ, LANES), F32),
        scratch_shapes=[pltpu.VMEM((slabs, SEQ, LANES), F32)] * 3,
        compiler_params=_params(1),
        name="swa_prompt",
    )(sq, sk, sv)


def _head_norm(o):
    mu = jnp.mean(o, axis=-1, keepdims=True)
    d = o - mu
    var = jnp.mean(d * d, axis=-1, keepdims=True)
    return d * lax.rsqrt(var + GN_EPS)


def _deepnorm_ln(x, h, gain, bias):
    z = DEEPNORM_ALPHA * x + h
    mu = jnp.mean(z, axis=-1, keepdims=True)
    d = z - mu
    var = jnp.mean(d * d, axis=-1, keepdims=True)
    return d * lax.rsqrt(var + LN_EPS) * gain + bias


def _softmax_rows(s):
    m = jnp.max(s, axis=-1, keepdims=True)
    p = jnp.exp(s - m)
    return p * (1.0 / jnp.sum(p, axis=-1, keepdims=True))


def _mix_prompt_kernel(x_ref, rq_ref, rk_ref, rv_ref, rg_ref, so_ref, sg_ref, mq_ref, mg_ref,
                       mk_ref, mv_ref, wout_ref, dmat_ref, kdec_ref, qdec_ref, gdec_ref,
                       gain_ref, bias_ref, y_ref, state_out_ref,
                       state_ref, mix_ref, mkb_ref, mvb_ref, *, tile):
    t = pl.program_id(1)

    @pl.when(t == 0)
    def _():
        state_ref[...] = jnp.zeros_like(state_ref)
        for h in range(MEM_HEADS):
            rows = pl.ds(h, N_MEM, stride=MEM_HEADS)
            mkb_ref[:, pl.ds(h * MEM_HD, MEM_HD)] = mk_ref[rows, :].astype(BF16)
            mvb_ref[:, pl.ds(h * MEM_HD, MEM_HD)] = mv_ref[rows, :].astype(BF16)

    ck = RET_CHUNK
    lane = lax.broadcasted_iota(jnp.int32, (ck, LANES), 1)
    lo = lane < RET_DK
    top = lax.broadcasted_iota(jnp.int32, (LANES, LANES), 0) < RET_DK
    gain = gain_ref[...]
    bias = bias_ref[...]

    n_ck = tile // ck
    n_pairs = RET_HEADS // 2
    rows_of = [pl.ds(c * ck, ck) for c in range(n_ck)]
    items = [(c, pair, hh) for c in range(n_ck) for pair in range(n_pairs) for hh in range(2)]
    hs_of = lambda pair, hh: pl.ds((2 * pair + hh) * RET_DV, RET_DV)
    sel = lambda x, hh: jnp.where(lo if hh == 0 else ~lo, x, 0.0).astype(BF16)
    pairs = [(c, pair) for c in range(n_ck) for pair in range(n_pairs)]
    qk = {}
    for c, pair in pairs:
        cs = pl.ds(pair * LANES, LANES)
        q = rq_ref[rows_of[c], cs]
        k = rk_ref[rows_of[c], cs]
        qk[c, pair] = (q.astype(BF16), jnp.concatenate([sel(k, 0), sel(k, 1)], axis=0),
                       (k * kdec_ref[:, cs]).astype(BF16), q * qdec_ref[:, cs])
    v_pair = lambda c, pair: rv_ref[rows_of[c], pl.ds(pair * 2 * RET_DV, 2 * RET_DV)]
    s2 = {key: _dot_nt(qk[key][0], qk[key][1]) for key in pairs}
    kv2 = {key: _dot_tn(qk[key][2], v_pair(*key)) for key in pairs}
    yield
    state = {(0, pair): state_ref[pair] for pair in range(n_pairs)}
    for c, pair in pairs:
        state[c + 1, pair] = (gdec_ref[pl.ds(pair * LANES, LANES), :] * state[c, pair]
                              + jnp.where(top, kv2[c, pair][:, :RET_DV], kv2[c, pair][:, RET_DV:]))
    for pair in range(n_pairs):
        state_ref[pair] = state[n_ck, pair]
    o = {}
    for c, pair, hh in items:
        sh = (s2[c, pair][:, hh * ck:(hh + 1) * ck] * dmat_ref[2 * pair + hh]).astype(BF16)
        lhs = jnp.concatenate([sh, sel(qk[c, pair][3], hh)], axis=1)
        rhs = jnp.concatenate([rv_ref[rows_of[c], hs_of(pair, hh)], state[c, pair].astype(BF16)], axis=0)
        o[c, pair, hh] = _dot(lhs, rhs)
    yield
    for c, pair, hh in items:
        hs = hs_of(pair, hh)
        mix_ref[rows_of[c], hs] = (rg_ref[rows_of[c], hs].astype(F32) * _head_norm(o[c, pair, hh])).astype(BF16)
    for pair in range(SWA_W // LANES):
        cs = pl.ds(pair * LANES, LANES)
        mix_ref[:, pl.ds(RET_W + pair * LANES, LANES)] = (
            sg_ref[:, cs].astype(F32) * so_ref[pair]).astype(BF16)
    for c in range(n_ck):
        yield
        ss = [_dot_nt(mq_ref[rows_of[c], pl.ds(h * MEM_HD, MEM_HD)], mkb_ref[:, pl.ds(h * MEM_HD, MEM_HD)])
              for h in range(MEM_HEADS)]
        ps = [jnp.exp2((s - jnp.max(s, axis=-1, keepdims=True)) * (MEM_SCALE * LOG2_E)) for s in ss]
        for h in range(MEM_HEADS):
            hs = pl.ds(h * MEM_HD, MEM_HD)
            o = _dot(ps[h].astype(BF16), mvb_ref[:, hs]) * (1.0 / jnp.sum(ps[h], axis=-1, keepdims=True))
            mix_ref[rows_of[c], pl.ds(RET_W + SWA_W + h * MEM_HD, MEM_HD)] = (
                mg_ref[rows_of[c], hs].astype(F32) * o).astype(BF16)
    yield
    blocks = [pl.ds(r * OUT_ROWS, OUT_ROWS) for r in range(tile // OUT_ROWS)]
    hout = _dot(mix_ref[blocks[0], :], wout_ref[...])
    for r, rows in enumerate(blocks):
        nxt = _dot(mix_ref[blocks[r + 1], :], wout_ref[...]) if r + 1 < len(blocks) else None
        y_ref[rows, :] = _deepnorm_ln(x_ref[rows, :], hout, gain, bias)
        hout = nxt

    @pl.when(t == pl.num_programs(1) - 1)
    def _():
        state_out_ref[pl.ds(0, LANES), :] = state_ref[0]
        state_out_ref[pl.ds(LANES, LANES), :] = state_ref[1]


def _mix_prompt_body(*refs, tile):
    _run_interleaved([_mix_prompt_kernel(*refs, tile=tile)])


def _mix_prompt(x2d, pr, swa_o, mk, mv, wout_bf, tabs_p, gain, bias, tile):
    nt = SEQ // tile
    step = lambda b, t: b * nt + t
    row = lambda w: pl.BlockSpec((tile, w), lambda b, t: (step(b, t), 0))
    const = lambda a: pl.BlockSpec(a.shape, lambda b, t: (0,) * a.ndim)
    mem_p = pl.BlockSpec((None, N_MEM * MEM_HEADS, MEM_HD), lambda b, t: (b, 0, 0))
    st_p = pl.BlockSpec((None, RET_HEADS * RET_DK, RET_DV), lambda b, t: (b, 0, 0))
    p_args = (x2d, pr["rq"], pr["rk"], pr["rv"], pr["rg"], swa_o, pr["sg"], pr["mq"], pr["mg"],
              mk, mv, wout_bf, *tabs_p, gain, bias)
    p_specs = [row(D_MODEL), row(RET_HEADS * RET_DK), row(RET_HEADS * RET_DK), row(RET_W), row(RET_W),
               pl.BlockSpec((SWA_W // LANES, tile, LANES), lambda b, t: (0, step(b, t), 0)),
               row(SWA_W), row(MEM_W), row(MEM_W), mem_p, mem_p, const(wout_bf),
               *[const(a) for a in tabs_p], const(gain), const(bias)]
    return pl.pallas_call(
        functools.partial(_mix_prompt_body, tile=tile),
        grid=(BATCH, nt),
        in_specs=p_specs,
        out_specs=[row(D_MODEL), st_p],
        out_shape=[jax.ShapeDtypeStruct((BATCH * SEQ, D_MODEL), F32),
                   jax.ShapeDtypeStruct((BATCH, RET_HEADS * RET_DK, RET_DV), F32)],
        scratch_shapes=[pltpu.VMEM((RET_HEADS // 2, LANES, RET_DV), F32),
                        pltpu.VMEM((tile, D_MIX), BF16),
                        pltpu.VMEM((N_MEM, MEM_W), BF16),
                        pltpu.VMEM((N_MEM, MEM_W), BF16)],
        compiler_params=_params(2),
        name="mix_prompt",
    )(*p_args)


def _mix_sample_kernel(rq_ref, rk_ref, rv_ref, rg_ref, sq_ref, sk_ref, sv_ref, sg_ref, mq_ref, mg_ref,
                       state_ref, ckt_ref, cvt_ref, cmk_ref, cmv_ref,
                       dmat_ref, kdec_ref, qdec_ref, gdec_ref,
                       mix_ref, state_out_ref):
    n = DEC_PAD
    q = rq_ref[...]
    k = rk_ref[...]
    kb = k.astype(BF16)
    kd = (k * kdec_ref[...]).astype(BF16)
    qd = q * qdec_ref[...]
    st = state_ref[...]
    stb = st.astype(BF16)
    vb = rv_ref[...].astype(BF16)
    kv = _dot_tn(kd, vb)
    lane_qk = lax.broadcasted_iota(jnp.int32, q.shape, 1) // RET_DK
    ret_s = [_dot_nt(jnp.where(lane_qk == h, q, 0.0).astype(BF16), kb) for h in range(RET_HEADS)]
    ret_cross = [_dot(jnp.where(lane_qk == h, qd, 0.0).astype(BF16), stb) for h in range(RET_HEADS)]
    mem_rows = [pl.ds(h, N_MEM, stride=MEM_HEADS) for h in range(MEM_HEADS)]
    mem_s = [_dot_nt(mq_ref[:, pl.ds(h * MEM_HD, MEM_HD)].astype(BF16), cmk_ref[mem_rows[h], :].astype(BF16))
             for h in range(MEM_HEADS)]

    blk = SWA_BLOCK
    wb = ckt_ref.shape[2]
    rt = 8
    lo = lax.broadcasted_iota(jnp.int32, (n, LANES), 1) < SWA_HD
    lo_t = lax.broadcasted_iota(jnp.int32, (rt, LANES), 1) < SWA_HD
    tok = lambda w: lax.broadcasted_iota(jnp.int32, (rt, w), 0)
    col = lambda w: lax.broadcasted_iota(jnp.int32, (rt, w), 1)
    windows = (
        (wb - blk, jnp.where(col(blk) >= tok(blk), 0.0, NEG), jnp.where(col(blk) <= tok(blk), 0.0, NEG)),
        (wb - 4 * blk, jnp.where(col(4 * blk) % 4 == tok(4 * blk), 0.0, NEG),
         jnp.where(col(blk) == tok(blk), 0.0, NEG)),
        (0, jnp.where(col(wb) % 16 == tok(wb), 0.0, NEG), jnp.where(col(blk) == tok(blk), 0.0, NEG)),
    )
    pad = jnp.zeros((blk - n, LANES), BF16)
    zero_c = jnp.zeros((rt, wb), F32)
    zero_n = jnp.zeros((rt, blk), F32)
    n_pairs = SWA_HEADS // 2
    n_win = len(windows)
    pair_cols = [pl.ds(pair * LANES, LANES) for pair in range(n_pairs)]
    vps = [cvt_ref[2 * pair:2 * pair + 2].reshape(2 * SWA_HD, wb).astype(BF16) for pair in range(n_pairs)]
    v_news = [jnp.concatenate([sv_ref[:, cs].astype(BF16), pad], axis=0) for cs in pair_cols]
    s_alls, sn_alls = [], []
    for pair, cs in enumerate(pair_cols):
        kp = ckt_ref[2 * pair:2 * pair + 2].reshape(2 * SWA_HD, wb).astype(BF16)
        k_new = jnp.concatenate([sk_ref[:, cs].astype(BF16), pad], axis=0)
        q = sq_ref[:, cs]
        qs = jnp.concatenate([jnp.where(lo, q, 0.0), jnp.where(lo, 0.0, q)], axis=0).astype(BF16)
        s_alls.append(_dot(qs, kp))
        sn_alls.append(_dot_nt(qs, k_new))
    yield
    p_rows = [[] for _ in range(n_pairs)]
    pn_rows = [[] for _ in range(n_pairs)]
    stats = [[] for _ in range(n_pairs)]
    for pair in range(n_pairs):
        for hh in range(2):
            s = s_alls[pair][hh * n:hh * n + rt]
            sn = sn_alls[pair][hh * n:hh * n + rt]
            for w0, bias_c, bias_n in windows:
                sc = s[:, w0:] + bias_c
                snb = sn + bias_n
                m = jnp.maximum(jnp.max(sc, axis=-1, keepdims=True), jnp.max(snb, axis=-1, keepdims=True))
                pc = jnp.exp(sc - m)
                pn = jnp.exp(snb - m)
                stats[pair].append((m, jnp.sum(pc, axis=-1, keepdims=True) + jnp.sum(pn, axis=-1, keepdims=True)))
                if w0:
                    pc = jnp.concatenate([jnp.zeros((rt, w0), F32), pc], axis=1)
                p_rows[pair].append(pc)
                pn_rows[pair].append(pn)
            p_rows[pair].append(zero_c)
            pn_rows[pair].append(zero_n)
    ret_sb = [(ret_s[h] * dmat_ref[h]).astype(BF16) for h in range(RET_HEADS)]
    mem_p = [_softmax_rows(s * MEM_SCALE).astype(BF16) for s in mem_s]
    yield
    pvs = [_dot_nt(jnp.concatenate(p_rows[pair], axis=0).astype(BF16), vps[pair])
           + _dot(jnp.concatenate(pn_rows[pair], axis=0).astype(BF16), v_news[pair])
           for pair in range(n_pairs)]
    ret_intra = [_dot(ret_sb[h], vb[:, h * RET_DV:(h + 1) * RET_DV]) for h in range(RET_HEADS)]
    mem_o = [_dot(mem_p[h], cmv_ref[mem_rows[h], :].astype(BF16)) for h in range(MEM_HEADS)]
    yield
    pieces = []
    for pair in range(n_pairs):
        heads = []
        for hh in range(2):
            parts = []
            for i in range(n_win):
                m, l = stats[pair][hh * n_win + i]
                r0 = (hh * (n_win + 1) + i) * rt
                parts.append((pvs[pair][r0:r0 + rt] / l, m, l))
            m_all = jnp.maximum(jnp.maximum(parts[0][1], parts[1][1]), parts[2][1])
            ws = [l * jnp.exp(m - m_all) for (_, m, l) in parts]
            heads.append((ws[0] * parts[0][0] + ws[1] * parts[1][0] + ws[2] * parts[2][0])
                         / (ws[0] + ws[1] + ws[2]))
        pieces.append(jnp.where(lo_t, heads[0], heads[1]))
    swa = jnp.concatenate(pieces, axis=1)
    swa = jnp.concatenate([swa, jnp.zeros((n - rt, SWA_W), F32)], axis=0)
    mix_ref[:, pl.ds(RET_W, SWA_W)] = sg_ref[...] * swa
    for h in range(RET_HEADS):
        hs = pl.ds(h * RET_DV, RET_DV)
        ks = pl.ds(h * RET_DK, RET_DK)
        mix_ref[:, hs] = rg_ref[:, hs] * _head_norm(ret_intra[h] + ret_cross[h])
        state_out_ref[ks, :] = (gdec_ref[ks, :] * st[h * RET_DK:(h + 1) * RET_DK, :]
                                + kv[h * RET_DK:(h + 1) * RET_DK, h * RET_DV:(h + 1) * RET_DV])
    for h in range(MEM_HEADS):
        hs = pl.ds(h * MEM_HD, MEM_HD)
        mix_ref[:, pl.ds(RET_W + SWA_W + h * MEM_HD, MEM_HD)] = mg_ref[:, hs] * mem_o[h]


def _finish_kernel(x_ref, mix_ref, wout_ref, gain_ref, bias_ref, y_ref):
    hout = _dot(mix_ref[...].astype(BF16), wout_ref[...])
    y_ref[...] = _deepnorm_ln(x_ref[...], hout, gain_ref[...], bias_ref[...])


def _finish(x2d, mix, wout_bf, gain, bias):
    n = x2d.shape[0]
    tile = 256
    const2 = lambda a: pl.BlockSpec(a.shape, lambda i: (0, 0))
    return pl.pallas_call(
        _finish_kernel,
        grid=(n // tile,),
        in_specs=[pl.BlockSpec((tile, D_MODEL), lambda i: (i, 0)),
                  pl.BlockSpec((tile, D_MIX), lambda i: (i, 0)),
                  const2(wout_bf), const2(gain), const2(bias)],
        out_specs=pl.BlockSpec((tile, D_MODEL), lambda i: (i, 0)),
        out_shape=jax.ShapeDtypeStruct((n, D_MODEL), F32),
        compiler_params=_params(1),
        name="finish",
    )(x2d, mix, wout_bf, gain, bias)


def _rope_tables(pos):
    half = SWA_HD // 2
    inv = ROPE_THETA ** (-np.arange(half, dtype=np.float64) * 2.0 / SWA_HD)
    ang = pos.astype(np.float64)[:, None] * inv[None, :]
    cos, sin = np.cos(ang), np.sin(ang)
    reps = LANES // SWA_HD
    return (jnp.asarray(np.tile(np.concatenate([cos, cos], axis=1), (1, reps)), F32),
            jnp.asarray(np.tile(np.concatenate([-sin, sin], axis=1), (1, reps)), F32))


def _retention_tables(chunk, rows):
    lg = np.log1p(-np.exp2(-5.0 - np.arange(RET_HEADS, dtype=np.float64)))
    idx = np.arange(rows, dtype=np.float64)
    live = idx < chunk
    rel = idx[:, None] - idx[None, :]
    ok = (rel >= 0) & live[:, None] & live[None, :]
    dmat = np.where(ok[None], np.exp(np.maximum(rel, 0.0)[None] * lg[:, None, None]), 0.0)
    kdec = np.where(live[:, None], np.exp((chunk - 1.0 - idx)[:, None] * lg[None, :]), 0.0)
    qdec = np.where(live[:, None], np.exp((idx + 1.0)[:, None] * lg[None, :]), 0.0)
    g = np.exp(chunk * lg)
    kdec = np.repeat(kdec, RET_DK, axis=1)
    qdec = np.repeat(qdec, RET_DK, axis=1)
    gdec = np.broadcast_to(np.repeat(g, RET_DK)[:, None], (RET_HEADS * RET_DK, RET_DV))
    return tuple(jnp.asarray(t, F32) for t in (dmat, kdec, qdec, gdec))


def kernel(x_prompt, x_sample, state_ret, cache_swa_k, cache_swa_v, cache_mem_k, cache_mem_v,
           mem_prompt, w_in, w_mem_kv, w_out, ln_gain, ln_bias):
    depth = w_in.shape[0]
    assert depth == 1
    win_bf = w_in[0].astype(BF16)
    wmem_bf = w_mem_kv[0].astype(BF16)
    wout_bf = w_out[0].astype(BF16)
    gain = ln_gain[0].reshape(1, D_MODEL)
    bias = ln_bias[0].reshape(1, D_MODEL)

    xs = jnp.pad(x_sample, ((0, 0), (0, DEC_PAD - DEC_SEQ), (0, 0))).reshape(DEC_BATCH * DEC_PAD, D_MODEL)
    pos_s = PAST_LEN + np.arange(DEC_BATCH * DEC_PAD) % DEC_PAD
    cos_s, sin_s = _rope_tables(pos_s)
    names = [c[0] for c in _PROJ_COLS]
    ps = dict(zip(names, _project(xs, win_bf, cos_s, sin_s, DEC_BATCH * DEC_PAD,
                                  [(k, "rows", F32) for k in names])))
    decode = _decode_operands(
        ps, state_ret[0].reshape(DEC_BATCH, RET_HEADS * RET_DK, RET_DV),
        cache_swa_k[0].transpose(0, 2, 3, 1), cache_swa_v[0].transpose(0, 2, 3, 1),
        cache_mem_k[0].reshape(DEC_BATCH, N_MEM * MEM_HEADS, MEM_HD),
        cache_mem_v[0].reshape(DEC_BATCH, N_MEM * MEM_HEADS, MEM_HD),
        _retention_tables(DEC_SEQ, DEC_PAD))

    xp = x_prompt.reshape(BATCH * SEQ, D_MODEL)
    assert xp.shape[0] // PERM_TILE == DEC_BATCH
    cos_p, sin_p = _rope_tables(np.arange(SEQ))
    p_outs = (("rq", "rows", F32), ("rk", "rows", F32), ("rv", "rows", BF16), ("rg", "rows", BF16),
              ("sq", "perm", F32), ("sk", "cols", F32), ("sk", "perm", F32), ("sv", "cols", F32),
              ("sv", "perm", F32), ("sg", "rows", BF16), ("mq", "rows", BF16), ("mg", "rows", BF16))
    p_keys = ("rq", "rk", "rv", "rg", "sq4", "sk", "sk4", "sv", "sv4", "sg", "mq", "mg")
    *p_res, mix_s, ret_s = _project(xp, win_bf, cos_p, sin_p, PERM_TILE, p_outs, decode)
    pr = dict(zip(p_keys, p_res))
    mk, mv = _memkv(mem_prompt.reshape(BATCH * N_MEM, D_MODEL), wmem_bf)
    slab = lambda a: a.reshape(SWA_W // LANES, BATCH, SEQ, LANES)
    swa_o = _swa_prompt(slab(pr["sq4"]), slab(pr["sk4"]), slab(pr["sv4"]))
    swa_o = swa_o.reshape(SWA_W // LANES, BATCH * SEQ, LANES)
    yp, ret_p = _mix_prompt(xp, pr, swa_o, mk, mv, wout_bf, _retention_tables(RET_CHUNK, RET_CHUNK),
                            gain, bias, 512)
    ys = _finish(xs, mix_s, wout_bf, gain, bias)

    take = lambda a, w: a.reshape(DEC_BATCH, DEC_PAD, w)[:, :DEC_SEQ]
    swa_rows = lambda a: a.reshape(BATCH, SWA_HEADS, SWA_HD, SEQ).transpose(0, 3, 1, 2)[None]
    return (
        yp.reshape(BATCH, SEQ, D_MODEL),
        take(ys, D_MODEL),
        ret_p.reshape(1, BATCH, RET_HEADS, RET_DK, RET_DV),
        ret_s.reshape(1, DEC_BATCH, RET_HEADS, RET_DK, RET_DV),
        swa_rows(pr["sk"]),
        swa_rows(pr["sv"]),
        take(ps["sk"], SWA_W).reshape(1, DEC_BATCH, DEC_SEQ, SWA_HEADS, SWA_HD),
        take(ps["sv"], SWA_W).reshape(1, DEC_BATCH, DEC_SEQ, SWA_HEADS, SWA_HD),
        mk.reshape(1, BATCH, N_MEM, MEM_HEADS, MEM_HD),
        mv.reshape(1, BATCH, N_MEM, MEM_HEADS, MEM_HD),
    )
```

```python
import functools

import jax
import jax.numpy as jnp
import numpy as np
from jax import lax
from jax.experimental import pallas as pl
from jax.experimental.pallas import tpu as pltpu

F32 = jnp.float32
BF16 = jnp.bfloat16

D_MODEL = 1024
BATCH = 8
SEQ = 2048
DEC_BATCH = 32
DEC_SEQ = 4
PAST_LEN = 8192
N_MEM = 256
MEM_HEADS = 4
MEM_HD = 128
RET_HEADS = 4
RET_DK = 64
RET_DV = 128
RET_CHUNK = 128
SWA_HEADS = 8
SWA_HD = 64
SWA_DILATIONS = (1, 4, 16)
SWA_STEPS = 128
SWA_BLOCK = 128
ROPE_THETA = 10000.0
LN_EPS = 1e-5
GN_EPS = 1e-5
RET_W = RET_HEADS * RET_DV
SWA_W = SWA_HEADS * SWA_HD
MEM_W = MEM_HEADS * MEM_HD
D_MIX = RET_W + SWA_W + MEM_W
DEEPNORM_ALPHA = 2.0 ** 0.25
MEM_SCALE = MEM_HD ** -0.5
LOG2_E = 1.4426950408889634
QK_SCALE = 0.125

LANES = 128
DEC_PAD = 16
PERM_TILE = 512
PERM_PITCH = 40
PERM_ROWS = 16 * PERM_PITCH
STAGE_BUFFERS = 4
OUT_ROWS = 256
SWA_PAR = 4
MEMKV_BATCHES = 4
VMEM_LIMIT = 56 * 1024 * 1024
NEG = -1e30

_PROJ_COLS = (
    ("rq", RET_HEADS * RET_DK, "rope", 1.0),
    ("rk", RET_HEADS * RET_DK, "rope", QK_SCALE),
    ("rv", RET_W, "id", 1.0),
    ("rg", RET_W, "silu", 1.0),
    ("sq", SWA_W, "rope", QK_SCALE),
    ("sk", SWA_W, "rope", 1.0),
    ("sv", SWA_W, "id", 1.0),
    ("sg", SWA_W, "silu", 1.0),
    ("mq", MEM_W, "id", 1.0),
    ("mg", MEM_W, "silu", 1.0),
)


def _dot(a, b):
    return jnp.dot(a, b, preferred_element_type=F32)


def _dot_nt(a, b):
    return lax.dot_general(a, b, (((1,), (1,)), ((), ())), preferred_element_type=F32)


def _dot_tn(a, b):
    return lax.dot_general(a, b, (((0,), (0,)), ((), ())), preferred_element_type=F32)


def _params(n_axes):
    return pltpu.CompilerParams(dimension_semantics=("arbitrary",) * n_axes,
                                vmem_limit_bytes=VMEM_LIMIT)


def _run_interleaved(stages):
    stages = list(stages)
    while stages:
        for g in list(stages):
            if next(g, "done") == "done":
                stages.remove(g)


def _proj_kernel(*refs, dests, n_decode_in=0):
    n_out = len(dests)
    p_in, s_in = refs[:4], refs[4:4 + n_decode_in]
    p_out = refs[4 + n_decode_in:4 + n_decode_in + n_out]
    s_out = refs[4 + n_decode_in + n_out:4 + n_decode_in + n_out + (2 if n_decode_in else 0)]
    stages = [_proj_stages(*p_in, *p_out, dests=dests, stage_ref=refs[-1])]
    if n_decode_in:
        stages.append(_mix_sample_kernel(*s_in, *s_out))
    _run_interleaved(stages)


def _proj_stages(x_ref, w_ref, cos_ref, sin_ref, *out_refs, dests, stage_ref):
    xb = x_ref[...].astype(BF16)
    cos = cos_ref[...]
    sin = sin_ref[...]
    lane = lax.broadcasted_iota(jnp.int32, cos.shape, 1)
    first_half = (lane % 64) < 32
    col = 0
    n_staged = 0
    for name, width, kind, scale in _PROJ_COLS:
        targets = [(o_ref, layout) for o_ref, (dname, layout) in zip(out_refs, dests) if dname == name]
        for c in range(0, width, 2 * LANES):
            h2 = _dot(xb, w_ref[:, col + c:col + c + 2 * LANES])
            for half in range(2):
                h = h2[:, half * LANES:(half + 1) * LANES]
                if kind == "rope":
                    swapped = jnp.where(first_half, pltpu.roll(h, 96, 1), pltpu.roll(h, 32, 1))
                    h = h * cos + swapped * sin
                    if scale != 1.0:
                        h = h * scale
                elif kind == "silu":
                    h = h * (1.0 / (1.0 + jnp.exp(-h)))
                lo = c + half * LANES
                for o_ref, layout in targets:
                    if layout == "rows":
                        o_ref[:, lo:lo + LANES] = h.astype(o_ref.dtype)
                    elif layout == "perm":
                        grp = h.shape[0] // 16
                        stage = stage_ref.at[n_staged % stage_ref.shape[0]]
                        n_staged += 1
                        for g in range(h.shape[0] // 8):
                            base = (g % 2) * 8 * PERM_PITCH + g // 2
                            stage[pl.ds(base, 8, stride=PERM_PITCH), :] = h[8 * g:8 * g + 8]
                        for cls in range(16):
                            o_ref[lo // LANES, pl.ds(cls * grp, grp), :] = (
                                stage[pl.ds(cls * PERM_PITCH, grp), :].astype(o_ref.dtype))
                    else:
                        o_ref[lo:lo + LANES, :] = h.T.astype(o_ref.dtype)
        col += width
        yield


def _project(x2d, w_bf, cos_t, sin_t, tile, outs, decode=None):
    n = x2d.shape[0]
    seq = cos_t.shape[0]
    n_tab = seq // tile
    d_in = w_bf.shape[1]
    widths = {name: w for name, w, _, _ in _PROJ_COLS}
    out_shape, out_specs = [], []
    for name, layout, dt in outs:
        w = widths[name]
        if layout == "rows":
            out_shape.append(jax.ShapeDtypeStruct((n, w), dt))
            out_specs.append(pl.BlockSpec((tile, w), lambda i: (i, 0)))
        elif layout == "perm":
            assert tile == PERM_TILE
            out_shape.append(jax.ShapeDtypeStruct((w // LANES, n, LANES), dt))
            out_specs.append(pl.BlockSpec((w // LANES, tile, LANES), lambda i: (0, i, 0)))
        else:
            out_shape.append(jax.ShapeDtypeStruct((n // seq, w, seq), dt))
            out_specs.append(pl.BlockSpec((None, w, tile), lambda i: (i // n_tab, 0, i % n_tab)))
    d_args, d_in_specs, d_out_specs, d_out_shape = decode if decode else ((), [], [], [])
    return pl.pallas_call(
        functools.partial(_proj_kernel, dests=tuple((name, layout) for name, layout, _ in outs),
                          n_decode_in=len(d_args)),
        grid=(n // tile,),
        in_specs=[
            pl.BlockSpec((tile, D_MODEL), lambda i: (i, 0)),
            pl.BlockSpec((D_MODEL, d_in), lambda i: (0, 0)),
            pl.BlockSpec((tile, LANES), lambda i: (i % n_tab, 0)),
            pl.BlockSpec((tile, LANES), lambda i: (i % n_tab, 0)),
        ] + list(d_in_specs),
        out_specs=out_specs + list(d_out_specs),
        out_shape=out_shape + list(d_out_shape),
        scratch_shapes=[pltpu.VMEM((STAGE_BUFFERS, PERM_ROWS, LANES), F32)],
        compiler_params=_params(1),
        name="proj_decode" if decode else "proj",
    )(x2d, w_bf, cos_t, sin_t, *d_args)


def _decode_operands(ps, state, ckt, cvt, cmk, cmv, tabs):
    n = DEC_PAD
    wb = ckt.shape[3]
    const = lambda a: pl.BlockSpec(a.shape, lambda i: (0,) * a.ndim)
    srow = lambda w: pl.BlockSpec((n, w), lambda i: (i, 0))
    st_s = pl.BlockSpec((None, RET_HEADS * RET_DK, RET_DV), lambda i: (i, 0, 0))
    cache_s = pl.BlockSpec((None, SWA_HEADS, SWA_HD, wb), lambda i: (i, 0, 0, 0))
    mem_s = pl.BlockSpec((None, N_MEM * MEM_HEADS, MEM_HD), lambda i: (i, 0, 0))
    s_names = ("rq", "rk", "rv", "rg", "sq", "sk", "sv", "sg", "mq", "mg")
    widths = {name: w for name, w, _, _ in _PROJ_COLS}
    args = (*[ps[k] for k in s_names], state, ckt, cvt, cmk, cmv, *tabs)
    in_specs = [*[srow(widths[k]) for k in s_names], st_s, cache_s, cache_s, mem_s, mem_s,
                *[const(a) for a in tabs]]
    out_shape = [jax.ShapeDtypeStruct((DEC_BATCH * n, D_MIX), F32),
                 jax.ShapeDtypeStruct((DEC_BATCH, RET_HEADS * RET_DK, RET_DV), F32)]
    return args, in_specs, [srow(D_MIX), st_s], out_shape


def _memkv_kernel(m_ref, w_ref, mk_ref, mv_ref):
    for b in range(m_ref.shape[0] // N_MEM):
        mb = m_ref[pl.ds(b * N_MEM, N_MEM), :].astype(BF16)
        for h in range(MEM_HEADS):
            rows = pl.ds(h, N_MEM, stride=MEM_HEADS)
            mk_ref[b, rows, :] = _dot(mb, w_ref[:, h * MEM_HD:(h + 1) * MEM_HD])
            mv_ref[b, rows, :] = _dot(mb, w_ref[:, MEM_W + h * MEM_HD:MEM_W + (h + 1) * MEM_HD])


def _memkv(mem2d, w_bf):
    n = mem2d.shape[0] // N_MEM
    per = MEMKV_BATCHES
    out = pl.BlockSpec((per, N_MEM * MEM_HEADS, MEM_HD), lambda i: (i, 0, 0))
    return pl.pallas_call(
        _memkv_kernel,
        grid=(n // per,),
        in_specs=[pl.BlockSpec((per * N_MEM, D_MODEL), lambda i: (i, 0)),
                  pl.BlockSpec((D_MODEL, 2 * MEM_W), lambda i: (0, 0))],
        out_specs=[out, out],
        out_shape=[jax.ShapeDtypeStruct((n, N_MEM * MEM_HEADS, MEM_HD), F32)] * 2,
        compiler_params=_params(1),
        name="memkv",
    )(mem2d, w_bf)


def _swa_prompt_kernel(q_ref, k_ref, v_ref, o_ref, a_ref, m_ref, l_ref):
    blk = SWA_BLOCK
    pt = PERM_TILE
    grp = pt // 16
    lo = lax.broadcasted_iota(jnp.int32, (blk, blk), 1) < SWA_HD

    def biases(seq_of):
        rq = seq_of(lax.broadcasted_iota(jnp.int32, (blk, 2 * blk), 0))
        c2 = lax.broadcasted_iota(jnp.int32, (blk, 2 * blk), 1)
        rel = blk + rq - (seq_of(c2 % blk) + blk * (c2 // blk))
        prev = jnp.where((rel >= 0) & (rel <= SWA_STEPS), 0.0, NEG).astype(F32)
        r1 = seq_of(lax.broadcasted_iota(jnp.int32, (blk, blk), 0))
        c1 = seq_of(lax.broadcasted_iota(jnp.int32, (blk, blk), 1))
        return prev, jnp.where(r1 >= c1, 0.0, NEG).astype(F32)

    def gather(ref, pair, starts, n):
        return jnp.concatenate([ref[pair, pl.ds(s, n), :] for s in starts], axis=0)

    def aligned(x):
        return x if isinstance(x, int) else pl.multiple_of(x, 8)

    def blocks(jobs, n, first_pattern, last_pattern):
        n_pairs = SWA_HEADS // 2
        in_at = lambda c: aligned(c[0] * pt + c[1] * grp + c[2])
        acc_starts, q_starts, k_starts = [], [], []
        for chunks, prev_chunks, _, _ in jobs:
            acc_starts.append([aligned(c[0] * pt + c[1] * grp + c[2]) for c in chunks])
            q_starts.append([in_at(c) for c in chunks])
            k_starts.append(([] if prev_chunks is None else [in_at(c) for c in prev_chunks]) + q_starts[-1])
        items = [(j, pair) for j in range(len(jobs)) for pair in range(n_pairs)]
        olds = {}
        if not first_pattern:
            for j, pair in items:
                olds[j, pair] = (gather(m_ref, pair, acc_starts[j], n), gather(l_ref, pair, acc_starts[j], n),
                                 gather(a_ref, pair, acc_starts[j], n))
        vbs = {(j, pair): gather(v_ref, pair, k_starts[j], n).astype(BF16) for j, pair in items}
        scores = {}
        for j, pair in items:
            q = gather(q_ref, pair, q_starts[j], n)
            kb = gather(k_ref, pair, k_starts[j], n).astype(BF16)
            for hh in range(2):
                qm = jnp.where(lo if hh == 0 else ~lo, q, 0.0).astype(BF16)
                scores[j, pair, hh] = _dot_nt(qm, kb) + jobs[j][2]
        ms = {key: jnp.max(s, axis=-1, keepdims=True) for key, s in scores.items()}
        ps = {key: jnp.exp(s - ms[key]) for key, s in scores.items()}
        def v_ext(v):
            lane = lax.broadcasted_iota(jnp.int32, v.shape, 1)
            own = lane < SWA_HD
            zero = jnp.zeros((), BF16)
            ones_lo = jnp.where(lane < SWA_HD, 1.0, 0.0).astype(BF16)
            ones_hi = jnp.where(lane < SWA_HD, 0.0, 1.0).astype(BF16)
            return jnp.concatenate(
                [jnp.concatenate([jnp.where(own, v, zero), ones_lo], axis=1),
                 jnp.concatenate([jnp.where(own, zero, v), ones_hi], axis=1)], axis=0)

        pvs = {(j, pair): _dot(jnp.concatenate([ps[j, pair, 0].astype(BF16), ps[j, pair, 1].astype(BF16)], axis=1),
                               v_ext(vbs[j, pair])) for j, pair in items}
        news = {}
        for j, pair in items:
            m_g = jnp.where(lo, ms[j, pair, 0], ms[j, pair, 1])
            a_g = pvs[j, pair][:, :LANES]
            l_g = pvs[j, pair][:, LANES:]
            if first_pattern:
                news[j, pair] = (m_g, l_g, a_g)
            else:
                m_old, l_old, a_old = olds[j, pair]
                m_new = jnp.maximum(m_old, m_g)
                w_old = jnp.exp(m_old - m_new)
                w_g = jnp.exp(m_g - m_new)
                news[j, pair] = (m_new, w_old * l_old + w_g * l_g, w_old * a_old + w_g * a_g)
        for (j, pair), (m_new, l_new, a_new) in news.items():
            starts = acc_starts[j]
            if last_pattern:
                o = a_new / l_new
                for i in range(len(starts)):
                    o_ref[pair, pl.ds(jobs[j][3] + i, n, stride=len(starts)), :] = o[i * n:(i + 1) * n]
            else:
                for i, s in enumerate(starts):
                    m_ref[pair, pl.ds(s, n), :] = m_new[i * n:(i + 1) * n]
                    l_ref[pair, pl.ds(s, n), :] = l_new[i * n:(i + 1) * n]
                    a_ref[pair, pl.ds(s, n), :] = a_new[i * n:(i + 1) * n]

    par = SWA_PAR
    _, bias16 = biases(lambda i: i)

    par16 = 2 * par

    def class16(r0, carry):
        blocks([([(t, r0 + k * (16 // par16), 0) for t in range(SEQ // pt)], None, bias16, None)
                for k in range(par16)], grp, True, False)
        return carry

    lax.fori_loop(0, 16 // par16, class16, 0)

    bias4_prev, bias4_own = biases(lambda i: 4 * (i % grp) + i // grp)

    def class4(r0, carry):
        chunks_of = lambda r, tile: [(tile, 4 * a + r, 0) for a in range(4)]
        classes = [r0 + k * (4 // par) for k in range(par)]
        blocks([(chunks_of(r, 0), None, bias4_own, None) for r in classes], grp, False, False)

        def per_block(b, c2):
            blocks([(chunks_of(r, b), chunks_of(r, b - 1), bias4_prev, None) for r in classes],
                   grp, False, False)
            return c2

        lax.fori_loop(1, SEQ // pt, per_block, 0)
        return carry

    lax.fori_loop(0, 4 // par, class4, 0)

    bias1_prev, bias1_own = biases(lambda i: 16 * (i % 8) + i // 8)
    per_tile = pt // blk
    chunks1 = lambda c: [(c // per_tile, i, (c % per_tile) * 8) for i in range(16)]
    job1 = lambda c: (chunks1(c), chunks1(c - 1), bias1_prev, aligned(c * blk))
    blocks([(chunks1(0), None, bias1_own, 0)] + [job1(c) for c in range(1, par)], 8, False, True)

    def block1(g, carry):
        blocks([job1(g * par + k) for k in range(par)], 8, False, True)
        return carry

    lax.fori_loop(1, SEQ // blk // par, block1, 0)


def _swa_prompt(sq, sk, sv):
    slabs = SWA_W // LANES
    spec = pl.BlockSpec((slabs, None, SEQ, LANES), lambda b: (0, b, 0, 0))
    in_spec = pl.BlockSpec((slabs, None, sq.shape[2], LANES), lambda b: (0, b, 0, 0))
    return pl.pallas_call(
        _swa_prompt_kernel,
        grid=(BATCH,),
        in_specs=[in_spec, in_spec, in_spec],
        out_specs=spec,
        out_shape=jax.ShapeDtypeStruct((slabs, BATCH, SEQ, LANES), F32),
        scratch_shapes=[pltpu.VMEM((slabs, SEQ, LANES), F32)] * 3,
        compiler_params=_params(1),
        name="swa_prompt",
    )(sq, sk, sv)


def _head_norm(o):
    mu = jnp.mean(o, axis=-1, keepdims=True)
    d = o - mu
    var = jnp.mean(d * d, axis=-1, keepdims=True)
    return d * lax.rsqrt(var + GN_EPS)


def _deepnorm_ln(x, h, gain, bias):
    z = DEEPNORM_ALPHA * x + h
    mu = jnp.mean(z, axis=-1, keepdims=True)
    d = z - mu
    var = jnp.mean(d * d, axis=-1, keepdims=True)
    return d * lax.rsqrt(var + LN_EPS) * gain + bias


def _softmax_rows(s):
    m = jnp.max(s, axis=-1, keepdims=True)
    p = jnp.exp(s - m)
    return p * (1.0 / jnp.sum(p, axis=-1, keepdims=True))


def _mix_prompt_kernel(x_ref, rq_ref, rk_ref, rv_ref, rg_ref, so_ref, sg_ref, mq_ref, mg_ref,
                       mk_ref, mv_ref, wout_ref, dmat_ref, kdec_ref, qdec_ref, gdec_ref,
                       gain_ref, bias_ref, y_ref, state_out_ref,
                       state_ref, mix_ref, mkb_ref, mvb_ref, *, tile):
    t = pl.program_id(1)

    @pl.when(t == 0)
    def _():
        state_ref[...] = jnp.zeros_like(state_ref)
        for h in range(MEM_HEADS):
            rows = pl.ds(h, N_MEM, stride=MEM_HEADS)
            mkb_ref[:, pl.ds(h * MEM_HD, MEM_HD)] = mk_ref[rows, :].astype(BF16)
            mvb_ref[:, pl.ds(h * MEM_HD, MEM_HD)] = mv_ref[rows, :].astype(BF16)

    ck = RET_CHUNK
    lane = lax.broadcasted_iota(jnp.int32, (ck, LANES), 1)
    lo = lane < RET_DK
    top = lax.broadcasted_iota(jnp.int32, (LANES, LANES), 0) < RET_DK
    gain = gain_ref[...]
    bias = bias_ref[...]

    n_ck = tile // ck
    n_pairs = RET_HEADS // 2
    rows_of = [pl.ds(c * ck, ck) for c in range(n_ck)]
    items = [(c, pair, hh) for c in range(n_ck) for pair in range(n_pairs) for hh in range(2)]
    hs_of = lambda pair, hh: pl.ds((2 * pair + hh) * RET_DV, RET_DV)
    sel = lambda x, hh: jnp.where(lo if hh == 0 else ~lo, x, 0.0).astype(BF16)
    pairs = [(c, pair) for c in range(n_ck) for pair in range(n_pairs)]
    qk = {}
    for c, pair in pairs:
        cs = pl.ds(pair * LANES, LANES)
        q = rq_ref[rows_of[c], cs]
        k = rk_ref[rows_of[c], cs]
        qk[c, pair] = (q.astype(BF16), jnp.concatenate([sel(k, 0), sel(k, 1)], axis=0),
                       (k * kdec_ref[:, cs]).astype(BF16), q * qdec_ref[:, cs])
    v_pair = lambda c, pair: rv_ref[rows_of[c], pl.ds(pair * 2 * RET_DV, 2 * RET_DV)]
    s2 = {key: _dot_nt(qk[key][0], qk[key][1]) for key in pairs}
    kv2 = {key: _dot_tn(qk[key][2], v_pair(*key)) for key in pairs}
    yield
    state = {(0, pair): state_ref[pair] for pair in range(n_pairs)}
    for c, pair in pairs:
        state[c + 1, pair] = (gdec_ref[pl.ds(pair * LANES, LANES), :] * state[c, pair]
                              + jnp.where(top, kv2[c, pair][:, :RET_DV], kv2[c, pair][:, RET_DV:]))
    for pair in range(n_pairs):
        state_ref[pair] = state[n_ck, pair]
    o = {}
    for c, pair, hh in items:
        sh = (s2[c, pair][:, hh * ck:(hh + 1) * ck] * dmat_ref[2 * pair + hh]).astype(BF16)
        lhs = jnp.concatenate([sh, sel(qk[c, pair][3], hh)], axis=1)
        rhs = jnp.concatenate([rv_ref[rows_of[c], hs_of(pair, hh)], state[c, pair].astype(BF16)], axis=0)
        o[c, pair, hh] = _dot(lhs, rhs)
    yield
    for c, pair, hh in items:
        hs = hs_of(pair, hh)
        mix_ref[rows_of[c], hs] = (rg_ref[rows_of[c], hs].astype(F32) * _head_norm(o[c, pair, hh])).astype(BF16)
    for pair in range(SWA_W // LANES):
        cs = pl.ds(pair * LANES, LANES)
        mix_ref[:, pl.ds(RET_W + pair * LANES, LANES)] = (
            sg_ref[:, cs].astype(F32) * so_ref[pair]).astype(BF16)
    for c in range(n_ck):
        yield
        ss = [_dot_nt(mq_ref[rows_of[c], pl.ds(h * MEM_HD, MEM_HD)], mkb_ref[:, pl.ds(h * MEM_HD, MEM_HD)])
              for h in range(MEM_HEADS)]
        ps = [jnp.exp2((s - jnp.max(s, axis=-1, keepdims=True)) * (MEM_SCALE * LOG2_E)) for s in ss]
        for h in range(MEM_HEADS):
            hs = pl.ds(h * MEM_HD, MEM_HD)
            o = _dot(ps[h].astype(BF16), mvb_ref[:, hs]) * (1.0 / jnp.sum(ps[h], axis=-1, keepdims=True))
            mix_ref[rows_of[c], pl.ds(RET_W + SWA_W + h * MEM_HD, MEM_HD)] = (
                mg_ref[rows_of[c], hs].astype(F32) * o).astype(BF16)
    yield
    blocks = [pl.ds(r * OUT_ROWS, OUT_ROWS) for r in range(tile // OUT_ROWS)]
    hout = _dot(mix_ref[blocks[0], :], wout_ref[...])
    for r, rows in enumerate(blocks):
        nxt = _dot(mix_ref[blocks[r + 1], :], wout_ref[...]) if r + 1 < len(blocks) else None
        y_ref[rows, :] = _deepnorm_ln(x_ref[rows, :], hout, gain, bias)
        hout = nxt

    @pl.when(t == pl.num_programs(1) - 1)
    def _():
        state_out_ref[pl.ds(0, LANES), :] = state_ref[0]
        state_out_ref[pl.ds(LANES, LANES), :] = state_ref[1]


def _mix_prompt_body(*refs, tile):
    _run_interleaved([_mix_prompt_kernel(*refs, tile=tile)])


def _mix_prompt(x2d, pr, swa_o, mk, mv, wout_bf, tabs_p, gain, bias, tile):
    nt = SEQ // tile
    step = lambda b, t: b * nt + t
    row = lambda w: pl.BlockSpec((tile, w), lambda b, t: (step(b, t), 0))
    const = lambda a: pl.BlockSpec(a.shape, lambda b, t: (0,) * a.ndim)
    mem_p = pl.BlockSpec((None, N_MEM * MEM_HEADS, MEM_HD), lambda b, t: (b, 0, 0))
    st_p = pl.BlockSpec((None, RET_HEADS * RET_DK, RET_DV), lambda b, t: (b, 0, 0))
    p_args = (x2d, pr["rq"], pr["rk"], pr["rv"], pr["rg"], swa_o, pr["sg"], pr["mq"], pr["mg"],
              mk, mv, wout_bf, *tabs_p, gain, bias)
    p_specs = [row(D_MODEL), row(RET_HEADS * RET_DK), row(RET_HEADS * RET_DK), row(RET_W), row(RET_W),
               pl.BlockSpec((SWA_W // LANES, tile, LANES), lambda b, t: (0, step(b, t), 0)),
               row(SWA_W), row(MEM_W), row(MEM_W), mem_p, mem_p, const(wout_bf),
               *[const(a) for a in tabs_p], const(gain), const(bias)]
    return pl.pallas_call(
        functools.partial(_mix_prompt_body, tile=tile),
        grid=(BATCH, nt),
        in_specs=p_specs,
        out_specs=[row(D_MODEL), st_p],
        out_shape=[jax.ShapeDtypeStruct((BATCH * SEQ, D_MODEL), F32),
                   jax.ShapeDtypeStruct((BATCH, RET_HEADS * RET_DK, RET_DV), F32)],
        scratch_shapes=[pltpu.VMEM((RET_HEADS // 2, LANES, RET_DV), F32),
                        pltpu.VMEM((tile, D_MIX), BF16),
                        pltpu.VMEM((N_MEM, MEM_W), BF16),
                        pltpu.VMEM((N_MEM, MEM_W), BF16)],
        compiler_params=_params(2),
        name="mix_prompt",
    )(*p_args)


def _mix_sample_kernel(rq_ref, rk_ref, rv_ref, rg_ref, sq_ref, sk_ref, sv_ref, sg_ref, mq_ref, mg_ref,
                       state_ref, ckt_ref, cvt_ref, cmk_ref, cmv_ref,
                       dmat_ref, kdec_ref, qdec_ref, gdec_ref,
                       mix_ref, state_out_ref):
    n = DEC_PAD
    q = rq_ref[...]
    k = rk_ref[...]
    kb = k.astype(BF16)
    kd = (k * kdec_ref[...]).astype(BF16)
    qd = q * qdec_ref[...]
    st = state_ref[...]
    stb = st.astype(BF16)
    vb = rv_ref[...].astype(BF16)
    kv = _dot_tn(kd, vb)
    lane_qk = lax.broadcasted_iota(jnp.int32, q.shape, 1) // RET_DK
    ret_s = [_dot_nt(jnp.where(lane_qk == h, q, 0.0).astype(BF16), kb) for h in range(RET_HEADS)]
    ret_cross = [_dot(jnp.where(lane_qk == h, qd, 0.0).astype(BF16), stb) for h in range(RET_HEADS)]
    mem_rows = [pl.ds(h, N_MEM, stride=MEM_HEADS) for h in range(MEM_HEADS)]
    mem_s = [_dot_nt(mq_ref[:, pl.ds(h * MEM_HD, MEM_HD)].astype(BF16), cmk_ref[mem_rows[h], :].astype(BF16))
             for h in range(MEM_HEADS)]

    blk = SWA_BLOCK
    wb = ckt_ref.shape[2]
    rt = 8
    lo = lax.broadcasted_iota(jnp.int32, (n, LANES), 1) < SWA_HD
    lo_t = lax.broadcasted_iota(jnp.int32, (rt, LANES), 1) < SWA_HD
    tok = lambda w: lax.broadcasted_iota(jnp.int32, (rt, w), 0)
    col = lambda w: lax.broadcasted_iota(jnp.int32, (rt, w), 1)
    windows = (
        (wb - blk, jnp.where(col(blk) >= tok(blk), 0.0, NEG), jnp.where(col(blk) <= tok(blk), 0.0, NEG)),
        (wb - 4 * blk, jnp.where(col(4 * blk) % 4 == tok(4 * blk), 0.0, NEG),
         jnp.where(col(blk) == tok(blk), 0.0, NEG)),
        (0, jnp.where(col(wb) % 16 == tok(wb), 0.0, NEG), jnp.where(col(blk) == tok(blk), 0.0, NEG)),
    )
    pad = jnp.zeros((blk - n, LANES), BF16)
    zero_c = jnp.zeros((rt, wb), F32)
    zero_n = jnp.zeros((rt, blk), F32)
    n_pairs = SWA_HEADS // 2
    n_win = len(windows)
    pair_cols = [pl.ds(pair * LANES, LANES) for pair in range(n_pairs)]
    vps = [cvt_ref[2 * pair:2 * pair + 2].reshape(2 * SWA_HD, wb).astype(BF16) for pair in range(n_pairs)]
    v_news = [jnp.concatenate([sv_ref[:, cs].astype(BF16), pad], axis=0) for cs in pair_cols]
    s_alls, sn_alls = [], []
    for pair, cs in enumerate(pair_cols):
        kp = ckt_ref[2 * pair:2 * pair + 2].reshape(2 * SWA_HD, wb).astype(BF16)
        k_new = jnp.concatenate([sk_ref[:, cs].astype(BF16), pad], axis=0)
        q = sq_ref[:, cs]
        qs = jnp.concatenate([jnp.where(lo, q, 0.0), jnp.where(lo, 0.0, q)], axis=0).astype(BF16)
        s_alls.append(_dot(qs, kp))
        sn_alls.append(_dot_nt(qs, k_new))
    yield
    p_rows = [[] for _ in range(n_pairs)]
    pn_rows = [[] for _ in range(n_pairs)]
    stats = [[] for _ in range(n_pairs)]
    for pair in range(n_pairs):
        for hh in range(2):
            s = s_alls[pair][hh * n:hh * n + rt]
            sn = sn_alls[pair][hh * n:hh * n + rt]
            for w0, bias_c, bias_n in windows:
                sc = s[:, w0:] + bias_c
                snb = sn + bias_n
                m = jnp.maximum(jnp.max(sc, axis=-1, keepdims=True), jnp.max(snb, axis=-1, keepdims=True))
                pc = jnp.exp(sc - m)
                pn = jnp.exp(snb - m)
                stats[pair].append((m, jnp.sum(pc, axis=-1, keepdims=True) + jnp.sum(pn, axis=-1, keepdims=True)))
                if w0:
                    pc = jnp.concatenate([jnp.zeros((rt, w0), F32), pc], axis=1)
                p_rows[pair].append(pc)
                pn_rows[pair].append(pn)
            p_rows[pair].append(zero_c)
            pn_rows[pair].append(zero_n)
    ret_sb = [(ret_s[h] * dmat_ref[h]).astype(BF16) for h in range(RET_HEADS)]
    mem_p = [_softmax_rows(s * MEM_SCALE).astype(BF16) for s in mem_s]
    yield
    pvs = [_dot_nt(jnp.concatenate(p_rows[pair], axis=0).astype(BF16), vps[pair])
           + _dot(jnp.concatenate(pn_rows[pair], axis=0).astype(BF16), v_news[pair])
           for pair in range(n_pairs)]
    ret_intra = [_dot(ret_sb[h], vb[:, h * RET_DV:(h + 1) * RET_DV]) for h in range(RET_HEADS)]
    mem_o = [_dot(mem_p[h], cmv_ref[mem_rows[h], :].astype(BF16)) for h in range(MEM_HEADS)]
    yield
    pieces = []
    for pair in range(n_pairs):
        heads = []
        for hh in range(2):
            parts = []
            for i in range(n_win):
                m, l = stats[pair][hh * n_win + i]
                r0 = (hh * (n_win + 1) + i) * rt
                parts.append((pvs[pair][r0:r0 + rt] / l, m, l))
            m_all = jnp.maximum(jnp.maximum(parts[0][1], parts[1][1]), parts[2][1])
            ws = [l * jnp.exp(m - m_all) for (_, m, l) in parts]
            heads.append((ws[0] * parts[0][0] + ws[1] * parts[1][0] + ws[2] * parts[2][0])
                         / (ws[0] + ws[1] + ws[2]))
        pieces.append(jnp.where(lo_t, heads[0], heads[1]))
    swa = jnp.concatenate(pieces, axis=1)
    swa = jnp.concatenate([swa, jnp.zeros((n - rt, SWA_W), F32)], axis=0)
    mix_ref[:, pl.ds(RET_W, SWA_W)] = sg_ref[...] * swa
    for h in range(RET_HEADS):
        hs = pl.ds(h * RET_DV, RET_DV)
        ks = pl.ds(h * RET_DK, RET_DK)
        mix_ref[:, hs] = rg_ref[:, hs] * _head_norm(ret_intra[h] + ret_cross[h])
        state_out_ref[ks, :] = (gdec_ref[ks, :] * st[h * RET_DK:(h + 1) * RET_DK, :]
                                + kv[h * RET_DK:(h + 1) * RET_DK, h * RET_DV:(h + 1) * RET_DV])
    for h in range(MEM_HEADS):
        hs = pl.ds(h * MEM_HD, MEM_HD)
        mix_ref[:, pl.ds(RET_W + SWA_W + h * MEM_HD, MEM_HD)] = mg_ref[:, hs] * mem_o[h]


def _finish_kernel(x_ref, mix_ref, wout_ref, gain_ref, bias_ref, y_ref):
    hout = _dot(mix_ref[...].astype(BF16), wout_ref[...])
    y_ref[...] = _deepnorm_ln(x_ref[...], hout, gain_ref[...], bias_ref[...])


def _finish(x2d, mix, wout_bf, gain, bias):
    n = x2d.shape[0]
    tile = 256
    const2 = lambda a: pl.BlockSpec(a.shape, lambda i: (0, 0))
    return pl.pallas_call(
        _finish_kernel,
        grid=(n // tile,),
        in_specs=[pl.BlockSpec((tile, D_MODEL), lambda i: (i, 0)),
                  pl.BlockSpec((tile, D_MIX), lambda i: (i, 0)),
                  const2(wout_bf), const2(gain), const2(bias)],
        out_specs=pl.BlockSpec((tile, D_MODEL), lambda i: (i, 0)),
        out_shape=jax.ShapeDtypeStruct((n, D_MODEL), F32),
        compiler_params=_params(1),
        name="finish",
    )(x2d, mix, wout_bf, gain, bias)


def _rope_tables(pos):
    half = SWA_HD // 2
    inv = ROPE_THETA ** (-np.arange(half, dtype=np.float64) * 2.0 / SWA_HD)
    ang = pos.astype(np.float64)[:, None] * inv[None, :]
    cos, sin = np.cos(ang), np.sin(ang)
    reps = LANES // SWA_HD
    return (jnp.asarray(np.tile(np.concatenate([cos, cos], axis=1), (1, reps)), F32),
            jnp.asarray(np.tile(np.concatenate([-sin, sin], axis=1), (1, reps)), F32))


def _retention_tables(chunk, rows):
    lg = np.log1p(-np.exp2(-5.0 - np.arange(RET_HEADS, dtype=np.float64)))
    idx = np.arange(rows, dtype=np.float64)
    live = idx < chunk
    rel = idx[:, None] - idx[None, :]
    ok = (rel >= 0) & live[:, None] & live[None, :]
    dmat = np.where(ok[None], np.exp(np.maximum(rel, 0.0)[None] * lg[:, None, None]), 0.0)
    kdec = np.where(live[:, None], np.exp((chunk - 1.0 - idx)[:, None] * lg[None, :]), 0.0)
    qdec = np.where(live[:, None], np.exp((idx + 1.0)[:, None] * lg[None, :]), 0.0)
    g = np.exp(chunk * lg)
    kdec = np.repeat(kdec, RET_DK, axis=1)
    qdec = np.repeat(qdec, RET_DK, axis=1)
    gdec = np.broadcast_to(np.repeat(g, RET_DK)[:, None], (RET_HEADS * RET_DK, RET_DV))
    return tuple(jnp.asarray(t, F32) for t in (dmat, kdec, qdec, gdec))


def kernel(x_prompt, x_sample, state_ret, cache_swa_k, cache_swa_v, cache_mem_k, cache_mem_v,
           mem_prompt, w_in, w_mem_kv, w_out, ln_gain, ln_bias):
    depth = w_in.shape[0]
    assert depth == 1
    win_bf = w_in[0].astype(BF16)
    wmem_bf = w_mem_kv[0].astype(BF16)
    wout_bf = w_out[0].astype(BF16)
    gain = ln_gain[0].reshape(1, D_MODEL)
    bias = ln_bias[0].reshape(1, D_MODEL)

    xs = jnp.pad(x_sample, ((0, 0), (0, DEC_PAD - DEC_SEQ), (0, 0))).reshape(DEC_BATCH * DEC_PAD, D_MODEL)
    pos_s = PAST_LEN + np.arange(DEC_BATCH * DEC_PAD) % DEC_PAD
    cos_s, sin_s = _rope_tables(pos_s)
    names = [c[0] for c in _PROJ_COLS]
    ps = dict(zip(names, _project(xs, win_bf, cos_s, sin_s, DEC_BATCH * DEC_PAD,
                                  [(k, "rows", F32) for k in names])))
    decode = _decode_operands(
        ps, state_ret[0].reshape(DEC_BATCH, RET_HEADS * RET_DK, RET_DV),
        cache_swa_k[0].transpose(0, 2, 3, 1), cache_swa_v[0].transpose(0, 2, 3, 1),
        cache_mem_k[0].reshape(DEC_BATCH, N_MEM * MEM_HEADS, MEM_HD),
        cache_mem_v[0].reshape(DEC_BATCH, N_MEM * MEM_HEADS, MEM_HD),
        _retention_tables(DEC_SEQ, DEC_PAD))

    xp = x_prompt.reshape(BATCH * SEQ, D_MODEL)
    assert xp.shape[0] // PERM_TILE == DEC_BATCH
    cos_p, sin_p = _rope_tables(np.arange(SEQ))
    p_outs = (("rq", "rows", F32), ("rk", "rows", F32), ("rv", "rows", BF16), ("rg", "rows", BF16),
              ("sq", "perm", F32), ("sk", "cols", F32), ("sk", "perm", F32), ("sv", "cols", F32),
              ("sv", "perm", F32), ("sg", "rows", BF16), ("mq", "rows", BF16), ("mg", "rows", BF16))
    p_keys = ("rq", "rk", "rv", "rg", "sq4", "sk", "sk4", "sv", "sv4", "sg", "mq", "mg")
    *p_res, mix_s, ret_s = _project(xp, win_bf, cos_p, sin_p, PERM_TILE, p_outs, decode)
    pr = dict(zip(p_keys, p_res))
    mk, mv = _memkv(mem_prompt.reshape(BATCH * N_MEM, D_MODEL), wmem_bf)
    slab = lambda a: a.reshape(SWA_W // LANES, BATCH, SEQ, LANES)
    swa_o = _swa_prompt(slab(pr["sq4"]), slab(pr["sk4"]), slab(pr["sv4"]))
    swa_o = swa_o.reshape(SWA_W // LANES, BATCH * SEQ, LANES)
    yp, ret_p = _mix_prompt(xp, pr, swa_o, mk, mv, wout_bf, _retention_tables(RET_CHUNK, RET_CHUNK),
                            gain, bias, 512)
    ys = _finish(xs, mix_s, wout_bf, gain, bias)

    take = lambda a, w: a.reshape(DEC_BATCH, DEC_PAD, w)[:, :DEC_SEQ]
    swa_rows = lambda a: a.reshape(BATCH, SWA_HEADS, SWA_HD, SEQ).transpose(0, 3, 1, 2)[None]
    return (
        yp.reshape(BATCH, SEQ, D_MODEL),
        take(ys, D_MODEL),
        ret_p.reshape(1, BATCH, RET_HEADS, RET_DK, RET_DV),
        ret_s.reshape(1, DEC_BATCH, RET_HEADS, RET_DK, RET_DV),
        swa_rows(pr["sk"]),
        swa_rows(pr["sv"]),
        take(ps["sk"], SWA_W).reshape(1, DEC_BATCH, DEC_SEQ, SWA_HEADS, SWA_HD),
        take(ps["sv"], SWA_W).reshape(1, DEC_BATCH, DEC_SEQ, SWA_HEADS, SWA_HD),
        mk.reshape(1, BATCH, N_MEM, MEM_HEADS, MEM_HD),
        mv.reshape(1, BATCH, N_MEM, MEM_HEADS, MEM_HD),
    )
```

```python
import functools

import jax
import jax.numpy as jnp
import numpy as np
from jax import lax
from jax.experimental import pallas as pl
from jax.experimental.pallas import tpu as pltpu

F32 = jnp.float32
BF16 = jnp.bfloat16

D_MODEL = 1024
BATCH = 8
SEQ = 2048
DEC_BATCH = 32
DEC_SEQ = 4
PAST_LEN = 8192
N_MEM = 256
MEM_HEADS = 4
MEM_HD = 128
RET_HEADS = 4
RET_DK = 64
RET_DV = 128
RET_CHUNK = 128
SWA_HEADS = 8
SWA_HD = 64
SWA_DILATIONS = (1, 4, 16)
SWA_STEPS = 128
SWA_BLOCK = 128
ROPE_THETA = 10000.0
LN_EPS = 1e-5
GN_EPS = 1e-5
RET_W = RET_HEADS * RET_DV
SWA_W = SWA_HEADS * SWA_HD
MEM_W = MEM_HEADS * MEM_HD
D_MIX = RET_W + SWA_W + MEM_W
DEEPNORM_ALPHA = 2.0 ** 0.25
MEM_SCALE = MEM_HD ** -0.5
LOG2_E = 1.4426950408889634
QK_SCALE = 0.125

LANES = 128
DEC_PAD = 16
PERM_TILE = 512
PERM_PITCH = 40
PERM_ROWS = 16 * PERM_PITCH
STAGE_BUFFERS = 4
OUT_ROWS = 256
SWA_PAR = 4
MEMKV_BATCHES = 4
VMEM_LIMIT = 56 * 1024 * 1024
NEG = -1e30

_PROJ_COLS = (
    ("rq", RET_HEADS * RET_DK, "rope", 1.0),
    ("rk", RET_HEADS * RET_DK, "rope", QK_SCALE),
    ("rv", RET_W, "id", 1.0),
    ("rg", RET_W, "silu", 1.0),
    ("sq", SWA_W, "rope", QK_SCALE),
    ("sk", SWA_W, "rope", 1.0),
    ("sv", SWA_W, "id", 1.0),
    ("sg", SWA_W, "silu", 1.0),
    ("mq", MEM_W, "id", 1.0),
    ("mg", MEM_W, "silu", 1.0),
)


def _dot(a, b):
    return jnp.dot(a, b, preferred_element_type=F32)


def _dot_nt(a, b):
    return lax.dot_general(a, b, (((1,), (1,)), ((), ())), preferred_element_type=F32)


def _dot_tn(a, b):
    return lax.dot_general(a, b, (((0,), (0,)), ((), ())), preferred_element_type=F32)


def _params(n_axes):
    return pltpu.CompilerParams(dimension_semantics=("arbitrary",) * n_axes,
                                vmem_limit_bytes=VMEM_LIMIT)


def _run_interleaved(stages):
    stages = list(stages)
    while stages:
        for g in list(stages):
            if next(g, "done") == "done":
                stages.remove(g)


def _proj_kernel(*refs, dests, n_decode_in=0):
    n_out = len(dests)
    p_in, s_in = refs[:4], refs[4:4 + n_decode_in]
    p_out = refs[4 + n_decode_in:4 + n_decode_in + n_out]
    s_out = refs[4 + n_decode_in + n_out:4 + n_decode_in + n_out + (2 if n_decode_in else 0)]
    stages = [_proj_stages(*p_in, *p_out, dests=dests, stage_ref=refs[-1])]
    if n_decode_in:
        stages.append(_mix_sample_kernel(*s_in, *s_out))
    _run_interleaved(stages)


def _proj_stages(x_ref, w_ref, cos_ref, sin_ref, *out_refs, dests, stage_ref):
    xb = x_ref[...].astype(BF16)
    cos = cos_ref[...]
    sin = sin_ref[...]
    lane = lax.broadcasted_iota(jnp.int32, cos.shape, 1)
    first_half = (lane % 64) < 32
    col = 0
    n_staged = 0
    for name, width, kind, scale in _PROJ_COLS:
        targets = [(o_ref, layout) for o_ref, (dname, layout) in zip(out_refs, dests) if dname == name]
        for c in range(0, width, 2 * LANES):
            h2 = _dot(xb, w_ref[:, col + c:col + c + 2 * LANES])
            for half in range(2):
                h = h2[:, half * LANES:(half + 1) * LANES]
                if kind == "rope":
                    swapped = jnp.where(first_half, pltpu.roll(h, 96, 1), pltpu.roll(h, 32, 1))
                    h = h * cos + swapped * sin
                    if scale != 1.0:
                        h = h * scale
                elif kind == "silu":
                    h = h * (1.0 / (1.0 + jnp.exp(-h)))
                lo = c + half * LANES
                for o_ref, layout in targets:
                    if layout == "rows":
                        o_ref[:, lo:lo + LANES] = h.astype(o_ref.dtype)
                    elif layout == "perm":
                        grp = h.shape[0] // 16
                        stage = stage_ref.at[n_staged % stage_ref.shape[0]]
                        n_staged += 1
                        for g in range(h.shape[0] // 8):
                            base = (g % 2) * 8 * PERM_PITCH + g // 2
                            stage[pl.ds(base, 8, stride=PERM_PITCH), :] = h[8 * g:8 * g + 8]
                        for cls in range(16):
                            o_ref[lo // LANES, pl.ds(cls * grp, grp), :] = (
                                stage[pl.ds(cls * PERM_PITCH, grp), :].astype(o_ref.dtype))
                    else:
                        o_ref[lo:lo + LANES, :] = h.T.astype(o_ref.dtype)
        col += width
        yield


def _project(x2d, w_bf, cos_t, sin_t, tile, outs, decode=None):
    n = x2d.shape[0]
    seq = cos_t.shape[0]
    n_tab = seq // tile
    d_in = w_bf.shape[1]
    widths = {name: w for name, w, _, _ in _PROJ_COLS}
    out_shape, out_specs = [], []
    for name, layout, dt in outs:
        w = widths[name]
        if layout == "rows":
            out_shape.append(jax.ShapeDtypeStruct((n, w), dt))
            out_specs.append(pl.BlockSpec((tile, w), lambda i: (i, 0)))
        elif layout == "perm":
            assert tile == PERM_TILE
            out_shape.append(jax.ShapeDtypeStruct((w // LANES, n, LANES), dt))
            out_specs.append(pl.BlockSpec((w // LANES, tile, LANES), lambda i: (0, i, 0)))
        else:
            out_shape.append(jax.ShapeDtypeStruct((n // seq, w, seq), dt))
            out_specs.append(pl.BlockSpec((None, w, tile), lambda i: (i // n_tab, 0, i % n_tab)))
    d_args, d_in_specs, d_out_specs, d_out_shape = decode if decode else ((), [], [], [])
    return pl.pallas_call(
        functools.partial(_proj_kernel, dests=tuple((name, layout) for name, layout, _ in outs),
                          n_decode_in=len(d_args)),
        grid=(n // tile,),
        in_specs=[
            pl.BlockSpec((tile, D_MODEL), lambda i: (i, 0)),
            pl.BlockSpec((D_MODEL, d_in), lambda i: (0, 0)),
            pl.BlockSpec((tile, LANES), lambda i: (i % n_tab, 0)),
            pl.BlockSpec((tile, LANES), lambda i: (i % n_tab, 0)),
        ] + list(d_in_specs),
        out_specs=out_specs + list(d_out_specs),
        out_shape=out_shape + list(d_out_shape),
        scratch_shapes=[pltpu.VMEM((STAGE_BUFFERS, PERM_ROWS, LANES), F32)],
        compiler_params=_params(1),
        name="proj_decode" if decode else "proj",
    )(x2d, w_bf, cos_t, sin_t, *d_args)


def _decode_operands(ps, state, ckt, cvt, cmk, cmv, tabs):
    n = DEC_PAD
    wb = ckt.shape[3]
    const = lambda a: pl.BlockSpec(a.shape, lambda i: (0,) * a.ndim)
    srow = lambda w: pl.BlockSpec((n, w), lambda i: (i, 0))
    st_s = pl.BlockSpec((None, RET_HEADS * RET_DK, RET_DV), lambda i: (i, 0, 0))
    cache_s = pl.BlockSpec((None, SWA_HEADS, SWA_HD, wb), lambda i: (i, 0, 0, 0))
    mem_s = pl.BlockSpec((None, N_MEM * MEM_HEADS, MEM_HD), lambda i: (i, 0, 0))
    s_names = ("rq", "rk", "rv", "rg", "sq", "sk", "sv", "sg", "mq", "mg")
    widths = {name: w for name, w, _, _ in _PROJ_COLS}
    args = (*[ps[k] for k in s_names], state, ckt, cvt, cmk, cmv, *tabs)
    in_specs = [*[srow(widths[k]) for k in s_names], st_s, cache_s, cache_s, mem_s, mem_s,
                *[const(a) for a in tabs]]
    out_shape = [jax.ShapeDtypeStruct((DEC_BATCH * n, D_MIX), F32),
                 jax.ShapeDtypeStruct((DEC_BATCH, RET_HEADS * RET_DK, RET_DV), F32)]
    return args, in_specs, [srow(D_MIX), st_s], out_shape


def _memkv_kernel(m_ref, w_ref, mk_ref, mv_ref):
    for b in range(m_ref.shape[0] // N_MEM):
        mb = m_ref[pl.ds(b * N_MEM, N_MEM), :].astype(BF16)
        for h in range(MEM_HEADS):
            rows = pl.ds(h, N_MEM, stride=MEM_HEADS)
            mk_ref[b, rows, :] = _dot(mb, w_ref[:, h * MEM_HD:(h + 1) * MEM_HD])
            mv_ref[b, rows, :] = _dot(mb, w_ref[:, MEM_W + h * MEM_HD:MEM_W + (h + 1) * MEM_HD])


def _memkv(mem2d, w_bf):
    n = mem2d.shape[0] // N_MEM
    per = MEMKV_BATCHES
    out = pl.BlockSpec((per, N_MEM * MEM_HEADS, MEM_HD), lambda i: (i, 0, 0))
    return pl.pallas_call(
        _memkv_kernel,
        grid=(n // per,),
        in_specs=[pl.BlockSpec((per * N_MEM, D_MODEL), lambda i: (i, 0)),
                  pl.BlockSpec((D_MODEL, 2 * MEM_W), lambda i: (0, 0))],
        out_specs=[out, out],
        out_shape=[jax.ShapeDtypeStruct((n, N_MEM * MEM_HEADS, MEM_HD), F32)] * 2,
        compiler_params=_params(1),
        name="memkv",
    )(mem2d, w_bf)


def _swa_prompt_kernel(q_ref, k_ref, v_ref, o_ref, a_ref, m_ref, l_ref):
    blk = SWA_BLOCK
    pt = PERM_TILE
    grp = pt // 16
    lo = lax.broadcasted_iota(jnp.int32, (blk, blk), 1) < SWA_HD

    def biases(seq_of):
        rq = seq_of(lax.broadcasted_iota(jnp.int32, (blk, 2 * blk), 0))
        c2 = lax.broadcasted_iota(jnp.int32, (blk, 2 * blk), 1)
        rel = blk + rq - (seq_of(c2 % blk) + blk * (c2 // blk))
        prev = jnp.where((rel >= 0) & (rel <= SWA_STEPS), 0.0, NEG).astype(F32)
        r1 = seq_of(lax.broadcasted_iota(jnp.int32, (blk, blk), 0))
        c1 = seq_of(lax.broadcasted_iota(jnp.int32, (blk, blk), 1))
        return prev, jnp.where(r1 >= c1, 0.0, NEG).astype(F32)

    def gather(ref, pair, starts, n):
        return jnp.concatenate([ref[pair, pl.ds(s, n), :] for s in starts], axis=0)

    def aligned(x):
        return x if isinstance(x, int) else pl.multiple_of(x, 8)

    def blocks(jobs, n, first_pattern, last_pattern):
        n_pairs = SWA_HEADS // 2
        in_at = lambda c: aligned(c[0] * pt + c[1] * grp + c[2])
        acc_starts, q_starts, k_starts = [], [], []
        for chunks, prev_chunks, _, _ in jobs:
            acc_starts.append([aligned(c[0] * pt + c[1] * grp + c[2]) for c in chunks])
            q_starts.append([in_at(c) for c in chunks])
            k_starts.append(([] if prev_chunks is None else [in_at(c) for c in prev_chunks]) + q_starts[-1])
        items = [(j, pair) for j in range(len(jobs)) for pair in range(n_pairs)]
        olds = {}
        if not first_pattern:
            for j, pair in items:
                olds[j, pair] = (gather(m_ref, pair, acc_starts[j], n), gather(l_ref, pair, acc_starts[j], n),
                                 gather(a_ref, pair, acc_starts[j], n))
        vbs = {(j, pair): gather(v_ref, pair, k_starts[j], n).astype(BF16) for j, pair in items}
        scores = {}
        for j, pair in items:
            q = gather(q_ref, pair, q_starts[j], n)
            kb = gather(k_ref, pair, k_starts[j], n).astype(BF16)
            for hh in range(2):
                qm = jnp.where(lo if hh == 0 else ~lo, q, 0.0).astype(BF16)
                scores[j, pair, hh] = _dot_nt(qm, kb) + jobs[j][2]
        ms = {key: jnp.max(s, axis=-1, keepdims=True) for key, s in scores.items()}
        ps = {key: jnp.exp(s - ms[key]) for key, s in scores.items()}
        def v_ext(v):
            lane = lax.broadcasted_iota(jnp.int32, v.shape, 1)
            own = lane < SWA_HD
            zero = jnp.zeros((), BF16)
            ones_lo = jnp.where(lane < SWA_HD, 1.0, 0.0).astype(BF16)
            ones_hi = jnp.where(lane < SWA_HD, 0.0, 1.0).astype(BF16)
            return jnp.concatenate(
                [jnp.concatenate([jnp.where(own, v, zero), ones_lo], axis=1),
                 jnp.concatenate([jnp.where(own, zero, v), ones_hi], axis=1)], axis=0)

        pvs = {(j, pair): _dot(jnp.concatenate([ps[j, pair, 0].astype(BF16), ps[j, pair, 1].astype(BF16)], axis=1),
                               v_ext(vbs[j, pair])) for j, pair in items}
        news = {}
        for j, pair in items:
            m_g = jnp.where(lo, ms[j, pair, 0], ms[j, pair, 1])
            a_g = pvs[j, pair][:, :LANES]
            l_g = pvs[j, pair][:, LANES:]
            if first_pattern:
                news[j, pair] = (m_g, l_g, a_g)
            else:
                m_old, l_old, a_old = olds[j, pair]
                m_new = jnp.maximum(m_old, m_g)
                w_old = jnp.exp(m_old - m_new)
                w_g = jnp.exp(m_g - m_new)
                news[j, pair] = (m_new, w_old * l_old + w_g * l_g, w_old * a_old + w_g * a_g)
        for (j, pair), (m_new, l_new, a_new) in news.items():
            starts = acc_starts[j]
            if last_pattern:
                o = a_new / l_new
                for i in range(len(starts)):
                    o_ref[pair, pl.ds(jobs[j][3] + i, n, stride=len(starts)), :] = o[i * n:(i + 1) * n]
            else:
                for i, s in enumerate(starts):
                    m_ref[pair, pl.ds(s, n), :] = m_new[i * n:(i + 1) * n]
                    l_ref[pair, pl.ds(s, n), :] = l_new[i * n:(i + 1) * n]
                    a_ref[pair, pl.ds(s, n), :] = a_new[i * n:(i + 1) * n]

    par = SWA_PAR
    _, bias16 = biases(lambda i: i)

    par16 = 2 * par

    def class16(r0, carry):
        blocks([([(t, r0 + k * (16 // par16), 0) for t in range(SEQ // pt)], None, bias16, None)
                for k in range(par16)], grp, True, False)
        return carry

    lax.fori_loop(0, 16 // par16, class16, 0)

    bias4_prev, bias4_own = biases(lambda i: 4 * (i % grp) + i // grp)

    def class4(r0, carry):
        chunks_of = lambda r, tile: [(tile, 4 * a + r, 0) for a in range(4)]
        classes = [r0 + k * (4 // par) for k in range(par)]
        blocks([(chunks_of(r, 0), None, bias4_own, None) for r in classes], grp, False, False)

        def per_block(b, c2):
            blocks([(chunks_of(r, b), chunks_of(r, b - 1), bias4_prev, None) for r in classes],
                   grp, False, False)
            return c2

        lax.fori_loop(1, SEQ // pt, per_block, 0)
        return carry

    lax.fori_loop(0, 4 // par, class4, 0)

    bias1_prev, bias1_own = biases(lambda i: 16 * (i % 8) + i // 8)
    per_tile = pt // blk
    chunks1 = lambda c: [(c // per_tile, i, (c % per_tile) * 8) for i in range(16)]
    job1 = lambda c: (chunks1(c), chunks1(c - 1), bias1_prev, aligned(c * blk))
    blocks([(chunks1(0), None, bias1_own, 0)] + [job1(c) for c in range(1, par)], 8, False, True)

    def block1(g, carry):
        blocks([job1(g * par + k) for k in range(par)], 8, False, True)
        return carry

    lax.fori_loop(1, SEQ // blk // par, block1, 0)


def _swa_prompt(sq, sk, sv):
    slabs = SWA_W // LANES
    spec = pl.BlockSpec((slabs, None, SEQ, LANES), lambda b: (0, b, 0, 0))
    in_spec = pl.BlockSpec((slabs, None, sq.shape[2], LANES), lambda b: (0, b, 0, 0))
    return pl.pallas_call(
        _swa_prompt_kernel,
        grid=(BATCH,),
        in_specs=[in_spec, in_spec, in_spec],
        out_specs=spec,
        out_shape=jax.ShapeDtypeStruct((slabs, BATCH, SEQ, LANES), F32),
        scratch_shapes=[pltpu.VMEM((slabs, SEQ, LANES), F32)] * 3,
        compiler_params=_params(1),
        name="swa_prompt",
    )(sq, sk, sv)


def _head_norm(o):
    mu = jnp.mean(o, axis=-1, keepdims=True)
    d = o - mu
    var = jnp.mean(d * d, axis=-1, keepdims=True)
    return d * lax.rsqrt(var + GN_EPS)


def _deepnorm_ln(x, h, gain, bias):
    z = DEEPNORM_ALPHA * x + h
    mu = jnp.mean(z, axis=-1, keepdims=True)
    d = z - mu
    var = jnp.mean(d * d, axis=-1, keepdims=True)
    return d * lax.rsqrt(var + LN_EPS) * gain + bias


def _softmax_rows(s):
    m = jnp.max(s, axis=-1, keepdims=True)
    p = jnp.exp(s - m)
    return p * (1.0 / jnp.sum(p, axis=-1, keepdims=True))


def _mix_prompt_kernel(x_ref, rq_ref, rk_ref, rv_ref, rg_ref, so_ref, sg_ref, mq_ref, mg_ref,
                       mk_ref, mv_ref, wout_ref, dmat_ref, kdec_ref, qdec_ref, gdec_ref,
                       gain_ref, bias_ref, y_ref, state_out_ref,
                       state_ref, mix_ref, mkb_ref, mvb_ref, *, tile):
    t = pl.program_id(1)

    @pl.when(t == 0)
    def _():
        state_ref[...] = jnp.zeros_like(state_ref)
        for h in range(MEM_HEADS):
            rows = pl.ds(h, N_MEM, stride=MEM_HEADS)
            mkb_ref[:, pl.ds(h * MEM_HD, MEM_HD)] = mk_ref[rows, :].astype(BF16)
            mvb_ref[:, pl.ds(h * MEM_HD, MEM_HD)] = mv_ref[rows, :].astype(BF16)

    ck = RET_CHUNK
    lane = lax.broadcasted_iota(jnp.int32, (ck, LANES), 1)
    lo = lane < RET_DK
    top = lax.broadcasted_iota(jnp.int32, (LANES, LANES), 0) < RET_DK
    gain = gain_ref[...]
    bias = bias_ref[...]

    n_ck = tile // ck
    n_pairs = RET_HEADS // 2
    rows_of = [pl.ds(c * ck, ck) for c in range(n_ck)]
    items = [(c, pair, hh) for c in range(n_ck) for pair in range(n_pairs) for hh in range(2)]
    hs_of = lambda pair, hh: pl.ds((2 * pair + hh) * RET_DV, RET_DV)
    sel = lambda x, hh: jnp.where(lo if hh == 0 else ~lo, x, 0.0).astype(BF16)
    pairs = [(c, pair) for c in range(n_ck) for pair in range(n_pairs)]
    qk = {}
    for c, pair in pairs:
        cs = pl.ds(pair * LANES, LANES)
        q = rq_ref[rows_of[c], cs].astype(F32)
        k = rk_ref[rows_of[c], cs].astype(F32)
        qk[c, pair] = (q.astype(BF16), jnp.concatenate([sel(k, 0), sel(k, 1)], axis=0),
                       (k * kdec_ref[:, cs]).astype(BF16), q * qdec_ref[:, cs])
    v_pair = lambda c, pair: rv_ref[rows_of[c], pl.ds(pair * 2 * RET_DV, 2 * RET_DV)]
    s2 = {key: _dot_nt(qk[key][0], qk[key][1]) for key in pairs}
    kv2 = {key: _dot_tn(qk[key][2], v_pair(*key)) for key in pairs}
    yield
    state = {(0, pair): state_ref[pair] for pair in range(n_pairs)}
    for c, pair in pairs:
        state[c + 1, pair] = (gdec_ref[pl.ds(pair * LANES, LANES), :] * state[c, pair]
                              + jnp.where(top, kv2[c, pair][:, :RET_DV], kv2[c, pair][:, RET_DV:]))
    for pair in range(n_pairs):
        state_ref[pair] = state[n_ck, pair]
    o = {}
    for c, pair, hh in items:
        sh = (s2[c, pair][:, hh * ck:(hh + 1) * ck] * dmat_ref[2 * pair + hh]).astype(BF16)
        lhs = jnp.concatenate([sh, sel(qk[c, pair][3], hh)], axis=1)
        rhs = jnp.concatenate([rv_ref[rows_of[c], hs_of(pair, hh)], state[c, pair].astype(BF16)], axis=0)
        o[c, pair, hh] = _dot(lhs, rhs)
    yield
    for c, pair, hh in items:
        hs = hs_of(pair, hh)
        mix_ref[rows_of[c], hs] = (rg_ref[rows_of[c], hs].astype(F32) * _head_norm(o[c, pair, hh])).astype(BF16)
    for pair in range(SWA_W // LANES):
        cs = pl.ds(pair * LANES, LANES)
        mix_ref[:, pl.ds(RET_W + pair * LANES, LANES)] = (
            sg_ref[:, cs].astype(F32) * so_ref[pair]).astype(BF16)
    for c in range(n_ck):
        yield
        ss = [_dot_nt(mq_ref[rows_of[c], pl.ds(h * MEM_HD, MEM_HD)], mkb_ref[:, pl.ds(h * MEM_HD, MEM_HD)])
              for h in range(MEM_HEADS)]
        ps = [jnp.exp2((s - jnp.max(s, axis=-1, keepdims=True)) * (MEM_SCALE * LOG2_E)) for s in ss]
        for h in range(MEM_HEADS):
            hs = pl.ds(h * MEM_HD, MEM_HD)
            o = _dot(ps[h].astype(BF16), mvb_ref[:, hs]) * (1.0 / jnp.sum(ps[h], axis=-1, keepdims=True))
            mix_ref[rows_of[c], pl.ds(RET_W + SWA_W + h * MEM_HD, MEM_HD)] = (
                mg_ref[rows_of[c], hs].astype(F32) * o).astype(BF16)
    yield
    blocks = [pl.ds(r * OUT_ROWS, OUT_ROWS) for r in range(tile // OUT_ROWS)]
    hout = _dot(mix_ref[blocks[0], :], wout_ref[...])
    for r, rows in enumerate(blocks):
        nxt = _dot(mix_ref[blocks[r + 1], :], wout_ref[...]) if r + 1 < len(blocks) else None
        y_ref[rows, :] = _deepnorm_ln(x_ref[rows, :], hout, gain, bias)
        hout = nxt

    @pl.when(t == pl.num_programs(1) - 1)
    def _():
        state_out_ref[pl.ds(0, LANES), :] = state_ref[0]
        state_out_ref[pl.ds(LANES, LANES), :] = state_ref[1]


def _mix_prompt_body(*refs, tile):
    _run_interleaved([_mix_prompt_kernel(*refs, tile=tile)])


def _mix_prompt(x2d, pr, swa_o, mk, mv, wout_bf, tabs_p, gain, bias, tile):
    nt = SEQ // tile
    step = lambda b, t: b * nt + t
    row = lambda w: pl.BlockSpec((tile, w), lambda b, t: (step(b, t), 0))
    const = lambda a: pl.BlockSpec(a.shape, lambda b, t: (0,) * a.ndim)
    mem_p = pl.BlockSpec((None, N_MEM * MEM_HEADS, MEM_HD), lambda b, t: (b, 0, 0))
    st_p = pl.BlockSpec((None, RET_HEADS * RET_DK, RET_DV), lambda b, t: (b, 0, 0))
    p_args = (x2d, pr["rq"], pr["rk"], pr["rv"], pr["rg"], swa_o, pr["sg"], pr["mq"], pr["mg"],
              mk, mv, wout_bf, *tabs_p, gain, bias)
    p_specs = [row(D_MODEL), row(RET_HEADS * RET_DK), row(RET_HEADS * RET_DK), row(RET_W), row(RET_W),
               pl.BlockSpec((SWA_W // LANES, tile, LANES), lambda b, t: (0, step(b, t), 0)),
               row(SWA_W), row(MEM_W), row(MEM_W), mem_p, mem_p, const(wout_bf),
               *[const(a) for a in tabs_p], const(gain), const(bias)]
    return pl.pallas_call(
        functools.partial(_mix_prompt_body, tile=tile),
        grid=(BATCH, nt),
        in_specs=p_specs,
        out_specs=[row(D_MODEL), st_p],
        out_shape=[jax.ShapeDtypeStruct((BATCH * SEQ, D_MODEL), F32),
                   jax.ShapeDtypeStruct((BATCH, RET_HEADS * RET_DK, RET_DV), F32)],
        scratch_shapes=[pltpu.VMEM((RET_HEADS // 2, LANES, RET_DV), F32),
                        pltpu.VMEM((tile, D_MIX), BF16),
                        pltpu.VMEM((N_MEM, MEM_W), BF16),
                        pltpu.VMEM((N_MEM, MEM_W), BF16)],
        compiler_params=_params(2),
        name="mix_prompt",
    )(*p_args)


def _mix_sample_kernel(rq_ref, rk_ref, rv_ref, rg_ref, sq_ref, sk_ref, sv_ref, sg_ref, mq_ref, mg_ref,
                       state_ref, ckt_ref, cvt_ref, cmk_ref, cmv_ref,
                       dmat_ref, kdec_ref, qdec_ref, gdec_ref,
                       mix_ref, state_out_ref):
    n = DEC_PAD
    q = rq_ref[...]
    k = rk_ref[...]
    kb = k.astype(BF16)
    kd = (k * kdec_ref[...]).astype(BF16)
    qd = q * qdec_ref[...]
    st = state_ref[...]
    stb = st.astype(BF16)
    vb = rv_ref[...].astype(BF16)
    kv = _dot_tn(kd, vb)
    lane_qk = lax.broadcasted_iota(jnp.int32, q.shape, 1) // RET_DK
    ret_s = [_dot_nt(jnp.where(lane_qk == h, q, 0.0).astype(BF16), kb) for h in range(RET_HEADS)]
    ret_cross = [_dot(jnp.where(lane_qk == h, qd, 0.0).astype(BF16), stb) for h in range(RET_HEADS)]
    mem_rows = [pl.ds(h, N_MEM, stride=MEM_HEADS) for h in range(MEM_HEADS)]
    mem_s = [_dot_nt(mq_ref[:, pl.ds(h * MEM_HD, MEM_HD)].astype(BF16), cmk_ref[mem_rows[h], :].astype(BF16))
             for h in range(MEM_HEADS)]

    blk = SWA_BLOCK
    wb = ckt_ref.shape[2]
    rt = 8
    lo = lax.broadcasted_iota(jnp.int32, (n, LANES), 1) < SWA_HD
    lo_t = lax.broadcasted_iota(jnp.int32, (rt, LANES), 1) < SWA_HD
    tok = lambda w: lax.broadcasted_iota(jnp.int32, (rt, w), 0)
    col = lambda w: lax.broadcasted_iota(jnp.int32, (rt, w), 1)
    windows = (
        (wb - blk, jnp.where(col(blk) >= tok(blk), 0.0, NEG), jnp.where(col(blk) <= tok(blk), 0.0, NEG)),
        (wb - 4 * blk, jnp.where(col(4 * blk) % 4 == tok(4 * blk), 0.0, NEG),
         jnp.where(col(blk) == tok(blk), 0.0, NEG)),
        (0, jnp.where(col(wb) % 16 == tok(wb), 0.0, NEG), jnp.where(col(blk) == tok(blk), 0.0, NEG)),
    )
    pad = jnp.zeros((blk - n, LANES), BF16)
    zero_c = jnp.zeros((rt, wb), F32)
    zero_n = jnp.zeros((rt, blk), F32)
    n_pairs = SWA_HEADS // 2
    n_win = len(windows)
    pair_cols = [pl.ds(pair * LANES, LANES) for pair in range(n_pairs)]
    vps = [cvt_ref[2 * pair:2 * pair + 2].reshape(2 * SWA_HD, wb).astype(BF16) for pair in range(n_pairs)]
    v_news = [jnp.concatenate([sv_ref[:, cs].astype(BF16), pad], axis=0) for cs in pair_cols]
    s_alls, sn_alls = [], []
    for pair, cs in enumerate(pair_cols):
        kp = ckt_ref[2 * pair:2 * pair + 2].reshape(2 * SWA_HD, wb).astype(BF16)
        k_new = jnp.concatenate([sk_ref[:, cs].astype(BF16), pad], axis=0)
        q = sq_ref[:, cs]
        qs = jnp.concatenate([jnp.where(lo, q, 0.0), jnp.where(lo, 0.0, q)], axis=0).astype(BF16)
        s_alls.append(_dot(qs, kp))
        sn_alls.append(_dot_nt(qs, k_new))
    yield
    p_rows = [[] for _ in range(n_pairs)]
    pn_rows = [[] for _ in range(n_pairs)]
    stats = [[] for _ in range(n_pairs)]
    for pair in range(n_pairs):
        for hh in range(2):
            s = s_alls[pair][hh * n:hh * n + rt]
            sn = sn_alls[pair][hh * n:hh * n + rt]
            for w0, bias_c, bias_n in windows:
                sc = s[:, w0:] + bias_c
                snb = sn + bias_n
                m = jnp.maximum(jnp.max(sc, axis=-1, keepdims=True), jnp.max(snb, axis=-1, keepdims=True))
                pc = jnp.exp(sc - m)
                pn = jnp.exp(snb - m)
                stats[pair].append((m, jnp.sum(pc, axis=-1, keepdims=True) + jnp.sum(pn, axis=-1, keepdims=True)))
                if w0:
                    pc = jnp.concatenate([jnp.zeros((rt, w0), F32), pc], axis=1)
                p_rows[pair].append(pc)
                pn_rows[pair].append(pn)
            p_rows[pair].append(zero_c)
            pn_rows[pair].append(zero_n)
    ret_sb = [(ret_s[h] * dmat_ref[h]).astype(BF16) for h in range(RET_HEADS)]
    mem_p = [_softmax_rows(s * MEM_SCALE).astype(BF16) for s in mem_s]
    yield
    pvs = [_dot_nt(jnp.concatenate(p_rows[pair], axis=0).astype(BF16), vps[pair])
           + _dot(jnp.concatenate(pn_rows[pair], axis=0).astype(BF16), v_news[pair])
           for pair in range(n_pairs)]
    ret_intra = [_dot(ret_sb[h], vb[:, h * RET_DV:(h + 1) * RET_DV]) for h in range(RET_HEADS)]
    mem_o = [_dot(mem_p[h], cmv_ref[mem_rows[h], :].astype(BF16)) for h in range(MEM_HEADS)]
    yield
    pieces = []
    for pair in range(n_pairs):
        heads = []
        for hh in range(2):
            parts = []
            for i in range(n_win):
                m, l = stats[pair][hh * n_win + i]
                r0 = (hh * (n_win + 1) + i) * rt
                parts.append((pvs[pair][r0:r0 + rt] / l, m, l))
            m_all = jnp.maximum(jnp.maximum(parts[0][1], parts[1][1]), parts[2][1])
            ws = [l * jnp.exp(m - m_all) for (_, m, l) in parts]
            heads.append((ws[0] * parts[0][0] + ws[1] * parts[1][0] + ws[2] * parts[2][0])
                         / (ws[0] + ws[1] + ws[2]))
        pieces.append(jnp.where(lo_t, heads[0], heads[1]))
    swa = jnp.concatenate(pieces, axis=1)
    swa = jnp.concatenate([swa, jnp.zeros((n - rt, SWA_W), F32)], axis=0)
    mix_ref[:, pl.ds(RET_W, SWA_W)] = sg_ref[...] * swa
    for h in range(RET_HEADS):
        hs = pl.ds(h * RET_DV, RET_DV)
        ks = pl.ds(h * RET_DK, RET_DK)
        mix_ref[:, hs] = rg_ref[:, hs] * _head_norm(ret_intra[h] + ret_cross[h])
        state_out_ref[ks, :] = (gdec_ref[ks, :] * st[h * RET_DK:(h + 1) * RET_DK, :]
                                + kv[h * RET_DK:(h + 1) * RET_DK, h * RET_DV:(h + 1) * RET_DV])
    for h in range(MEM_HEADS):
        hs = pl.ds(h * MEM_HD, MEM_HD)
        mix_ref[:, pl.ds(RET_W + SWA_W + h * MEM_HD, MEM_HD)] = mg_ref[:, hs] * mem_o[h]


def _finish_kernel(x_ref, mix_ref, wout_ref, gain_ref, bias_ref, y_ref):
    hout = _dot(mix_ref[...].astype(BF16), wout_ref[...])
    y_ref[...] = _deepnorm_ln(x_ref[...], hout, gain_ref[...], bias_ref[...])


def _finish(x2d, mix, wout_bf, gain, bias):
    n = x2d.shape[0]
    tile = 256
    const2 = lambda a: pl.BlockSpec(a.shape, lambda i: (0, 0))
    return pl.pallas_call(
        _finish_kernel,
        grid=(n // tile,),
        in_specs=[pl.BlockSpec((tile, D_MODEL), lambda i: (i, 0)),
                  pl.BlockSpec((tile, D_MIX), lambda i: (i, 0)),
                  const2(wout_bf), const2(gain), const2(bias)],
        out_specs=pl.BlockSpec((tile, D_MODEL), lambda i: (i, 0)),
        out_shape=jax.ShapeDtypeStruct((n, D_MODEL), F32),
        compiler_params=_params(1),
        name="finish",
    )(x2d, mix, wout_bf, gain, bias)


def _rope_tables(pos):
    half = SWA_HD // 2
    inv = ROPE_THETA ** (-np.arange(half, dtype=np.float64) * 2.0 / SWA_HD)
    ang = pos.astype(np.float64)[:, None] * inv[None, :]
    cos, sin = np.cos(ang), np.sin(ang)
    reps = LANES // SWA_HD
    return (jnp.asarray(np.tile(np.concatenate([cos, cos], axis=1), (1, reps)), F32),
            jnp.asarray(np.tile(np.concatenate([-sin, sin], axis=1), (1, reps)), F32))


def _retention_tables(chunk, rows):
    lg = np.log1p(-np.exp2(-5.0 - np.arange(RET_HEADS, dtype=np.float64)))
    idx = np.arange(rows, dtype=np.float64)
    live = idx < chunk
    rel = idx[:, None] - idx[None, :]
    ok = (rel >= 0) & live[:, None] & live[None, :]
    dmat = np.where(ok[None], np.exp(np.maximum(rel, 0.0)[None] * lg[:, None, None]), 0.0)
    kdec = np.where(live[:, None], np.exp((chunk - 1.0 - idx)[:, None] * lg[None, :]), 0.0)
    qdec = np.where(live[:, None], np.exp((idx + 1.0)[:, None] * lg[None, :]), 0.0)
    g = np.exp(chunk * lg)
    kdec = np.repeat(kdec, RET_DK, axis=1)
    qdec = np.repeat(qdec, RET_DK, axis=1)
    gdec = np.broadcast_to(np.repeat(g, RET_DK)[:, None], (RET_HEADS * RET_DK, RET_DV))
    return tuple(jnp.asarray(t, F32) for t in (dmat, kdec, qdec, gdec))


def kernel(x_prompt, x_sample, state_ret, cache_swa_k, cache_swa_v, cache_mem_k, cache_mem_v,
           mem_prompt, w_in, w_mem_kv, w_out, ln_gain, ln_bias):
    depth = w_in.shape[0]
    assert depth == 1
    win_bf = w_in[0].astype(BF16)
    wmem_bf = w_mem_kv[0].astype(BF16)
    wout_bf = w_out[0].astype(BF16)
    gain = ln_gain[0].reshape(1, D_MODEL)
    bias = ln_bias[0].reshape(1, D_MODEL)

    xs = jnp.pad(x_sample, ((0, 0), (0, DEC_PAD - DEC_SEQ), (0, 0))).reshape(DEC_BATCH * DEC_PAD, D_MODEL)
    pos_s = PAST_LEN + np.arange(DEC_BATCH * DEC_PAD) % DEC_PAD
    cos_s, sin_s = _rope_tables(pos_s)
    names = [c[0] for c in _PROJ_COLS]
    ps = dict(zip(names, _project(xs, win_bf, cos_s, sin_s, DEC_BATCH * DEC_PAD,
                                  [(k, "rows", F32) for k in names])))
    decode = _decode_operands(
        ps, state_ret[0].reshape(DEC_BATCH, RET_HEADS * RET_DK, RET_DV),
        cache_swa_k[0].transpose(0, 2, 3, 1), cache_swa_v[0].transpose(0, 2, 3, 1),
        cache_mem_k[0].reshape(DEC_BATCH, N_MEM * MEM_HEADS, MEM_HD),
        cache_mem_v[0].reshape(DEC_BATCH, N_MEM * MEM_HEADS, MEM_HD),
        _retention_tables(DEC_SEQ, DEC_PAD))

    xp = x_prompt.reshape(BATCH * SEQ, D_MODEL)
    assert xp.shape[0] // PERM_TILE == DEC_BATCH
    cos_p, sin_p = _rope_tables(np.arange(SEQ))
    p_outs = (("rq", "rows", BF16), ("rk", "rows", BF16), ("rv", "rows", BF16), ("rg", "rows", BF16),
              ("sq", "perm", F32), ("sk", "cols", F32), ("sk", "perm", F32), ("sv", "cols", F32),
              ("sv", "perm", F32), ("sg", "rows", BF16), ("mq", "rows", BF16), ("mg", "rows", BF16))
    p_keys = ("rq", "rk", "rv", "rg", "sq4", "sk", "sk4", "sv", "sv4", "sg", "mq", "mg")
    *p_res, mix_s, ret_s = _project(xp, win_bf, cos_p, sin_p, PERM_TILE, p_outs, decode)
    pr = dict(zip(p_keys, p_res))
    mk, mv = _memkv(mem_prompt.reshape(BATCH * N_MEM, D_MODEL), wmem_bf)
    slab = lambda a: a.reshape(SWA_W // LANES, BATCH, SEQ, LANES)
    swa_o = _swa_prompt(slab(pr["sq4"]), slab(pr["sk4"]), slab(pr["sv4"]))
    swa_o = swa_o.reshape(SWA_W // LANES, BATCH * SEQ, LANES)
    yp, ret_p = _mix_prompt(xp, pr, swa_o, mk, mv, wout_bf, _retention_tables(RET_CHUNK, RET_CHUNK),
                            gain, bias, 512)
    ys = _finish(xs, mix_s, wout_bf, gain, bias)

    take = lambda a, w: a.reshape(DEC_BATCH, DEC_PAD, w)[:, :DEC_SEQ]
    swa_rows = lambda a: a.reshape(BATCH, SWA_HEADS, SWA_HD, SEQ).transpose(0, 3, 1, 2)[None]
    return (
        yp.reshape(BATCH, SEQ, D_MODEL),
        take(ys, D_MODEL),
        ret_p.reshape(1, BATCH, RET_HEADS, RET_DK, RET_DV),
        ret_s.reshape(1, DEC_BATCH, RET_HEADS, RET_DK, RET_DV),
        swa_rows(pr["sk"]),
        swa_rows(pr["sv"]),
        take(ps["sk"], SWA_W).reshape(1, DEC_BATCH, DEC_SEQ, SWA_HEADS, SWA_HD),
        take(ps["sv"], SWA_W).reshape(1, DEC_BATCH, DEC_SEQ, SWA_HEADS, SWA_HD),
        mk.reshape(1, BATCH, N_MEM, MEM_HEADS, MEM_HD),
        mv.reshape(1, BATCH, N_MEM, MEM_HEADS, MEM_HD),
    )
```
